```python
import jax, jax.numpy as jnp
from jax import lax
import numpy as np

D_MODEL = 1024
BATCH = 32
SEQ = 256
DEPTH = 2
DEC_BATCH = 2
DEC_SEQ = 4096
PAST_LEN = 512

GRID_W = 64
N_RET_HEADS = 4
RET_DK = 128
RET_DV = 128
RET_CHUNK = 128
N_HEADS = 8
N_KV_HEADS = 2
HEAD_DIM = 64
Q_BLOCK = 128
ROPE_THETA = 10000.0
CONV_DIM = 512
CONV_K = 3
FFN_DIM = 2816
N_MOD = 9
EPS = 1e-6
RET_QK_W = N_RET_HEADS * RET_DK
RET_V_W = N_RET_HEADS * RET_DV
ATT_Q_W = N_HEADS * HEAD_DIM
ATT_KV_W = N_KV_HEADS * HEAD_DIM
IN_SIZES = (RET_QK_W, RET_QK_W, RET_V_W, RET_V_W, ATT_Q_W, ATT_KV_W, ATT_KV_W,
            CONV_DIM, CONV_DIM, CONV_DIM, D_MODEL, D_MODEL, D_MODEL)
IN_COLS = sum(IN_SIZES)

kernel_name = "hybrid_prefix_diffusion_trunk_step"


def rms_norm(x, g):
    xf = x.astype(jnp.float32)
    y = xf * lax.rsqrt(jnp.mean(xf * xf, axis=-1, keepdims=True) + EPS)
    return (y * g.astype(jnp.float32)).astype(x.dtype)


def modulate_norm(x, g, shift, scale):
    return rms_norm(x, g) * (1 + scale) + shift


def swiglu(x, w_in, w_out):
    gate, up = jnp.split(x @ w_in, 2, axis=-1)
    return (jax.nn.silu(gate) * up) @ w_out


def axial_rope_tables(n_tok):
    rows = n_tok // GRID_W
    t_row = jnp.repeat(jnp.arange(rows, dtype=jnp.float32), GRID_W)
    t_col = jnp.tile(jnp.arange(GRID_W, dtype=jnp.float32), rows)
    n_freq = HEAD_DIM // 4
    inv = ROPE_THETA ** (-jnp.arange(n_freq, dtype=jnp.float32) / n_freq)
    ang = jnp.concatenate([t_row[:, None] * inv, t_col[:, None] * inv], axis=-1)
    return jnp.cos(ang), jnp.sin(ang)


def apply_rope(x, cos, sin):
    half = HEAD_DIM // 2
    shape = (1, cos.shape[0]) + (1,) * (x.ndim - 3) + (half,)
    c = cos.reshape(shape)
    s = sin.reshape(shape)
    xf = x.astype(jnp.float32)
    x1, x2 = xf[..., :half], xf[..., half:]
    return jnp.concatenate([x1 * c - x2 * s, x1 * s + x2 * c], axis=-1).astype(x.dtype)


def retention_chunked(q, k, v, log_gamma, s0):
    B, T, H, _ = q.shape
    DV = v.shape[-1]
    n = T // RET_CHUNK

    def chunks(a):
        return a.astype(jnp.float32).reshape(B, n, RET_CHUNK, H, a.shape[-1]).transpose(1, 0, 3, 2, 4)

    qc, kc, vc = chunks(q), chunks(k), chunks(v)
    pos = jnp.arange(RET_CHUNK, dtype=jnp.float32)
    lg = log_gamma.astype(jnp.float32)[:, None]
    diff = pos[:, None] - pos[None, :]
    decay_in = jnp.where(diff >= 0, jnp.exp(lg[:, :, None] * jnp.maximum(diff, 0.0)), 0.0)
    decay_q = jnp.exp(lg * (pos + 1.0))[..., None]
    decay_k = jnp.exp(lg * (RET_CHUNK - 1.0 - pos))[..., None]
    decay_c = jnp.exp(lg * RET_CHUNK)[..., None]

    def step(s, inp):
        qi, ki, vi = inp
        att = jnp.einsum('bhcd,bhmd->bhcm', qi, ki) * decay_in
        o = jnp.einsum('bhcm,bhme->bhce', att, vi) + jnp.einsum('bhcd,bhde->bhce', qi, s) * decay_q
        s = s * decay_c + jnp.einsum('bhmd,bhme->bhde', ki * decay_k, vi)
        return s, o

    s_fin, o = lax.scan(step, s0.astype(jnp.float32), (qc, kc, vc))
    o = o.transpose(1, 0, 3, 2, 4).reshape(B, T, H, DV)
    return o, s_fin


def blocked_attention(q, k, v):
    B, T, KV, G, Dh = q.shape
    nb = T // Q_BLOCK
    qb = q.reshape(B, nb, Q_BLOCK, KV, G, Dh).swapaxes(0, 1)
    scale = Dh ** -0.5

    def one_block(q_blk):
        s = jnp.einsum('bqkgd,bskd->bkgqs', q_blk, k, preferred_element_type=jnp.float32) * scale
        p = jax.nn.softmax(s, axis=-1).astype(v.dtype)
        return jnp.einsum('bkgqs,bskd->bqkgd', p, v)

    o = lax.map(one_block, qb)
    return o.swapaxes(0, 1).reshape(B, T, KV, G, Dh)


def short_conv3(u, w):
    up = jnp.pad(u, ((0, 0), (1, 1), (0, 0)))
    return up[:, :-2] * w[0] + up[:, 1:-1] * w[1] + up[:, 2:] * w[2]


def token_mix(xn, w_in, decay_logit, ret_gn, q_gain, k_gain, conv_w, w_ret_o, w_att_o, w_conv_o, w_o,
              ret_s0, kv_prefix, rope):
    B, T, _ = xn.shape
    parts = jnp.split(xn @ w_in, np.cumsum(IN_SIZES)[:-1].tolist(), axis=-1)
    rq, rk, rv, rg, aq, ak, av, cb, cc, cx, g_ret, g_att, g_conv = parts

    rq = rq.reshape(B, T, N_RET_HEADS, RET_DK)
    rk = rk.reshape(B, T, N_RET_HEADS, RET_DK) * (RET_DK ** -0.5)
    rv = rv.reshape(B, T, N_RET_HEADS, RET_DV)
    log_gamma = jax.nn.log_sigmoid(decay_logit.astype(jnp.float32))
    o_f, s_f = retention_chunked(rq, rk, rv, log_gamma[0], ret_s0[:, 0])
    o_b, s_b = retention_chunked(rq[:, ::-1], rk[:, ::-1], rv[:, ::-1], log_gamma[1], ret_s0[:, 1])
    o_r = (o_f + o_b[:, ::-1]).astype(xn.dtype)
    o_r = rms_norm(o_r, ret_gn.reshape(N_RET_HEADS, RET_DV)).reshape(B, T, RET_V_W)
    y_ret = (jax.nn.silu(rg) * o_r) @ w_ret_o
    ret_state = jnp.stack([s_f, s_b], axis=1).astype(xn.dtype)

    q = rms_norm(aq.reshape(B, T, N_KV_HEADS, N_HEADS // N_KV_HEADS, HEAD_DIM), q_gain)
    k = rms_norm(ak.reshape(B, T, N_KV_HEADS, HEAD_DIM), k_gain)
    v = av.reshape(B, T, N_KV_HEADS, HEAD_DIM)
    if rope is None:
        k_all, v_all = k, v
    else:
        q = apply_rope(q, rope[0], rope[1])
        k_all = jnp.concatenate([kv_prefix[0], apply_rope(k, rope[0], rope[1])], axis=1)
        v_all = jnp.concatenate([kv_prefix[1], v], axis=1)
    y_att = blocked_attention(q, k_all, v_all).reshape(B, T, ATT_Q_W) @ w_att_o

    y_conv = (cb * short_conv3(cc * cx, conv_w)) @ w_conv_o

    merged = (jax.nn.sigmoid(g_ret) * y_ret + jax.nn.sigmoid(g_att) * y_att
              + jax.nn.sigmoid(g_conv) * y_conv)
    return merged @ w_o, ret_state, k, v


def layer(h, cond, W, l, ret_s0, kv_prefix, rope):
    (w_ada, b_ada, norm_w, w_ffn_in, w_ffn_out, w_in, ret_decay_logit, ret_gn, q_gain, k_gain,
     conv_w, w_ret_o, w_att_o, w_conv_o, w_o) = W
    mod = (jax.nn.silu(cond) @ w_ada[l] + b_ada[l])[:, None, :]
    sh1, sc1, g1, sh2, sc2, g2, sh3, sc3, g3 = jnp.split(mod, N_MOD, axis=-1)
    h = h + 0.5 * g1 * swiglu(modulate_norm(h, norm_w[l, 0], sh1, sc1), w_ffn_in[l, 0], w_ffn_out[l, 0])
    y, ret_state, k, v = token_mix(modulate_norm(h, norm_w[l, 1], sh2, sc2), w_in[l], ret_decay_logit[l],
                                   ret_gn[l], q_gain[l], k_gain[l], conv_w[l], w_ret_o[l], w_att_o[l],
                                   w_conv_o[l], w_o[l], ret_s0, kv_prefix, rope)
    h = h + g2 * y
    h = h + 0.5 * g3 * swiglu(modulate_norm(h, norm_w[l, 2], sh3, sc3), w_ffn_in[l, 1], w_ffn_out[l, 1])
    return h, ret_state, k, v


def setup_inputs(seed: int = 0) -> dict:
    key = jax.random.key(seed)
    ks = jax.random.split(key, 24)
    f32 = jnp.float32

    def nrm(k, shape, s=1.0):
        return jax.random.normal(k, shape, f32) * s

    base_logit = jnp.log(2.0 ** (5.0 + jnp.arange(N_RET_HEADS, dtype=f32)) - 1.0)
    return {
        "x_prompt": nrm(ks[0], (BATCH, SEQ, D_MODEL)),
        "x_sample": nrm(ks[1], (DEC_BATCH, DEC_SEQ, D_MODEL)),
        "c": nrm(ks[2], (DEC_BATCH, D_MODEL)),
        "state_ret": nrm(ks[3], (DEC_BATCH, DEPTH, 2, N_RET_HEADS, RET_DK, RET_DV), 0.5),
        "cache_k": nrm(ks[4], (DEC_BATCH, DEPTH, PAST_LEN, N_KV_HEADS, HEAD_DIM)),
        "cache_v": nrm(ks[5], (DEC_BATCH, DEPTH, PAST_LEN, N_KV_HEADS, HEAD_DIM)),
        "c_ctx": nrm(ks[6], (D_MODEL,)),
        "w_ada": nrm(ks[7], (DEPTH, D_MODEL, N_MOD * D_MODEL), D_MODEL ** -0.5),
        "b_ada": nrm(ks[8], (DEPTH, N_MOD * D_MODEL), 0.02),
        "norm_w": 1.0 + nrm(ks[9], (DEPTH, 3, D_MODEL), 0.02),
        "w_ffn_in": nrm(ks[10], (DEPTH, 2, D_MODEL, 2 * FFN_DIM), D_MODEL ** -0.5),
        "w_ffn_out": nrm(ks[11], (DEPTH, 2, FFN_DIM, D_MODEL), FFN_DIM ** -0.5),
        "w_in": nrm(ks[12], (DEPTH, D_MODEL, IN_COLS), D_MODEL ** -0.5),
        "ret_decay_logit": base_logit + nrm(ks[13], (DEPTH, 2, N_RET_HEADS), 0.1),
        "ret_gn": 1.0 + nrm(ks[14], (DEPTH, RET_V_W), 0.02),
        "q_gain": 1.0 + nrm(ks[15], (DEPTH, HEAD_DIM), 0.02),
        "k_gain": 1.0 + nrm(ks[16], (DEPTH, HEAD_DIM), 0.02),
        "conv_w": nrm(ks[17], (DEPTH, CONV_K, CONV_DIM), CONV_K ** -0.5),
        "w_ret_o": nrm(ks[18], (DEPTH, RET_V_W, D_MODEL), RET_V_W ** -0.5),
        "w_att_o": nrm(ks[19], (DEPTH, ATT_Q_W, D_MODEL), ATT_Q_W ** -0.5),
        "w_conv_o": nrm(ks[20], (DEPTH, CONV_DIM, D_MODEL), CONV_DIM ** -0.5),
        "w_o": nrm(ks[21], (DEPTH, D_MODEL, D_MODEL), D_MODEL ** -0.5),
    }


def reference(x_prompt, x_sample, c, state_ret, cache_k, cache_v, c_ctx, w_ada, b_ada, norm_w, w_ffn_in,
              w_ffn_out, w_in, ret_decay_logit, ret_gn, q_gain, k_gain, conv_w, w_ret_o, w_att_o, w_conv_o, w_o):
    W = (w_ada, b_ada, norm_w, w_ffn_in, w_ffn_out, w_in, ret_decay_logit, ret_gn, q_gain, k_gain,
         conv_w, w_ret_o, w_att_o, w_conv_o, w_o)

    h = x_prompt
    zero_state = jnp.zeros((x_prompt.shape[0], 2, N_RET_HEADS, RET_DK, RET_DV), jnp.float32)
    states, keys, values = [], [], []
    for l in range(DEPTH):
        h, st, k, v = layer(h, c_ctx[None, :], W, l, zero_state, None, None)
        states.append(st)
        keys.append(k)
        values.append(v)
    y_prompt = h
    new_state_ret = jnp.stack(states, axis=1)
    new_cache_k = jnp.stack(keys, axis=1)
    new_cache_v = jnp.stack(values, axis=1)

    rope = axial_rope_tables(x_sample.shape[1])
    h = x_sample
    for l in range(DEPTH):
        h, _, _, _ = layer(h, c, W, l, state_ret[:, l], (cache_k[:, l], cache_v[:, l]), rope)
    y_sample = h
    return (y_prompt, y_sample, new_state_ret, new_cache_k, new_cache_v)
```

```python
import functools

import numpy as np
import jax
import jax.numpy as jnp
from jax import lax
from jax.experimental import pallas as pl
from jax.experimental.pallas import tpu as pltpu

D_MODEL = 1024
DEPTH = 2
GRID_W = 64
N_RET_HEADS = 4
RET_DK = 128
RET_DV = 128
RET_CHUNK = 128
N_HEADS = 8
N_KV_HEADS = 2
HEAD_DIM = 64
ROPE_THETA = 10000.0
CONV_DIM = 512
FFN_DIM = 2816
N_MOD = 9
EPS = 1e-6
RET_W = N_RET_HEADS * RET_DK
ATT_Q_W = N_HEADS * HEAD_DIM
ATT_KV_W = N_KV_HEADS * HEAD_DIM
GROUP = N_HEADS // N_KV_HEADS
KV_LANES = GROUP * HEAD_DIM
OFF_RQ, OFF_RK, OFF_RV, OFF_RG = 0, 512, 1024, 1536
OFF_AQ, OFF_AK, OFF_AV = 2048, 2560, 2688
OFF_CB, OFF_CC, OFF_CX = 2816, 3328, 3840
OFF_GATES = 4352
IN_COLS = 7424

LANES = 128
BF16_SUBLANES = 16
VMEM_LIMIT = 56 * 1024 * 1024

F32 = jnp.float32
BF16 = jnp.bfloat16


def _dot(a, b):
    return jnp.dot(a, b, preferred_element_type=F32)


def _dot_nt(a, b):
    return lax.dot_general(a, b, (((1,), (1,)), ((), ())), preferred_element_type=F32)


def _sigmoid(x):
    return 1.0 / (1.0 + jnp.exp(-x))


def _silu(x):
    return x * _sigmoid(x)


def _mod_norm(x, mp):
    ms = jnp.mean(x * x, axis=-1, keepdims=True)
    y = x * lax.rsqrt(ms + EPS)
    return (y * mp[3:4]) * (1.0 + mp[1:2]) + mp[0:1]


def _resident(shape, index):
    return pl.BlockSpec(shape, index, pipeline_mode=pl.Buffered(1))


ADA_TN = 1152


def _ada_kernel(c_ref, w_ref, b_ref, o_ref):
    a = _silu(c_ref[...]).astype(BF16)
    o_ref[...] = _dot(a, w_ref[...].astype(BF16)) + b_ref[...]


def _ada(cond8, w_ada, b_ada):
    n = N_MOD * D_MODEL
    return pl.pallas_call(
        _ada_kernel,
        grid=(DEPTH, n // ADA_TN),
        in_specs=[
            pl.BlockSpec((8, D_MODEL), lambda l, j: (0, 0)),
            pl.BlockSpec((None, D_MODEL, ADA_TN), lambda l, j: (l, 0, j)),
            pl.BlockSpec((None, 1, ADA_TN), lambda l, j: (l, 0, j)),
        ],
        out_specs=pl.BlockSpec((None, 8, ADA_TN), lambda l, j: (l, 0, j)),
        out_shape=jax.ShapeDtypeStruct((DEPTH, 8, n), F32),
        compiler_params=pltpu.CompilerParams(
            dimension_semantics=("arbitrary", "arbitrary"), vmem_limit_bytes=VMEM_LIMIT),
        name="ada",
    )(cond8, w_ada, b_ada.reshape(DEPTH, 1, n))


FFN_TM = 1024
FFN_FC = 256


def _ffn_kernel(x_ref, mp_ref, wi_ref, wo_ref, o_ref, h_ref):
    x = x_ref[...]
    mp = mp_ref[...]
    xb = _mod_norm(x, mp).astype(BF16)
    for c in range(FFN_DIM // FFN_FC):
        lo = c * FFN_FC
        g = _dot(xb, wi_ref[:, lo:lo + FFN_FC])
        u = _dot(xb, wi_ref[:, FFN_DIM + lo:FFN_DIM + lo + FFN_FC])
        h_ref[:, lo:lo + FFN_FC] = (_silu(g) * u).astype(BF16)
    y = _dot(h_ref[...], wo_ref[...])
    o_ref[...] = x + (0.5 * mp[2:3]) * y


def _ffn(x, mp, rows_per_mod, w_in, w_out, l, j):
    n = x.shape[0]
    tm = FFN_TM
    return pl.pallas_call(
        _ffn_kernel,
        grid=(n // tm,),
        in_specs=[
            pl.BlockSpec((tm, D_MODEL), lambda i: (i, 0)),
            pl.BlockSpec((None, 8, D_MODEL), lambda i: ((i * tm) // rows_per_mod, 0, 0)),
            _resident((None, None, D_MODEL, 2 * FFN_DIM), lambda i: (l, j, 0, 0)),
            _resident((None, None, FFN_DIM, D_MODEL), lambda i: (l, j, 0, 0)),
        ],
        out_specs=pl.BlockSpec((tm, D_MODEL), lambda i: (i, 0)),
        out_shape=jax.ShapeDtypeStruct((n, D_MODEL), F32),
        scratch_shapes=[pltpu.VMEM((tm, FFN_DIM), BF16)],
        compiler_params=pltpu.CompilerParams(
            dimension_semantics=("arbitrary",), vmem_limit_bytes=VMEM_LIMIT),
        name="ffn",
    )(x, mp, w_in, w_out)


INPROJ_TM = 512


def _group_rms(a, bd, gain):
    sq = a * a
    hi = sq.astype(BF16)
    lo = (sq - hi.astype(F32)).astype(BF16)
    ss = _dot(hi, bd) + _dot(lo, bd)
    return (a * lax.rsqrt(ss * (1.0 / HEAD_DIM) + EPS)) * gain


def _rope(x, cos, sin_signed):
    half = HEAD_DIM // 2
    reps = x.shape[1] // LANES
    lane = lax.broadcasted_iota(jnp.int32, (x.shape[0], LANES), 1)
    first = (lane % HEAD_DIM) < half
    out = []
    for r in range(reps):
        xs = x[:, r * LANES:(r + 1) * LANES]
        partner = jnp.where(first, pltpu.roll(xs, LANES - half, axis=1), pltpu.roll(xs, half, axis=1))
        out.append(xs * cos + partner * sin_signed)
    return out[0] if reps == 1 else jnp.concatenate(out, axis=1)


def _tile_kv(a):
    lane = lax.broadcasted_iota(jnp.int32, a.shape, 1)
    sw = pltpu.roll(a, HEAD_DIM, axis=1)
    h0 = jnp.where(lane < HEAD_DIM, a, sw)
    h1 = jnp.where(lane < HEAD_DIM, sw, a)
    return jnp.concatenate([h0, h0, h1, h1], axis=1)


def _inproj_kernel(*refs, use_rope, emit_kv):
    x_ref, mp_ref, w_ref, qg_ref, kg_ref, bd_ref = refs[:6]
    pos = 6
    if use_rope:
        cos_ref, sin_ref = refs[pos:pos + 2]
        pos += 2
    ret_ref, q_ref, kt_ref, vt_ref, cb_ref, u_ref, gates_ref = refs[pos:pos + 7]
    pos += 7
    if emit_kv:
        kout_ref, vout_ref = refs[pos:pos + 2]

    xb = _mod_norm(x_ref[...], mp_ref[...]).astype(BF16)

    def proj(off, width):
        return _dot(xb, w_ref[:, off:off + width])

    ret_ref[:, OFF_RQ:OFF_RQ + RET_W] = proj(OFF_RQ, RET_W).astype(BF16)
    ret_ref[:, OFF_RK:OFF_RK + RET_W] = (proj(OFF_RK, RET_W) * (RET_DK ** -0.5)).astype(BF16)
    ret_ref[:, OFF_RV:OFF_RV + RET_W] = proj(OFF_RV, RET_W).astype(BF16)
    ret_ref[:, OFF_RG:OFF_RG + RET_W] = _silu(proj(OFF_RG, RET_W)).astype(BF16)

    q = _group_rms(proj(OFF_AQ, ATT_Q_W), bd_ref[...], qg_ref[...])
    if use_rope:
        q = _rope(q, cos_ref[...], sin_ref[...])
    q_ref[...] = (q * (HEAD_DIM ** -0.5)).astype(BF16)

    k = _group_rms(proj(OFF_AK, ATT_KV_W), bd_ref[0:ATT_KV_W, 0:ATT_KV_W], kg_ref[...])
    v = proj(OFF_AV, ATT_KV_W)
    if emit_kv:
        kout_ref[...] = k
        vout_ref[...] = v
    if use_rope:
        k = _rope(k, cos_ref[...], sin_ref[...])
    kt_ref[...] = _tile_kv(k).astype(BF16)
    vt_ref[...] = _tile_kv(v).astype(BF16)

    cb_ref[...] = proj(OFF_CB, CONV_DIM).astype(BF16)
    u_ref[...] = (proj(OFF_CC, CONV_DIM) * proj(OFF_CX, CONV_DIM)).astype(BF16)

    for c in range(3 * D_MODEL // 512):
        gates_ref[:, c * 512:(c + 1) * 512] = _sigmoid(proj(OFF_GATES + c * 512, 512)).astype(BF16)


def _inproj(x, mp, rows_per_mod, w_in, l, q_gain_t, k_gain_t, bd, rope, seq_len, emit_kv):
    n = x.shape[0]
    tm = INPROJ_TM
    use_rope = rope is not None
    row = lambda i: (i, 0)
    in_specs = [
        pl.BlockSpec((tm, D_MODEL), row),
        pl.BlockSpec((None, 8, D_MODEL), lambda i: ((i * tm) // rows_per_mod, 0, 0)),
        _resident((None, D_MODEL, IN_COLS), lambda i: (l, 0, 0)),
        pl.BlockSpec((None, 1, ATT_Q_W), lambda i: (l, 0, 0)),
        pl.BlockSpec((None, 1, ATT_KV_W), lambda i: (l, 0, 0)),
        _resident((ATT_Q_W, ATT_Q_W), lambda i: (0, 0)),
    ]
    args = [x, mp, w_in, q_gain_t, k_gain_t, bd]
    if use_rope:
        tiles_per_seq = seq_len // tm
        in_specs += [pl.BlockSpec((tm, LANES), lambda i: (i % tiles_per_seq, 0))] * 2
        args += list(rope)
    widths = [4 * RET_W, ATT_Q_W, ATT_Q_W, ATT_Q_W, CONV_DIM, CONV_DIM, 3 * D_MODEL]
    out_specs = [pl.BlockSpec((tm, w), row) for w in widths]
    out_shape = [jax.ShapeDtypeStruct((n, w), BF16) for w in widths]
    if emit_kv:
        out_specs += [pl.BlockSpec((tm, ATT_KV_W), row)] * 2
        out_shape += [jax.ShapeDtypeStruct((n, ATT_KV_W), F32)] * 2
    return pl.pallas_call(
        functools.partial(_inproj_kernel, use_rope=use_rope, emit_kv=emit_kv),
        grid=(n // tm,),
        in_specs=in_specs,
        out_specs=out_specs,
        out_shape=out_shape,
        compiler_params=pltpu.CompilerParams(
            dimension_semantics=("arbitrary",), vmem_limit_bytes=VMEM_LIMIT),
        name="inproj",
    )(*args)


def _log_sigmoid(x):
    return jnp.minimum(x, 0.0) - jnp.log(1.0 + jnp.exp(-jnp.abs(x)))


def _retention_kernel(*refs, n_seq, n_chunks, has_s0, emit_state):
    dl_ref, r_ref, gn_ref = refs[:3]
    pos = 3
    if has_s0:
        s0_ref = refs[pos]
        pos += 1
    o_ref = refs[pos]
    pos += 1
    if emit_state:
        st_ref = refs[pos]
        pos += 1
    kv_ref, sf_ref, rb_ref = refs[pos:pos + 3]

    C = RET_CHUNK
    row = lax.broadcasted_iota(jnp.int32, (C, C), 0).astype(F32)
    col = lax.broadcasted_iota(jnp.int32, (C, C), 1).astype(F32)
    diff = row - col

    for h in range(N_RET_HEADS):
        lgf = _log_sigmoid(jnp.full((C, C), dl_ref[0, h], F32))
        lgb = _log_sigmoid(jnp.full((C, C), dl_ref[1, h], F32))
        dmask = (jnp.where(diff >= 0, jnp.exp(lgf * jnp.maximum(diff, 0.0)), 0.0)
                 + jnp.where(diff <= 0, jnp.exp(lgb * jnp.maximum(-diff, 0.0)), 0.0))
        dq_f = jnp.exp(lgf * (row + 1.0))
        dq_b = jnp.exp(lgb * (C - row))
        dk_f = jnp.exp(lgf * (C - 1.0 - row))
        dk_b = jnp.exp(lgb * row)
        dc_f = jnp.exp(lgf * C)
        dc_b = jnp.exp(lgb * C)
        cq = slice(OFF_RQ + h * RET_DK, OFF_RQ + (h + 1) * RET_DK)
        ck = slice(OFF_RK + h * RET_DK, OFF_RK + (h + 1) * RET_DK)
        cv = slice(OFF_RV + h * RET_DV, OFF_RV + (h + 1) * RET_DV)
        cg = slice(OFF_RG + h * RET_DV, OFF_RG + (h + 1) * RET_DV)
        co = slice(h * RET_DV, (h + 1) * RET_DV)
        gn = gn_ref[:, co]

        for s in range(n_seq):
            base = s * n_chunks * C

            def rows(c):
                return pl.ds(pl.multiple_of(base + c * C, C), C)

            def kv_body(c, carry):
                k = r_ref[rows(c), ck].astype(F32)
                v = r_ref[rows(c), cv]
                kk = jnp.concatenate([(k * dk_f).T.astype(BF16), (k * dk_b).T.astype(BF16)], axis=0)
                kv_ref[c] = _dot(kk, v)
                return carry

            lax.fori_loop(0, n_chunks, kv_body, 0)

            def fwd_body(c, st):
                sf_ref[c] = st.astype(BF16)
                return st * dc_f + kv_ref[c, 0:RET_DK, :]

            def bwd_body(t, st):
                c = n_chunks - 1 - t
                rb_ref[c] = st.astype(BF16)
                return st * dc_b + kv_ref[c, RET_DK:2 * RET_DK, :]

            if has_s0:
                st_f0 = s0_ref[0, h]
                st_b0 = s0_ref[1, h]
            else:
                st_f0 = jnp.zeros((RET_DK, RET_DV), F32)
                st_b0 = jnp.zeros((RET_DK, RET_DV), F32)
            fin_f = lax.fori_loop(0, n_chunks, fwd_body, st_f0)
            fin_b = lax.fori_loop(0, n_chunks, bwd_body, st_b0)
            if emit_state:
                st_ref[s, 0, h] = fin_f
                st_ref[s, 1, h] = fin_b

            def out_body(c, carry):
                q = r_ref[rows(c), cq]
                k = r_ref[rows(c), ck]
                v = r_ref[rows(c), cv]
                att = (_dot_nt(q, k) * dmask).astype(BF16)
                qf = q.astype(F32)
                lhs = jnp.concatenate([att, (qf * dq_f).astype(BF16), (qf * dq_b).astype(BF16)], axis=1)
                rhs = jnp.concatenate([v, sf_ref[c], rb_ref[c]], axis=0)
                o = _dot(lhs, rhs)
                ms = jnp.mean(o * o, axis=-1, keepdims=True)
                on = (o * lax.rsqrt(ms + EPS)) * gn
                o_ref[rows(c), co] = (r_ref[rows(c), cg].astype(F32) * on).astype(BF16)
                return carry

            lax.fori_loop(0, n_chunks, out_body, 0)


def _retention(ret_in, decay_logit_l, gn_l, s0, seq_len, n_seq_blk, emit_state):
    n = ret_in.shape[0]
    n_chunks = seq_len // RET_CHUNK
    tb = n_seq_blk * seq_len
    has_s0 = s0 is not None
    in_specs = [
        pl.BlockSpec(memory_space=pltpu.SMEM),
        pl.BlockSpec((tb, 4 * RET_W), lambda i: (i, 0)),
        pl.BlockSpec((1, RET_W), lambda i: (0, 0)),
    ]
    args = [decay_logit_l, ret_in, gn_l]
    if has_s0:
        assert n_seq_blk == 1
        in_specs.append(pl.BlockSpec((None, 2, N_RET_HEADS, RET_DK, RET_DV), lambda i: (i, 0, 0, 0, 0)))
        args.append(s0)
    out_specs = [pl.BlockSpec((tb, RET_W), lambda i: (i, 0))]
    out_shape = [jax.ShapeDtypeStruct((n, RET_W), BF16)]
    if emit_state:
        out_specs.append(
            pl.BlockSpec((n_seq_blk, 2, N_RET_HEADS, RET_DK, RET_DV), lambda i: (i, 0, 0, 0, 0)))
        out_shape.append(jax.ShapeDtypeStruct((n // seq_len, 2, N_RET_HEADS, RET_DK, RET_DV), F32))
    res = pl.pallas_call(
        functools.partial(_retention_kernel, n_seq=n_seq_blk, n_chunks=n_chunks, has_s0=has_s0,
                          emit_state=emit_state),
        grid=(n // tb,),
        in_specs=in_specs,
        out_specs=out_specs,
        out_shape=out_shape,
        scratch_shapes=[
            pltpu.VMEM((n_chunks, 2 * RET_DK, RET_DV), F32),
            pltpu.VMEM((n_chunks, RET_DK, RET_DV), BF16),
            pltpu.VMEM((n_chunks, RET_DK, RET_DV), BF16),
        ],
        compiler_params=pltpu.CompilerParams(
            dimension_semantics=("arbitrary",), vmem_limit_bytes=VMEM_LIMIT),
        name="retention",
    )(*args)
    return res


ATT_TQ = 256
ATT_KB = 512
NEG_BIG = -1e30


def _attention_kernel(*refs, n_cache, n_new):
    q_ref = refs[0]
    pos = 1
    if n_cache:
        kc_ref, vc_ref = refs[pos:pos + 2]
        pos += 2
    kn_ref, vn_ref, o_ref = refs[pos:pos + 3]

    tq = q_ref.shape[0]
    qf = q_ref[...].astype(F32)
    head = lax.broadcasted_iota(jnp.int32, (tq, KV_LANES), 1) // HEAD_DIM
    qs = jnp.concatenate([jnp.where(head == g, qf, 0.0) for g in range(GROUP)], axis=0).astype(BF16)

    m = jnp.full((GROUP * tq, 1), NEG_BIG, F32)
    l = jnp.zeros((GROUP * tq, 1), F32)
    acc = jnp.zeros((GROUP * tq, KV_LANES), F32)

    blocks = []
    if n_cache:
        kb = min(ATT_KB, n_cache)
        blocks += [(kc_ref, vc_ref, j * kb, kb) for j in range(n_cache // kb)]
    kb = min(ATT_KB, n_new)
    blocks += [(kn_ref, vn_ref, j * kb, kb) for j in range(n_new // kb)]
    for k_ref, v_ref, lo, kb in blocks:
        s = _dot_nt(qs, k_ref[lo:lo + kb, :])
        m_new = jnp.maximum(m, jnp.max(s, axis=-1, keepdims=True))
        alpha = jnp.exp(m - m_new)
        p = jnp.exp(s - m_new)
        l = alpha * l + jnp.sum(p, axis=-1, keepdims=True)
        acc = alpha * acc + _dot(p.astype(BF16), v_ref[lo:lo + kb, :])
        m = m_new

    acc = acc / l
    out = jnp.zeros((tq, KV_LANES), F32)
    for g in range(GROUP):
        out = out + jnp.where(head == g, acc[g * tq:(g + 1) * tq], 0.0)
    o_ref[...] = out.astype(BF16)


def _attention(q, kt, vt, cache, seq_len):
    n = q.shape[0]
    n_batch = n // seq_len
    tq = ATT_TQ
    nq = seq_len // tq
    n_cache = 0 if cache is None else cache[0].shape[1]
    in_specs = [pl.BlockSpec((tq, KV_LANES), lambda b, kv, i: (b * nq + i, kv))]
    args = [q]
    if n_cache:
        in_specs += [pl.BlockSpec((None, n_cache, KV_LANES), lambda b, kv, i: (b, 0, kv))] * 2
        args += list(cache)
    in_specs += [pl.BlockSpec((seq_len, KV_LANES), lambda b, kv, i: (b, kv))] * 2
    args += [kt, vt]
    return pl.pallas_call(
        functools.partial(_attention_kernel, n_cache=n_cache, n_new=seq_len),
        grid=(n_batch, N_KV_HEADS, nq),
        in_specs=in_specs,
        out_specs=pl.BlockSpec((tq, KV_LANES), lambda b, kv, i: (b * nq + i, kv)),
        out_shape=jax.ShapeDtypeStruct((n, ATT_Q_W), BF16),
        compiler_params=pltpu.CompilerParams(
            dimension_semantics=("arbitrary", "arbitrary", "arbitrary"), vmem_limit_bytes=VMEM_LIMIT),
        name="attention",
    )(*args)


MERGE_TM = 512


def _merge_kernel(h_ref, mp_ref, ro_ref, ao_ref, cb_ref, u_ref, up_ref, un_ref, g_ref, cw_ref,
                  wr_ref, wa_ref, wc_ref, wo_ref, o_ref, *, seq_len):
    tm = h_ref.shape[0]
    i = pl.program_id(0)
    u = u_ref[...].astype(F32)
    r = lax.broadcasted_iota(jnp.int32, u.shape, 0)
    t = (i * tm + r) % seq_len
    prev_row = up_ref[BF16_SUBLANES - 1:BF16_SUBLANES, :].astype(F32)
    next_row = un_ref[0:1, :].astype(F32)
    u_prev = jnp.where(r == 0, prev_row, pltpu.roll(u, 1, axis=0))
    u_next = jnp.where(r == tm - 1, next_row, pltpu.roll(u, tm - 1, axis=0))
    u_prev = jnp.where(t == 0, 0.0, u_prev)
    u_next = jnp.where(t == seq_len - 1, 0.0, u_next)
    cw = cw_ref[...]
    conv = u_prev * cw[0:1] + u * cw[1:2] + u_next * cw[2:3]
    yc = _dot((cb_ref[...].astype(F32) * conv).astype(BF16), wc_ref[...])
    yr = _dot(ro_ref[...], wr_ref[...])
    ya = _dot(ao_ref[...], wa_ref[...])
    merged = (g_ref[:, 0:D_MODEL].astype(F32) * yr
              + g_ref[:, D_MODEL:2 * D_MODEL].astype(F32) * ya
              + g_ref[:, 2 * D_MODEL:3 * D_MODEL].astype(F32) * yc)
    y = _dot(merged.astype(BF16), wo_ref[...])
    o_ref[...] = h_ref[...] + mp_ref[2:3, :] * y


def _merge(h, mp, rows_per_mod, ret_o, att_o, cb, u, gates, conv_w8, w_ret_o, w_att_o, w_conv_o, w_o,
           l, seq_len):
    n = h.shape[0]
    tm = MERGE_TM
    hb = tm // BF16_SUBLANES
    last = n // BF16_SUBLANES - 1
    row = lambda i: (i, 0)
    wspec = lambda k: _resident((None, k, D_MODEL), lambda i: (l, 0, 0))
    return pl.pallas_call(
        functools.partial(_merge_kernel, seq_len=seq_len),
        grid=(n // tm,),
        in_specs=[
            pl.BlockSpec((tm, D_MODEL), row),
            pl.BlockSpec((None, 8, D_MODEL), lambda i: ((i * tm) // rows_per_mod, 0, 0)),
            pl.BlockSpec((tm, RET_W), row),
            pl.BlockSpec((tm, ATT_Q_W), row),
            pl.BlockSpec((tm, CONV_DIM), row),
            pl.BlockSpec((tm, CONV_DIM), row),
            pl.BlockSpec((BF16_SUBLANES, CONV_DIM), lambda i: (jnp.maximum(i * hb - 1, 0), 0)),
            pl.BlockSpec((BF16_SUBLANES, CONV_DIM), lambda i: (jnp.minimum((i + 1) * hb, last), 0)),
            pl.BlockSpec((tm, 3 * D_MODEL), row),
            pl.BlockSpec((None, 8, CONV_DIM), lambda i: (l, 0, 0)),
            wspec(RET_W), wspec(ATT_Q_W), wspec(CONV_DIM), wspec(D_MODEL),
        ],
        out_specs=pl.BlockSpec((tm, D_MODEL), row),
        out_shape=jax.ShapeDtypeStruct((n, D_MODEL), F32),
        compiler_params=pltpu.CompilerParams(
            dimension_semantics=("arbitrary",), vmem_limit_bytes=VMEM_LIMIT),
        name="merge",
    )(h, mp, ret_o, att_o, cb, u, u, u, gates, conv_w8, w_ret_o, w_att_o, w_conv_o, w_o)


def _rope_tables(n_tok):
    rows = n_tok // GRID_W
    t_row = np.repeat(np.arange(rows, dtype=np.float64), GRID_W)
    t_col = np.tile(np.arange(GRID_W, dtype=np.float64), rows)
    n_freq = HEAD_DIM // 4
    inv = ROPE_THETA ** (-np.arange(n_freq, dtype=np.float64) / n_freq)
    ang = np.concatenate([t_row[:, None] * inv, t_col[:, None] * inv], axis=-1)
    cos, sin = np.cos(ang), np.sin(ang)
    cos64 = np.concatenate([cos, cos], axis=-1)
    sin64 = np.concatenate([-sin, sin], axis=-1)
    return (jnp.asarray(np.tile(cos64, (1, LANES // HEAD_DIM)), F32),
            jnp.asarray(np.tile(sin64, (1, LANES // HEAD_DIM)), F32))


def _mod_pack(mod_l, rows, sub, norm_w_row):
    r0 = rows[0]
    nr = len(rows)
    m = mod_l[r0:r0 + nr, 3 * sub * D_MODEL:3 * (sub + 1) * D_MODEL].reshape(nr, 3, D_MODEL)
    nw = jnp.broadcast_to(norm_w_row[None, None, :], (nr, 1, D_MODEL))
    pad = jnp.zeros((nr, 4, D_MODEL), F32)
    return jnp.concatenate([m, nw, pad], axis=1)


def kernel(x_prompt, x_sample, c, state_ret, cache_k, cache_v, c_ctx, w_ada, b_ada, norm_w, w_ffn_in,
           w_ffn_out, w_in, ret_decay_logit, ret_gn, q_gain, k_gain, conv_w, w_ret_o, w_att_o, w_conv_o,
           w_o):
    n_ctx_b, ctx_len, _ = x_prompt.shape
    n_lat_b, lat_len, _ = x_sample.shape

    w_ffn_in_b = w_ffn_in.astype(BF16)
    w_ffn_out_b = w_ffn_out.astype(BF16)
    w_in_b = w_in.astype(BF16)
    w_ret_o_b = w_ret_o.astype(BF16)
    w_att_o_b = w_att_o.astype(BF16)
    w_conv_o_b = w_conv_o.astype(BF16)
    w_o_b = w_o.astype(BF16)

    q_gain_t = jnp.tile(q_gain, (1, N_HEADS)).reshape(DEPTH, 1, ATT_Q_W)
    k_gain_t = jnp.tile(k_gain, (1, N_KV_HEADS)).reshape(DEPTH, 1, ATT_KV_W)
    conv_w8 = jnp.pad(conv_w, ((0, 0), (0, 8 - conv_w.shape[1]), (0, 0)))
    gid = np.arange(ATT_Q_W) // HEAD_DIM
    bd = jnp.asarray(gid[:, None] == gid[None, :], BF16)
    rope = _rope_tables(lat_len)

    cond8 = jnp.zeros((8, D_MODEL), F32).at[0].set(c_ctx).at[1:1 + n_lat_b].set(c)
    mod = _ada(cond8, w_ada, b_ada)

    def tile_cache(a):
        a = a.astype(BF16)
        a = jnp.broadcast_to(a[:, :, :, :, None, :], a.shape[:4] + (GROUP, HEAD_DIM))
        return a.reshape(a.shape[0], a.shape[1], a.shape[2], N_KV_HEADS * KV_LANES)

    cache_kt = tile_cache(cache_k)
    cache_vt = tile_cache(cache_v)

    groups = (
        dict(x=x_prompt.reshape(n_ctx_b * ctx_len, D_MODEL), rows=[0], seq=ctx_len, ctx=True),
        dict(x=x_sample.reshape(n_lat_b * lat_len, D_MODEL), rows=list(range(1, 1 + n_lat_b)),
             seq=lat_len, ctx=False),
    )
    results = []
    for grp in groups:
        h = grp["x"]
        seq = grp["seq"]
        is_ctx = grp["ctx"]
        rpm = h.shape[0] // len(grp["rows"])
        states, keys, values = [], [], []
        for l in range(DEPTH):
            mp = [_mod_pack(mod[l], grp["rows"], s, norm_w[l, s]) for s in range(3)]
            h = _ffn(h, mp[0], rpm, w_ffn_in_b, w_ffn_out_b, l, 0)
            outs = _inproj(h, mp[1], rpm, w_in_b, l, q_gain_t, k_gain_t, bd,
                           None if is_ctx else rope, seq, emit_kv=is_ctx)
            ret_in, qn, kt, vt, cb, u, gates = outs[:7]
            if is_ctx:
                ret_o, st = _retention(ret_in, ret_decay_logit[l], ret_gn[l][None, :], None, seq,
                                       n_seq_blk=4, emit_state=True)
                states.append(st)
                keys.append(outs[7].reshape(n_ctx_b, seq, N_KV_HEADS, HEAD_DIM))
                values.append(outs[8].reshape(n_ctx_b, seq, N_KV_HEADS, HEAD_DIM))
                att_o = _attention(qn, kt, vt, None, seq)
            else:
                (ret_o,) = _retention(ret_in, ret_decay_logit[l], ret_gn[l][None, :], state_ret[:, l],
                                      seq, n_seq_blk=1, emit_state=False)
                att_o = _attention(qn, kt, vt, (cache_kt[:, l], cache_vt[:, l]), seq)
            h = _merge(h, mp[1], rpm, ret_o, att_o, cb, u, gates, conv_w8, w_ret_o_b, w_att_o_b,
                       w_conv_o_b, w_o_b, l, seq)
            h = _ffn(h, mp[2], rpm, w_ffn_in_b, w_ffn_out_b, l, 1)
        results.append((h, states, keys, values))

    (y_ctx, states, keys, values), (y_lat, _, _, _) = results
    y_prompt = y_ctx.reshape(x_prompt.shape)
    y_sample = y_lat.reshape(x_sample.shape)
    new_state_ret = jnp.stack(states, axis=1)
    new_cache_k = jnp.stack(keys, axis=1)
    new_cache_v = jnp.stack(values, axis=1)
    return (y_prompt, y_sample, new_state_ret, new_cache_k, new_cache_v)
```

```python
import functools
import math

import numpy as np
import jax
import jax.numpy as jnp
from jax import lax
from jax.experimental import pallas as pl
from jax.experimental.pallas import tpu as pltpu

D_MODEL = 1024
DEPTH = 2
GRID_W = 64
N_RET_HEADS = 4
RET_DK = 128
RET_DV = 128
RET_CHUNK = 128
N_HEADS = 8
N_KV_HEADS = 2
HEAD_DIM = 64
ROPE_THETA = 10000.0
CONV_DIM = 512
FFN_DIM = 2816
N_MOD = 9
EPS = 1e-6
RET_W = N_RET_HEADS * RET_DK
ATT_Q_W = N_HEADS * HEAD_DIM
ATT_KV_W = N_KV_HEADS * HEAD_DIM
GROUP = N_HEADS // N_KV_HEADS
KV_LANES = GROUP * HEAD_DIM
OFF_RQ, OFF_RK, OFF_RV, OFF_RG = 0, 512, 1024, 1536
OFF_AQ, OFF_AK, OFF_AV = 2048, 2560, 2688
OFF_CB, OFF_CC, OFF_CX = 2816, 3328, 3840
OFF_GATES = 4352
IN_COLS = 7424

LANES = 128
BF16_SUBLANES = 16
VMEM_LIMIT = 56 * 1024 * 1024

F32 = jnp.float32
BF16 = jnp.bfloat16


def _dot(a, b):
    return jnp.dot(a, b, preferred_element_type=F32)


def _dot_nt(a, b):
    return lax.dot_general(a, b, (((1,), (1,)), ((), ())), preferred_element_type=F32)


def _sigmoid(x):
    return 1.0 / (1.0 + jnp.exp(-x))


def _silu(x):
    return x * _sigmoid(x)


def _mod_norm(x, mp):
    ms = jnp.mean(x * x, axis=-1, keepdims=True)
    y = x * lax.rsqrt(ms + EPS)
    return (y * mp[3:4]) * (1.0 + mp[1:2]) + mp[0:1]


def _resident(shape, index):
    return pl.BlockSpec(shape, index, pipeline_mode=pl.Buffered(1))


ADA_TN = 1152


def _ada_kernel(c_ref, w_ref, b_ref, o_ref):
    a = _silu(c_ref[...]).astype(BF16)
    o_ref[...] = _dot(a, w_ref[...].astype(BF16)) + b_ref[...]


def _ada(cond8, w_ada, b_ada):
    n = N_MOD * D_MODEL
    return pl.pallas_call(
        _ada_kernel,
        grid=(DEPTH, n // ADA_TN),
        in_specs=[
            pl.BlockSpec((8, D_MODEL), lambda l, j: (0, 0)),
            pl.BlockSpec((None, D_MODEL, ADA_TN), lambda l, j: (l, 0, j)),
            pl.BlockSpec((None, 1, ADA_TN), lambda l, j: (l, 0, j)),
        ],
        out_specs=pl.BlockSpec((None, 8, ADA_TN), lambda l, j: (l, 0, j)),
        out_shape=jax.ShapeDtypeStruct((DEPTH, 8, n), F32),
        compiler_params=pltpu.CompilerParams(
            dimension_semantics=("arbitrary", "arbitrary"), vmem_limit_bytes=VMEM_LIMIT),
        name="ada",
    )(cond8, w_ada, b_ada.reshape(DEPTH, 1, n))


FFN_TM = 1024
FFN_FC = 256


def _ffn_kernel(x_ref, mp_ref, wi_ref, wo_ref, o_ref, h_ref):
    x = x_ref[...]
    mp = mp_ref[...]
    xb = _mod_norm(x, mp).astype(BF16)
    for c in range(FFN_DIM // FFN_FC):
        lo = c * FFN_FC
        g = _dot(xb, wi_ref[:, lo:lo + FFN_FC])
        u = _dot(xb, wi_ref[:, FFN_DIM + lo:FFN_DIM + lo + FFN_FC])
        h_ref[:, lo:lo + FFN_FC] = (_silu(g) * u).astype(BF16)
    y = _dot(h_ref[...], wo_ref[...])
    o_ref[...] = x + (0.5 * mp[2:3]) * y


def _ffn(x, mp, rows_per_mod, w_in, w_out, l, j):
    n = x.shape[0]
    tm = FFN_TM
    return pl.pallas_call(
        _ffn_kernel,
        grid=(n // tm,),
        in_specs=[
            pl.BlockSpec((tm, D_MODEL), lambda i: (i, 0)),
            pl.BlockSpec((None, 8, D_MODEL), lambda i: ((i * tm) // rows_per_mod, 0, 0)),
            _resident((None, None, D_MODEL, 2 * FFN_DIM), lambda i: (l, j, 0, 0)),
            _resident((None, None, FFN_DIM, D_MODEL), lambda i: (l, j, 0, 0)),
        ],
        out_specs=pl.BlockSpec((tm, D_MODEL), lambda i: (i, 0)),
        out_shape=jax.ShapeDtypeStruct((n, D_MODEL), F32),
        scratch_shapes=[pltpu.VMEM((tm, FFN_DIM), BF16)],
        compiler_params=pltpu.CompilerParams(
            dimension_semantics=("arbitrary",), vmem_limit_bytes=VMEM_LIMIT),
        name="ffn",
    )(x, mp, w_in, w_out)


INPROJ_TM = 512
Q_SCALE = HEAD_DIM ** -0.5 * math.log2(math.e)


def _group_rms(a, bd, gain):
    sq = a * a
    hi = sq.astype(BF16)
    lo = (sq - hi.astype(F32)).astype(BF16)
    ss = _dot(hi, bd) + _dot(lo, bd)
    return (a * lax.rsqrt(ss * (1.0 / HEAD_DIM) + EPS)) * gain


def _rope(x, cos, sin_signed):
    half = HEAD_DIM // 2
    reps = x.shape[1] // LANES
    lane = lax.broadcasted_iota(jnp.int32, (x.shape[0], LANES), 1)
    first = (lane % HEAD_DIM) < half
    out = []
    for r in range(reps):
        xs = x[:, r * LANES:(r + 1) * LANES]
        partner = jnp.where(first, pltpu.roll(xs, LANES - half, axis=1), pltpu.roll(xs, half, axis=1))
        out.append(xs * cos + partner * sin_signed)
    return out[0] if reps == 1 else jnp.concatenate(out, axis=1)


def _tile_kv(a):
    lane = lax.broadcasted_iota(jnp.int32, a.shape, 1)
    sw = pltpu.roll(a, HEAD_DIM, axis=1)
    h0 = jnp.where(lane < HEAD_DIM, a, sw)
    h1 = jnp.where(lane < HEAD_DIM, sw, a)
    return jnp.concatenate([h0, h0, h1, h1], axis=1)


def _inproj_kernel(*refs, use_rope, emit_kv):
    x_ref, mp_ref, w_ref, qg_ref, kg_ref, bd_ref = refs[:6]
    pos = 6
    if use_rope:
        cos_ref, sin_ref = refs[pos:pos + 2]
        pos += 2
    ret_ref, q_ref, kt_ref, vT_ref, cb_ref, u_ref, gates_ref = refs[pos:pos + 7]
    pos += 7
    if emit_kv:
        kout_ref, vout_ref = refs[pos:pos + 2]

    xb = _mod_norm(x_ref[...], mp_ref[...]).astype(BF16)

    def proj(off, width):
        return _dot(xb, w_ref[:, off:off + width])

    ret_ref[:, OFF_RQ:OFF_RQ + RET_W] = proj(OFF_RQ, RET_W).astype(BF16)
    ret_ref[:, OFF_RK:OFF_RK + RET_W] = (proj(OFF_RK, RET_W) * (RET_DK ** -0.5)).astype(BF16)
    ret_ref[:, OFF_RV:OFF_RV + RET_W] = proj(OFF_RV, RET_W).astype(BF16)
    ret_ref[:, OFF_RG:OFF_RG + RET_W] = _silu(proj(OFF_RG, RET_W)).astype(BF16)

    q = _group_rms(proj(OFF_AQ, ATT_Q_W), bd_ref[...], qg_ref[...])
    if use_rope:
        q = _rope(q, cos_ref[...], sin_ref[...])
    q_ref[...] = (q * Q_SCALE).astype(BF16)

    k = _group_rms(proj(OFF_AK, ATT_KV_W), bd_ref[0:ATT_KV_W, 0:ATT_KV_W], kg_ref[...])
    v = proj(OFF_AV, ATT_KV_W)
    if emit_kv:
        kout_ref[...] = k
        vout_ref[...] = v
    if use_rope:
        k = _rope(k, cos_ref[...], sin_ref[...])
    kt_ref[...] = _tile_kv(k).astype(BF16)
    vT_ref[...] = v.T.astype(BF16)

    cb_ref[...] = proj(OFF_CB, CONV_DIM).astype(BF16)
    u_ref[...] = (proj(OFF_CC, CONV_DIM) * proj(OFF_CX, CONV_DIM)).astype(BF16)

    for c in range(3 * D_MODEL // 512):
        gates_ref[:, c * 512:(c + 1) * 512] = _sigmoid(proj(OFF_GATES + c * 512, 512)).astype(BF16)


def _inproj(x, mp, rows_per_mod, w_in, l, q_gain_t, k_gain_t, bd, rope, seq_len, emit_kv):
    n = x.shape[0]
    tm = INPROJ_TM
    use_rope = rope is not None
    row = lambda i: (i, 0)
    in_specs = [
        pl.BlockSpec((tm, D_MODEL), row),
        pl.BlockSpec((None, 8, D_MODEL), lambda i: ((i * tm) // rows_per_mod, 0, 0)),
        _resident((None, D_MODEL, IN_COLS), lambda i: (l, 0, 0)),
        pl.BlockSpec((None, 1, ATT_Q_W), lambda i: (l, 0, 0)),
        pl.BlockSpec((None, 1, ATT_KV_W), lambda i: (l, 0, 0)),
        _resident((ATT_Q_W, ATT_Q_W), lambda i: (0, 0)),
    ]
    args = [x, mp, w_in, q_gain_t, k_gain_t, bd]
    if use_rope:
        tiles_per_seq = seq_len // tm
        in_specs += [pl.BlockSpec((tm, LANES), lambda i: (i % tiles_per_seq, 0))] * 2
        args += list(rope)
    widths = [4 * RET_W, ATT_Q_W, ATT_Q_W, None, CONV_DIM, CONV_DIM, 3 * D_MODEL]
    out_specs = [pl.BlockSpec((tm, w), row) for w in widths if w]
    out_shape = [jax.ShapeDtypeStruct((n, w), BF16) for w in widths if w]
    out_specs.insert(3, pl.BlockSpec((ATT_KV_W, tm), lambda i: (0, i)))
    out_shape.insert(3, jax.ShapeDtypeStruct((ATT_KV_W, n), BF16))
    if emit_kv:
        out_specs += [pl.BlockSpec((tm, ATT_KV_W), row)] * 2
        out_shape += [jax.ShapeDtypeStruct((n, ATT_KV_W), F32)] * 2
    return pl.pallas_call(
        functools.partial(_inproj_kernel, use_rope=use_rope, emit_kv=emit_kv),
        grid=(n // tm,),
        in_specs=in_specs,
        out_specs=out_specs,
        out_shape=out_shape,
        compiler_params=pltpu.CompilerParams(
            dimension_semantics=("arbitrary",), vmem_limit_bytes=VMEM_LIMIT),
        name="inproj",
    )(*args)


RET_CHUNK_UNROLL = 8
RET_SEQ_UNROLL = 4
TAB_DMASK, TAB_DQF, TAB_DQB, TAB_DKF, TAB_DKB, TAB_DCF, TAB_DCB, N_TAB = range(8)


def _log_sigmoid(x):
    return jnp.minimum(x, 0.0) - jnp.log(1.0 + jnp.exp(-jnp.abs(x)))


def _chunk_rows(start):
    if isinstance(start, int):
        return pl.ds(start, RET_CHUNK)
    return pl.ds(pl.multiple_of(start, RET_CHUNK), RET_CHUNK)


def _for_range(n, unroll, body):
    if n <= unroll:
        for i in range(n):
            body(i)
        return
    assert n % unroll == 0

    def step(t, carry):
        for j in range(unroll):
            body(t * unroll + j)
        return carry

    lax.fori_loop(0, n // unroll, step, 0)


def _retention_kernel(*refs, n_seq, n_chunks, has_s0, emit_state):
    dl_ref, q_ref, k_ref, v_ref, g_ref, gn_ref = refs[:6]
    pos = 6
    if has_s0:
        s0_ref = refs[pos]
        pos += 1
    o_ref = refs[pos]
    pos += 1
    if emit_state:
        st_ref = refs[pos]
        pos += 1
    tab_ref, kv_ref, ent_ref = refs[pos:pos + 3]

    C = RET_CHUNK
    hd = pl.program_id(1)
    row = lax.broadcasted_iota(jnp.int32, (C, C), 0).astype(F32)
    col = lax.broadcasted_iota(jnp.int32, (C, C), 1).astype(F32)
    diff = row - col
    lgf = _log_sigmoid(jnp.full((C, C), dl_ref[0, hd], F32))
    lgb = _log_sigmoid(jnp.full((C, C), dl_ref[1, hd], F32))
    tab_ref[TAB_DMASK] = (jnp.where(diff >= 0, jnp.exp(lgf * jnp.maximum(diff, 0.0)), 0.0)
                          + jnp.where(diff <= 0, jnp.exp(lgb * jnp.maximum(-diff, 0.0)), 0.0))
    tab_ref[TAB_DQF] = jnp.exp(lgf * (row + 1.0))
    tab_ref[TAB_DQB] = jnp.exp(lgb * (C - row))
    tab_ref[TAB_DKF] = jnp.exp(lgf * (C - 1.0 - row))
    tab_ref[TAB_DKB] = jnp.exp(lgb * row)
    tab_ref[TAB_DCF] = jnp.exp(lgf * C)
    tab_ref[TAB_DCB] = jnp.exp(lgb * C)

    def one_sequence(s, slot):
        base = s * (n_chunks * C)
        sc = slot * n_chunks

        def kv_chunk(c):
            rows = _chunk_rows(base + c * C)
            k = k_ref[rows, :].astype(F32)
            vT = v_ref[rows, :].astype(F32).T.astype(BF16)
            kk = jnp.concatenate([(k * tab_ref[TAB_DKF]).astype(BF16),
                                  (k * tab_ref[TAB_DKB]).astype(BF16)], axis=1)
            kv_ref[sc + c] = _dot(vT, kk)

        _for_range(n_chunks, RET_CHUNK_UNROLL, kv_chunk)

        if has_s0:
            st0 = (s0_ref[0].T, s0_ref[1].T)
        else:
            st0 = (jnp.zeros((RET_DV, RET_DK), F32), jnp.zeros((RET_DV, RET_DK), F32))

        def scan_step(t, st):
            st_f, st_b = st
            cb = n_chunks - 1 - t
            ent_ref[sc + t, :, 0:RET_DK] = st_f.astype(BF16)
            ent_ref[sc + cb, :, RET_DK:2 * RET_DK] = st_b.astype(BF16)
            st_f = st_f * tab_ref[TAB_DCF] + kv_ref[sc + t, :, 0:RET_DK]
            st_b = st_b * tab_ref[TAB_DCB] + kv_ref[sc + cb, :, RET_DK:2 * RET_DK]
            return st_f, st_b

        if n_chunks <= RET_CHUNK_UNROLL:
            st = st0
            for t in range(n_chunks):
                st = scan_step(t, st)
        else:
            st = lax.fori_loop(0, n_chunks, scan_step, st0)
        if emit_state:
            st_ref[s, 0] = st[0].T
            st_ref[s, 1] = st[1].T

        def out_chunk(c):
            rows = _chunk_rows(base + c * C)
            q = q_ref[rows, :]
            att = (_dot_nt(q, k_ref[rows, :]) * tab_ref[TAB_DMASK]).astype(BF16)
            qf = q.astype(F32)
            qq = jnp.concatenate([(qf * tab_ref[TAB_DQF]).astype(BF16),
                                  (qf * tab_ref[TAB_DQB]).astype(BF16)], axis=1)
            o = _dot(att, v_ref[rows, :]) + _dot_nt(qq, ent_ref[sc + c])
            ms = jnp.mean(o * o, axis=-1, keepdims=True)
            on = (o * lax.rsqrt(ms + EPS)) * gn_ref[...]
            o_ref[rows, :] = (g_ref[rows, :].astype(F32) * on).astype(BF16)

        _for_range(n_chunks, RET_CHUNK_UNROLL, out_chunk)

    if n_seq == 1:
        one_sequence(0, 0)
    else:
        assert n_seq % RET_SEQ_UNROLL == 0

        def seq_step(t, carry):
            for j in range(RET_SEQ_UNROLL):
                one_sequence(t * RET_SEQ_UNROLL + j, j)
            return carry

        lax.fori_loop(0, n_seq // RET_SEQ_UNROLL, seq_step, 0)


def _retention(ret_in, decay_logit_l, gn_l, s0, seq_len, n_seq_blk, emit_state):
    n = ret_in.shape[0]
    n_chunks = seq_len // RET_CHUNK
    tb = n_seq_blk * seq_len
    has_s0 = s0 is not None
    n_off = RET_W // RET_DK

    def branch(k):
        return pl.BlockSpec((tb, RET_DK), lambda i, h: (i, k * n_off + h))

    in_specs = [pl.BlockSpec(memory_space=pltpu.SMEM), branch(0), branch(1), branch(2), branch(3),
                pl.BlockSpec((1, RET_DV), lambda i, h: (0, h))]
    args = [decay_logit_l, ret_in, ret_in, ret_in, ret_in, gn_l]
    if has_s0:
        assert n_seq_blk == 1
        in_specs.append(pl.BlockSpec((None, 2, None, RET_DK, RET_DV), lambda i, h: (i, 0, h, 0, 0)))
        args.append(s0)
    out_specs = [pl.BlockSpec((tb, RET_DV), lambda i, h: (i, h))]
    out_shape = [jax.ShapeDtypeStruct((n, RET_W), BF16)]
    if emit_state:
        out_specs.append(pl.BlockSpec((n_seq_blk, 2, None, RET_DK, RET_DV), lambda i, h: (i, 0, h, 0, 0)))
        out_shape.append(jax.ShapeDtypeStruct((n // seq_len, 2, N_RET_HEADS, RET_DK, RET_DV), F32))
    n_slots = n_chunks * (1 if n_seq_blk == 1 else RET_SEQ_UNROLL)
    return pl.pallas_call(
        functools.partial(_retention_kernel, n_seq=n_seq_blk, n_chunks=n_chunks, has_s0=has_s0,
                          emit_state=emit_state),
        grid=(n // tb, N_RET_HEADS),
        in_specs=in_specs,
        out_specs=out_specs,
        out_shape=out_shape,
        scratch_shapes=[
            pltpu.VMEM((N_TAB, RET_CHUNK, RET_CHUNK), F32),
            pltpu.VMEM((n_slots, RET_DV, 2 * RET_DK), F32),
            pltpu.VMEM((n_slots, RET_DV, 2 * RET_DK), BF16),
        ],
        compiler_params=pltpu.CompilerParams(
            dimension_semantics=("arbitrary", "arbitrary"), vmem_limit_bytes=VMEM_LIMIT),
        name="retention",
    )(*args)


ATT_TQ = 256
ATT_KB = 512
NEG_BIG = -1e30
SUM_ROWS = BF16_SUBLANES


def _attention_kernel(*refs, n_cache, n_new):
    q_ref = refs[0]
    pos = 1
    if n_cache:
        kc_ref, vc_ref = refs[pos:pos + 2]
        pos += 2
    kn_ref, vn_ref, o_ref = refs[pos:pos + 3]

    tq = q_ref.shape[0]
    nq = GROUP * tq
    qf = q_ref[...].astype(F32)
    head = lax.broadcasted_iota(jnp.int32, (tq, KV_LANES), 1) // HEAD_DIM
    qs = jnp.concatenate([jnp.where(head == g, qf, 0.0) for g in range(GROUP)], axis=0).astype(BF16)

    m = jnp.full((1, nq), NEG_BIG, F32)
    acc = jnp.zeros((HEAD_DIM + SUM_ROWS, nq), F32)

    blocks = []
    if n_cache:
        kb = min(ATT_KB, n_cache)
        blocks += [(kc_ref, vc_ref, j * kb, kb) for j in range(n_cache // kb)]
    kb = min(ATT_KB, n_new)
    blocks += [(kn_ref, vn_ref, j * kb, kb) for j in range(n_new // kb)]
    for k_ref, vT_ref, lo, kb in blocks:
        sT = _dot_nt(k_ref[lo:lo + kb, :], qs)
        m_new = jnp.maximum(m, jnp.max(sT, axis=0, keepdims=True))
        alpha = jnp.exp2(m - m_new)
        pT = jnp.exp2(sT - m_new).astype(BF16)
        v1 = jnp.concatenate([vT_ref[:, lo:lo + kb], jnp.ones((SUM_ROWS, kb), BF16)], axis=0)
        acc = alpha * acc + _dot(v1, pT)
        m = m_new

    oT = acc[0:HEAD_DIM] / acc[HEAD_DIM:HEAD_DIM + 1]
    o4 = jnp.concatenate([oT[:, g * tq:(g + 1) * tq] for g in range(GROUP)], axis=0)
    o_ref[...] = o4.T.astype(BF16)


def _attention(q, kt, vT, cache, seq_len):
    n = q.shape[0]
    n_batch = n // seq_len
    tq = ATT_TQ
    nq = seq_len // tq
    n_cache = 0 if cache is None else cache[0].shape[1]
    in_specs = [pl.BlockSpec((tq, KV_LANES), lambda b, kv, i: (b * nq + i, kv))]
    args = [q]
    if n_cache:
        in_specs += [pl.BlockSpec((None, n_cache, KV_LANES), lambda b, kv, i: (b, 0, kv)),
                     pl.BlockSpec((None, HEAD_DIM, n_cache), lambda b, kv, i: (b, kv, 0))]
        args += list(cache)
    in_specs += [pl.BlockSpec((seq_len, KV_LANES), lambda b, kv, i: (b, kv)),
                 pl.BlockSpec((HEAD_DIM, seq_len), lambda b, kv, i: (kv, b))]
    args += [kt, vT]
    return pl.pallas_call(
        functools.partial(_attention_kernel, n_cache=n_cache, n_new=seq_len),
        grid=(n_batch, N_KV_HEADS, nq),
        in_specs=in_specs,
        out_specs=pl.BlockSpec((tq, KV_LANES), lambda b, kv, i: (b * nq + i, kv)),
        out_shape=jax.ShapeDtypeStruct((n, ATT_Q_W), BF16),
        compiler_params=pltpu.CompilerParams(
            dimension_semantics=("arbitrary", "arbitrary", "arbitrary"), vmem_limit_bytes=VMEM_LIMIT),
        name="attention",
    )(*args)


MERGE_TM = 512


def _merge_kernel(h_ref, mp_ref, ro_ref, ao_ref, cb_ref, u_ref, up_ref, un_ref, g_ref, cw_ref,
                  wr_ref, wa_ref, wc_ref, wo_ref, o_ref, *, seq_len):
    tm = h_ref.shape[0]
    i = pl.program_id(0)
    u = u_ref[...].astype(F32)
    r = lax.broadcasted_iota(jnp.int32, u.shape, 0)
    t = (i * tm + r) % seq_len
    prev_row = up_ref[BF16_SUBLANES - 1:BF16_SUBLANES, :].astype(F32)
    next_row = un_ref[0:1, :].astype(F32)
    u_prev = jnp.where(r == 0, prev_row, pltpu.roll(u, 1, axis=0))
    u_next = jnp.where(r == tm - 1, next_row, pltpu.roll(u, tm - 1, axis=0))
    u_prev = jnp.where(t == 0, 0.0, u_prev)
    u_next = jnp.where(t == seq_len - 1, 0.0, u_next)
    cw = cw_ref[...]
    conv = u_prev * cw[0:1] + u * cw[1:2] + u_next * cw[2:3]
    yc = _dot((cb_ref[...].astype(F32) * conv).astype(BF16), wc_ref[...])
    yr = _dot(ro_ref[...], wr_ref[...])
    ya = _dot(ao_ref[...], wa_ref[...])
    merged = (g_ref[:, 0:D_MODEL].astype(F32) * yr
              + g_ref[:, D_MODEL:2 * D_MODEL].astype(F32) * ya
              + g_ref[:, 2 * D_MODEL:3 * D_MODEL].astype(F32) * yc)
    y = _dot(merged.astype(BF16), wo_ref[...])
    o_ref[...] = h_ref[...] + mp_ref[2:3, :] * y


def _merge(h, mp, rows_per_mod, ret_o, att_o, cb, u, gates, conv_w8, w_ret_o, w_att_o, w_conv_o, w_o,
           l, seq_len):
    n = h.shape[0]
    tm = MERGE_TM
    hb = tm // BF16_SUBLANES
    last = n // BF16_SUBLANES - 1
    row = lambda i: (i, 0)
    wspec = lambda k: _resident((None, k, D_MODEL), lambda i: (l, 0, 0))
    return pl.pallas_call(
        functools.partial(_merge_kernel, seq_len=seq_len),
        grid=(n // tm,),
        in_specs=[
            pl.BlockSpec((tm, D_MODEL), row),
            pl.BlockSpec((None, 8, D_MODEL), lambda i: ((i * tm) // rows_per_mod, 0, 0)),
            pl.BlockSpec((tm, RET_W), row),
            pl.BlockSpec((tm, ATT_Q_W), row),
            pl.BlockSpec((tm, CONV_DIM), row),
            pl.BlockSpec((tm, CONV_DIM), row),
            pl.BlockSpec((BF16_SUBLANES, CONV_DIM), lambda i: (jnp.maximum(i * hb - 1, 0), 0)),
            pl.BlockSpec((BF16_SUBLANES, CONV_DIM), lambda i: (jnp.minimum((i + 1) * hb, last), 0)),
            pl.BlockSpec((tm, 3 * D_MODEL), row),
            pl.BlockSpec((None, 8, CONV_DIM), lambda i: (l, 0, 0)),
            wspec(RET_W), wspec(ATT_Q_W), wspec(CONV_DIM), wspec(D_MODEL),
        ],
        out_specs=pl.BlockSpec((tm, D_MODEL), row),
        out_shape=jax.ShapeDtypeStruct((n, D_MODEL), F32),
        compiler_params=pltpu.CompilerParams(
            dimension_semantics=("arbitrary",), vmem_limit_bytes=VMEM_LIMIT),
        name="merge",
    )(h, mp, ret_o, att_o, cb, u, u, u, gates, conv_w8, w_ret_o, w_att_o, w_conv_o, w_o)


def _rope_tables(n_tok):
    rows = n_tok // GRID_W
    t_row = np.repeat(np.arange(rows, dtype=np.float64), GRID_W)
    t_col = np.tile(np.arange(GRID_W, dtype=np.float64), rows)
    n_freq = HEAD_DIM // 4
    inv = ROPE_THETA ** (-np.arange(n_freq, dtype=np.float64) / n_freq)
    ang = np.concatenate([t_row[:, None] * inv, t_col[:, None] * inv], axis=-1)
    cos, sin = np.cos(ang), np.sin(ang)
    cos64 = np.concatenate([cos, cos], axis=-1)
    sin64 = np.concatenate([-sin, sin], axis=-1)
    return (jnp.asarray(np.tile(cos64, (1, LANES // HEAD_DIM)), F32),
            jnp.asarray(np.tile(sin64, (1, LANES // HEAD_DIM)), F32))


def _mod_pack(mod_l, rows, sub, norm_w_row):
    r0 = rows[0]
    nr = len(rows)
    m = mod_l[r0:r0 + nr, 3 * sub * D_MODEL:3 * (sub + 1) * D_MODEL].reshape(nr, 3, D_MODEL)
    nw = jnp.broadcast_to(norm_w_row[None, None, :], (nr, 1, D_MODEL))
    pad = jnp.zeros((nr, 4, D_MODEL), F32)
    return jnp.concatenate([m, nw, pad], axis=1)


def kernel(x_prompt, x_sample, c, state_ret, cache_k, cache_v, c_ctx, w_ada, b_ada, norm_w, w_ffn_in,
           w_ffn_out, w_in, ret_decay_logit, ret_gn, q_gain, k_gain, conv_w, w_ret_o, w_att_o, w_conv_o,
           w_o):
    n_ctx_b, ctx_len, _ = x_prompt.shape
    n_lat_b, lat_len, _ = x_sample.shape

    w_ffn_in_b = w_ffn_in.astype(BF16)
    w_ffn_out_b = w_ffn_out.astype(BF16)
    w_in_b = w_in.astype(BF16)
    w_ret_o_b = w_ret_o.astype(BF16)
    w_att_o_b = w_att_o.astype(BF16)
    w_conv_o_b = w_conv_o.astype(BF16)
    w_o_b = w_o.astype(BF16)

    q_gain_t = jnp.tile(q_gain, (1, N_HEADS)).reshape(DEPTH, 1, ATT_Q_W)
    k_gain_t = jnp.tile(k_gain, (1, N_KV_HEADS)).reshape(DEPTH, 1, ATT_KV_W)
    conv_w8 = jnp.pad(conv_w, ((0, 0), (0, 8 - conv_w.shape[1]), (0, 0)))
    gid = np.arange(ATT_Q_W) // HEAD_DIM
    bd = jnp.asarray(gid[:, None] == gid[None, :], BF16)
    rope = _rope_tables(lat_len)

    cond8 = jnp.zeros((8, D_MODEL), F32).at[0].set(c_ctx).at[1:1 + n_lat_b].set(c)
    mod = _ada(cond8, w_ada, b_ada)

    ck = cache_k.astype(BF16)
    ck = jnp.broadcast_to(ck[:, :, :, :, None, :], ck.shape[:4] + (GROUP, HEAD_DIM))
    cache_kt = ck.reshape(ck.shape[0], ck.shape[1], ck.shape[2], N_KV_HEADS * KV_LANES)
    cache_vT = jnp.transpose(cache_v.astype(BF16), (0, 1, 3, 4, 2)).reshape(
        cache_v.shape[0], cache_v.shape[1], ATT_KV_W, cache_v.shape[2])

    groups = (
        dict(x=x_prompt.reshape(n_ctx_b * ctx_len, D_MODEL), rows=[0], seq=ctx_len, ctx=True),
        dict(x=x_sample.reshape(n_lat_b * lat_len, D_MODEL), rows=list(range(1, 1 + n_lat_b)),
             seq=lat_len, ctx=False),
    )
    results = []
    for grp in groups:
        h = grp["x"]
        seq = grp["seq"]
        is_ctx = grp["ctx"]
        rpm = h.shape[0] // len(grp["rows"])
        states, keys, values = [], [], []
        for l in range(DEPTH):
            mp = [_mod_pack(mod[l], grp["rows"], s, norm_w[l, s]) for s in range(3)]
            h = _ffn(h, mp[0], rpm, w_ffn_in_b, w_ffn_out_b, l, 0)
            outs = _inproj(h, mp[1], rpm, w_in_b, l, q_gain_t, k_gain_t, bd,
                           None if is_ctx else rope, seq, emit_kv=is_ctx)
            ret_in, qn, kt, vT, cb, u, gates = outs[:7]
            if is_ctx:
                ret_o, st = _retention(ret_in, ret_decay_logit[l], ret_gn[l][None, :], None, seq,
                                       n_seq_blk=8, emit_state=True)
                states.append(st)
                keys.append(outs[7].reshape(n_ctx_b, seq, N_KV_HEADS, HEAD_DIM))
                values.append(outs[8].reshape(n_ctx_b, seq, N_KV_HEADS, HEAD_DIM))
                att_o = _attention(qn, kt, vT, None, seq)
            else:
                (ret_o,) = _retention(ret_in, ret_decay_logit[l], ret_gn[l][None, :], state_ret[:, l],
                                      seq, n_seq_blk=1, emit_state=False)
                att_o = _attention(qn, kt, vT, (cache_kt[:, l], cache_vT[:, l]), seq)
            h = _merge(h, mp[1], rpm, ret_o, att_o, cb, u, gates, conv_w8, w_ret_o_b, w_att_o_b,
                       w_conv_o_b, w_o_b, l, seq)
            h = _ffn(h, mp[2], rpm, w_ffn_in_b, w_ffn_out_b, l, 1)
        results.append((h, states, keys, values))

    (y_ctx, states, keys, values), (y_lat, _, _, _) = results
    y_prompt = y_ctx.reshape(x_prompt.shape)
    y_sample = y_lat.reshape(x_sample.shape)
    new_state_ret = jnp.stack(states, axis=1)
    new_cache_k = jnp.stack(keys, axis=1)
    new_cache_v = jnp.stack(values, axis=1)
    return (y_prompt, y_sample, new_state_ret, new_cache_k, new_cache_v)
```

```python
import functools
import math

import numpy as np
import jax
import jax.numpy as jnp
from jax import lax
from jax.experimental import pallas as pl
from jax.experimental.pallas import tpu as pltpu

D_MODEL = 1024
DEPTH = 2
GRID_W = 64
N_RET_HEADS = 4
RET_DK = 128
RET_DV = 128
RET_CHUNK = 128
N_HEADS = 8
N_KV_HEADS = 2
HEAD_DIM = 64
ROPE_THETA = 10000.0
CONV_DIM = 512
FFN_DIM = 2816
N_MOD = 9
EPS = 1e-6
RET_W = N_RET_HEADS * RET_DK
ATT_Q_W = N_HEADS * HEAD_DIM
ATT_KV_W = N_KV_HEADS * HEAD_DIM
GROUP = N_HEADS // N_KV_HEADS
KV_LANES = GROUP * HEAD_DIM
OFF_RQ, OFF_RK, OFF_RV, OFF_RG = 0, 512, 1024, 1536
OFF_AQ, OFF_AK, OFF_AV = 2048, 2560, 2688
OFF_CB, OFF_CC, OFF_CX = 2816, 3328, 3840
OFF_GATES = 4352
IN_COLS = 7424

LANES = 128
BF16_SUBLANES = 16
VMEM_LIMIT = 56 * 1024 * 1024

F32 = jnp.float32
BF16 = jnp.bfloat16


def _dot(a, b):
    return jnp.dot(a, b, preferred_element_type=F32)


def _dot_nt(a, b):
    return lax.dot_general(a, b, (((1,), (1,)), ((), ())), preferred_element_type=F32)


def _sigmoid(x):
    return 1.0 / (1.0 + jnp.exp(-x))


def _silu(x):
    return x * _sigmoid(x)


def _mod_norm(x, mp):
    ms = jnp.mean(x * x, axis=-1, keepdims=True)
    y = x * lax.rsqrt(ms + EPS)
    return (y * mp[3:4]) * (1.0 + mp[1:2]) + mp[0:1]


def _resident(shape, index):
    return pl.BlockSpec(shape, index, pipeline_mode=pl.Buffered(1))


ADA_TN = 1152


def _ada_kernel(c_ref, w_ref, b_ref, o_ref):
    a = _silu(c_ref[...]).astype(BF16)
    o_ref[...] = _dot(a, w_ref[...].astype(BF16)) + b_ref[...]


def _ada(cond8, w_ada, b_ada):
    n = N_MOD * D_MODEL
    return pl.pallas_call(
        _ada_kernel,
        grid=(DEPTH, n // ADA_TN),
        in_specs=[
            pl.BlockSpec((8, D_MODEL), lambda l, j: (0, 0)),
            pl.BlockSpec((None, D_MODEL, ADA_TN), lambda l, j: (l, 0, j)),
            pl.BlockSpec((None, 1, ADA_TN), lambda l, j: (l, 0, j)),
        ],
        out_specs=pl.BlockSpec((None, 8, ADA_TN), lambda l, j: (l, 0, j)),
        out_shape=jax.ShapeDtypeStruct((DEPTH, 8, n), F32),
        compiler_params=pltpu.CompilerParams(
            dimension_semantics=("arbitrary", "arbitrary"), vmem_limit_bytes=VMEM_LIMIT),
        name="ada",
    )(cond8, w_ada, b_ada.reshape(DEPTH, 1, n))


FFN_TM = 1024
FFN_FC = 256


def _ffn_kernel(x_ref, mp_ref, wi_ref, wo_ref, o_ref, h_ref):
    x = x_ref[...]
    mp = mp_ref[...]
    xb = _mod_norm(x, mp).astype(BF16)
    for c in range(FFN_DIM // FFN_FC):
        lo = c * FFN_FC
        g = _dot(xb, wi_ref[:, lo:lo + FFN_FC])
        u = _dot(xb, wi_ref[:, FFN_DIM + lo:FFN_DIM + lo + FFN_FC])
        h_ref[:, lo:lo + FFN_FC] = (_silu(g) * u).astype(BF16)
    y = _dot(h_ref[...], wo_ref[...])
    o_ref[...] = x + (0.5 * mp[2:3]) * y


def _ffn(x, mp, rows_per_mod, w_in, w_out, l, j):
    n = x.shape[0]
    tm = FFN_TM
    return pl.pallas_call(
        _ffn_kernel,
        grid=(n // tm,),
        in_specs=[
            pl.BlockSpec((tm, D_MODEL), lambda i: (i, 0)),
            pl.BlockSpec((None, 8, D_MODEL), lambda i: ((i * tm) // rows_per_mod, 0, 0)),
            _resident((None, None, D_MODEL, 2 * FFN_DIM), lambda i: (l, j, 0, 0)),
            _resident((None, None, FFN_DIM, D_MODEL), lambda i: (l, j, 0, 0)),
        ],
        out_specs=pl.BlockSpec((tm, D_MODEL), lambda i: (i, 0)),
        out_shape=jax.ShapeDtypeStruct((n, D_MODEL), F32),
        scratch_shapes=[pltpu.VMEM((tm, FFN_DIM), BF16)],
        compiler_params=pltpu.CompilerParams(
            dimension_semantics=("arbitrary",), vmem_limit_bytes=VMEM_LIMIT),
        name="ffn",
    )(x, mp, w_in, w_out)


INPROJ_TM = 512
Q_SCALE = HEAD_DIM ** -0.5 * math.log2(math.e)


def _group_rms(a, bd, gain):
    sq = a * a
    hi = sq.astype(BF16)
    lo = (sq - hi.astype(F32)).astype(BF16)
    ss = _dot(hi, bd) + _dot(lo, bd)
    return (a * lax.rsqrt(ss * (1.0 / HEAD_DIM) + EPS)) * gain


def _rope(x, cos, sin_signed):
    half = HEAD_DIM // 2
    reps = x.shape[1] // LANES
    lane = lax.broadcasted_iota(jnp.int32, (x.shape[0], LANES), 1)
    first = (lane % HEAD_DIM) < half
    out = []
    for r in range(reps):
        xs = x[:, r * LANES:(r + 1) * LANES]
        partner = jnp.where(first, pltpu.roll(xs, LANES - half, axis=1), pltpu.roll(xs, half, axis=1))
        out.append(xs * cos + partner * sin_signed)
    return out[0] if reps == 1 else jnp.concatenate(out, axis=1)


def _tile_kv(a):
    lane = lax.broadcasted_iota(jnp.int32, a.shape, 1)
    sw = pltpu.roll(a, HEAD_DIM, axis=1)
    h0 = jnp.where(lane < HEAD_DIM, a, sw)
    h1 = jnp.where(lane < HEAD_DIM, sw, a)
    return jnp.concatenate([h0, h0, h1, h1], axis=1)


def _inproj_kernel(*refs, use_rope, emit_kv):
    x_ref, mp_ref, w_ref, qg_ref, kg_ref, bd_ref = refs[:6]
    pos = 6
    if use_rope:
        cos_ref, sin_ref = refs[pos:pos + 2]
        pos += 2
    ret_ref, q_ref, kt_ref, vT_ref, cb_ref, u_ref, gates_ref = refs[pos:pos + 7]
    pos += 7
    if emit_kv:
        kout_ref, vout_ref = refs[pos:pos + 2]

    xb = _mod_norm(x_ref[...], mp_ref[...]).astype(BF16)

    def proj(off, width):
        return _dot(xb, w_ref[:, off:off + width])

    ret_ref[:, OFF_RQ:OFF_RQ + RET_W] = proj(OFF_RQ, RET_W).astype(BF16)
    ret_ref[:, OFF_RK:OFF_RK + RET_W] = (proj(OFF_RK, RET_W) * (RET_DK ** -0.5)).astype(BF16)
    ret_ref[:, OFF_RV:OFF_RV + RET_W] = proj(OFF_RV, RET_W).astype(BF16)
    ret_ref[:, OFF_RG:OFF_RG + RET_W] = _silu(proj(OFF_RG, RET_W)).astype(BF16)

    q = _group_rms(proj(OFF_AQ, ATT_Q_W), bd_ref[...], qg_ref[...])
    if use_rope:
        q = _rope(q, cos_ref[...], sin_ref[...])
    q_ref[...] = (q * Q_SCALE).astype(BF16)

    k = _group_rms(proj(OFF_AK, ATT_KV_W), bd_ref[0:ATT_KV_W, 0:ATT_KV_W], kg_ref[...])
    v = proj(OFF_AV, ATT_KV_W)
    if emit_kv:
        kout_ref[...] = k
        vout_ref[...] = v
    if use_rope:
        k = _rope(k, cos_ref[...], sin_ref[...])
    kt_ref[...] = _tile_kv(k).astype(BF16)
    vT_ref[...] = v.T.astype(BF16)

    cb_ref[...] = proj(OFF_CB, CONV_DIM).astype(BF16)
    u_ref[...] = (proj(OFF_CC, CONV_DIM) * proj(OFF_CX, CONV_DIM)).astype(BF16)

    for c in range(3 * D_MODEL // 512):
        gates_ref[:, c * 512:(c + 1) * 512] = _sigmoid(proj(OFF_GATES + c * 512, 512)).astype(BF16)


def _inproj(x, mp, rows_per_mod, w_in, l, q_gain_t, k_gain_t, bd, rope, seq_len, emit_kv):
    n = x.shape[0]
    tm = INPROJ_TM
    use_rope = rope is not None
    row = lambda i: (i, 0)
    in_specs = [
        pl.BlockSpec((tm, D_MODEL), row),
        pl.BlockSpec((None, 8, D_MODEL), lambda i: ((i * tm) // rows_per_mod, 0, 0)),
        _resident((None, D_MODEL, IN_COLS), lambda i: (l, 0, 0)),
        pl.BlockSpec((None, 1, ATT_Q_W), lambda i: (l, 0, 0)),
        pl.BlockSpec((None, 1, ATT_KV_W), lambda i: (l, 0, 0)),
        _resident((ATT_Q_W, ATT_Q_W), lambda i: (0, 0)),
    ]
    args = [x, mp, w_in, q_gain_t, k_gain_t, bd]
    if use_rope:
        tiles_per_seq = seq_len // tm
        in_specs += [pl.BlockSpec((tm, LANES), lambda i: (i % tiles_per_seq, 0))] * 2
        args += list(rope)
    widths = [4 * RET_W, ATT_Q_W, ATT_Q_W, None, CONV_DIM, CONV_DIM, 3 * D_MODEL]
    out_specs = [pl.BlockSpec((tm, w), row) for w in widths if w]
    out_shape = [jax.ShapeDtypeStruct((n, w), BF16) for w in widths if w]
    out_specs.insert(3, pl.BlockSpec((ATT_KV_W, tm), lambda i: (0, i)))
    out_shape.insert(3, jax.ShapeDtypeStruct((ATT_KV_W, n), BF16))
    if emit_kv:
        out_specs += [pl.BlockSpec((tm, ATT_KV_W), row)] * 2
        out_shape += [jax.ShapeDtypeStruct((n, ATT_KV_W), F32)] * 2
    return pl.pallas_call(
        functools.partial(_inproj_kernel, use_rope=use_rope, emit_kv=emit_kv),
        grid=(n // tm,),
        in_specs=in_specs,
        out_specs=out_specs,
        out_shape=out_shape,
        compiler_params=pltpu.CompilerParams(
            dimension_semantics=("arbitrary",), vmem_limit_bytes=VMEM_LIMIT),
        name="inproj",
    )(*args)


RET_CHUNK_UNROLL = 8
RET_SEQ_UNROLL = 4
TAB_DMASK, TAB_DQF, TAB_DQB, TAB_DKF, TAB_DKB, TAB_DCF, TAB_DCB, N_TAB = range(8)


def _log_sigmoid(x):
    return jnp.minimum(x, 0.0) - jnp.log(1.0 + jnp.exp(-jnp.abs(x)))


def _chunk_rows(start):
    if isinstance(start, int):
        return pl.ds(start, RET_CHUNK)
    return pl.ds(pl.multiple_of(start, RET_CHUNK), RET_CHUNK)


def _for_range(n, unroll, body):
    if n <= unroll:
        for i in range(n):
            body(i)
        return
    assert n % unroll == 0

    def step(t, carry):
        for j in range(unroll):
            body(t * unroll + j)
        return carry

    lax.fori_loop(0, n // unroll, step, 0)


def _retention_kernel(*refs, n_seq, n_chunks, has_s0, emit_state):
    dl_ref, q_ref, k_ref, v_ref, g_ref, gn_ref = refs[:6]
    pos = 6
    if has_s0:
        s0_ref = refs[pos]
        pos += 1
    o_ref = refs[pos]
    pos += 1
    if emit_state:
        st_ref = refs[pos]
        pos += 1
    tab_ref, kv_ref, ent_ref = refs[pos:pos + 3]

    C = RET_CHUNK
    hd = pl.program_id(1)
    row = lax.broadcasted_iota(jnp.int32, (C, C), 0).astype(F32)
    col = lax.broadcasted_iota(jnp.int32, (C, C), 1).astype(F32)
    diff = row - col
    lgf = _log_sigmoid(jnp.full((C, C), dl_ref[0, hd], F32))
    lgb = _log_sigmoid(jnp.full((C, C), dl_ref[1, hd], F32))
    tab_ref[TAB_DMASK] = (jnp.where(diff >= 0, jnp.exp(lgf * jnp.maximum(diff, 0.0)), 0.0)
                          + jnp.where(diff <= 0, jnp.exp(lgb * jnp.maximum(-diff, 0.0)), 0.0))
    tab_ref[TAB_DQF] = jnp.exp(lgf * (row + 1.0))
    tab_ref[TAB_DQB] = jnp.exp(lgb * (C - row))
    tab_ref[TAB_DKF] = jnp.exp(lgf * (C - 1.0 - row))
    tab_ref[TAB_DKB] = jnp.exp(lgb * row)
    tab_ref[TAB_DCF] = jnp.exp(lgf * C)
    tab_ref[TAB_DCB] = jnp.exp(lgb * C)

    def one_sequence(s, slot):
        base = s * (n_chunks * C)
        sc = slot * n_chunks

        def kv_chunk(c):
            rows = _chunk_rows(base + c * C)
            k = k_ref[rows, :].astype(F32)
            vT = v_ref[rows, :].astype(F32).T.astype(BF16)
            kk = jnp.concatenate([(k * tab_ref[TAB_DKF]).astype(BF16),
                                  (k * tab_ref[TAB_DKB]).astype(BF16)], axis=1)
            kv_ref[sc + c] = _dot(vT, kk)

        _for_range(n_chunks, RET_CHUNK_UNROLL, kv_chunk)

        if has_s0:
            st0 = (s0_ref[0].T, s0_ref[1].T)
        else:
            st0 = (jnp.zeros((RET_DV, RET_DK), F32), jnp.zeros((RET_DV, RET_DK), F32))

        def scan_step(t, st):
            st_f, st_b = st
            cb = n_chunks - 1 - t
            ent_ref[sc + t, :, 0:RET_DK] = st_f.astype(BF16)
            ent_ref[sc + cb, :, RET_DK:2 * RET_DK] = st_b.astype(BF16)
            st_f = st_f * tab_ref[TAB_DCF] + kv_ref[sc + t, :, 0:RET_DK]
            st_b = st_b * tab_ref[TAB_DCB] + kv_ref[sc + cb, :, RET_DK:2 * RET_DK]
            return st_f, st_b

        if n_chunks <= RET_CHUNK_UNROLL:
            st = st0
            for t in range(n_chunks):
                st = scan_step(t, st)
        else:
            st = lax.fori_loop(0, n_chunks, scan_step, st0)
        if emit_state:
            st_ref[s, 0] = st[0].T
            st_ref[s, 1] = st[1].T

        def out_chunk(c):
            rows = _chunk_rows(base + c * C)
            q = q_ref[rows, :]
            att = (_dot_nt(q, k_ref[rows, :]) * tab_ref[TAB_DMASK]).astype(BF16)
            qf = q.astype(F32)
            qq = jnp.concatenate([(qf * tab_ref[TAB_DQF]).astype(BF16),
                                  (qf * tab_ref[TAB_DQB]).astype(BF16)], axis=1)
            o = _dot(att, v_ref[rows, :]) + _dot_nt(qq, ent_ref[sc + c])
            ms = jnp.mean(o * o, axis=-1, keepdims=True)
            on = (o * lax.rsqrt(ms + EPS)) * gn_ref[...]
            o_ref[rows, :] = (g_ref[rows, :].astype(F32) * on).astype(BF16)

        _for_range(n_chunks, RET_CHUNK_UNROLL, out_chunk)

    if n_seq == 1:
        one_sequence(0, 0)
    else:
        assert n_seq % RET_SEQ_UNROLL == 0

        def seq_step(t, carry):
            for j in range(RET_SEQ_UNROLL):
                one_sequence(t * RET_SEQ_UNROLL + j, j)
            return carry

        lax.fori_loop(0, n_seq // RET_SEQ_UNROLL, seq_step, 0)


def _retention(ret_in, decay_logit_l, gn_l, s0, seq_len, n_seq_blk, emit_state):
    n = ret_in.shape[0]
    n_chunks = seq_len // RET_CHUNK
    tb = n_seq_blk * seq_len
    has_s0 = s0 is not None
    n_off = RET_W // RET_DK

    def branch(k):
        return pl.BlockSpec((tb, RET_DK), lambda i, h: (i, k * n_off + h))

    in_specs = [pl.BlockSpec(memory_space=pltpu.SMEM), branch(0), branch(1), branch(2), branch(3),
                pl.BlockSpec((1, RET_DV), lambda i, h: (0, h))]
    args = [decay_logit_l, ret_in, ret_in, ret_in, ret_in, gn_l]
    if has_s0:
        assert n_seq_blk == 1
        in_specs.append(pl.BlockSpec((None, 2, None, RET_DK, RET_DV), lambda i, h: (i, 0, h, 0, 0)))
        args.append(s0)
    out_specs = [pl.BlockSpec((tb, RET_DV), lambda i, h: (i, h))]
    out_shape = [jax.ShapeDtypeStruct((n, RET_W), BF16)]
    if emit_state:
        out_specs.append(pl.BlockSpec((n_seq_blk, 2, None, RET_DK, RET_DV), lambda i, h: (i, 0, h, 0, 0)))
        out_shape.append(jax.ShapeDtypeStruct((n // seq_len, 2, N_RET_HEADS, RET_DK, RET_DV), F32))
    n_slots = n_chunks * (1 if n_seq_blk == 1 else RET_SEQ_UNROLL)
    return pl.pallas_call(
        functools.partial(_retention_kernel, n_seq=n_seq_blk, n_chunks=n_chunks, has_s0=has_s0,
                          emit_state=emit_state),
        grid=(n // tb, N_RET_HEADS),
        in_specs=in_specs,
        out_specs=out_specs,
        out_shape=out_shape,
        scratch_shapes=[
            pltpu.VMEM((N_TAB, RET_CHUNK, RET_CHUNK), F32),
            pltpu.VMEM((n_slots, RET_DV, 2 * RET_DK), F32),
            pltpu.VMEM((n_slots, RET_DV, 2 * RET_DK), BF16),
        ],
        compiler_params=pltpu.CompilerParams(
            dimension_semantics=("arbitrary", "arbitrary"), vmem_limit_bytes=VMEM_LIMIT),
        name="retention",
    )(*args)


ATT_TQ = 256
ATT_KB = 512
ATT_RC = 16
NEG_BIG = -1e30
SUM_ROWS = BF16_SUBLANES


def _attention_kernel(*refs, n_cache, n_new):
    q_ref = refs[0]
    pos = 1
    if n_cache:
        kc_ref, vc_ref = refs[pos:pos + 2]
        pos += 2
    kn_ref, vn_ref, o_ref, qs_ref, acc_ref = refs[pos:pos + 5]
    s_slots = refs[pos + 5:pos + 7]
    p_slots = refs[pos + 7:pos + 9]

    tq = q_ref.shape[0]
    nq = GROUP * tq
    qf = q_ref[...].astype(F32)
    head = lax.broadcasted_iota(jnp.int32, (tq, KV_LANES), 1) // HEAD_DIM
    for g in range(GROUP):
        qs_ref[g * tq:(g + 1) * tq, :] = jnp.where(head == g, qf, 0.0).astype(BF16)
    acc_ref[...] = jnp.zeros(acc_ref.shape, F32)

    kb = min(ATT_KB, n_new)
    n_cb = n_cache // kb
    n_nb = n_new // kb
    assert n_cache % kb == 0 and n_new % kb == 0

    def span(lo):
        return pl.ds(lo, kb) if isinstance(lo, int) else pl.ds(pl.multiple_of(lo, kb), kb)

    n_blk = n_cb + n_nb

    def block(t):
        if isinstance(t, int) and t < n_cb:
            return kc_ref, vc_ref, t * kb
        return kn_ref, vn_ref, (t - n_cb) * kb

    def scores(t, par):
        k_ref, _, lo = block(t)
        s_slots[par][...] = _dot_nt(k_ref[span(lo), :], qs_ref[...])

    def softmax(par, m):
        s_ref, p_ref = s_slots[par], p_slots[par]
        mx = s_ref[0:ATT_RC, :]
        for r in range(ATT_RC, kb, ATT_RC):
            mx = jnp.maximum(mx, s_ref[r:r + ATT_RC, :])
        m_new = jnp.maximum(m, jnp.max(mx, axis=0, keepdims=True))
        m_rows = jnp.broadcast_to(m_new, (ATT_RC, nq))
        for r in range(0, kb, ATT_RC):
            p_ref[r:r + ATT_RC, :] = jnp.exp2(s_ref[r:r + ATT_RC, :] - m_rows).astype(BF16)
        return m_new, jnp.exp2(m - m_new)

    def pv(t, par, alpha):
        _, vT_ref, lo = block(t)
        v1 = jnp.concatenate([vT_ref[:, span(lo)], jnp.ones((SUM_ROWS, kb), BF16)], axis=0)
        acc_ref[...] = alpha * acc_ref[...] + _dot(v1, p_slots[par][...])

    def stage(t, par, carry):
        m, alpha = carry
        if not isinstance(t, int) or t + 1 < n_blk:
            scores(t + 1, 1 - par)
        if not isinstance(t, int) or t >= 1:
            pv(t - 1, 1 - par, alpha)
        return softmax(par, m)

    carry = (jnp.full((1, nq), NEG_BIG, F32), jnp.zeros((1, nq), F32))
    scores(0, 0)
    lo_t = min(n_cb + 1, n_blk)
    n_loop = max(n_blk - 1 - lo_t, 0)
    if n_loop % 2:
        lo_t += 1
        n_loop -= 1
    for t in range(lo_t):
        carry = stage(t, t % 2, carry)

    def pair(i, carry):
        t = lo_t + 2 * i
        carry = stage(t, lo_t % 2, carry)
        return stage(t + 1, (lo_t + 1) % 2, carry)

    carry = lax.fori_loop(0, n_loop // 2, pair, carry)
    for t in range(lo_t + n_loop, n_blk):
        carry = stage(t, t % 2, carry)
    pv(n_blk - 1, (n_blk - 1) % 2, carry[1])

    acc = acc_ref[...]
    oT = acc[0:HEAD_DIM] / acc[HEAD_DIM:HEAD_DIM + 1]
    o4 = jnp.concatenate([oT[:, g * tq:(g + 1) * tq] for g in range(GROUP)], axis=0)
    o_ref[...] = o4.T.astype(BF16)


def _attention(q, kt, vT, cache, seq_len):
    n = q.shape[0]
    n_batch = n // seq_len
    tq = ATT_TQ
    nq = seq_len // tq
    n_cache = 0 if cache is None else cache[0].shape[1]
    in_specs = [pl.BlockSpec((tq, KV_LANES), lambda b, kv, i: (b * nq + i, kv))]
    args = [q]
    if n_cache:
        in_specs += [pl.BlockSpec((None, n_cache, KV_LANES), lambda b, kv, i: (b, 0, kv)),
                     pl.BlockSpec((None, HEAD_DIM, n_cache), lambda b, kv, i: (b, kv, 0))]
        args += list(cache)
    in_specs += [pl.BlockSpec((seq_len, KV_LANES), lambda b, kv, i: (b, kv)),
                 pl.BlockSpec((HEAD_DIM, seq_len), lambda b, kv, i: (kv, b))]
    args += [kt, vT]
    return pl.pallas_call(
        functools.partial(_attention_kernel, n_cache=n_cache, n_new=seq_len),
        grid=(n_batch, N_KV_HEADS, nq),
        in_specs=in_specs,
        out_specs=pl.BlockSpec((tq, KV_LANES), lambda b, kv, i: (b * nq + i, kv)),
        out_shape=jax.ShapeDtypeStruct((n, ATT_Q_W), BF16),
        scratch_shapes=[
            pltpu.VMEM((GROUP * tq, KV_LANES), BF16),
            pltpu.VMEM((HEAD_DIM + SUM_ROWS, GROUP * tq), F32),
            pltpu.VMEM((min(ATT_KB, seq_len), GROUP * tq), F32),
            pltpu.VMEM((min(ATT_KB, seq_len), GROUP * tq), F32),
            pltpu.VMEM((min(ATT_KB, seq_len), GROUP * tq), BF16),
            pltpu.VMEM((min(ATT_KB, seq_len), GROUP * tq), BF16),
        ],
        compiler_params=pltpu.CompilerParams(
            dimension_semantics=("arbitrary", "arbitrary", "arbitrary"), vmem_limit_bytes=VMEM_LIMIT),
        name="attention",
    )(*args)


MERGE_TM = 512


def _merge_kernel(h_ref, mp_ref, ro_ref, ao_ref, cb_ref, u_ref, up_ref, un_ref, g_ref, cw_ref,
                  wr_ref, wa_ref, wc_ref, wo_ref, o_ref, *, seq_len):
    tm = h_ref.shape[0]
    i = pl.program_id(0)
    u = u_ref[...].astype(F32)
    r = lax.broadcasted_iota(jnp.int32, u.shape, 0)
    t = (i * tm + r) % seq_len
    prev_row = up_ref[BF16_SUBLANES - 1:BF16_SUBLANES, :].astype(F32)
    next_row = un_ref[0:1, :].astype(F32)
    u_prev = jnp.where(r == 0, prev_row, pltpu.roll(u, 1, axis=0))
    u_next = jnp.where(r == tm - 1, next_row, pltpu.roll(u, tm - 1, axis=0))
    u_prev = jnp.where(t == 0, 0.0, u_prev)
    u_next = jnp.where(t == seq_len - 1, 0.0, u_next)
    cw = cw_ref[...]
    conv = u_prev * cw[0:1] + u * cw[1:2] + u_next * cw[2:3]
    yc = _dot((cb_ref[...].astype(F32) * conv).astype(BF16), wc_ref[...])
    yr = _dot(ro_ref[...], wr_ref[...])
    ya = _dot(ao_ref[...], wa_ref[...])
    merged = (g_ref[:, 0:D_MODEL].astype(F32) * yr
              + g_ref[:, D_MODEL:2 * D_MODEL].astype(F32) * ya
              + g_ref[:, 2 * D_MODEL:3 * D_MODEL].astype(F32) * yc)
    y = _dot(merged.astype(BF16), wo_ref[...])
    o_ref[...] = h_ref[...] + mp_ref[2:3, :] * y


def _merge(h, mp, rows_per_mod, ret_o, att_o, cb, u, gates, conv_w8, w_ret_o, w_att_o, w_conv_o, w_o,
           l, seq_len):
    n = h.shape[0]
    tm = MERGE_TM
    hb = tm // BF16_SUBLANES
    last = n // BF16_SUBLANES - 1
    row = lambda i: (i, 0)
    wspec = lambda k: _resident((None, k, D_MODEL), lambda i: (l, 0, 0))
    return pl.pallas_call(
        functools.partial(_merge_kernel, seq_len=seq_len),
        grid=(n // tm,),
        in_specs=[
            pl.BlockSpec((tm, D_MODEL), row),
            pl.BlockSpec((None, 8, D_MODEL), lambda i: ((i * tm) // rows_per_mod, 0, 0)),
            pl.BlockSpec((tm, RET_W), row),
            pl.BlockSpec((tm, ATT_Q_W), row),
            pl.BlockSpec((tm, CONV_DIM), row),
            pl.BlockSpec((tm, CONV_DIM), row),
            pl.BlockSpec((BF16_SUBLANES, CONV_DIM), lambda i: (jnp.maximum(i * hb - 1, 0), 0)),
            pl.BlockSpec((BF16_SUBLANES, CONV_DIM), lambda i: (jnp.minimum((i + 1) * hb, last), 0)),
            pl.BlockSpec((tm, 3 * D_MODEL), row),
            pl.BlockSpec((None, 8, CONV_DIM), lambda i: (l, 0, 0)),
            wspec(RET_W), wspec(ATT_Q_W), wspec(CONV_DIM), wspec(D_MODEL),
        ],
        out_specs=pl.BlockSpec((tm, D_MODEL), row),
        out_shape=jax.ShapeDtypeStruct((n, D_MODEL), F32),
        compiler_params=pltpu.CompilerParams(
            dimension_semantics=("arbitrary",), vmem_limit_bytes=VMEM_LIMIT),
        name="merge",
    )(h, mp, ret_o, att_o, cb, u, u, u, gates, conv_w8, w_ret_o, w_att_o, w_conv_o, w_o)


def _rope_tables(n_tok):
    rows = n_tok // GRID_W
    t_row = np.repeat(np.arange(rows, dtype=np.float64), GRID_W)
    t_col = np.tile(np.arange(GRID_W, dtype=np.float64), rows)
    n_freq = HEAD_DIM // 4
    inv = ROPE_THETA ** (-np.arange(n_freq, dtype=np.float64) / n_freq)
    ang = np.concatenate([t_row[:, None] * inv, t_col[:, None] * inv], axis=-1)
    cos, sin = np.cos(ang), np.sin(ang)
    cos64 = np.concatenate([cos, cos], axis=-1)
    sin64 = np.concatenate([-sin, sin], axis=-1)
    return (jnp.asarray(np.tile(cos64, (1, LANES // HEAD_DIM)), F32),
            jnp.asarray(np.tile(sin64, (1, LANES // HEAD_DIM)), F32))


def _mod_pack(mod_l, rows, sub, norm_w_row):
    r0 = rows[0]
    nr = len(rows)
    m = mod_l[r0:r0 + nr, 3 * sub * D_MODEL:3 * (sub + 1) * D_MODEL].reshape(nr, 3, D_MODEL)
    nw = jnp.broadcast_to(norm_w_row[None, None, :], (nr, 1, D_MODEL))
    pad = jnp.zeros((nr, 4, D_MODEL), F32)
    return jnp.concatenate([m, nw, pad], axis=1)


def kernel(x_prompt, x_sample, c, state_ret, cache_k, cache_v, c_ctx, w_ada, b_ada, norm_w, w_ffn_in,
           w_ffn_out, w_in, ret_decay_logit, ret_gn, q_gain, k_gain, conv_w, w_ret_o, w_att_o, w_conv_o,
           w_o):
    n_ctx_b, ctx_len, _ = x_prompt.shape
    n_lat_b, lat_len, _ = x_sample.shape

    w_ffn_in_b = w_ffn_in.astype(BF16)
    w_ffn_out_b = w_ffn_out.astype(BF16)
    w_in_b = w_in.astype(BF16)
    w_ret_o_b = w_ret_o.astype(BF16)
    w_att_o_b = w_att_o.astype(BF16)
    w_conv_o_b = w_conv_o.astype(BF16)
    w_o_b = w_o.astype(BF16)

    q_gain_t = jnp.tile(q_gain, (1, N_HEADS)).reshape(DEPTH, 1, ATT_Q_W)
    k_gain_t = jnp.tile(k_gain, (1, N_KV_HEADS)).reshape(DEPTH, 1, ATT_KV_W)
    conv_w8 = jnp.pad(conv_w, ((0, 0), (0, 8 - conv_w.shape[1]), (0, 0)))
    gid = np.arange(ATT_Q_W) // HEAD_DIM
    bd = jnp.asarray(gid[:, None] == gid[None, :], BF16)
    rope = _rope_tables(lat_len)

    cond8 = jnp.zeros((8, D_MODEL), F32).at[0].set(c_ctx).at[1:1 + n_lat_b].set(c)
    mod = _ada(cond8, w_ada, b_ada)

    ck = cache_k.astype(BF16)
    ck = jnp.broadcast_to(ck[:, :, :, :, None, :], ck.shape[:4] + (GROUP, HEAD_DIM))
    cache_kt = ck.reshape(ck.shape[0], ck.shape[1], ck.shape[2], N_KV_HEADS * KV_LANES)
    cache_vT = jnp.transpose(cache_v.astype(BF16), (0, 1, 3, 4, 2)).reshape(
        cache_v.shape[0], cache_v.shape[1], ATT_KV_W, cache_v.shape[2])

    groups = (
        dict(x=x_prompt.reshape(n_ctx_b * ctx_len, D_MODEL), rows=[0], seq=ctx_len, ctx=True),
        dict(x=x_sample.reshape(n_lat_b * lat_len, D_MODEL), rows=list(range(1, 1 + n_lat_b)),
             seq=lat_len, ctx=False),
    )
    results = []
    for grp in groups:
        h = grp["x"]
        seq = grp["seq"]
        is_ctx = grp["ctx"]
        rpm = h.shape[0] // len(grp["rows"])
        states, keys, values = [], [], []
        for l in range(DEPTH):
            mp = [_mod_pack(mod[l], grp["rows"], s, norm_w[l, s]) for s in range(3)]
            h = _ffn(h, mp[0], rpm, w_ffn_in_b, w_ffn_out_b, l, 0)
            outs = _inproj(h, mp[1], rpm, w_in_b, l, q_gain_t, k_gain_t, bd,
                           None if is_ctx else rope, seq, emit_kv=is_ctx)
            ret_in, qn, kt, vT, cb, u, gates = outs[:7]
            if is_ctx:
                ret_o, st = _retention(ret_in, ret_decay_logit[l], ret_gn[l][None, :], None, seq,
                                       n_seq_blk=8, emit_state=True)
                states.append(st)
                keys.append(outs[7].reshape(n_ctx_b, seq, N_KV_HEADS, HEAD_DIM))
                values.append(outs[8].reshape(n_ctx_b, seq, N_KV_HEADS, HEAD_DIM))
                att_o = _attention(qn, kt, vT, None, seq)
            else:
                (ret_o,) = _retention(ret_in, ret_decay_logit[l], ret_gn[l][None, :], state_ret[:, l],
                                      seq, n_seq_blk=1, emit_state=False)
                att_o = _attention(qn, kt, vT, (cache_kt[:, l], cache_vT[:, l]), seq)
            h = _merge(h, mp[1], rpm, ret_o, att_o, cb, u, gates, conv_w8, w_ret_o_b, w_att_o_b,
                       w_conv_o_b, w_o_b, l, seq)
            h = _ffn(h, mp[2], rpm, w_ffn_in_b, w_ffn_out_b, l, 1)
        results.append((h, states, keys, values))

    (y_ctx, states, keys, values), (y_lat, _, _, _) = results
    y_prompt = y_ctx.reshape(x_prompt.shape)
    y_sample = y_lat.reshape(x_sample.shape)
    new_state_ret = jnp.stack(states, axis=1)
    new_cache_k = jnp.stack(keys, axis=1)
    new_cache_v = jnp.stack(values, axis=1)
    return (y_prompt, y_sample, new_state_ret, new_cache_k, new_cache_v)
```

```python
import functools
import math

import numpy as np
import jax
import jax.numpy as jnp
from jax import lax
from jax.experimental import pallas as pl
from jax.experimental.pallas import tpu as pltpu

D_MODEL = 1024
DEPTH = 2
GRID_W = 64
N_RET_HEADS = 4
RET_DK = 128
RET_DV = 128
RET_CHUNK = 128
N_HEADS = 8
N_KV_HEADS = 2
HEAD_DIM = 64
ROPE_THETA = 10000.0
CONV_DIM = 512
FFN_DIM = 2816
N_MOD = 9
EPS = 1e-6
RET_W = N_RET_HEADS * RET_DK
ATT_Q_W = N_HEADS * HEAD_DIM
ATT_KV_W = N_KV_HEADS * HEAD_DIM
GROUP = N_HEADS // N_KV_HEADS
KV_LANES = GROUP * HEAD_DIM
OFF_RQ, OFF_RK, OFF_RV, OFF_RG = 0, 512, 1024, 1536
OFF_AQ, OFF_AK, OFF_AV = 2048, 2560, 2688
OFF_CB, OFF_CC, OFF_CX = 2816, 3328, 3840
OFF_GATES = 4352
IN_COLS = 7424

LANES = 128
BF16_SUBLANES = 16
VMEM_LIMIT = 56 * 1024 * 1024

F32 = jnp.float32
BF16 = jnp.bfloat16


def _dot(a, b):
    return jnp.dot(a, b, preferred_element_type=F32)


def _dot_nt(a, b):
    return lax.dot_general(a, b, (((1,), (1,)), ((), ())), preferred_element_type=F32)


def _sigmoid(x):
    return 1.0 / (1.0 + jnp.exp(-x))


def _silu(x):
    return x * _sigmoid(x)


def _mod_norm(x, mp):
    ms = jnp.mean(x * x, axis=-1, keepdims=True)
    y = x * lax.rsqrt(ms + EPS)
    return (y * mp[3:4]) * (1.0 + mp[1:2]) + mp[0:1]


def _resident(shape, index):
    return pl.BlockSpec(shape, index, pipeline_mode=pl.Buffered(1))


ADA_TN = 1152


def _ada_kernel(c_ref, w_ref, b_ref, o_ref):
    a = _silu(c_ref[...]).astype(BF16)
    o_ref[...] = _dot(a, w_ref[...].astype(BF16)) + b_ref[...]


def _ada(cond8, w_ada, b_ada):
    n = N_MOD * D_MODEL
    return pl.pallas_call(
        _ada_kernel,
        grid=(DEPTH, n // ADA_TN),
        in_specs=[
            pl.BlockSpec((8, D_MODEL), lambda l, j: (0, 0)),
            pl.BlockSpec((None, D_MODEL, ADA_TN), lambda l, j: (l, 0, j)),
            pl.BlockSpec((None, 1, ADA_TN), lambda l, j: (l, 0, j)),
        ],
        out_specs=pl.BlockSpec((None, 8, ADA_TN), lambda l, j: (l, 0, j)),
        out_shape=jax.ShapeDtypeStruct((DEPTH, 8, n), F32),
        compiler_params=pltpu.CompilerParams(
            dimension_semantics=("arbitrary", "arbitrary"), vmem_limit_bytes=VMEM_LIMIT),
        name="ada",
    )(cond8, w_ada, b_ada.reshape(DEPTH, 1, n))


FFN_TM = 1024
FFN_FC = 256


def _ffn_kernel(x_ref, mp_ref, wi_ref, wo_ref, o_ref, h_ref):
    x = x_ref[...]
    mp = mp_ref[...]
    xb = _mod_norm(x, mp).astype(BF16)
    for c in range(FFN_DIM // FFN_FC):
        lo = c * FFN_FC
        g = _dot(xb, wi_ref[:, lo:lo + FFN_FC])
        u = _dot(xb, wi_ref[:, FFN_DIM + lo:FFN_DIM + lo + FFN_FC])
        h_ref[:, lo:lo + FFN_FC] = (_silu(g) * u).astype(BF16)
    y = _dot(h_ref[...], wo_ref[...])
    o_ref[...] = x + (0.5 * mp[2:3]) * y


def _ffn(x, mp, rows_per_mod, w_in, w_out, l, j):
    n = x.shape[0]
    tm = FFN_TM
    return pl.pallas_call(
        _ffn_kernel,
        grid=(n // tm,),
        in_specs=[
            pl.BlockSpec((tm, D_MODEL), lambda i: (i, 0)),
            pl.BlockSpec((None, 8, D_MODEL), lambda i: ((i * tm) // rows_per_mod, 0, 0)),
            _resident((None, None, D_MODEL, 2 * FFN_DIM), lambda i: (l, j, 0, 0)),
            _resident((None, None, FFN_DIM, D_MODEL), lambda i: (l, j, 0, 0)),
        ],
        out_specs=pl.BlockSpec((tm, D_MODEL), lambda i: (i, 0)),
        out_shape=jax.ShapeDtypeStruct((n, D_MODEL), F32),
        scratch_shapes=[pltpu.VMEM((tm, FFN_DIM), BF16)],
        compiler_params=pltpu.CompilerParams(
            dimension_semantics=("arbitrary",), vmem_limit_bytes=VMEM_LIMIT),
        name="ffn",
    )(x, mp, w_in, w_out)


INPROJ_TM = 512
Q_SCALE = HEAD_DIM ** -0.5 * math.log2(math.e)


def _group_rms(a, bd, gain):
    sq = a * a
    hi = sq.astype(BF16)
    lo = (sq - hi.astype(F32)).astype(BF16)
    ss = _dot(hi, bd) + _dot(lo, bd)
    return (a * lax.rsqrt(ss * (1.0 / HEAD_DIM) + EPS)) * gain


def _rope(x, cos, sin_signed):
    half = HEAD_DIM // 2
    reps = x.shape[1] // LANES
    lane = lax.broadcasted_iota(jnp.int32, (x.shape[0], LANES), 1)
    first = (lane % HEAD_DIM) < half
    out = []
    for r in range(reps):
        xs = x[:, r * LANES:(r + 1) * LANES]
        partner = jnp.where(first, pltpu.roll(xs, LANES - half, axis=1), pltpu.roll(xs, half, axis=1))
        out.append(xs * cos + partner * sin_signed)
    return out[0] if reps == 1 else jnp.concatenate(out, axis=1)


def _tile_kv(a):
    lane = lax.broadcasted_iota(jnp.int32, a.shape, 1)
    sw = pltpu.roll(a, HEAD_DIM, axis=1)
    h0 = jnp.where(lane < HEAD_DIM, a, sw)
    h1 = jnp.where(lane < HEAD_DIM, sw, a)
    return jnp.concatenate([h0, h0, h1, h1], axis=1)


def _inproj_kernel(*refs, use_rope, emit_kv):
    x_ref, mp_ref, w_ref, qg_ref, kg_ref, bd_ref = refs[:6]
    pos = 6
    if use_rope:
        cos_ref, sin_ref = refs[pos:pos + 2]
        pos += 2
    ret_ref, q_ref, kt_ref, vT_ref, cb_ref, u_ref, gates_ref = refs[pos:pos + 7]
    pos += 7
    if emit_kv:
        kout_ref, vout_ref = refs[pos:pos + 2]

    xb = _mod_norm(x_ref[...], mp_ref[...]).astype(BF16)

    def proj(off, width):
        return _dot(xb, w_ref[:, off:off + width])

    ret_ref[:, OFF_RQ:OFF_RQ + RET_W] = proj(OFF_RQ, RET_W).astype(BF16)
    ret_ref[:, OFF_RK:OFF_RK + RET_W] = (proj(OFF_RK, RET_W) * (RET_DK ** -0.5)).astype(BF16)
    ret_ref[:, OFF_RV:OFF_RV + RET_W] = proj(OFF_RV, RET_W).astype(BF16)
    ret_ref[:, OFF_RG:OFF_RG + RET_W] = _silu(proj(OFF_RG, RET_W)).astype(BF16)

    q = _group_rms(proj(OFF_AQ, ATT_Q_W), bd_ref[...], qg_ref[...])
    if use_rope:
        q = _rope(q, cos_ref[...], sin_ref[...])
    q_ref[...] = (q * Q_SCALE).astype(BF16)

    k = _group_rms(proj(OFF_AK, ATT_KV_W), bd_ref[0:ATT_KV_W, 0:ATT_KV_W], kg_ref[...])
    v = proj(OFF_AV, ATT_KV_W)
    if emit_kv:
        kout_ref[...] = k
        vout_ref[...] = v
    if use_rope:
        k = _rope(k, cos_ref[...], sin_ref[...])
    kt_ref[...] = _tile_kv(k).astype(BF16)
    vT_ref[...] = v.T.astype(BF16)

    cb_ref[...] = proj(OFF_CB, CONV_DIM).astype(BF16)
    u_ref[...] = (proj(OFF_CC, CONV_DIM) * proj(OFF_CX, CONV_DIM)).astype(BF16)

    for c in range(3 * D_MODEL // 512):
        gates_ref[:, c * 512:(c + 1) * 512] = _sigmoid(proj(OFF_GATES + c * 512, 512)).astype(BF16)


def _inproj(x, mp, rows_per_mod, w_in, l, q_gain_t, k_gain_t, bd, rope, seq_len, emit_kv):
    n = x.shape[0]
    tm = INPROJ_TM
    use_rope = rope is not None
    row = lambda i: (i, 0)
    in_specs = [
        pl.BlockSpec((tm, D_MODEL), row),
        pl.BlockSpec((None, 8, D_MODEL), lambda i: ((i * tm) // rows_per_mod, 0, 0)),
        _resident((None, D_MODEL, IN_COLS), lambda i: (l, 0, 0)),
        pl.BlockSpec((None, 1, ATT_Q_W), lambda i: (l, 0, 0)),
        pl.BlockSpec((None, 1, ATT_KV_W), lambda i: (l, 0, 0)),
        _resident((ATT_Q_W, ATT_Q_W), lambda i: (0, 0)),
    ]
    args = [x, mp, w_in, q_gain_t, k_gain_t, bd]
    if use_rope:
        tiles_per_seq = seq_len // tm
        in_specs += [pl.BlockSpec((tm, LANES), lambda i: (i % tiles_per_seq, 0))] * 2
        args += list(rope)
    widths = [4 * RET_W, ATT_Q_W, ATT_Q_W, None, CONV_DIM, CONV_DIM, 3 * D_MODEL]
    out_specs = [pl.BlockSpec((tm, w), row) for w in widths if w]
    out_shape = [jax.ShapeDtypeStruct((n, w), BF16) for w in widths if w]
    out_specs.insert(3, pl.BlockSpec((ATT_KV_W, tm), lambda i: (0, i)))
    out_shape.insert(3, jax.ShapeDtypeStruct((ATT_KV_W, n), BF16))
    if emit_kv:
        out_specs += [pl.BlockSpec((tm, ATT_KV_W), row)] * 2
        out_shape += [jax.ShapeDtypeStruct((n, ATT_KV_W), F32)] * 2
    return pl.pallas_call(
        functools.partial(_inproj_kernel, use_rope=use_rope, emit_kv=emit_kv),
        grid=(n // tm,),
        in_specs=in_specs,
        out_specs=out_specs,
        out_shape=out_shape,
        compiler_params=pltpu.CompilerParams(
            dimension_semantics=("arbitrary",), vmem_limit_bytes=VMEM_LIMIT),
        name="inproj",
    )(*args)


RET_CHUNK_UNROLL = 8
RET_SEQ_UNROLL = 4
TAB_DMASK, TAB_DQF, TAB_DQB, TAB_DKF, TAB_DKB, TAB_DCF, TAB_DCB, N_TAB = range(8)


def _log_sigmoid(x):
    return jnp.minimum(x, 0.0) - jnp.log(1.0 + jnp.exp(-jnp.abs(x)))


def _chunk_rows(start):
    if isinstance(start, int):
        return pl.ds(start, RET_CHUNK)
    return pl.ds(pl.multiple_of(start, RET_CHUNK), RET_CHUNK)


def _for_range(n, unroll, body):
    if n <= unroll:
        for i in range(n):
            body(i)
        return
    assert n % unroll == 0

    def step(t, carry):
        for j in range(unroll):
            body(t * unroll + j)
        return carry

    lax.fori_loop(0, n // unroll, step, 0)


def _retention_kernel(*refs, n_seq, n_chunks, has_s0, emit_state):
    dl_ref, q_ref, k_ref, v_ref, g_ref, gn_ref = refs[:6]
    pos = 6
    if has_s0:
        s0_ref = refs[pos]
        pos += 1
    o_ref = refs[pos]
    pos += 1
    if emit_state:
        st_ref = refs[pos]
        pos += 1
    tab_ref, kv_ref, ent_ref = refs[pos:pos + 3]

    C = RET_CHUNK
    hd = pl.program_id(1)
    row = lax.broadcasted_iota(jnp.int32, (C, C), 0).astype(F32)
    col = lax.broadcasted_iota(jnp.int32, (C, C), 1).astype(F32)
    diff = row - col
    lgf = _log_sigmoid(jnp.full((C, C), dl_ref[0, hd], F32))
    lgb = _log_sigmoid(jnp.full((C, C), dl_ref[1, hd], F32))
    tab_ref[TAB_DMASK] = (jnp.where(diff >= 0, jnp.exp(lgf * jnp.maximum(diff, 0.0)), 0.0)
                          + jnp.where(diff <= 0, jnp.exp(lgb * jnp.maximum(-diff, 0.0)), 0.0))
    tab_ref[TAB_DQF] = jnp.exp(lgf * (row + 1.0))
    tab_ref[TAB_DQB] = jnp.exp(lgb * (C - row))
    tab_ref[TAB_DKF] = jnp.exp(lgf * (C - 1.0 - row))
    tab_ref[TAB_DKB] = jnp.exp(lgb * row)
    tab_ref[TAB_DCF] = jnp.exp(lgf * C)
    tab_ref[TAB_DCB] = jnp.exp(lgb * C)

    def one_sequence(s, slot):
        base = s * (n_chunks * C)
        sc = slot * n_chunks

        def kv_chunk(c):
            rows = _chunk_rows(base + c * C)
            k = k_ref[rows, :].astype(F32)
            vT = v_ref[rows, :].astype(F32).T.astype(BF16)
            kk = jnp.concatenate([(k * tab_ref[TAB_DKF]).astype(BF16),
                                  (k * tab_ref[TAB_DKB]).astype(BF16)], axis=1)
            kv_ref[sc + c] = _dot(vT, kk)

        _for_range(n_chunks, RET_CHUNK_UNROLL, kv_chunk)

        if has_s0:
            st0 = (s0_ref[0].T, s0_ref[1].T)
        else:
            st0 = (jnp.zeros((RET_DV, RET_DK), F32), jnp.zeros((RET_DV, RET_DK), F32))

        def scan_step(t, st):
            st_f, st_b = st
            cb = n_chunks - 1 - t
            ent_ref[sc + t, :, 0:RET_DK] = st_f.astype(BF16)
            ent_ref[sc + cb, :, RET_DK:2 * RET_DK] = st_b.astype(BF16)
            st_f = st_f * tab_ref[TAB_DCF] + kv_ref[sc + t, :, 0:RET_DK]
            st_b = st_b * tab_ref[TAB_DCB] + kv_ref[sc + cb, :, RET_DK:2 * RET_DK]
            return st_f, st_b

        if n_chunks <= RET_CHUNK_UNROLL:
            st = st0
            for t in range(n_chunks):
                st = scan_step(t, st)
        else:
            st = lax.fori_loop(0, n_chunks, scan_step, st0)
        if emit_state:
            st_ref[s, 0] = st[0].T
            st_ref[s, 1] = st[1].T

        def out_chunk(c):
            rows = _chunk_rows(base + c * C)
            q = q_ref[rows, :]
            att = (_dot_nt(q, k_ref[rows, :]) * tab_ref[TAB_DMASK]).astype(BF16)
            qf = q.astype(F32)
            qq = jnp.concatenate([(qf * tab_ref[TAB_DQF]).astype(BF16),
                                  (qf * tab_ref[TAB_DQB]).astype(BF16)], axis=1)
            o = _dot(att, v_ref[rows, :]) + _dot_nt(qq, ent_ref[sc + c])
            ms = jnp.mean(o * o, axis=-1, keepdims=True)
            on = (o * lax.rsqrt(ms + EPS)) * gn_ref[...]
            o_ref[rows, :] = (g_ref[rows, :].astype(F32) * on).astype(BF16)

        _for_range(n_chunks, RET_CHUNK_UNROLL, out_chunk)

    if n_seq == 1:
        one_sequence(0, 0)
    else:
        assert n_seq % RET_SEQ_UNROLL == 0

        def seq_step(t, carry):
            for j in range(RET_SEQ_UNROLL):
                one_sequence(t * RET_SEQ_UNROLL + j, j)
            return carry

        lax.fori_loop(0, n_seq // RET_SEQ_UNROLL, seq_step, 0)


def _retention(ret_in, decay_logit_l, gn_l, s0, seq_len, n_seq_blk, emit_state):
    n = ret_in.shape[0]
    n_chunks = seq_len // RET_CHUNK
    tb = n_seq_blk * seq_len
    has_s0 = s0 is not None
    n_off = RET_W // RET_DK

    def branch(k):
        return pl.BlockSpec((tb, RET_DK), lambda i, h: (i, k * n_off + h))

    in_specs = [pl.BlockSpec(memory_space=pltpu.SMEM), branch(0), branch(1), branch(2), branch(3),
                pl.BlockSpec((1, RET_DV), lambda i, h: (0, h))]
    args = [decay_logit_l, ret_in, ret_in, ret_in, ret_in, gn_l]
    if has_s0:
        assert n_seq_blk == 1
        in_specs.append(pl.BlockSpec((None, 2, None, RET_DK, RET_DV), lambda i, h: (i, 0, h, 0, 0)))
        args.append(s0)
    out_specs = [pl.BlockSpec((tb, RET_DV), lambda i, h: (i, h))]
    out_shape = [jax.ShapeDtypeStruct((n, RET_W), BF16)]
    if emit_state:
        out_specs.append(pl.BlockSpec((n_seq_blk, 2, None, RET_DK, RET_DV), lambda i, h: (i, 0, h, 0, 0)))
        out_shape.append(jax.ShapeDtypeStruct((n // seq_len, 2, N_RET_HEADS, RET_DK, RET_DV), F32))
    n_slots = n_chunks * (1 if n_seq_blk == 1 else RET_SEQ_UNROLL)
    return pl.pallas_call(
        functools.partial(_retention_kernel, n_seq=n_seq_blk, n_chunks=n_chunks, has_s0=has_s0,
                          emit_state=emit_state),
        grid=(n // tb, N_RET_HEADS),
        in_specs=in_specs,
        out_specs=out_specs,
        out_shape=out_shape,
        scratch_shapes=[
            pltpu.VMEM((N_TAB, RET_CHUNK, RET_CHUNK), F32),
            pltpu.VMEM((n_slots, RET_DV, 2 * RET_DK), F32),
            pltpu.VMEM((n_slots, RET_DV, 2 * RET_DK), BF16),
        ],
        compiler_params=pltpu.CompilerParams(
            dimension_semantics=("arbitrary", "arbitrary"), vmem_limit_bytes=VMEM_LIMIT),
        name="retention",
    )(*args)


ATT_TQ = 256
ATT_KB = 512
ATT_RC = 16
NEG_BIG = -1e30
SUM_ROWS = BF16_SUBLANES


def _attention_kernel(*refs, n_cache, n_new):
    q_ref = refs[0]
    pos = 1
    if n_cache:
        kc_ref, vc_ref = refs[pos:pos + 2]
        pos += 2
    kn_ref, vn_ref, o_ref, qs_ref, acc_ref = refs[pos:pos + 5]
    s_slots = refs[pos + 5:pos + 7]
    p_slots = refs[pos + 7:pos + 9]

    tq = q_ref.shape[0]
    nq = GROUP * tq
    qf = q_ref[...].astype(F32)
    head = lax.broadcasted_iota(jnp.int32, (tq, KV_LANES), 1) // HEAD_DIM
    for g in range(GROUP):
        qs_ref[:, g * tq:(g + 1) * tq] = jnp.where(head == g, qf, 0.0).T.astype(BF16)
    acc_ref[...] = jnp.zeros(acc_ref.shape, F32)

    kb = min(ATT_KB, n_new)
    n_cb = n_cache // kb
    n_nb = n_new // kb
    assert n_cache % kb == 0 and n_new % kb == 0

    def span(lo):
        return pl.ds(lo, kb) if isinstance(lo, int) else pl.ds(pl.multiple_of(lo, kb), kb)

    n_blk = n_cb + n_nb

    def block(t):
        if isinstance(t, int) and t < n_cb:
            return kc_ref, vc_ref, t * kb
        return kn_ref, vn_ref, (t - n_cb) * kb

    def scores(t, par):
        k_ref, _, lo = block(t)
        s = _dot(k_ref[span(lo), :], qs_ref[...])
        s_slots[par][...] = s
        return jnp.max(s, axis=0, keepdims=True)

    def softmax(par, m, s_max):
        s_ref, p_ref = s_slots[par], p_slots[par]
        m_new = jnp.maximum(m, s_max)
        m_rows = jnp.broadcast_to(m_new, (ATT_RC, nq))
        for r in range(0, kb, ATT_RC):
            p_ref[r:r + ATT_RC, :] = jnp.exp2(s_ref[r:r + ATT_RC, :] - m_rows).astype(BF16)
        return m_new, jnp.exp2(m - m_new)

    def pv(t, par, alpha):
        _, vT_ref, lo = block(t)
        v1 = jnp.concatenate([vT_ref[:, span(lo)], jnp.ones((SUM_ROWS, kb), BF16)], axis=0)
        acc_ref[...] = alpha * acc_ref[...] + _dot(v1, p_slots[par][...])

    def stage(t, par, carry):
        m, alpha, s_max = carry
        next_max = s_max
        if not isinstance(t, int) or t + 1 < n_blk:
            next_max = scores(t + 1, 1 - par)
        if not isinstance(t, int) or t >= 1:
            pv(t - 1, 1 - par, alpha)
        return softmax(par, m, s_max) + (next_max,)

    carry = (jnp.full((1, nq), NEG_BIG, F32), jnp.zeros((1, nq), F32), scores(0, 0))
    lo_t = min(n_cb + 1, n_blk)
    n_loop = max(n_blk - 1 - lo_t, 0)
    if n_loop % 2:
        lo_t += 1
        n_loop -= 1
    for t in range(lo_t):
        carry = stage(t, t % 2, carry)

    def pair(i, carry):
        t = lo_t + 2 * i
        carry = stage(t, lo_t % 2, carry)
        return stage(t + 1, (lo_t + 1) % 2, carry)

    carry = lax.fori_loop(0, n_loop // 2, pair, carry)
    for t in range(lo_t + n_loop, n_blk):
        carry = stage(t, t % 2, carry)
    pv(n_blk - 1, (n_blk - 1) % 2, carry[1])

    acc = acc_ref[...]
    oT = acc[0:HEAD_DIM] / acc[HEAD_DIM:HEAD_DIM + 1]
    o4 = jnp.concatenate([oT[:, g * tq:(g + 1) * tq] for g in range(GROUP)], axis=0)
    o_ref[...] = o4.T.astype(BF16)


ATT_SEQ_PER_STEP = 2


def _attention_short_kernel(q_ref, k_ref, vT_ref, o_ref, *, n_seq):
    t_len = q_ref.shape[0] // n_seq
    head = lax.broadcasted_iota(jnp.int32, (t_len, KV_LANES), 1) // HEAD_DIM
    ones = jnp.ones((SUM_ROWS, t_len), BF16)
    for s in range(n_seq):
        rows = slice(s * t_len, (s + 1) * t_len)
        for kv in range(N_KV_HEADS):
            cols = slice(kv * KV_LANES, (kv + 1) * KV_LANES)
            qf = q_ref[rows, cols].astype(F32)
            qsT = jnp.concatenate([jnp.where(head == g, qf, 0.0).T.astype(BF16) for g in range(GROUP)],
                                  axis=1)
            sT = _dot(k_ref[rows, cols], qsT)
            pT = jnp.exp2(sT - jnp.max(sT, axis=0, keepdims=True)).astype(BF16)
            v1 = jnp.concatenate([vT_ref[kv * HEAD_DIM:(kv + 1) * HEAD_DIM, rows], ones], axis=0)
            acc = _dot(v1, pT)
            oT = acc[0:HEAD_DIM] / acc[HEAD_DIM:HEAD_DIM + 1]
            o4 = jnp.concatenate([oT[:, g * t_len:(g + 1) * t_len] for g in range(GROUP)], axis=0)
            o_ref[rows, cols] = o4.T.astype(BF16)


def _attention_short(q, kt, vT, seq_len):
    n = q.shape[0]
    tb = ATT_SEQ_PER_STEP * seq_len
    return pl.pallas_call(
        functools.partial(_attention_short_kernel, n_seq=ATT_SEQ_PER_STEP),
        grid=(n // tb,),
        in_specs=[pl.BlockSpec((tb, ATT_Q_W), lambda i: (i, 0)),
                  pl.BlockSpec((tb, ATT_Q_W), lambda i: (i, 0)),
                  pl.BlockSpec((ATT_KV_W, tb), lambda i: (0, i))],
        out_specs=pl.BlockSpec((tb, ATT_Q_W), lambda i: (i, 0)),
        out_shape=jax.ShapeDtypeStruct((n, ATT_Q_W), BF16),
        compiler_params=pltpu.CompilerParams(
            dimension_semantics=("arbitrary",), vmem_limit_bytes=VMEM_LIMIT),
        name="attention_short",
    )(q, kt, vT)


def _attention(q, kt, vT, cache, seq_len):
    n = q.shape[0]
    n_batch = n // seq_len
    tq = ATT_TQ
    nq = seq_len // tq
    n_cache = 0 if cache is None else cache[0].shape[1]
    if n_cache == 0 and seq_len <= ATT_KB:
        return _attention_short(q, kt, vT, seq_len)
    in_specs = [pl.BlockSpec((tq, KV_LANES), lambda b, kv, i: (b * nq + i, kv))]
    args = [q]
    if n_cache:
        in_specs += [pl.BlockSpec((None, n_cache, KV_LANES), lambda b, kv, i: (b, 0, kv)),
                     pl.BlockSpec((None, HEAD_DIM, n_cache), lambda b, kv, i: (b, kv, 0))]
        args += list(cache)
    in_specs += [pl.BlockSpec((seq_len, KV_LANES), lambda b, kv, i: (b, kv)),
                 pl.BlockSpec((HEAD_DIM, seq_len), lambda b, kv, i: (kv, b))]
    args += [kt, vT]
    return pl.pallas_call(
        functools.partial(_attention_kernel, n_cache=n_cache, n_new=seq_len),
        grid=(n_batch, N_KV_HEADS, nq),
        in_specs=in_specs,
        out_specs=pl.BlockSpec((tq, KV_LANES), lambda b, kv, i: (b * nq + i, kv)),
        out_shape=jax.ShapeDtypeStruct((n, ATT_Q_W), BF16),
        scratch_shapes=[
            pltpu.VMEM((KV_LANES, GROUP * tq), BF16),
            pltpu.VMEM((HEAD_DIM + SUM_ROWS, GROUP * tq), F32),
            pltpu.VMEM((min(ATT_KB, seq_len), GROUP * tq), F32),
            pltpu.VMEM((min(ATT_KB, seq_len), GROUP * tq), F32),
            pltpu.VMEM((min(ATT_KB, seq_len), GROUP * tq), BF16),
            pltpu.VMEM((min(ATT_KB, seq_len), GROUP * tq), BF16),
        ],
        compiler_params=pltpu.CompilerParams(
            dimension_semantics=("arbitrary", "arbitrary", "arbitrary"), vmem_limit_bytes=VMEM_LIMIT),
        name="attention",
    )(*args)


MERGE_TM = 1024


def _merge_kernel(h_ref, mp_ref, ro_ref, ao_ref, cb_ref, u_ref, up_ref, un_ref, g_ref, cw_ref,
                  wr_ref, wa_ref, wc_ref, wo_ref, o_ref, *, seq_len):
    tm = h_ref.shape[0]
    i = pl.program_id(0)
    u = u_ref[...].astype(F32)
    r = lax.broadcasted_iota(jnp.int32, u.shape, 0)
    t = (i * tm + r) % seq_len
    prev_row = up_ref[BF16_SUBLANES - 1:BF16_SUBLANES, :].astype(F32)
    next_row = un_ref[0:1, :].astype(F32)
    u_prev = jnp.where(r == 0, prev_row, pltpu.roll(u, 1, axis=0))
    u_next = jnp.where(r == tm - 1, next_row, pltpu.roll(u, tm - 1, axis=0))
    u_prev = jnp.where(t == 0, 0.0, u_prev)
    u_next = jnp.where(t == seq_len - 1, 0.0, u_next)
    cw = cw_ref[...]
    conv = u_prev * cw[0:1] + u * cw[1:2] + u_next * cw[2:3]
    yc = _dot((cb_ref[...].astype(F32) * conv).astype(BF16), wc_ref[...])
    yr = _dot(ro_ref[...], wr_ref[...])
    ya = _dot(ao_ref[...], wa_ref[...])
    merged = (g_ref[:, 0:D_MODEL].astype(F32) * yr
              + g_ref[:, D_MODEL:2 * D_MODEL].astype(F32) * ya
              + g_ref[:, 2 * D_MODEL:3 * D_MODEL].astype(F32) * yc)
    y = _dot(merged.astype(BF16), wo_ref[...])
    o_ref[...] = h_ref[...] + mp_ref[2:3, :] * y


def _merge(h, mp, rows_per_mod, ret_o, att_o, cb, u, gates, conv_w8, w_ret_o, w_att_o, w_conv_o, w_o,
           l, seq_len):
    n = h.shape[0]
    tm = MERGE_TM
    hb = tm // BF16_SUBLANES
    last = n // BF16_SUBLANES - 1
    row = lambda i: (i, 0)
    wspec = lambda k: _resident((None, k, D_MODEL), lambda i: (l, 0, 0))
    return pl.pallas_call(
        functools.partial(_merge_kernel, seq_len=seq_len),
        grid=(n // tm,),
        in_specs=[
            pl.BlockSpec((tm, D_MODEL), row),
            pl.BlockSpec((None, 8, D_MODEL), lambda i: ((i * tm) // rows_per_mod, 0, 0)),
            pl.BlockSpec((tm, RET_W), row),
            pl.BlockSpec((tm, ATT_Q_W), row),
            pl.BlockSpec((tm, CONV_DIM), row),
            pl.BlockSpec((tm, CONV_DIM), row),
            pl.BlockSpec((BF16_SUBLANES, CONV_DIM), lambda i: (jnp.maximum(i * hb - 1, 0), 0)),
            pl.BlockSpec((BF16_SUBLANES, CONV_DIM), lambda i: (jnp.minimum((i + 1) * hb, last), 0)),
            pl.BlockSpec((tm, 3 * D_MODEL), row),
            pl.BlockSpec((None, 8, CONV_DIM), lambda i: (l, 0, 0)),
            wspec(RET_W), wspec(ATT_Q_W), wspec(CONV_DIM), wspec(D_MODEL),
        ],
        out_specs=pl.BlockSpec((tm, D_MODEL), row),
        out_shape=jax.ShapeDtypeStruct((n, D_MODEL), F32),
        compiler_params=pltpu.CompilerParams(
            dimension_semantics=("arbitrary",), vmem_limit_bytes=VMEM_LIMIT),
        name="merge",
    )(h, mp, ret_o, att_o, cb, u, u, u, gates, conv_w8, w_ret_o, w_att_o, w_conv_o, w_o)


def _rope_tables(n_tok):
    rows = n_tok // GRID_W
    t_row = np.repeat(np.arange(rows, dtype=np.float64), GRID_W)
    t_col = np.tile(np.arange(GRID_W, dtype=np.float64), rows)
    n_freq = HEAD_DIM // 4
    inv = ROPE_THETA ** (-np.arange(n_freq, dtype=np.float64) / n_freq)
    ang = np.concatenate([t_row[:, None] * inv, t_col[:, None] * inv], axis=-1)
    cos, sin = np.cos(ang), np.sin(ang)
    cos64 = np.concatenate([cos, cos], axis=-1)
    sin64 = np.concatenate([-sin, sin], axis=-1)
    return (jnp.asarray(np.tile(cos64, (1, LANES // HEAD_DIM)), F32),
            jnp.asarray(np.tile(sin64, (1, LANES // HEAD_DIM)), F32))


def _mod_pack(mod_l, rows, sub, norm_w_row):
    r0 = rows[0]
    nr = len(rows)
    m = mod_l[r0:r0 + nr, 3 * sub * D_MODEL:3 * (sub + 1) * D_MODEL].reshape(nr, 3, D_MODEL)
    nw = jnp.broadcast_to(norm_w_row[None, None, :], (nr, 1, D_MODEL))
    pad = jnp.zeros((nr, 4, D_MODEL), F32)
    return jnp.concatenate([m, nw, pad], axis=1)


def kernel(x_prompt, x_sample, c, state_ret, cache_k, cache_v, c_ctx, w_ada, b_ada, norm_w, w_ffn_in,
           w_ffn_out, w_in, ret_decay_logit, ret_gn, q_gain, k_gain, conv_w, w_ret_o, w_att_o, w_conv_o,
           w_o):
    n_ctx_b, ctx_len, _ = x_prompt.shape
    n_lat_b, lat_len, _ = x_sample.shape

    w_ffn_in_b = w_ffn_in.astype(BF16)
    w_ffn_out_b = w_ffn_out.astype(BF16)
    w_in_b = w_in.astype(BF16)
    w_ret_o_b = w_ret_o.astype(BF16)
    w_att_o_b = w_att_o.astype(BF16)
    w_conv_o_b = w_conv_o.astype(BF16)
    w_o_b = w_o.astype(BF16)

    q_gain_t = jnp.tile(q_gain, (1, N_HEADS)).reshape(DEPTH, 1, ATT_Q_W)
    k_gain_t = jnp.tile(k_gain, (1, N_KV_HEADS)).reshape(DEPTH, 1, ATT_KV_W)
    conv_w8 = jnp.pad(conv_w, ((0, 0), (0, 8 - conv_w.shape[1]), (0, 0)))
    gid = np.arange(ATT_Q_W) // HEAD_DIM
    bd = jnp.asarray(gid[:, None] == gid[None, :], BF16)
    rope = _rope_tables(lat_len)

    cond8 = jnp.zeros((8, D_MODEL), F32).at[0].set(c_ctx).at[1:1 + n_lat_b].set(c)
    mod = _ada(cond8, w_ada, b_ada)

    ck = cache_k.astype(BF16)
    ck = jnp.broadcast_to(ck[:, :, :, :, None, :], ck.shape[:4] + (GROUP, HEAD_DIM))
    cache_kt = ck.reshape(ck.shape[0], ck.shape[1], ck.shape[2], N_KV_HEADS * KV_LANES)
    cache_vT = jnp.transpose(cache_v.astype(BF16), (0, 1, 3, 4, 2)).reshape(
        cache_v.shape[0], cache_v.shape[1], ATT_KV_W, cache_v.shape[2])

    groups = (
        dict(x=x_prompt.reshape(n_ctx_b * ctx_len, D_MODEL), rows=[0], seq=ctx_len, ctx=True),
        dict(x=x_sample.reshape(n_lat_b * lat_len, D_MODEL), rows=list(range(1, 1 + n_lat_b)),
             seq=lat_len, ctx=False),
    )
    results = []
    for grp in groups:
        h = grp["x"]
        seq = grp["seq"]
        is_ctx = grp["ctx"]
        rpm = h.shape[0] // len(grp["rows"])
        states, keys, values = [], [], []
        for l in range(DEPTH):
            mp = [_mod_pack(mod[l], grp["rows"], s, norm_w[l, s]) for s in range(3)]
            h = _ffn(h, mp[0], rpm, w_ffn_in_b, w_ffn_out_b, l, 0)
            outs = _inproj(h, mp[1], rpm, w_in_b, l, q_gain_t, k_gain_t, bd,
                           None if is_ctx else rope, seq, emit_kv=is_ctx)
            ret_in, qn, kt, vT, cb, u, gates = outs[:7]
            if is_ctx:
                ret_o, st = _retention(ret_in, ret_decay_logit[l], ret_gn[l][None, :], None, seq,
                                       n_seq_blk=8, emit_state=True)
                states.append(st)
                keys.append(outs[7].reshape(n_ctx_b, seq, N_KV_HEADS, HEAD_DIM))
                values.append(outs[8].reshape(n_ctx_b, seq, N_KV_HEADS, HEAD_DIM))
                att_o = _attention(qn, kt, vT, None, seq)
            else:
                (ret_o,) = _retention(ret_in, ret_decay_logit[l], ret_gn[l][None, :], state_ret[:, l],
                                      seq, n_seq_blk=1, emit_state=False)
                att_o = _attention(qn, kt, vT, (cache_kt[:, l], cache_vT[:, l]), seq)
            h = _merge(h, mp[1], rpm, ret_o, att_o, cb, u, gates, conv_w8, w_ret_o_b, w_att_o_b,
                       w_conv_o_b, w_o_b, l, seq)
            h = _ffn(h, mp[2], rpm, w_ffn_in_b, w_ffn_out_b, l, 1)
        results.append((h, states, keys, values))

    (y_ctx, states, keys, values), (y_lat, _, _, _) = results
    y_prompt = y_ctx.reshape(x_prompt.shape)
    y_sample = y_lat.reshape(x_sample.shape)
    new_state_ret = jnp.stack(states, axis=1)
    new_cache_k = jnp.stack(keys, axis=1)
    new_cache_v = jnp.stack(values, axis=1)
    return (y_prompt, y_sample, new_state_ret, new_cache_k, new_cache_v)
```

```python
import functools
import math

import numpy as np
import jax
import jax.numpy as jnp
from jax import lax
from jax.experimental import pallas as pl
from jax.experimental.pallas import tpu as pltpu

D_MODEL = 1024
DEPTH = 2
GRID_W = 64
N_RET_HEADS = 4
RET_DK = 128
RET_DV = 128
RET_CHUNK = 128
N_HEADS = 8
N_KV_HEADS = 2
HEAD_DIM = 64
ROPE_THETA = 10000.0
CONV_DIM = 512
FFN_DIM = 2816
N_MOD = 9
EPS = 1e-6
RET_W = N_RET_HEADS * RET_DK
ATT_Q_W = N_HEADS * HEAD_DIM
ATT_KV_W = N_KV_HEADS * HEAD_DIM
GROUP = N_HEADS // N_KV_HEADS
KV_LANES = GROUP * HEAD_DIM
OFF_RQ, OFF_RK, OFF_RV, OFF_RG = 0, 512, 1024, 1536
OFF_AQ, OFF_AK, OFF_AV = 2048, 2560, 2688
OFF_CB, OFF_CC, OFF_CX = 2816, 3328, 3840
OFF_GATES = 4352
IN_COLS = 7424

LANES = 128
BF16_SUBLANES = 16
VMEM_LIMIT = 56 * 1024 * 1024

F32 = jnp.float32
BF16 = jnp.bfloat16


def _dot(a, b):
    return jnp.dot(a, b, preferred_element_type=F32)


def _dot_nt(a, b):
    return lax.dot_general(a, b, (((1,), (1,)), ((), ())), preferred_element_type=F32)


def _sigmoid(x):
    return 1.0 / (1.0 + jnp.exp(-x))


def _silu(x):
    return x * _sigmoid(x)


def _mod_norm(x, mp):
    ms = jnp.mean(x * x, axis=-1, keepdims=True)
    y = x * lax.rsqrt(ms + EPS)
    return (y * mp[3:4]) * (1.0 + mp[1:2]) + mp[0:1]


def _resident(shape, index):
    return pl.BlockSpec(shape, index, pipeline_mode=pl.Buffered(1))


ADA_TN = 1152


def _ada_kernel(c_ref, w_ref, b_ref, o_ref):
    a = _silu(c_ref[...]).astype(BF16)
    o_ref[...] = _dot(a, w_ref[...].astype(BF16)) + b_ref[...]


def _ada(cond8, w_ada, b_ada):
    n = N_MOD * D_MODEL
    return pl.pallas_call(
        _ada_kernel,
        grid=(DEPTH, n // ADA_TN),
        in_specs=[
            pl.BlockSpec((8, D_MODEL), lambda l, j: (0, 0)),
            pl.BlockSpec((None, D_MODEL, ADA_TN), lambda l, j: (l, 0, j)),
            pl.BlockSpec((None, 1, ADA_TN), lambda l, j: (l, 0, j)),
        ],
        out_specs=pl.BlockSpec((None, 8, ADA_TN), lambda l, j: (l, 0, j)),
        out_shape=jax.ShapeDtypeStruct((DEPTH, 8, n), F32),
        compiler_params=pltpu.CompilerParams(
            dimension_semantics=("arbitrary", "arbitrary"), vmem_limit_bytes=VMEM_LIMIT),
        name="ada",
    )(cond8, w_ada, b_ada.reshape(DEPTH, 1, n))


FFN_TM = 1024
FFN_FC = 256


def _ffn_kernel(x_ref, mp_ref, wi_ref, wo_ref, o_ref, h_ref):
    x = x_ref[...]
    mp = mp_ref[...]
    xb = _mod_norm(x, mp).astype(BF16)
    for c in range(FFN_DIM // FFN_FC):
        lo = c * FFN_FC
        g = _dot(xb, wi_ref[:, lo:lo + FFN_FC])
        u = _dot(xb, wi_ref[:, FFN_DIM + lo:FFN_DIM + lo + FFN_FC])
        h_ref[:, lo:lo + FFN_FC] = (_silu(g) * u).astype(BF16)
    y = _dot(h_ref[...], wo_ref[...])
    o_ref[...] = x + (0.5 * mp[2:3]) * y


def _ffn(x, mp, rows_per_mod, w_in, w_out, l, j):
    n = x.shape[0]
    tm = FFN_TM
    return pl.pallas_call(
        _ffn_kernel,
        grid=(n // tm,),
        in_specs=[
            pl.BlockSpec((tm, D_MODEL), lambda i: (i, 0)),
            pl.BlockSpec((None, 8, D_MODEL), lambda i: ((i * tm) // rows_per_mod, 0, 0)),
            _resident((None, None, D_MODEL, 2 * FFN_DIM), lambda i: (l, j, 0, 0)),
            _resident((None, None, FFN_DIM, D_MODEL), lambda i: (l, j, 0, 0)),
        ],
        out_specs=pl.BlockSpec((tm, D_MODEL), lambda i: (i, 0)),
        out_shape=jax.ShapeDtypeStruct((n, D_MODEL), F32),
        scratch_shapes=[pltpu.VMEM((tm, FFN_DIM), BF16)],
        compiler_params=pltpu.CompilerParams(
            dimension_semantics=("arbitrary",), vmem_limit_bytes=VMEM_LIMIT),
        name="ffn",
    )(x, mp, w_in, w_out)


INPROJ_TM = 512
Q_SCALE = HEAD_DIM ** -0.5 * math.log2(math.e)


def _group_rms(a, bd, gain):
    sq = a * a
    hi = sq.astype(BF16)
    lo = (sq - hi.astype(F32)).astype(BF16)
    ss = _dot(hi, bd) + _dot(lo, bd)
    return (a * lax.rsqrt(ss * (1.0 / HEAD_DIM) + EPS)) * gain


def _rope(x, cos, sin_signed):
    half = HEAD_DIM // 2
    reps = x.shape[1] // LANES
    lane = lax.broadcasted_iota(jnp.int32, (x.shape[0], LANES), 1)
    first = (lane % HEAD_DIM) < half
    out = []
    for r in range(reps):
        xs = x[:, r * LANES:(r + 1) * LANES]
        partner = jnp.where(first, pltpu.roll(xs, LANES - half, axis=1), pltpu.roll(xs, half, axis=1))
        out.append(xs * cos + partner * sin_signed)
    return out[0] if reps == 1 else jnp.concatenate(out, axis=1)


def _tile_kv(a):
    lane = lax.broadcasted_iota(jnp.int32, a.shape, 1)
    sw = pltpu.roll(a, HEAD_DIM, axis=1)
    h0 = jnp.where(lane < HEAD_DIM, a, sw)
    h1 = jnp.where(lane < HEAD_DIM, sw, a)
    return jnp.concatenate([h0, h0, h1, h1], axis=1)


def _inproj_kernel(*refs, use_rope, emit_kv):
    x_ref, mp_ref, w_ref, qg_ref, kg_ref, bd_ref = refs[:6]
    pos = 6
    if use_rope:
        cos_ref, sin_ref = refs[pos:pos + 2]
        pos += 2
    ret_ref, q_ref, kt_ref, vT_ref, cb_ref, u_ref, gates_ref = refs[pos:pos + 7]
    pos += 7
    if emit_kv:
        kout_ref, vout_ref = refs[pos:pos + 2]

    xb = _mod_norm(x_ref[...], mp_ref[...]).astype(BF16)

    def proj(off, width):
        return _dot(xb, w_ref[:, off:off + width])

    ret_ref[:, OFF_RQ:OFF_RQ + RET_W] = proj(OFF_RQ, RET_W).astype(BF16)
    ret_ref[:, OFF_RK:OFF_RK + RET_W] = (proj(OFF_RK, RET_W) * (RET_DK ** -0.5)).astype(BF16)
    ret_ref[:, OFF_RV:OFF_RV + RET_W] = proj(OFF_RV, RET_W).astype(BF16)
    ret_ref[:, OFF_RG:OFF_RG + RET_W] = _silu(proj(OFF_RG, RET_W)).astype(BF16)

    q = _group_rms(proj(OFF_AQ, ATT_Q_W), bd_ref[...], qg_ref[...])
    if use_rope:
        q = _rope(q, cos_ref[...], sin_ref[...])
    q_ref[...] = (q * Q_SCALE).astype(BF16)

    k = _group_rms(proj(OFF_AK, ATT_KV_W), bd_ref[0:ATT_KV_W, 0:ATT_KV_W], kg_ref[...])
    v = proj(OFF_AV, ATT_KV_W)
    if emit_kv:
        kout_ref[...] = k
        vout_ref[...] = v
    if use_rope:
        k = _rope(k, cos_ref[...], sin_ref[...])
    kt_ref[...] = _tile_kv(k).astype(BF16)
    vT_ref[...] = v.T.astype(BF16)

    cb_ref[...] = proj(OFF_CB, CONV_DIM).astype(BF16)
    u_ref[...] = (proj(OFF_CC, CONV_DIM) * proj(OFF_CX, CONV_DIM)).astype(BF16)

    for c in range(3 * D_MODEL // 512):
        gates_ref[:, c * 512:(c + 1) * 512] = _sigmoid(proj(OFF_GATES + c * 512, 512)).astype(BF16)


def _inproj(x, mp, rows_per_mod, w_in, l, q_gain_t, k_gain_t, bd, rope, seq_len, emit_kv):
    n = x.shape[0]
    tm = INPROJ_TM
    use_rope = rope is not None
    row = lambda i: (i, 0)
    in_specs = [
        pl.BlockSpec((tm, D_MODEL), row),
        pl.BlockSpec((None, 8, D_MODEL), lambda i: ((i * tm) // rows_per_mod, 0, 0)),
        _resident((None, D_MODEL, IN_COLS), lambda i: (l, 0, 0)),
        pl.BlockSpec((None, 1, ATT_Q_W), lambda i: (l, 0, 0)),
        pl.BlockSpec((None, 1, ATT_KV_W), lambda i: (l, 0, 0)),
        _resident((ATT_Q_W, ATT_Q_W), lambda i: (0, 0)),
    ]
    args = [x, mp, w_in, q_gain_t, k_gain_t, bd]
    if use_rope:
        tiles_per_seq = seq_len // tm
        in_specs += [pl.BlockSpec((tm, LANES), lambda i: (i % tiles_per_seq, 0))] * 2
        args += list(rope)
    widths = [4 * RET_W, ATT_Q_W, ATT_Q_W, None, CONV_DIM, CONV_DIM, 3 * D_MODEL]
    out_specs = [pl.BlockSpec((tm, w), row) for w in widths if w]
    out_shape = [jax.ShapeDtypeStruct((n, w), BF16) for w in widths if w]
    out_specs.insert(3, pl.BlockSpec((ATT_KV_W, tm), lambda i: (0, i)))
    out_shape.insert(3, jax.ShapeDtypeStruct((ATT_KV_W, n), BF16))
    if emit_kv:
        out_specs += [pl.BlockSpec((tm, ATT_KV_W), row)] * 2
        out_shape += [jax.ShapeDtypeStruct((n, ATT_KV_W), F32)] * 2
    return pl.pallas_call(
        functools.partial(_inproj_kernel, use_rope=use_rope, emit_kv=emit_kv),
        grid=(n // tm,),
        in_specs=in_specs,
        out_specs=out_specs,
        out_shape=out_shape,
        compiler_params=pltpu.CompilerParams(
            dimension_semantics=("arbitrary",), vmem_limit_bytes=VMEM_LIMIT),
        name="inproj",
    )(*args)


RET_CHUNK_UNROLL = 8
RET_SEQ_UNROLL = 4
TAB_DMASK, TAB_DQF, TAB_DQB, TAB_DKF, TAB_DKB, TAB_DCF, TAB_DCB, N_TAB = range(8)


def _log_sigmoid(x):
    return jnp.minimum(x, 0.0) - jnp.log(1.0 + jnp.exp(-jnp.abs(x)))


def _chunk_rows(start):
    if isinstance(start, int):
        return pl.ds(start, RET_CHUNK)
    return pl.ds(pl.multiple_of(start, RET_CHUNK), RET_CHUNK)


def _retention_kernel(*refs, n_seq, n_chunks, has_s0, emit_state):
    dl_ref, q_ref, k_ref, v_ref, g_ref, gn_ref = refs[:6]
    pos = 6
    if has_s0:
        s0_ref = refs[pos]
        pos += 1
    o_ref = refs[pos]
    pos += 1
    if emit_state:
        st_ref = refs[pos]
        pos += 1
    tab_ref, kv_ref, ent_ref = refs[pos:pos + 3]

    C = RET_CHUNK
    hd = pl.program_id(1)
    row = lax.broadcasted_iota(jnp.int32, (C, C), 0).astype(F32)
    col = lax.broadcasted_iota(jnp.int32, (C, C), 1).astype(F32)
    diff = row - col
    lgf = _log_sigmoid(jnp.full((C, C), dl_ref[0, hd], F32))
    lgb = _log_sigmoid(jnp.full((C, C), dl_ref[1, hd], F32))
    tab_ref[TAB_DMASK] = (jnp.where(diff >= 0, jnp.exp(lgf * jnp.maximum(diff, 0.0)), 0.0)
                          + jnp.where(diff <= 0, jnp.exp(lgb * jnp.maximum(-diff, 0.0)), 0.0))
    tab_ref[TAB_DQF] = jnp.exp(lgf * (row + 1.0))
    tab_ref[TAB_DQB] = jnp.exp(lgb * (C - row))
    tab_ref[TAB_DKF] = jnp.exp(lgf * (C - 1.0 - row))
    tab_ref[TAB_DKB] = jnp.exp(lgb * row)
    tab_ref[TAB_DCF] = jnp.exp(lgf * C)
    tab_ref[TAB_DCB] = jnp.exp(lgb * C)

    def kv_phase(items):
        ops = []
        for r, _ in items:
            rows = _chunk_rows(r)
            k = k_ref[rows, :].astype(F32)
            vT = v_ref[rows, :].astype(F32).T.astype(BF16)
            kk = jnp.concatenate([(k * tab_ref[TAB_DKF]).astype(BF16),
                                  (k * tab_ref[TAB_DKB]).astype(BF16)], axis=1)
            ops.append((vT, kk))
        for (_, i), (vT, kk) in zip(items, ops):
            kv_ref[i] = _dot(vT, kk)

    def scan_phase(s, sc):
        if has_s0:
            st0 = (s0_ref[0].T, s0_ref[1].T)
        else:
            st0 = (jnp.zeros((RET_DV, RET_DK), F32), jnp.zeros((RET_DV, RET_DK), F32))

        def scan_step(t, st):
            st_f, st_b = st
            cb = n_chunks - 1 - t
            ent_ref[sc + t, :, 0:RET_DK] = st_f.astype(BF16)
            ent_ref[sc + cb, :, RET_DK:2 * RET_DK] = st_b.astype(BF16)
            st_f = st_f * tab_ref[TAB_DCF] + kv_ref[sc + t, :, 0:RET_DK]
            st_b = st_b * tab_ref[TAB_DCB] + kv_ref[sc + cb, :, RET_DK:2 * RET_DK]
            return st_f, st_b

        if n_chunks <= RET_CHUNK_UNROLL:
            st = st0
            for t in range(n_chunks):
                st = scan_step(t, st)
        else:
            st = lax.fori_loop(0, n_chunks, scan_step, st0)
        if emit_state:
            st_ref[s, 0] = st[0].T
            st_ref[s, 1] = st[1].T

    def out_phase(items):
        qs = [q_ref[_chunk_rows(r), :] for r, _ in items]
        att = [_dot_nt(q, k_ref[_chunk_rows(r), :]) for q, (r, _) in zip(qs, items)]
        att = [(a * tab_ref[TAB_DMASK]).astype(BF16) for a in att]
        qq = []
        for q in qs:
            qf = q.astype(F32)
            qq.append(jnp.concatenate([(qf * tab_ref[TAB_DQF]).astype(BF16),
                                       (qf * tab_ref[TAB_DQB]).astype(BF16)], axis=1))
        outs = [_dot(a, v_ref[_chunk_rows(r), :]) + _dot_nt(x, ent_ref[i])
                for a, x, (r, i) in zip(att, qq, items)]
        for o, (r, _) in zip(outs, items):
            rows = _chunk_rows(r)
            ms = jnp.mean(o * o, axis=-1, keepdims=True)
            on = (o * lax.rsqrt(ms + EPS)) * gn_ref[...]
            o_ref[rows, :] = (g_ref[rows, :].astype(F32) * on).astype(BF16)

    if n_seq == 1:
        assert n_chunks % RET_CHUNK_UNROLL == 0

        def chunk_group(phase):
            def step(t, carry):
                c0 = t * RET_CHUNK_UNROLL
                phase([((c0 + j) * C, c0 + j) for j in range(RET_CHUNK_UNROLL)])
                return carry
            lax.fori_loop(0, n_chunks // RET_CHUNK_UNROLL, step, 0)

        chunk_group(kv_phase)
        scan_phase(0, 0)
        chunk_group(out_phase)
    else:
        assert n_seq % RET_SEQ_UNROLL == 0 and n_chunks <= RET_CHUNK_UNROLL

        def seq_group(t, carry):
            s0 = t * RET_SEQ_UNROLL
            items = [((s0 + j) * (n_chunks * C) + c * C, j * n_chunks + c)
                     for j in range(RET_SEQ_UNROLL) for c in range(n_chunks)]
            kv_phase(items)
            for j in range(RET_SEQ_UNROLL):
                scan_phase(s0 + j, j * n_chunks)
            out_phase(items)
            return carry

        lax.fori_loop(0, n_seq // RET_SEQ_UNROLL, seq_group, 0)


def _retention(ret_in, decay_logit_l, gn_l, s0, seq_len, n_seq_blk, emit_state):
    n = ret_in.shape[0]
    n_chunks = seq_len // RET_CHUNK
    tb = n_seq_blk * seq_len
    has_s0 = s0 is not None
    n_off = RET_W // RET_DK

    def branch(k):
        return pl.BlockSpec((tb, RET_DK), lambda i, h: (i, k * n_off + h))

    in_specs = [pl.BlockSpec(memory_space=pltpu.SMEM), branch(0), branch(1), branch(2), branch(3),
                pl.BlockSpec((1, RET_DV), lambda i, h: (0, h))]
    args = [decay_logit_l, ret_in, ret_in, ret_in, ret_in, gn_l]
    if has_s0:
        assert n_seq_blk == 1
        in_specs.append(pl.BlockSpec((None, 2, None, RET_DK, RET_DV), lambda i, h: (i, 0, h, 0, 0)))
        args.append(s0)
    out_specs = [pl.BlockSpec((tb, RET_DV), lambda i, h: (i, h))]
    out_shape = [jax.ShapeDtypeStruct((n, RET_W), BF16)]
    if emit_state:
        out_specs.append(pl.BlockSpec((n_seq_blk, 2, None, RET_DK, RET_DV), lambda i, h: (i, 0, h, 0, 0)))
        out_shape.append(jax.ShapeDtypeStruct((n // seq_len, 2, N_RET_HEADS, RET_DK, RET_DV), F32))
    n_slots = n_chunks * (1 if n_seq_blk == 1 else RET_SEQ_UNROLL)
    return pl.pallas_call(
        functools.partial(_retention_kernel, n_seq=n_seq_blk, n_chunks=n_chunks, has_s0=has_s0,
                          emit_state=emit_state),
        grid=(n // tb, N_RET_HEADS),
        in_specs=in_specs,
        out_specs=out_specs,
        out_shape=out_shape,
        scratch_shapes=[
            pltpu.VMEM((N_TAB, RET_CHUNK, RET_CHUNK), F32),
            pltpu.VMEM((n_slots, RET_DV, 2 * RET_DK), F32),
            pltpu.VMEM((n_slots, RET_DV, 2 * RET_DK), BF16),
        ],
        compiler_params=pltpu.CompilerParams(
            dimension_semantics=("arbitrary", "arbitrary"), vmem_limit_bytes=VMEM_LIMIT),
        name="retention",
    )(*args)


ATT_TQ = 256
ATT_KB = 512
ATT_RC = 16
NEG_BIG = -1e30
SUM_ROWS = BF16_SUBLANES


def _attention_kernel(*refs, n_cache, n_new):
    q_ref = refs[0]
    pos = 1
    if n_cache:
        kc_ref, vc_ref = refs[pos:pos + 2]
        pos += 2
    kn_ref, vn_ref, o_ref, qs_ref, acc_ref = refs[pos:pos + 5]
    s_slots = refs[pos + 5:pos + 7]
    p_slots = refs[pos + 7:pos + 9]

    tq = q_ref.shape[0]
    nq = GROUP * tq
    qf = q_ref[...].astype(F32)
    head = lax.broadcasted_iota(jnp.int32, (tq, KV_LANES), 1) // HEAD_DIM
    for g in range(GROUP):
        qs_ref[:, g * tq:(g + 1) * tq] = jnp.where(head == g, qf, 0.0).T.astype(BF16)
    acc_ref[...] = jnp.zeros(acc_ref.shape, F32)

    kb = min(ATT_KB, n_new)
    n_cb = 1 if n_cache else 0
    n_nb = n_new // kb
    assert n_cache <= kb and n_new % kb == 0
    n_blk = n_cb + n_nb

    def block(t):
        if isinstance(t, int) and t < n_cb:
            return kc_ref, vc_ref, pl.ds(0, n_cache)
        lo = (t - n_cb) * kb
        return kn_ref, vn_ref, (pl.ds(lo, kb) if isinstance(lo, int) else pl.ds(pl.multiple_of(lo, kb), kb))

    def scores(t, par):
        k_ref, _, keys = block(t)
        s = _dot(k_ref[keys, :], qs_ref[...])
        s_slots[par][0:keys.size, :] = s
        return jnp.max(s, axis=0, keepdims=True)

    def softmax(n_keys, par, m, s_max):
        s_ref, p_ref = s_slots[par], p_slots[par]
        m_new = jnp.maximum(m, s_max)
        m_rows = jnp.broadcast_to(m_new, (ATT_RC, nq))
        for r in range(0, n_keys, ATT_RC):
            p_ref[r:r + ATT_RC, :] = jnp.exp2(s_ref[r:r + ATT_RC, :] - m_rows).astype(BF16)
        return m_new, jnp.exp2(m - m_new)

    def pv(t, par, alpha):
        _, vT_ref, keys = block(t)
        v1 = jnp.concatenate([vT_ref[:, keys], jnp.ones((SUM_ROWS, keys.size), BF16)], axis=0)
        acc_ref[...] = alpha * acc_ref[...] + _dot(v1, p_slots[par][0:keys.size, :])

    def stage(t, par, carry):
        m, alpha, s_max = carry
        next_max = s_max
        if not isinstance(t, int) or t + 1 < n_blk:
            next_max = scores(t + 1, 1 - par)
        if not isinstance(t, int) or t >= 1:
            pv(t - 1, 1 - par, alpha)
        return softmax(block(t)[2].size, par, m, s_max) + (next_max,)

    carry = (jnp.full((1, nq), NEG_BIG, F32), jnp.zeros((1, nq), F32), scores(0, 0))
    lo_t = min(n_cb + 1, n_blk)
    n_loop = max(n_blk - 1 - lo_t, 0)
    if n_loop % 2:
        lo_t += 1
        n_loop -= 1
    for t in range(lo_t):
        carry = stage(t, t % 2, carry)

    def pair(i, carry):
        t = lo_t + 2 * i
        carry = stage(t, lo_t % 2, carry)
        return stage(t + 1, (lo_t + 1) % 2, carry)

    carry = lax.fori_loop(0, n_loop // 2, pair, carry)
    for t in range(lo_t + n_loop, n_blk):
        carry = stage(t, t % 2, carry)
    pv(n_blk - 1, (n_blk - 1) % 2, carry[1])

    acc = acc_ref[...]
    oT = acc[0:HEAD_DIM] / acc[HEAD_DIM:HEAD_DIM + 1]
    o4 = jnp.concatenate([oT[:, g * tq:(g + 1) * tq] for g in range(GROUP)], axis=0)
    o_ref[...] = o4.T.astype(BF16)


ATT_SEQ_PER_STEP = 2


def _attention_short_kernel(q_ref, k_ref, vT_ref, o_ref, *, n_seq):
    t_len = q_ref.shape[0] // n_seq
    head = lax.broadcasted_iota(jnp.int32, (t_len, KV_LANES), 1) // HEAD_DIM
    ones = jnp.ones((SUM_ROWS, t_len), BF16)
    for s in range(n_seq):
        rows = slice(s * t_len, (s + 1) * t_len)
        for kv in range(N_KV_HEADS):
            cols = slice(kv * KV_LANES, (kv + 1) * KV_LANES)
            qf = q_ref[rows, cols].astype(F32)
            qsT = jnp.concatenate([jnp.where(head == g, qf, 0.0).T.astype(BF16) for g in range(GROUP)],
                                  axis=1)
            sT = _dot(k_ref[rows, cols], qsT)
            pT = jnp.exp2(sT - jnp.max(sT, axis=0, keepdims=True)).astype(BF16)
            v1 = jnp.concatenate([vT_ref[kv * HEAD_DIM:(kv + 1) * HEAD_DIM, rows], ones], axis=0)
            acc = _dot(v1, pT)
            oT = acc[0:HEAD_DIM] / acc[HEAD_DIM:HEAD_DIM + 1]
            o4 = jnp.concatenate([oT[:, g * t_len:(g + 1) * t_len] for g in range(GROUP)], axis=0)
            o_ref[rows, cols] = o4.T.astype(BF16)


def _attention_short(q, kt, vT, seq_len):
    n = q.shape[0]
    tb = ATT_SEQ_PER_STEP * seq_len
    return pl.pallas_call(
        functools.partial(_attention_short_kernel, n_seq=ATT_SEQ_PER_STEP),
        grid=(n // tb,),
        in_specs=[pl.BlockSpec((tb, ATT_Q_W), lambda i: (i, 0)),
                  pl.BlockSpec((tb, ATT_Q_W), lambda i: (i, 0)),
                  pl.BlockSpec((ATT_KV_W, tb), lambda i: (0, i))],
        out_specs=pl.BlockSpec((tb, ATT_Q_W), lambda i: (i, 0)),
        out_shape=jax.ShapeDtypeStruct((n, ATT_Q_W), BF16),
        compiler_params=pltpu.CompilerParams(
            dimension_semantics=("arbitrary",), vmem_limit_bytes=VMEM_LIMIT),
        name="attention_short",
    )(q, kt, vT)


def _attention(q, kt, vT, cache, seq_len):
    n = q.shape[0]
    n_batch = n // seq_len
    tq = ATT_TQ
    nq = seq_len // tq
    n_cache = 0 if cache is None else cache[0].shape[1]
    if n_cache == 0 and seq_len <= ATT_KB:
        return _attention_short(q, kt, vT, seq_len)
    in_specs = [pl.BlockSpec((tq, KV_LANES), lambda b, kv, i: (b * nq + i, kv))]
    args = [q]
    if n_cache:
        in_specs += [pl.BlockSpec((None, n_cache, KV_LANES), lambda b, kv, i: (b, 0, kv)),
                     pl.BlockSpec((None, HEAD_DIM, n_cache), lambda b, kv, i: (b, kv, 0))]
        args += list(cache)
    in_specs += [pl.BlockSpec((seq_len, KV_LANES), lambda b, kv, i: (b, kv)),
                 pl.BlockSpec((HEAD_DIM, seq_len), lambda b, kv, i: (kv, b))]
    args += [kt, vT]
    return pl.pallas_call(
        functools.partial(_attention_kernel, n_cache=n_cache, n_new=seq_len),
        grid=(n_batch, N_KV_HEADS, nq),
        in_specs=in_specs,
        out_specs=pl.BlockSpec((tq, KV_LANES), lambda b, kv, i: (b * nq + i, kv)),
        out_shape=jax.ShapeDtypeStruct((n, ATT_Q_W), BF16),
        scratch_shapes=[
            pltpu.VMEM((KV_LANES, GROUP * tq), BF16),
            pltpu.VMEM((HEAD_DIM + SUM_ROWS, GROUP * tq), F32),
            pltpu.VMEM((min(ATT_KB, seq_len), GROUP * tq), F32),
            pltpu.VMEM((min(ATT_KB, seq_len), GROUP * tq), F32),
            pltpu.VMEM((min(ATT_KB, seq_len), GROUP * tq), BF16),
            pltpu.VMEM((min(ATT_KB, seq_len), GROUP * tq), BF16),
        ],
        compiler_params=pltpu.CompilerParams(
            dimension_semantics=("arbitrary", "arbitrary", "arbitrary"), vmem_limit_bytes=VMEM_LIMIT),
        name="attention",
    )(*args)


MERGE_TM = 1024


def _merge_kernel(h_ref, mp_ref, ro_ref, ao_ref, cb_ref, u_ref, up_ref, un_ref, g_ref, cw_ref,
                  wr_ref, wa_ref, wc_ref, wo_ref, o_ref, *, seq_len):
    tm = h_ref.shape[0]
    i = pl.program_id(0)
    u = u_ref[...].astype(F32)
    r = lax.broadcasted_iota(jnp.int32, u.shape, 0)
    t = (i * tm + r) % seq_len
    prev_row = up_ref[BF16_SUBLANES - 1:BF16_SUBLANES, :].astype(F32)
    next_row = un_ref[0:1, :].astype(F32)
    u_prev = jnp.where(r == 0, prev_row, pltpu.roll(u, 1, axis=0))
    u_next = jnp.where(r == tm - 1, next_row, pltpu.roll(u, tm - 1, axis=0))
    u_prev = jnp.where(t == 0, 0.0, u_prev)
    u_next = jnp.where(t == seq_len - 1, 0.0, u_next)
    cw = cw_ref[...]
    conv = u_prev * cw[0:1] + u * cw[1:2] + u_next * cw[2:3]
    yc = _dot((cb_ref[...].astype(F32) * conv).astype(BF16), wc_ref[...])
    yr = _dot(ro_ref[...], wr_ref[...])
    ya = _dot(ao_ref[...], wa_ref[...])
    merged = (g_ref[:, 0:D_MODEL].astype(F32) * yr
              + g_ref[:, D_MODEL:2 * D_MODEL].astype(F32) * ya
              + g_ref[:, 2 * D_MODEL:3 * D_MODEL].astype(F32) * yc)
    y = _dot(merged.astype(BF16), wo_ref[...])
    o_ref[...] = h_ref[...] + mp_ref[2:3, :] * y


def _merge(h, mp, rows_per_mod, ret_o, att_o, cb, u, gates, conv_w8, w_ret_o, w_att_o, w_conv_o, w_o,
           l, seq_len):
    n = h.shape[0]
    tm = MERGE_TM
    hb = tm // BF16_SUBLANES
    last = n // BF16_SUBLANES - 1
    row = lambda i: (i, 0)
    wspec = lambda k: _resident((None, k, D_MODEL), lambda i: (l, 0, 0))
    return pl.pallas_call(
        functools.partial(_merge_kernel, seq_len=seq_len),
        grid=(n // tm,),
        in_specs=[
            pl.BlockSpec((tm, D_MODEL), row),
            pl.BlockSpec((None, 8, D_MODEL), lambda i: ((i * tm) // rows_per_mod, 0, 0)),
            pl.BlockSpec((tm, RET_W), row),
            pl.BlockSpec((tm, ATT_Q_W), row),
            pl.BlockSpec((tm, CONV_DIM), row),
            pl.BlockSpec((tm, CONV_DIM), row),
            pl.BlockSpec((BF16_SUBLANES, CONV_DIM), lambda i: (jnp.maximum(i * hb - 1, 0), 0)),
            pl.BlockSpec((BF16_SUBLANES, CONV_DIM), lambda i: (jnp.minimum((i + 1) * hb, last), 0)),
            pl.BlockSpec((tm, 3 * D_MODEL), row),
            pl.BlockSpec((None, 8, CONV_DIM), lambda i: (l, 0, 0)),
            wspec(RET_W), wspec(ATT_Q_W), wspec(CONV_DIM), wspec(D_MODEL),
        ],
        out_specs=pl.BlockSpec((tm, D_MODEL), row),
        out_shape=jax.ShapeDtypeStruct((n, D_MODEL), F32),
        compiler_params=pltpu.CompilerParams(
            dimension_semantics=("arbitrary",), vmem_limit_bytes=VMEM_LIMIT),
        name="merge",
    )(h, mp, ret_o, att_o, cb, u, u, u, gates, conv_w8, w_ret_o, w_att_o, w_conv_o, w_o)


def _rope_tables(n_tok):
    rows = n_tok // GRID_W
    t_row = np.repeat(np.arange(rows, dtype=np.float64), GRID_W)
    t_col = np.tile(np.arange(GRID_W, dtype=np.float64), rows)
    n_freq = HEAD_DIM // 4
    inv = ROPE_THETA ** (-np.arange(n_freq, dtype=np.float64) / n_freq)
    ang = np.concatenate([t_row[:, None] * inv, t_col[:, None] * inv], axis=-1)
    cos, sin = np.cos(ang), np.sin(ang)
    cos64 = np.concatenate([cos, cos], axis=-1)
    sin64 = np.concatenate([-sin, sin], axis=-1)
    return (jnp.asarray(np.tile(cos64, (1, LANES // HEAD_DIM)), F32),
            jnp.asarray(np.tile(sin64, (1, LANES // HEAD_DIM)), F32))


def _mod_pack(mod_l, rows, sub, norm_w_row):
    r0 = rows[0]
    nr = len(rows)
    m = mod_l[r0:r0 + nr, 3 * sub * D_MODEL:3 * (sub + 1) * D_MODEL].reshape(nr, 3, D_MODEL)
    nw = jnp.broadcast_to(norm_w_row[None, None, :], (nr, 1, D_MODEL))
    pad = jnp.zeros((nr, 4, D_MODEL), F32)
    return jnp.concatenate([m, nw, pad], axis=1)


def kernel(x_prompt, x_sample, c, state_ret, cache_k, cache_v, c_ctx, w_ada, b_ada, norm_w, w_ffn_in,
           w_ffn_out, w_in, ret_decay_logit, ret_gn, q_gain, k_gain, conv_w, w_ret_o, w_att_o, w_conv_o,
           w_o):
    n_ctx_b, ctx_len, _ = x_prompt.shape
    n_lat_b, lat_len, _ = x_sample.shape

    w_ffn_in_b = w_ffn_in.astype(BF16)
    w_ffn_out_b = w_ffn_out.astype(BF16)
    w_in_b = w_in.astype(BF16)
    w_ret_o_b = w_ret_o.astype(BF16)
    w_att_o_b = w_att_o.astype(BF16)
    w_conv_o_b = w_conv_o.astype(BF16)
    w_o_b = w_o.astype(BF16)

    q_gain_t = jnp.tile(q_gain, (1, N_HEADS)).reshape(DEPTH, 1, ATT_Q_W)
    k_gain_t = jnp.tile(k_gain, (1, N_KV_HEADS)).reshape(DEPTH, 1, ATT_KV_W)
    conv_w8 = jnp.pad(conv_w, ((0, 0), (0, 8 - conv_w.shape[1]), (0, 0)))
    gid = np.arange(ATT_Q_W) // HEAD_DIM
    bd = jnp.asarray(gid[:, None] == gid[None, :], BF16)
    rope = _rope_tables(lat_len)

    cond8 = jnp.zeros((8, D_MODEL), F32).at[0].set(c_ctx).at[1:1 + n_lat_b].set(c)
    mod = _ada(cond8, w_ada, b_ada)

    ck = cache_k.astype(BF16)
    ck = jnp.broadcast_to(ck[:, :, :, :, None, :], ck.shape[:4] + (GROUP, HEAD_DIM))
    cache_kt = ck.reshape(ck.shape[0], ck.shape[1], ck.shape[2], N_KV_HEADS * KV_LANES)
    cache_vT = jnp.transpose(cache_v.astype(BF16), (0, 1, 3, 4, 2)).reshape(
        cache_v.shape[0], cache_v.shape[1], ATT_KV_W, cache_v.shape[2])

    groups = (
        dict(x=x_prompt.reshape(n_ctx_b * ctx_len, D_MODEL), rows=[0], seq=ctx_len, ctx=True),
        dict(x=x_sample.reshape(n_lat_b * lat_len, D_MODEL), rows=list(range(1, 1 + n_lat_b)),
             seq=lat_len, ctx=False),
    )
    results = []
    for grp in groups:
        h = grp["x"]
        seq = grp["seq"]
        is_ctx = grp["ctx"]
        rpm = h.shape[0] // len(grp["rows"])
        states, keys, values = [], [], []
        for l in range(DEPTH):
            mp = [_mod_pack(mod[l], grp["rows"], s, norm_w[l, s]) for s in range(3)]
            h = _ffn(h, mp[0], rpm, w_ffn_in_b, w_ffn_out_b, l, 0)
            outs = _inproj(h, mp[1], rpm, w_in_b, l, q_gain_t, k_gain_t, bd,
                           None if is_ctx else rope, seq, emit_kv=is_ctx)
            ret_in, qn, kt, vT, cb, u, gates = outs[:7]
            if is_ctx:
                ret_o, st = _retention(ret_in, ret_decay_logit[l], ret_gn[l][None, :], None, seq,
                                       n_seq_blk=8, emit_state=True)
                states.append(st)
                keys.append(outs[7].reshape(n_ctx_b, seq, N_KV_HEADS, HEAD_DIM))
                values.append(outs[8].reshape(n_ctx_b, seq, N_KV_HEADS, HEAD_DIM))
                att_o = _attention(qn, kt, vT, None, seq)
            else:
                (ret_o,) = _retention(ret_in, ret_decay_logit[l], ret_gn[l][None, :], state_ret[:, l],
                                      seq, n_seq_blk=1, emit_state=False)
                att_o = _attention(qn, kt, vT, (cache_kt[:, l], cache_vT[:, l]), seq)
            h = _merge(h, mp[1], rpm, ret_o, att_o, cb, u, gates, conv_w8, w_ret_o_b, w_att_o_b,
                       w_conv_o_b, w_o_b, l, seq)
            h = _ffn(h, mp[2], rpm, w_ffn_in_b, w_ffn_out_b, l, 1)
        results.append((h, states, keys, values))

    (y_ctx, states, keys, values), (y_lat, _, _, _) = results
    y_prompt = y_ctx.reshape(x_prompt.shape)
    y_sample = y_lat.reshape(x_sample.shape)
    new_state_ret = jnp.stack(states, axis=1)
    new_cache_k = jnp.stack(keys, axis=1)
    new_cache_v = jnp.stack(values, axis=1)
    return (y_prompt, y_sample, new_state_ret, new_cache_k, new_cache_v)
```

```python
import functools
import math

import numpy as np
import jax
import jax.numpy as jnp
from jax import lax
from jax.experimental import pallas as pl
from jax.experimental.pallas import tpu as pltpu

D_MODEL = 1024
DEPTH = 2
GRID_W = 64
N_RET_HEADS = 4
RET_DK = 128
RET_DV = 128
RET_CHUNK = 128
N_HEADS = 8
N_KV_HEADS = 2
HEAD_DIM = 64
ROPE_THETA = 10000.0
CONV_DIM = 512
FFN_DIM = 2816
N_MOD = 9
EPS = 1e-6
RET_W = N_RET_HEADS * RET_DK
ATT_Q_W = N_HEADS * HEAD_DIM
ATT_KV_W = N_KV_HEADS * HEAD_DIM
GROUP = N_HEADS // N_KV_HEADS
KV_LANES = GROUP * HEAD_DIM
OFF_RQ, OFF_RK, OFF_RV, OFF_RG = 0, 512, 1024, 1536
OFF_AQ, OFF_AK, OFF_AV = 2048, 2560, 2688
OFF_CB, OFF_CC, OFF_CX = 2816, 3328, 3840
OFF_GATES = 4352
IN_COLS = 7424

LANES = 128
BF16_SUBLANES = 16
VMEM_LIMIT = 56 * 1024 * 1024

F32 = jnp.float32
BF16 = jnp.bfloat16


def _dot(a, b):
    return jnp.dot(a, b, preferred_element_type=F32)


def _dot_nt(a, b):
    return lax.dot_general(a, b, (((1,), (1,)), ((), ())), preferred_element_type=F32)


def _sigmoid(x):
    return 1.0 / (1.0 + jnp.exp(-x))


def _silu(x):
    return x * _sigmoid(x)


def _mod_norm(x, mp):
    ms = jnp.mean(x * x, axis=-1, keepdims=True)
    y = x * lax.rsqrt(ms + EPS)
    return (y * mp[3:4]) * (1.0 + mp[1:2]) + mp[0:1]


def _resident(shape, index):
    return pl.BlockSpec(shape, index, pipeline_mode=pl.Buffered(1))


ADA_TN = 1152


def _ada_kernel(c_ref, w_ref, b_ref, o_ref):
    a = _silu(c_ref[...]).astype(BF16)
    o_ref[...] = _dot(a, w_ref[...].astype(BF16)) + b_ref[...]


def _ada(cond8, w_ada, b_ada):
    n = N_MOD * D_MODEL
    return pl.pallas_call(
        _ada_kernel,
        grid=(DEPTH, n // ADA_TN),
        in_specs=[
            pl.BlockSpec((8, D_MODEL), lambda l, j: (0, 0)),
            pl.BlockSpec((None, D_MODEL, ADA_TN), lambda l, j: (l, 0, j)),
            pl.BlockSpec((None, 1, ADA_TN), lambda l, j: (l, 0, j)),
        ],
        out_specs=pl.BlockSpec((None, 8, ADA_TN), lambda l, j: (l, 0, j)),
        out_shape=jax.ShapeDtypeStruct((DEPTH, 8, n), F32),
        compiler_params=pltpu.CompilerParams(
            dimension_semantics=("arbitrary", "arbitrary"), vmem_limit_bytes=VMEM_LIMIT),
        name="ada",
    )(cond8, w_ada, b_ada.reshape(DEPTH, 1, n))


FFN_TM = 1024
FFN_FC = 256


def _ffn_kernel(x_ref, mp_ref, wi_ref, wo_ref, o_ref, h_ref):
    x = x_ref[...]
    mp = mp_ref[...]
    xb = _mod_norm(x, mp).astype(BF16)
    for c in range(FFN_DIM // FFN_FC):
        lo = c * FFN_FC
        g = _dot(xb, wi_ref[:, lo:lo + FFN_FC])
        u = _dot(xb, wi_ref[:, FFN_DIM + lo:FFN_DIM + lo + FFN_FC])
        h_ref[:, lo:lo + FFN_FC] = (_silu(g) * u).astype(BF16)
    y = _dot(h_ref[...], wo_ref[...])
    o_ref[...] = x + (0.5 * mp[2:3]) * y


def _ffn(x, mp, rows_per_mod, w_in, w_out, l, j):
    n = x.shape[0]
    tm = FFN_TM
    return pl.pallas_call(
        _ffn_kernel,
        grid=(n // tm,),
        in_specs=[
            pl.BlockSpec((tm, D_MODEL), lambda i: (i, 0)),
            pl.BlockSpec((None, 8, D_MODEL), lambda i: ((i * tm) // rows_per_mod, 0, 0)),
            _resident((None, None, D_MODEL, 2 * FFN_DIM), lambda i: (l, j, 0, 0)),
            _resident((None, None, FFN_DIM, D_MODEL), lambda i: (l, j, 0, 0)),
        ],
        out_specs=pl.BlockSpec((tm, D_MODEL), lambda i: (i, 0)),
        out_shape=jax.ShapeDtypeStruct((n, D_MODEL), F32),
        scratch_shapes=[pltpu.VMEM((tm, FFN_DIM), BF16)],
        compiler_params=pltpu.CompilerParams(
            dimension_semantics=("arbitrary",), vmem_limit_bytes=VMEM_LIMIT),
        name="ffn",
    )(x, mp, w_in, w_out)


INPROJ_TM = 512
Q_SCALE = HEAD_DIM ** -0.5 * math.log2(math.e)


def _group_sumsq(a, bd):
    sq = a * a
    hi = sq.astype(BF16)
    lo = (sq - hi.astype(F32)).astype(BF16)
    return _dot(hi, bd) + _dot(lo, bd)


def _group_rms(a, sumsq, gain):
    return (a * lax.rsqrt(sumsq * (1.0 / HEAD_DIM) + EPS)) * gain


def _rope(x, cos, sin_signed):
    half = HEAD_DIM // 2
    reps = x.shape[1] // LANES
    lane = lax.broadcasted_iota(jnp.int32, (x.shape[0], LANES), 1)
    first = (lane % HEAD_DIM) < half
    out = []
    for r in range(reps):
        xs = x[:, r * LANES:(r + 1) * LANES]
        partner = jnp.where(first, pltpu.roll(xs, LANES - half, axis=1), pltpu.roll(xs, half, axis=1))
        out.append(xs * cos + partner * sin_signed)
    return out[0] if reps == 1 else jnp.concatenate(out, axis=1)


def _tile_kv(a):
    lane = lax.broadcasted_iota(jnp.int32, a.shape, 1)
    sw = pltpu.roll(a, HEAD_DIM, axis=1)
    h0 = jnp.where(lane < HEAD_DIM, a, sw)
    h1 = jnp.where(lane < HEAD_DIM, sw, a)
    return jnp.concatenate([h0, h0, h1, h1], axis=1)


def _inproj_kernel(*refs, use_rope, emit_kv):
    x_ref, mp_ref, w_ref, qg_ref, kg_ref, bd_ref = refs[:6]
    pos = 6
    if use_rope:
        cos_ref, sin_ref = refs[pos:pos + 2]
        pos += 2
    ret_ref, q_ref, kt_ref, vT_ref, cb_ref, u_ref, gates_ref = refs[pos:pos + 7]
    pos += 7
    if emit_kv:
        kout_ref, vout_ref = refs[pos:pos + 2]

    xb = _mod_norm(x_ref[...], mp_ref[...]).astype(BF16)

    def proj(off, width):
        return _dot(xb, w_ref[:, off:off + width])

    def gate_chunk(c):
        gates_ref[:, c * 512:(c + 1) * 512] = _sigmoid(proj(OFF_GATES + c * 512, 512)).astype(BF16)

    aq = proj(OFF_AQ, ATT_Q_W)
    gate_chunk(0)
    ss_q = _group_sumsq(aq, bd_ref[...])
    ak = proj(OFF_AK, ATT_KV_W)
    v = proj(OFF_AV, ATT_KV_W)
    gate_chunk(1)

    q = _group_rms(aq, ss_q, qg_ref[...])
    if use_rope:
        q = _rope(q, cos_ref[...], sin_ref[...])
    q_ref[...] = (q * Q_SCALE).astype(BF16)
    ss_k = _group_sumsq(ak, bd_ref[0:ATT_KV_W, 0:ATT_KV_W])
    ret_ref[:, OFF_RQ:OFF_RQ + RET_W] = proj(OFF_RQ, RET_W).astype(BF16)

    k = _group_rms(ak, ss_k, kg_ref[...])
    if emit_kv:
        kout_ref[...] = k
        vout_ref[...] = v
    if use_rope:
        k = _rope(k, cos_ref[...], sin_ref[...])
    kt_ref[...] = _tile_kv(k).astype(BF16)
    vT_ref[...] = v.T.astype(BF16)
    gate_chunk(2)

    ret_ref[:, OFF_RK:OFF_RK + RET_W] = (proj(OFF_RK, RET_W) * (RET_DK ** -0.5)).astype(BF16)
    gate_chunk(3)
    ret_ref[:, OFF_RV:OFF_RV + RET_W] = proj(OFF_RV, RET_W).astype(BF16)
    ret_ref[:, OFF_RG:OFF_RG + RET_W] = _silu(proj(OFF_RG, RET_W)).astype(BF16)
    gate_chunk(4)
    cb_ref[...] = proj(OFF_CB, CONV_DIM).astype(BF16)
    u_ref[...] = (proj(OFF_CC, CONV_DIM) * proj(OFF_CX, CONV_DIM)).astype(BF16)
    gate_chunk(5)


def _inproj(x, mp, rows_per_mod, w_in, l, q_gain_t, k_gain_t, bd, rope, seq_len, emit_kv):
    n = x.shape[0]
    tm = INPROJ_TM
    use_rope = rope is not None
    row = lambda i: (i, 0)
    in_specs = [
        pl.BlockSpec((tm, D_MODEL), row),
        pl.BlockSpec((None, 8, D_MODEL), lambda i: ((i * tm) // rows_per_mod, 0, 0)),
        _resident((None, D_MODEL, IN_COLS), lambda i: (l, 0, 0)),
        pl.BlockSpec((None, 1, ATT_Q_W), lambda i: (l, 0, 0)),
        pl.BlockSpec((None, 1, ATT_KV_W), lambda i: (l, 0, 0)),
        _resident((ATT_Q_W, ATT_Q_W), lambda i: (0, 0)),
    ]
    args = [x, mp, w_in, q_gain_t, k_gain_t, bd]
    if use_rope:
        tiles_per_seq = seq_len // tm
        in_specs += [pl.BlockSpec((tm, LANES), lambda i: (i % tiles_per_seq, 0))] * 2
        args += list(rope)
    widths = [4 * RET_W, ATT_Q_W, ATT_Q_W, None, CONV_DIM, CONV_DIM, 3 * D_MODEL]
    out_specs = [pl.BlockSpec((tm, w), row) for w in widths if w]
    out_shape = [jax.ShapeDtypeStruct((n, w), BF16) for w in widths if w]
    out_specs.insert(3, pl.BlockSpec((ATT_KV_W, tm), lambda i: (0, i)))
    out_shape.insert(3, jax.ShapeDtypeStruct((ATT_KV_W, n), BF16))
    if emit_kv:
        out_specs += [pl.BlockSpec((tm, ATT_KV_W), row)] * 2
        out_shape += [jax.ShapeDtypeStruct((n, ATT_KV_W), F32)] * 2
    return pl.pallas_call(
        functools.partial(_inproj_kernel, use_rope=use_rope, emit_kv=emit_kv),
        grid=(n // tm,),
        in_specs=in_specs,
        out_specs=out_specs,
        out_shape=out_shape,
        compiler_params=pltpu.CompilerParams(
            dimension_semantics=("arbitrary",), vmem_limit_bytes=VMEM_LIMIT),
        name="inproj",
    )(*args)


RET_CHUNK_UNROLL = 8
RET_SEQ_UNROLL = 4
TAB_DMASK, TAB_DQF, TAB_DQB, TAB_DKF, TAB_DKB, TAB_DCF, TAB_DCB, N_TAB = range(8)


def _log_sigmoid(x):
    return jnp.minimum(x, 0.0) - jnp.log(1.0 + jnp.exp(-jnp.abs(x)))


def _chunk_rows(start):
    if isinstance(start, int):
        return pl.ds(start, RET_CHUNK)
    return pl.ds(pl.multiple_of(start, RET_CHUNK), RET_CHUNK)


def _retention_kernel(*refs, n_seq, n_chunks, has_s0, emit_state):
    dl_ref, q_ref, k_ref, v_ref, g_ref, gn_ref = refs[:6]
    pos = 6
    if has_s0:
        s0_ref = refs[pos]
        pos += 1
    o_ref = refs[pos]
    pos += 1
    if emit_state:
        st_ref = refs[pos]
        pos += 1
    tab_ref, kv_ref, ent_ref = refs[pos:pos + 3]

    C = RET_CHUNK
    hd = pl.program_id(1)
    row = lax.broadcasted_iota(jnp.int32, (C, C), 0).astype(F32)
    col = lax.broadcasted_iota(jnp.int32, (C, C), 1).astype(F32)
    diff = row - col
    lgf = _log_sigmoid(jnp.full((C, C), dl_ref[0, hd], F32))
    lgb = _log_sigmoid(jnp.full((C, C), dl_ref[1, hd], F32))
    tab_ref[TAB_DMASK] = (jnp.where(diff >= 0, jnp.exp(lgf * jnp.maximum(diff, 0.0)), 0.0)
                          + jnp.where(diff <= 0, jnp.exp(lgb * jnp.maximum(-diff, 0.0)), 0.0))
    tab_ref[TAB_DQF] = jnp.exp(lgf * (row + 1.0))
    tab_ref[TAB_DQB] = jnp.exp(lgb * (C - row))
    tab_ref[TAB_DKF] = jnp.exp(lgf * (C - 1.0 - row))
    tab_ref[TAB_DKB] = jnp.exp(lgb * row)
    tab_ref[TAB_DCF] = jnp.exp(lgf * C)
    tab_ref[TAB_DCB] = jnp.exp(lgb * C)

    def kv_phase(items):
        ops = []
        for r, _ in items:
            rows = _chunk_rows(r)
            k = k_ref[rows, :].astype(F32)
            vT = v_ref[rows, :].astype(F32).T.astype(BF16)
            kk = jnp.concatenate([(k * tab_ref[TAB_DKF]).astype(BF16),
                                  (k * tab_ref[TAB_DKB]).astype(BF16)], axis=1)
            ops.append((vT, kk))
        for (_, i), (vT, kk) in zip(items, ops):
            kv_ref[i] = _dot(vT, kk)

    def scan_phase(s, sc):
        if has_s0:
            st0 = (s0_ref[0].T, s0_ref[1].T)
        else:
            st0 = (jnp.zeros((RET_DV, RET_DK), F32), jnp.zeros((RET_DV, RET_DK), F32))

        def scan_step(t, st):
            st_f, st_b = st
            cb = n_chunks - 1 - t
            ent_ref[sc + t, :, 0:RET_DK] = st_f.astype(BF16)
            ent_ref[sc + cb, :, RET_DK:2 * RET_DK] = st_b.astype(BF16)
            st_f = st_f * tab_ref[TAB_DCF] + kv_ref[sc + t, :, 0:RET_DK]
            st_b = st_b * tab_ref[TAB_DCB] + kv_ref[sc + cb, :, RET_DK:2 * RET_DK]
            return st_f, st_b

        if n_chunks <= RET_CHUNK_UNROLL:
            st = st0
            for t in range(n_chunks):
                st = scan_step(t, st)
        else:
            st = lax.fori_loop(0, n_chunks, scan_step, st0)
        if emit_state:
            st_ref[s, 0] = st[0].T
            st_ref[s, 1] = st[1].T

    def out_phase(items):
        qs = [q_ref[_chunk_rows(r), :] for r, _ in items]
        att = [_dot_nt(q, k_ref[_chunk_rows(r), :]) for q, (r, _) in zip(qs, items)]
        att = [(a * tab_ref[TAB_DMASK]).astype(BF16) for a in att]
        qq = []
        for q in qs:
            qf = q.astype(F32)
            qq.append(jnp.concatenate([(qf * tab_ref[TAB_DQF]).astype(BF16),
                                       (qf * tab_ref[TAB_DQB]).astype(BF16)], axis=1))
        outs = [_dot(a, v_ref[_chunk_rows(r), :]) + _dot_nt(x, ent_ref[i])
                for a, x, (r, i) in zip(att, qq, items)]
        for o, (r, _) in zip(outs, items):
            rows = _chunk_rows(r)
            ms = jnp.mean(o * o, axis=-1, keepdims=True)
            on = (o * lax.rsqrt(ms + EPS)) * gn_ref[...]
            o_ref[rows, :] = (g_ref[rows, :].astype(F32) * on).astype(BF16)

    if n_seq == 1:
        assert n_chunks % RET_CHUNK_UNROLL == 0

        def chunk_group(phase):
            def step(t, carry):
                c0 = t * RET_CHUNK_UNROLL
                phase([((c0 + j) * C, c0 + j) for j in range(RET_CHUNK_UNROLL)])
                return carry
            lax.fori_loop(0, n_chunks // RET_CHUNK_UNROLL, step, 0)

        chunk_group(kv_phase)
        scan_phase(0, 0)
        chunk_group(out_phase)
    else:
        assert n_seq % RET_SEQ_UNROLL == 0 and n_chunks <= RET_CHUNK_UNROLL

        def seq_group(t, carry):
            s0 = t * RET_SEQ_UNROLL
            items = [((s0 + j) * (n_chunks * C) + c * C, j * n_chunks + c)
                     for j in range(RET_SEQ_UNROLL) for c in range(n_chunks)]
            kv_phase(items)
            for j in range(RET_SEQ_UNROLL):
                scan_phase(s0 + j, j * n_chunks)
            out_phase(items)
            return carry

        lax.fori_loop(0, n_seq // RET_SEQ_UNROLL, seq_group, 0)


def _retention(ret_in, decay_logit_l, gn_l, s0, seq_len, n_seq_blk, emit_state):
    n = ret_in.shape[0]
    n_chunks = seq_len // RET_CHUNK
    tb = n_seq_blk * seq_len
    has_s0 = s0 is not None
    n_off = RET_W // RET_DK

    def branch(k):
        return pl.BlockSpec((tb, RET_DK), lambda i, h: (i, k * n_off + h))

    in_specs = [pl.BlockSpec(memory_space=pltpu.SMEM), branch(0), branch(1), branch(2), branch(3),
                pl.BlockSpec((1, RET_DV), lambda i, h: (0, h))]
    args = [decay_logit_l, ret_in, ret_in, ret_in, ret_in, gn_l]
    if has_s0:
        assert n_seq_blk == 1
        in_specs.append(pl.BlockSpec((None, 2, None, RET_DK, RET_DV), lambda i, h: (i, 0, h, 0, 0)))
        args.append(s0)
    out_specs = [pl.BlockSpec((tb, RET_DV), lambda i, h: (i, h))]
    out_shape = [jax.ShapeDtypeStruct((n, RET_W), BF16)]
    if emit_state:
        out_specs.append(pl.BlockSpec((n_seq_blk, 2, None, RET_DK, RET_DV), lambda i, h: (i, 0, h, 0, 0)))
        out_shape.append(jax.ShapeDtypeStruct((n // seq_len, 2, N_RET_HEADS, RET_DK, RET_DV), F32))
    n_slots = n_chunks * (1 if n_seq_blk == 1 else RET_SEQ_UNROLL)
    return pl.pallas_call(
        functools.partial(_retention_kernel, n_seq=n_seq_blk, n_chunks=n_chunks, has_s0=has_s0,
                          emit_state=emit_state),
        grid=(n // tb, N_RET_HEADS),
        in_specs=in_specs,
        out_specs=out_specs,
        out_shape=out_shape,
        scratch_shapes=[
            pltpu.VMEM((N_TAB, RET_CHUNK, RET_CHUNK), F32),
            pltpu.VMEM((n_slots, RET_DV, 2 * RET_DK), F32),
            pltpu.VMEM((n_slots, RET_DV, 2 * RET_DK), BF16),
        ],
        compiler_params=pltpu.CompilerParams(
            dimension_semantics=("arbitrary", "arbitrary"), vmem_limit_bytes=VMEM_LIMIT),
        name="retention",
    )(*args)


ATT_TQ = 1024
ATT_KB = 512
ATT_RC = 16
NEG_BIG = -1e30
SUM_ROWS = BF16_SUBLANES


def _attention_kernel(*refs, n_cache, n_new):
    q_ref = refs[0]
    pos = 1
    if n_cache:
        kc_ref, vc_ref = refs[pos:pos + 2]
        pos += 2
    kn_ref, vn_ref, o_ref, qs_ref, acc_ref = refs[pos:pos + 5]
    s_slots = refs[pos + 5:pos + 7]
    p_slots = refs[pos + 7:pos + 9]

    tq = q_ref.shape[0]
    nq = GROUP * tq
    qf = q_ref[...].astype(F32)
    head = lax.broadcasted_iota(jnp.int32, (tq, KV_LANES), 1) // HEAD_DIM
    for g in range(GROUP):
        qs_ref[:, g * tq:(g + 1) * tq] = jnp.where(head == g, qf, 0.0).T.astype(BF16)
    acc_ref[...] = jnp.zeros(acc_ref.shape, F32)

    kb = min(ATT_KB, n_new)
    n_cb = 1 if n_cache else 0
    n_nb = n_new // kb
    assert n_cache <= kb and n_new % kb == 0
    n_blk = n_cb + n_nb

    def block(t):
        if isinstance(t, int) and t < n_cb:
            return kc_ref, vc_ref, pl.ds(0, n_cache)
        lo = (t - n_cb) * kb
        return kn_ref, vn_ref, (pl.ds(lo, kb) if isinstance(lo, int) else pl.ds(pl.multiple_of(lo, kb), kb))

    def scores(t, par):
        k_ref, _, keys = block(t)
        s = _dot(k_ref[keys, :], qs_ref[...])
        s_slots[par][0:keys.size, :] = s
        return jnp.max(s, axis=0, keepdims=True)

    def softmax(n_keys, par, m, s_max):
        s_ref, p_ref = s_slots[par], p_slots[par]
        m_new = jnp.maximum(m, s_max)
        m_rows = jnp.broadcast_to(m_new, (ATT_RC, nq))
        for r in range(0, n_keys, ATT_RC):
            p_ref[r:r + ATT_RC, :] = jnp.exp2(s_ref[r:r + ATT_RC, :] - m_rows).astype(BF16)
        return m_new, jnp.exp2(m - m_new)

    def pv(t, par, alpha):
        _, vT_ref, keys = block(t)
        v1 = jnp.concatenate([vT_ref[:, keys], jnp.ones((SUM_ROWS, keys.size), BF16)], axis=0)
        acc_ref[...] = alpha * acc_ref[...] + _dot(v1, p_slots[par][0:keys.size, :])

    def stage(t, par, carry):
        m, alpha, s_max = carry
        next_max = s_max
        if not isinstance(t, int) or t + 1 < n_blk:
            next_max = scores(t + 1, 1 - par)
        if not isinstance(t, int) or t >= 1:
            pv(t - 1, 1 - par, alpha)
        return softmax(block(t)[2].size, par, m, s_max) + (next_max,)

    carry = (jnp.full((1, nq), NEG_BIG, F32), jnp.zeros((1, nq), F32), scores(0, 0))
    lo_t = min(n_cb + 1, n_blk)
    n_loop = max(n_blk - 1 - lo_t, 0)
    if n_loop % 2:
        lo_t += 1
        n_loop -= 1
    for t in range(lo_t):
        carry = stage(t, t % 2, carry)

    def pair(i, carry):
        t = lo_t + 2 * i
        carry = stage(t, lo_t % 2, carry)
        return stage(t + 1, (lo_t + 1) % 2, carry)

    carry = lax.fori_loop(0, n_loop // 2, pair, carry)
    for t in range(lo_t + n_loop, n_blk):
        carry = stage(t, t % 2, carry)
    pv(n_blk - 1, (n_blk - 1) % 2, carry[1])

    acc = acc_ref[...]
    oT = acc[0:HEAD_DIM] / acc[HEAD_DIM:HEAD_DIM + 1]
    o4 = jnp.concatenate([oT[:, g * tq:(g + 1) * tq] for g in range(GROUP)], axis=0)
    o_ref[...] = o4.T.astype(BF16)


ATT_SEQ_PER_STEP = 2


def _attention_short_kernel(q_ref, k_ref, vT_ref, o_ref, *, n_seq):
    t_len = q_ref.shape[0] // n_seq
    head = lax.broadcasted_iota(jnp.int32, (t_len, KV_LANES), 1) // HEAD_DIM
    ones = jnp.ones((SUM_ROWS, t_len), BF16)
    for s in range(n_seq):
        rows = slice(s * t_len, (s + 1) * t_len)
        for kv in range(N_KV_HEADS):
            cols = slice(kv * KV_LANES, (kv + 1) * KV_LANES)
            qf = q_ref[rows, cols].astype(F32)
            qsT = jnp.concatenate([jnp.where(head == g, qf, 0.0).T.astype(BF16) for g in range(GROUP)],
                                  axis=1)
            sT = _dot(k_ref[rows, cols], qsT)
            pT = jnp.exp2(sT - jnp.max(sT, axis=0, keepdims=True)).astype(BF16)
            v1 = jnp.concatenate([vT_ref[kv * HEAD_DIM:(kv + 1) * HEAD_DIM, rows], ones], axis=0)
            acc = _dot(v1, pT)
            oT = acc[0:HEAD_DIM] / acc[HEAD_DIM:HEAD_DIM + 1]
            o4 = jnp.concatenate([oT[:, g * t_len:(g + 1) * t_len] for g in range(GROUP)], axis=0)
            o_ref[rows, cols] = o4.T.astype(BF16)


def _attention_short(q, kt, vT, seq_len):
    n = q.shape[0]
    tb = ATT_SEQ_PER_STEP * seq_len
    return pl.pallas_call(
        functools.partial(_attention_short_kernel, n_seq=ATT_SEQ_PER_STEP),
        grid=(n // tb,),
        in_specs=[pl.BlockSpec((tb, ATT_Q_W), lambda i: (i, 0)),
                  pl.BlockSpec((tb, ATT_Q_W), lambda i: (i, 0)),
                  pl.BlockSpec((ATT_KV_W, tb), lambda i: (0, i))],
        out_specs=pl.BlockSpec((tb, ATT_Q_W), lambda i: (i, 0)),
        out_shape=jax.ShapeDtypeStruct((n, ATT_Q_W), BF16),
        compiler_params=pltpu.CompilerParams(
            dimension_semantics=("arbitrary",), vmem_limit_bytes=VMEM_LIMIT),
        name="attention_short",
    )(q, kt, vT)


def _attention(q, kt, vT, cache, seq_len):
    n = q.shape[0]
    n_batch = n // seq_len
    tq = ATT_TQ
    nq = seq_len // tq
    n_cache = 0 if cache is None else cache[0].shape[1]
    if n_cache == 0 and seq_len <= ATT_KB:
        return _attention_short(q, kt, vT, seq_len)
    in_specs = [pl.BlockSpec((tq, KV_LANES), lambda b, kv, i: (b * nq + i, kv))]
    args = [q]
    if n_cache:
        in_specs += [pl.BlockSpec((None, n_cache, KV_LANES), lambda b, kv, i: (b, 0, kv)),
                     pl.BlockSpec((None, HEAD_DIM, n_cache), lambda b, kv, i: (b, kv, 0))]
        args += list(cache)
    in_specs += [pl.BlockSpec((seq_len, KV_LANES), lambda b, kv, i: (b, kv)),
                 pl.BlockSpec((HEAD_DIM, seq_len), lambda b, kv, i: (kv, b))]
    args += [kt, vT]
    return pl.pallas_call(
        functools.partial(_attention_kernel, n_cache=n_cache, n_new=seq_len),
        grid=(n_batch, N_KV_HEADS, nq),
        in_specs=in_specs,
        out_specs=pl.BlockSpec((tq, KV_LANES), lambda b, kv, i: (b * nq + i, kv)),
        out_shape=jax.ShapeDtypeStruct((n, ATT_Q_W), BF16),
        scratch_shapes=[
            pltpu.VMEM((KV_LANES, GROUP * tq), BF16),
            pltpu.VMEM((HEAD_DIM + SUM_ROWS, GROUP * tq), F32),
            pltpu.VMEM((min(ATT_KB, seq_len), GROUP * tq), F32),
            pltpu.VMEM((min(ATT_KB, seq_len), GROUP * tq), F32),
            pltpu.VMEM((min(ATT_KB, seq_len), GROUP * tq), BF16),
            pltpu.VMEM((min(ATT_KB, seq_len), GROUP * tq), BF16),
        ],
        compiler_params=pltpu.CompilerParams(
            dimension_semantics=("arbitrary", "arbitrary", "arbitrary"), vmem_limit_bytes=VMEM_LIMIT),
        name="attention",
    )(*args)


MERGE_TM = 1024


def _merge_kernel(h_ref, mp_ref, ro_ref, ao_ref, cb_ref, u_ref, up_ref, un_ref, g_ref, cw_ref,
                  wr_ref, wa_ref, wc_ref, wo_ref, o_ref, *, seq_len):
    tm = h_ref.shape[0]
    i = pl.program_id(0)
    yr = _dot(ro_ref[...], wr_ref[...])
    ya = _dot(ao_ref[...], wa_ref[...])
    u = u_ref[...].astype(F32)
    r = lax.broadcasted_iota(jnp.int32, u.shape, 0)
    t = (i * tm + r) % seq_len
    prev_row = up_ref[BF16_SUBLANES - 1:BF16_SUBLANES, :].astype(F32)
    next_row = un_ref[0:1, :].astype(F32)
    u_prev = jnp.where(r == 0, prev_row, pltpu.roll(u, 1, axis=0))
    u_next = jnp.where(r == tm - 1, next_row, pltpu.roll(u, tm - 1, axis=0))
    u_prev = jnp.where(t == 0, 0.0, u_prev)
    u_next = jnp.where(t == seq_len - 1, 0.0, u_next)
    cw = cw_ref[...]
    conv = u_prev * cw[0:1] + u * cw[1:2] + u_next * cw[2:3]
    yc = _dot((cb_ref[...].astype(F32) * conv).astype(BF16), wc_ref[...])
    merged = (g_ref[:, 0:D_MODEL].astype(F32) * yr
              + g_ref[:, D_MODEL:2 * D_MODEL].astype(F32) * ya
              + g_ref[:, 2 * D_MODEL:3 * D_MODEL].astype(F32) * yc)
    y = _dot(merged.astype(BF16), wo_ref[...])
    o_ref[...] = h_ref[...] + mp_ref[2:3, :] * y


def _merge(h, mp, rows_per_mod, ret_o, att_o, cb, u, gates, conv_w8, w_ret_o, w_att_o, w_conv_o, w_o,
           l, seq_len):
    n = h.shape[0]
    tm = MERGE_TM
    hb = tm // BF16_SUBLANES
    last = n // BF16_SUBLANES - 1
    row = lambda i: (i, 0)
    wspec = lambda k: _resident((None, k, D_MODEL), lambda i: (l, 0, 0))
    return pl.pallas_call(
        functools.partial(_merge_kernel, seq_len=seq_len),
        grid=(n // tm,),
        in_specs=[
            pl.BlockSpec((tm, D_MODEL), row),
            pl.BlockSpec((None, 8, D_MODEL), lambda i: ((i * tm) // rows_per_mod, 0, 0)),
            pl.BlockSpec((tm, RET_W), row),
            pl.BlockSpec((tm, ATT_Q_W), row),
            pl.BlockSpec((tm, CONV_DIM), row),
            pl.BlockSpec((tm, CONV_DIM), row),
            pl.BlockSpec((BF16_SUBLANES, CONV_DIM), lambda i: (jnp.maximum(i * hb - 1, 0), 0)),
            pl.BlockSpec((BF16_SUBLANES, CONV_DIM), lambda i: (jnp.minimum((i + 1) * hb, last), 0)),
            pl.BlockSpec((tm, 3 * D_MODEL), row),
            pl.BlockSpec((None, 8, CONV_DIM), lambda i: (l, 0, 0)),
            wspec(RET_W), wspec(ATT_Q_W), wspec(CONV_DIM), wspec(D_MODEL),
        ],
        out_specs=pl.BlockSpec((tm, D_MODEL), row),
        out_shape=jax.ShapeDtypeStruct((n, D_MODEL), F32),
        compiler_params=pltpu.CompilerParams(
            dimension_semantics=("arbitrary",), vmem_limit_bytes=VMEM_LIMIT),
        name="merge",
    )(h, mp, ret_o, att_o, cb, u, u, u, gates, conv_w8, w_ret_o, w_att_o, w_conv_o, w_o)


def _rope_tables(n_tok):
    rows = n_tok // GRID_W
    t_row = np.repeat(np.arange(rows, dtype=np.float64), GRID_W)
    t_col = np.tile(np.arange(GRID_W, dtype=np.float64), rows)
    n_freq = HEAD_DIM // 4
    inv = ROPE_THETA ** (-np.arange(n_freq, dtype=np.float64) / n_freq)
    ang = np.concatenate([t_row[:, None] * inv, t_col[:, None] * inv], axis=-1)
    cos, sin = np.cos(ang), np.sin(ang)
    cos64 = np.concatenate([cos, cos], axis=-1)
    sin64 = np.concatenate([-sin, sin], axis=-1)
    return (jnp.asarray(np.tile(cos64, (1, LANES // HEAD_DIM)), F32),
            jnp.asarray(np.tile(sin64, (1, LANES // HEAD_DIM)), F32))


def _mod_pack(mod_l, rows, sub, norm_w_row):
    r0 = rows[0]
    nr = len(rows)
    m = mod_l[r0:r0 + nr, 3 * sub * D_MODEL:3 * (sub + 1) * D_MODEL].reshape(nr, 3, D_MODEL)
    nw = jnp.broadcast_to(norm_w_row[None, None, :], (nr, 1, D_MODEL))
    pad = jnp.zeros((nr, 4, D_MODEL), F32)
    return jnp.concatenate([m, nw, pad], axis=1)


def kernel(x_prompt, x_sample, c, state_ret, cache_k, cache_v, c_ctx, w_ada, b_ada, norm_w, w_ffn_in,
           w_ffn_out, w_in, ret_decay_logit, ret_gn, q_gain, k_gain, conv_w, w_ret_o, w_att_o, w_conv_o,
           w_o):
    n_ctx_b, ctx_len, _ = x_prompt.shape
    n_lat_b, lat_len, _ = x_sample.shape

    w_ffn_in_b = w_ffn_in.astype(BF16)
    w_ffn_out_b = w_ffn_out.astype(BF16)
    w_in_b = w_in.astype(BF16)
    w_ret_o_b = w_ret_o.astype(BF16)
    w_att_o_b = w_att_o.astype(BF16)
    w_conv_o_b = w_conv_o.astype(BF16)
    w_o_b = w_o.astype(BF16)

    q_gain_t = jnp.tile(q_gain, (1, N_HEADS)).reshape(DEPTH, 1, ATT_Q_W)
    k_gain_t = jnp.tile(k_gain, (1, N_KV_HEADS)).reshape(DEPTH, 1, ATT_KV_W)
    conv_w8 = jnp.pad(conv_w, ((0, 0), (0, 8 - conv_w.shape[1]), (0, 0)))
    gid = np.arange(ATT_Q_W) // HEAD_DIM
    bd = jnp.asarray(gid[:, None] == gid[None, :], BF16)
    rope = _rope_tables(lat_len)

    cond8 = jnp.zeros((8, D_MODEL), F32).at[0].set(c_ctx).at[1:1 + n_lat_b].set(c)
    mod = _ada(cond8, w_ada, b_ada)

    ck = cache_k.astype(BF16)
    ck = jnp.broadcast_to(ck[:, :, :, :, None, :], ck.shape[:4] + (GROUP, HEAD_DIM))
    cache_kt = ck.reshape(ck.shape[0], ck.shape[1], ck.shape[2], N_KV_HEADS * KV_LANES)
    cache_vT = jnp.transpose(cache_v.astype(BF16), (0, 1, 3, 4, 2)).reshape(
        cache_v.shape[0], cache_v.shape[1], ATT_KV_W, cache_v.shape[2])

    groups = (
        dict(x=x_prompt.reshape(n_ctx_b * ctx_len, D_MODEL), rows=[0], seq=ctx_len, ctx=True),
        dict(x=x_sample.reshape(n_lat_b * lat_len, D_MODEL), rows=list(range(1, 1 + n_lat_b)),
             seq=lat_len, ctx=False),
    )
    results = []
    for grp in groups:
        h = grp["x"]
        seq = grp["seq"]
        is_ctx = grp["ctx"]
        rpm = h.shape[0] // len(grp["rows"])
        states, keys, values = [], [], []
        for l in range(DEPTH):
            mp = [_mod_pack(mod[l], grp["rows"], s, norm_w[l, s]) for s in range(3)]
            h = _ffn(h, mp[0], rpm, w_ffn_in_b, w_ffn_out_b, l, 0)
            outs = _inproj(h, mp[1], rpm, w_in_b, l, q_gain_t, k_gain_t, bd,
                           None if is_ctx else rope, seq, emit_kv=is_ctx)
            ret_in, qn, kt, vT, cb, u, gates = outs[:7]
            if is_ctx:
                ret_o, st = _retention(ret_in, ret_decay_logit[l], ret_gn[l][None, :], None, seq,
                                       n_seq_blk=8, emit_state=True)
                states.append(st)
                keys.append(outs[7].reshape(n_ctx_b, seq, N_KV_HEADS, HEAD_DIM))
                values.append(outs[8].reshape(n_ctx_b, seq, N_KV_HEADS, HEAD_DIM))
                att_o = _attention(qn, kt, vT, None, seq)
            else:
                (ret_o,) = _retention(ret_in, ret_decay_logit[l], ret_gn[l][None, :], state_ret[:, l],
                                      seq, n_seq_blk=1, emit_state=False)
                att_o = _attention(qn, kt, vT, (cache_kt[:, l], cache_vT[:, l]), seq)
            h = _merge(h, mp[1], rpm, ret_o, att_o, cb, u, gates, conv_w8, w_ret_o_b, w_att_o_b,
                       w_conv_o_b, w_o_b, l, seq)
            h = _ffn(h, mp[2], rpm, w_ffn_in_b, w_ffn_out_b, l, 1)
        results.append((h, states, keys, values))

    (y_ctx, states, keys, values), (y_lat, _, _, _) = results
    y_prompt = y_ctx.reshape(x_prompt.shape)
    y_sample = y_lat.reshape(x_sample.shape)
    new_state_ret = jnp.stack(states, axis=1)
    new_cache_k = jnp.stack(keys, axis=1)
    new_cache_v = jnp.stack(values, axis=1)
    return (y_prompt, y_sample, new_state_ret, new_cache_k, new_cache_v)
```

```python
import functools
import math

import numpy as np
import jax
import jax.numpy as jnp
from jax import lax
from jax.experimental import pallas as pl
from jax.experimental.pallas import tpu as pltpu

D_MODEL = 1024
DEPTH = 2
GRID_W = 64
N_RET_HEADS = 4
RET_DK = 128
RET_DV = 128
RET_CHUNK = 128
N_HEADS = 8
N_KV_HEADS = 2
HEAD_DIM = 64
ROPE_THETA = 10000.0
CONV_DIM = 512
FFN_DIM = 2816
N_MOD = 9
EPS = 1e-6
RET_W = N_RET_HEADS * RET_DK
ATT_Q_W = N_HEADS * HEAD_DIM
ATT_KV_W = N_KV_HEADS * HEAD_DIM
GROUP = N_HEADS // N_KV_HEADS
KV_LANES = GROUP * HEAD_DIM
OFF_RQ, OFF_RK, OFF_RV, OFF_RG = 0, 512, 1024, 1536
OFF_AQ, OFF_AK, OFF_AV = 2048, 2560, 2688
OFF_CB, OFF_CC, OFF_CX = 2816, 3328, 3840
OFF_GATES = 4352
IN_COLS = 7424

LANES = 128
BF16_SUBLANES = 16
VMEM_LIMIT = 56 * 1024 * 1024

F32 = jnp.float32
BF16 = jnp.bfloat16


def _dot(a, b):
    return jnp.dot(a, b, preferred_element_type=F32)


def _dot_nt(a, b):
    return lax.dot_general(a, b, (((1,), (1,)), ((), ())), preferred_element_type=F32)


def _sigmoid(x):
    return 1.0 / (1.0 + jnp.exp(-x))


def _silu(x):
    return x * _sigmoid(x)


def _mod_norm(x, mp):
    ms = jnp.mean(x * x, axis=-1, keepdims=True)
    y = x * lax.rsqrt(ms + EPS)
    return (y * mp[3:4]) * (1.0 + mp[1:2]) + mp[0:1]


def _resident(shape, index):
    return pl.BlockSpec(shape, index, pipeline_mode=pl.Buffered(1))


ADA_TN = 1152


def _ada_kernel(c_ref, w_ref, b_ref, o_ref):
    a = _silu(c_ref[...]).astype(BF16)
    o_ref[...] = _dot(a, w_ref[...].astype(BF16)) + b_ref[...]


def _ada(cond8, w_ada, b_ada):
    n = N_MOD * D_MODEL
    return pl.pallas_call(
        _ada_kernel,
        grid=(DEPTH, n // ADA_TN),
        in_specs=[
            pl.BlockSpec((8, D_MODEL), lambda l, j: (0, 0)),
            pl.BlockSpec((None, D_MODEL, ADA_TN), lambda l, j: (l, 0, j)),
            pl.BlockSpec((None, 1, ADA_TN), lambda l, j: (l, 0, j)),
        ],
        out_specs=pl.BlockSpec((None, 8, ADA_TN), lambda l, j: (l, 0, j)),
        out_shape=jax.ShapeDtypeStruct((DEPTH, 8, n), F32),
        compiler_params=pltpu.CompilerParams(
            dimension_semantics=("arbitrary", "arbitrary"), vmem_limit_bytes=VMEM_LIMIT),
        name="ada",
    )(cond8, w_ada, b_ada.reshape(DEPTH, 1, n))


FFN_TM = 1024
FFN_FC = 256


def _ffn_kernel(x_ref, mp_ref, wi_ref, wo_ref, o_ref, h_ref):
    x = x_ref[...]
    mp = mp_ref[...]
    xb = _mod_norm(x, mp).astype(BF16)
    for c in range(FFN_DIM // FFN_FC):
        lo = c * FFN_FC
        g = _dot(xb, wi_ref[:, lo:lo + FFN_FC])
        u = _dot(xb, wi_ref[:, FFN_DIM + lo:FFN_DIM + lo + FFN_FC])
        h_ref[:, lo:lo + FFN_FC] = (_silu(g) * u).astype(BF16)
    y = _dot(h_ref[...], wo_ref[...])
    o_ref[...] = x + (0.5 * mp[2:3]) * y


def _ffn(x, mp, rows_per_mod, w_in, w_out, l, j):
    n = x.shape[0]
    tm = FFN_TM
    return pl.pallas_call(
        _ffn_kernel,
        grid=(n // tm,),
        in_specs=[
            pl.BlockSpec((tm, D_MODEL), lambda i: (i, 0)),
            pl.BlockSpec((None, 8, D_MODEL), lambda i: ((i * tm) // rows_per_mod, 0, 0)),
            _resident((None, None, D_MODEL, 2 * FFN_DIM), lambda i: (l, j, 0, 0)),
            _resident((None, None, FFN_DIM, D_MODEL), lambda i: (l, j, 0, 0)),
        ],
        out_specs=pl.BlockSpec((tm, D_MODEL), lambda i: (i, 0)),
        out_shape=jax.ShapeDtypeStruct((n, D_MODEL), F32),
        scratch_shapes=[pltpu.VMEM((tm, FFN_DIM), BF16)],
        compiler_params=pltpu.CompilerParams(
            dimension_semantics=("arbitrary",), vmem_limit_bytes=VMEM_LIMIT),
        name="ffn",
    )(x, mp, w_in, w_out)


INPROJ_TM = 512
INPROJ_SUB = 256
Q_SCALE = HEAD_DIM ** -0.5 * math.log2(math.e)


def _group_sumsq(a, bd):
    sq = a * a
    hi = sq.astype(BF16)
    lo = (sq - hi.astype(F32)).astype(BF16)
    return _dot(hi, bd) + _dot(lo, bd)


def _group_rms(a, sumsq, gain):
    return (a * lax.rsqrt(sumsq * (1.0 / HEAD_DIM) + EPS)) * gain


def _rope(x, cos, sin_signed):
    half = HEAD_DIM // 2
    reps = x.shape[1] // LANES
    lane = lax.broadcasted_iota(jnp.int32, (x.shape[0], LANES), 1)
    first = (lane % HEAD_DIM) < half
    out = []
    for r in range(reps):
        xs = x[:, r * LANES:(r + 1) * LANES]
        partner = jnp.where(first, pltpu.roll(xs, LANES - half, axis=1), pltpu.roll(xs, half, axis=1))
        out.append(xs * cos + partner * sin_signed)
    return out[0] if reps == 1 else jnp.concatenate(out, axis=1)


def _tile_kv(a):
    lane = lax.broadcasted_iota(jnp.int32, a.shape, 1)
    sw = pltpu.roll(a, HEAD_DIM, axis=1)
    h0 = jnp.where(lane < HEAD_DIM, a, sw)
    h1 = jnp.where(lane < HEAD_DIM, sw, a)
    return jnp.concatenate([h0, h0, h1, h1], axis=1)


def _inproj_kernel(*refs, use_rope, emit_kv):
    x_ref, mp_ref, w_ref, qg_ref, kg_ref, bd_ref = refs[:6]
    pos = 6
    if use_rope:
        cos_ref, sin_ref = refs[pos:pos + 2]
        pos += 2
    ret_ref, q_ref, kt_ref, vT_ref, cb_ref, u_ref, gates_ref = refs[pos:pos + 7]
    pos += 7
    if emit_kv:
        kout_ref, vout_ref = refs[pos:pos + 2]

    mp = mp_ref[...]
    tm = x_ref.shape[0]
    n_sub = tm // INPROJ_SUB
    xbs = [None] * n_sub
    xbs[0] = _mod_norm(x_ref[0:INPROJ_SUB, :], mp).astype(BF16)
    for s in range(n_sub):
        rows = slice(s * INPROJ_SUB, (s + 1) * INPROJ_SUB)
        xb = xbs[s]

        def proj(off, width):
            return _dot(xb, w_ref[:, off:off + width])

        def gate_chunk(c):
            gates_ref[rows, c * 512:(c + 1) * 512] = _sigmoid(proj(OFF_GATES + c * 512, 512)).astype(BF16)

        aq = proj(OFF_AQ, ATT_Q_W)
        gate_chunk(0)
        if s + 1 < n_sub:
            nxt = slice((s + 1) * INPROJ_SUB, (s + 2) * INPROJ_SUB)
            xbs[s + 1] = _mod_norm(x_ref[nxt, :], mp).astype(BF16)
        ss_q = _group_sumsq(aq, bd_ref[...])
        ak = proj(OFF_AK, ATT_KV_W)
        v = proj(OFF_AV, ATT_KV_W)
        gate_chunk(1)

        q = _group_rms(aq, ss_q, qg_ref[...])
        if use_rope:
            q = _rope(q, cos_ref[rows, :], sin_ref[rows, :])
        q_ref[rows, :] = (q * Q_SCALE).astype(BF16)
        ss_k = _group_sumsq(ak, bd_ref[0:ATT_KV_W, 0:ATT_KV_W])
        ret_ref[rows, OFF_RQ:OFF_RQ + RET_W] = proj(OFF_RQ, RET_W).astype(BF16)

        k = _group_rms(ak, ss_k, kg_ref[...])
        if emit_kv:
            kout_ref[rows, :] = k
            vout_ref[rows, :] = v
        if use_rope:
            k = _rope(k, cos_ref[rows, :], sin_ref[rows, :])
        kt_ref[rows, :] = _tile_kv(k).astype(BF16)
        vT_ref[:, rows] = v.T.astype(BF16)
        gate_chunk(2)

        ret_ref[rows, OFF_RK:OFF_RK + RET_W] = (proj(OFF_RK, RET_W) * (RET_DK ** -0.5)).astype(BF16)
        gate_chunk(3)
        ret_ref[rows, OFF_RV:OFF_RV + RET_W] = proj(OFF_RV, RET_W).astype(BF16)
        ret_ref[rows, OFF_RG:OFF_RG + RET_W] = _silu(proj(OFF_RG, RET_W)).astype(BF16)
        gate_chunk(4)
        cb_ref[rows, :] = proj(OFF_CB, CONV_DIM).astype(BF16)
        u_ref[rows, :] = (proj(OFF_CC, CONV_DIM) * proj(OFF_CX, CONV_DIM)).astype(BF16)
        gate_chunk(5)


def _inproj(x, mp, rows_per_mod, w_in, l, q_gain_t, k_gain_t, bd, rope, seq_len, emit_kv):
    n = x.shape[0]
    tm = INPROJ_TM
    use_rope = rope is not None
    row = lambda i: (i, 0)
    in_specs = [
        pl.BlockSpec((tm, D_MODEL), row),
        pl.BlockSpec((None, 8, D_MODEL), lambda i: ((i * tm) // rows_per_mod, 0, 0)),
        _resident((None, D_MODEL, IN_COLS), lambda i: (l, 0, 0)),
        pl.BlockSpec((None, 1, ATT_Q_W), lambda i: (l, 0, 0)),
        pl.BlockSpec((None, 1, ATT_KV_W), lambda i: (l, 0, 0)),
        _resident((ATT_Q_W, ATT_Q_W), lambda i: (0, 0)),
    ]
    args = [x, mp, w_in, q_gain_t, k_gain_t, bd]
    if use_rope:
        tiles_per_seq = seq_len // tm
        in_specs += [pl.BlockSpec((tm, LANES), lambda i: (i % tiles_per_seq, 0))] * 2
        args += list(rope)
    widths = [4 * RET_W, ATT_Q_W, ATT_Q_W, None, CONV_DIM, CONV_DIM, 3 * D_MODEL]
    out_specs = [pl.BlockSpec((tm, w), row) for w in widths if w]
    out_shape = [jax.ShapeDtypeStruct((n, w), BF16) for w in widths if w]
    out_specs.insert(3, pl.BlockSpec((ATT_KV_W, tm), lambda i: (0, i)))
    out_shape.insert(3, jax.ShapeDtypeStruct((ATT_KV_W, n), BF16))
    if emit_kv:
        out_specs += [pl.BlockSpec((tm, ATT_KV_W), row)] * 2
        out_shape += [jax.ShapeDtypeStruct((n, ATT_KV_W), F32)] * 2
    return pl.pallas_call(
        functools.partial(_inproj_kernel, use_rope=use_rope, emit_kv=emit_kv),
        grid=(n // tm,),
        in_specs=in_specs,
        out_specs=out_specs,
        out_shape=out_shape,
        compiler_params=pltpu.CompilerParams(
            dimension_semantics=("arbitrary",), vmem_limit_bytes=VMEM_LIMIT),
        name="inproj",
    )(*args)


RET_CHUNK_UNROLL = 8
RET_SEQ_UNROLL = 4
TAB_DMASK, TAB_DQF, TAB_DQB, TAB_DKF, TAB_DKB, TAB_DCF, TAB_DCB, N_TAB = range(8)


def _log_sigmoid(x):
    return jnp.minimum(x, 0.0) - jnp.log(1.0 + jnp.exp(-jnp.abs(x)))


def _chunk_rows(start):
    if isinstance(start, int):
        return pl.ds(start, RET_CHUNK)
    return pl.ds(pl.multiple_of(start, RET_CHUNK), RET_CHUNK)


def _retention_kernel(*refs, n_seq, n_chunks, has_s0, emit_state):
    dl_ref, q_ref, k_ref, v_ref, g_ref, gn_ref = refs[:6]
    pos = 6
    if has_s0:
        s0_ref = refs[pos]
        pos += 1
    o_ref = refs[pos]
    pos += 1
    if emit_state:
        st_ref = refs[pos]
        pos += 1
    tab_ref, kv_ref, ent_ref = refs[pos:pos + 3]

    C = RET_CHUNK
    hd = pl.program_id(1)
    row = lax.broadcasted_iota(jnp.int32, (C, C), 0).astype(F32)
    col = lax.broadcasted_iota(jnp.int32, (C, C), 1).astype(F32)
    diff = row - col
    lgf = _log_sigmoid(jnp.full((C, C), dl_ref[0, hd], F32))
    lgb = _log_sigmoid(jnp.full((C, C), dl_ref[1, hd], F32))
    tab_ref[TAB_DMASK] = (jnp.where(diff >= 0, jnp.exp(lgf * jnp.maximum(diff, 0.0)), 0.0)
                          + jnp.where(diff <= 0, jnp.exp(lgb * jnp.maximum(-diff, 0.0)), 0.0))
    tab_ref[TAB_DQF] = jnp.exp(lgf * (row + 1.0))
    tab_ref[TAB_DQB] = jnp.exp(lgb * (C - row))
    tab_ref[TAB_DKF] = jnp.exp(lgf * (C - 1.0 - row))
    tab_ref[TAB_DKB] = jnp.exp(lgb * row)
    tab_ref[TAB_DCF] = jnp.exp(lgf * C)
    tab_ref[TAB_DCB] = jnp.exp(lgb * C)

    def kv_phase(items):
        ops = []
        for r, _ in items:
            rows = _chunk_rows(r)
            k = k_ref[rows, :].astype(F32)
            vT = v_ref[rows, :].astype(F32).T.astype(BF16)
            kk = jnp.concatenate([(k * tab_ref[TAB_DKF]).astype(BF16),
                                  (k * tab_ref[TAB_DKB]).astype(BF16)], axis=1)
            ops.append((vT, kk))
        for (_, i), (vT, kk) in zip(items, ops):
            kv_ref[i] = _dot(vT, kk)

    def scan_phase(s, sc):
        if has_s0:
            st0 = (s0_ref[0].T, s0_ref[1].T)
        else:
            st0 = (jnp.zeros((RET_DV, RET_DK), F32), jnp.zeros((RET_DV, RET_DK), F32))

        def scan_step(t, st):
            st_f, st_b = st
            cb = n_chunks - 1 - t
            ent_ref[sc + t, :, 0:RET_DK] = st_f.astype(BF16)
            ent_ref[sc + cb, :, RET_DK:2 * RET_DK] = st_b.astype(BF16)
            st_f = st_f * tab_ref[TAB_DCF] + kv_ref[sc + t, :, 0:RET_DK]
            st_b = st_b * tab_ref[TAB_DCB] + kv_ref[sc + cb, :, RET_DK:2 * RET_DK]
            return st_f, st_b

        if n_chunks <= RET_CHUNK_UNROLL:
            st = st0
            for t in range(n_chunks):
                st = scan_step(t, st)
        else:
            st = lax.fori_loop(0, n_chunks, scan_step, st0)
        if emit_state:
            st_ref[s, 0] = st[0].T
            st_ref[s, 1] = st[1].T

    def out_phase(items):
        qs = [q_ref[_chunk_rows(r), :] for r, _ in items]
        att = [_dot_nt(q, k_ref[_chunk_rows(r), :]) for q, (r, _) in zip(qs, items)]
        att = [(a * tab_ref[TAB_DMASK]).astype(BF16) for a in att]
        qq = []
        for q in qs:
            qf = q.astype(F32)
            qq.append(jnp.concatenate([(qf * tab_ref[TAB_DQF]).astype(BF16),
                                       (qf * tab_ref[TAB_DQB]).astype(BF16)], axis=1))
        outs = [_dot(a, v_ref[_chunk_rows(r), :]) + _dot_nt(x, ent_ref[i])
                for a, x, (r, i) in zip(att, qq, items)]
        for o, (r, _) in zip(outs, items):
            rows = _chunk_rows(r)
            ms = jnp.mean(o * o, axis=-1, keepdims=True)
            on = (o * lax.rsqrt(ms + EPS)) * gn_ref[...]
            o_ref[rows, :] = (g_ref[rows, :].astype(F32) * on).astype(BF16)

    if n_seq == 1:
        assert n_chunks % RET_CHUNK_UNROLL == 0

        def chunk_group(phase):
            def step(t, carry):
                c0 = t * RET_CHUNK_UNROLL
                phase([((c0 + j) * C, c0 + j) for j in range(RET_CHUNK_UNROLL)])
                return carry
            lax.fori_loop(0, n_chunks // RET_CHUNK_UNROLL, step, 0)

        chunk_group(kv_phase)
        scan_phase(0, 0)
        chunk_group(out_phase)
    else:
        assert n_seq % RET_SEQ_UNROLL == 0 and n_chunks <= RET_CHUNK_UNROLL

        def seq_group(t, carry):
            s0 = t * RET_SEQ_UNROLL
            items = [((s0 + j) * (n_chunks * C) + c * C, j * n_chunks + c)
                     for j in range(RET_SEQ_UNROLL) for c in range(n_chunks)]
            kv_phase(items)
            for j in range(RET_SEQ_UNROLL):
                scan_phase(s0 + j, j * n_chunks)
            out_phase(items)
            return carry

        lax.fori_loop(0, n_seq // RET_SEQ_UNROLL, seq_group, 0)


def _retention(ret_in, decay_logit_l, gn_l, s0, seq_len, n_seq_blk, emit_state):
    n = ret_in.shape[0]
    n_chunks = seq_len // RET_CHUNK
    tb = n_seq_blk * seq_len
    has_s0 = s0 is not None
    n_off = RET_W // RET_DK

    def branch(k):
        return pl.BlockSpec((tb, RET_DK), lambda i, h: (i, k * n_off + h))

    in_specs = [pl.BlockSpec(memory_space=pltpu.SMEM), branch(0), branch(1), branch(2), branch(3),
                pl.BlockSpec((1, RET_DV), lambda i, h: (0, h))]
    args = [decay_logit_l, ret_in, ret_in, ret_in, ret_in, gn_l]
    if has_s0:
        assert n_seq_blk == 1
        in_specs.append(pl.BlockSpec((None, 2, None, RET_DK, RET_DV), lambda i, h: (i, 0, h, 0, 0)))
        args.append(s0)
    out_specs = [pl.BlockSpec((tb, RET_DV), lambda i, h: (i, h))]
    out_shape = [jax.ShapeDtypeStruct((n, RET_W), BF16)]
    if emit_state:
        out_specs.append(pl.BlockSpec((n_seq_blk, 2, None, RET_DK, RET_DV), lambda i, h: (i, 0, h, 0, 0)))
        out_shape.append(jax.ShapeDtypeStruct((n // seq_len, 2, N_RET_HEADS, RET_DK, RET_DV), F32))
    n_slots = n_chunks * (1 if n_seq_blk == 1 else RET_SEQ_UNROLL)
    return pl.pallas_call(
        functools.partial(_retention_kernel, n_seq=n_seq_blk, n_chunks=n_chunks, has_s0=has_s0,
                          emit_state=emit_state),
        grid=(n // tb, N_RET_HEADS),
        in_specs=in_specs,
        out_specs=out_specs,
        out_shape=out_shape,
        scratch_shapes=[
            pltpu.VMEM((N_TAB, RET_CHUNK, RET_CHUNK), F32),
            pltpu.VMEM((n_slots, RET_DV, 2 * RET_DK), F32),
            pltpu.VMEM((n_slots, RET_DV, 2 * RET_DK), BF16),
        ],
        compiler_params=pltpu.CompilerParams(
            dimension_semantics=("arbitrary", "arbitrary"), vmem_limit_bytes=VMEM_LIMIT),
        name="retention",
    )(*args)


ATT_TQ = 1024
ATT_KB = 512
ATT_SHORT_MAX = 512
ATT_RC = 16
NEG_BIG = -1e30
SUM_ROWS = BF16_SUBLANES


def _attention_kernel(*refs, n_cache, n_new):
    q_ref = refs[0]
    pos = 1
    if n_cache:
        kc_ref, vc_ref = refs[pos:pos + 2]
        pos += 2
    kn_ref, vn_ref, o_ref, qs_ref, acc_ref = refs[pos:pos + 5]
    s_slots = refs[pos + 5:pos + 7]
    p_slots = refs[pos + 7:pos + 9]

    tq = q_ref.shape[0]
    nq = GROUP * tq
    qf = q_ref[...].astype(F32)
    head = lax.broadcasted_iota(jnp.int32, (tq, KV_LANES), 1) // HEAD_DIM
    for g in range(GROUP):
        qs_ref[:, g * tq:(g + 1) * tq] = jnp.where(head == g, qf, 0.0).T.astype(BF16)
    acc_ref[...] = jnp.zeros(acc_ref.shape, F32)

    kb = min(ATT_KB, n_new)
    n_cb = 1 if n_cache else 0
    n_nb = n_new // kb
    assert n_new % kb == 0
    n_blk = n_cb + n_nb

    def block(t):
        if isinstance(t, int) and t < n_cb:
            return kc_ref, vc_ref, pl.ds(0, n_cache)
        lo = (t - n_cb) * kb
        return kn_ref, vn_ref, (pl.ds(lo, kb) if isinstance(lo, int) else pl.ds(pl.multiple_of(lo, kb), kb))

    def scores(t, par):
        k_ref, _, keys = block(t)
        s = _dot(k_ref[keys, :], qs_ref[...])
        s_slots[par][0:keys.size, :] = s
        return jnp.max(s, axis=0, keepdims=True)

    def softmax(n_keys, par, m, s_max):
        s_ref, p_ref = s_slots[par], p_slots[par]
        m_new = jnp.maximum(m, s_max)
        m_rows = jnp.broadcast_to(m_new, (ATT_RC, nq))
        for r in range(0, n_keys, ATT_RC):
            p_ref[r:r + ATT_RC, :] = jnp.exp2(s_ref[r:r + ATT_RC, :] - m_rows).astype(BF16)
        return m_new, jnp.exp2(m - m_new)

    def pv(t, par, alpha):
        _, vT_ref, keys = block(t)
        v1 = jnp.concatenate([vT_ref[:, keys], jnp.ones((SUM_ROWS, keys.size), BF16)], axis=0)
        acc_ref[...] = alpha * acc_ref[...] + _dot(v1, p_slots[par][0:keys.size, :])

    def stage(t, par, carry):
        m, alpha, s_max = carry
        next_max = s_max
        if not isinstance(t, int) or t + 1 < n_blk:
            next_max = scores(t + 1, 1 - par)
        if not isinstance(t, int) or t >= 1:
            pv(t - 1, 1 - par, alpha)
        return softmax(block(t)[2].size, par, m, s_max) + (next_max,)

    carry = (jnp.full((1, nq), NEG_BIG, F32), jnp.zeros((1, nq), F32), scores(0, 0))
    lo_t = min(n_cb + 1, n_blk)
    n_loop = max(n_blk - 1 - lo_t, 0)
    if n_loop % 2:
        lo_t += 1
        n_loop -= 1
    for t in range(lo_t):
        carry = stage(t, t % 2, carry)

    def pair(i, carry):
        t = lo_t + 2 * i
        carry = stage(t, lo_t % 2, carry)
        return stage(t + 1, (lo_t + 1) % 2, carry)

    carry = lax.fori_loop(0, n_loop // 2, pair, carry)
    for t in range(lo_t + n_loop, n_blk):
        carry = stage(t, t % 2, carry)
    pv(n_blk - 1, (n_blk - 1) % 2, carry[1])

    acc = acc_ref[...]
    oT = acc[0:HEAD_DIM] / acc[HEAD_DIM:HEAD_DIM + 1]
    o4 = jnp.concatenate([oT[:, g * tq:(g + 1) * tq] for g in range(GROUP)], axis=0)
    o_ref[...] = o4.T.astype(BF16)


ATT_SEQ_PER_STEP = 2


def _attention_short_kernel(q_ref, k_ref, vT_ref, o_ref, *, n_seq):
    t_len = q_ref.shape[0] // n_seq
    head = lax.broadcasted_iota(jnp.int32, (t_len, KV_LANES), 1) // HEAD_DIM
    ones = jnp.ones((SUM_ROWS, t_len), BF16)
    for s in range(n_seq):
        rows = slice(s * t_len, (s + 1) * t_len)
        for kv in range(N_KV_HEADS):
            cols = slice(kv * KV_LANES, (kv + 1) * KV_LANES)
            qf = q_ref[rows, cols].astype(F32)
            qsT = jnp.concatenate([jnp.where(head == g, qf, 0.0).T.astype(BF16) for g in range(GROUP)],
                                  axis=1)
            sT = _dot(k_ref[rows, cols], qsT)
            pT = jnp.exp2(sT - jnp.max(sT, axis=0, keepdims=True)).astype(BF16)
            v1 = jnp.concatenate([vT_ref[kv * HEAD_DIM:(kv + 1) * HEAD_DIM, rows], ones], axis=0)
            acc = _dot(v1, pT)
            oT = acc[0:HEAD_DIM] / acc[HEAD_DIM:HEAD_DIM + 1]
            o4 = jnp.concatenate([oT[:, g * t_len:(g + 1) * t_len] for g in range(GROUP)], axis=0)
            o_ref[rows, cols] = o4.T.astype(BF16)


def _attention_short(q, kt, vT, seq_len):
    n = q.shape[0]
    tb = ATT_SEQ_PER_STEP * seq_len
    return pl.pallas_call(
        functools.partial(_attention_short_kernel, n_seq=ATT_SEQ_PER_STEP),
        grid=(n // tb,),
        in_specs=[pl.BlockSpec((tb, ATT_Q_W), lambda i: (i, 0)),
                  pl.BlockSpec((tb, ATT_Q_W), lambda i: (i, 0)),
                  pl.BlockSpec((ATT_KV_W, tb), lambda i: (0, i))],
        out_specs=pl.BlockSpec((tb, ATT_Q_W), lambda i: (i, 0)),
        out_shape=jax.ShapeDtypeStruct((n, ATT_Q_W), BF16),
        compiler_params=pltpu.CompilerParams(
            dimension_semantics=("arbitrary",), vmem_limit_bytes=VMEM_LIMIT),
        name="attention_short",
    )(q, kt, vT)


def _attention(q, kt, vT, cache, seq_len):
    n = q.shape[0]
    n_batch = n // seq_len
    tq = ATT_TQ
    nq = seq_len // tq
    n_cache = 0 if cache is None else cache[0].shape[1]
    if n_cache == 0 and seq_len <= ATT_SHORT_MAX:
        return _attention_short(q, kt, vT, seq_len)
    blk_rows = max(min(ATT_KB, seq_len), n_cache)
    in_specs = [pl.BlockSpec((tq, KV_LANES), lambda b, kv, i: (b * nq + i, kv))]
    args = [q]
    if n_cache:
        in_specs += [pl.BlockSpec((None, n_cache, KV_LANES), lambda b, kv, i: (b, 0, kv)),
                     pl.BlockSpec((None, HEAD_DIM, n_cache), lambda b, kv, i: (b, kv, 0))]
        args += list(cache)
    in_specs += [pl.BlockSpec((seq_len, KV_LANES), lambda b, kv, i: (b, kv)),
                 pl.BlockSpec((HEAD_DIM, seq_len), lambda b, kv, i: (kv, b))]
    args += [kt, vT]
    return pl.pallas_call(
        functools.partial(_attention_kernel, n_cache=n_cache, n_new=seq_len),
        grid=(n_batch, N_KV_HEADS, nq),
        in_specs=in_specs,
        out_specs=pl.BlockSpec((tq, KV_LANES), lambda b, kv, i: (b * nq + i, kv)),
        out_shape=jax.ShapeDtypeStruct((n, ATT_Q_W), BF16),
        scratch_shapes=[
            pltpu.VMEM((KV_LANES, GROUP * tq), BF16),
            pltpu.VMEM((HEAD_DIM + SUM_ROWS, GROUP * tq), F32),
            pltpu.VMEM((blk_rows, GROUP * tq), F32),
            pltpu.VMEM((blk_rows, GROUP * tq), F32),
            pltpu.VMEM((blk_rows, GROUP * tq), BF16),
            pltpu.VMEM((blk_rows, GROUP * tq), BF16),
        ],
        compiler_params=pltpu.CompilerParams(
            dimension_semantics=("arbitrary", "arbitrary", "arbitrary"), vmem_limit_bytes=VMEM_LIMIT),
        name="attention",
    )(*args)


MERGE_TM = 1024
MERGE_SUB = 256


def _merge_kernel(h_ref, mp_ref, ro_ref, ao_ref, cb_ref, u_ref, up_ref, un_ref, g_ref, cw_ref,
                  wr_ref, wa_ref, wc_ref, wo_ref, o_ref, *, seq_len):
    tm = h_ref.shape[0]
    sub = MERGE_SUB
    n_sub = tm // sub
    i = pl.program_id(0)
    cw = cw_ref[...]
    r = lax.broadcasted_iota(jnp.int32, (sub, CONV_DIM), 0)
    edge = BF16_SUBLANES

    def branches(s):
        r0 = s * sub
        rows = slice(r0, r0 + sub)
        yr = _dot(ro_ref[rows, :], wr_ref[...])
        ya = _dot(ao_ref[rows, :], wa_ref[...])
        u = u_ref[rows, :].astype(F32)
        before = up_ref if s == 0 else u_ref.at[r0 - edge:r0, :]
        after = un_ref if s == n_sub - 1 else u_ref.at[r0 + sub:r0 + sub + edge, :]
        prev_row = before[edge - 1:edge, :].astype(F32)
        next_row = after[0:1, :].astype(F32)
        t = (i * tm + r0 + r) % seq_len
        u_prev = jnp.where(r == 0, prev_row, pltpu.roll(u, 1, axis=0))
        u_next = jnp.where(r == sub - 1, next_row, pltpu.roll(u, sub - 1, axis=0))
        u_prev = jnp.where(t == 0, 0.0, u_prev)
        u_next = jnp.where(t == seq_len - 1, 0.0, u_next)
        conv = u_prev * cw[0:1] + u * cw[1:2] + u_next * cw[2:3]
        yc = _dot((cb_ref[rows, :].astype(F32) * conv).astype(BF16), wc_ref[...])
        merged = (g_ref[rows, 0:D_MODEL].astype(F32) * yr
                  + g_ref[rows, D_MODEL:2 * D_MODEL].astype(F32) * ya
                  + g_ref[rows, 2 * D_MODEL:3 * D_MODEL].astype(F32) * yc)
        return merged.astype(BF16)

    def project(s, merged):
        rows = slice(s * sub, (s + 1) * sub)
        o_ref[rows, :] = h_ref[rows, :] + mp_ref[2:3, :] * _dot(merged, wo_ref[...])

    pending = branches(0)
    for s in range(1, n_sub):
        nxt = branches(s)
        project(s - 1, pending)
        pending = nxt
    project(n_sub - 1, pending)


def _merge(h, mp, rows_per_mod, ret_o, att_o, cb, u, gates, conv_w8, w_ret_o, w_att_o, w_conv_o, w_o,
           l, seq_len):
    n = h.shape[0]
    tm = MERGE_TM
    hb = tm // BF16_SUBLANES
    last = n // BF16_SUBLANES - 1
    row = lambda i: (i, 0)
    wspec = lambda k: _resident((None, k, D_MODEL), lambda i: (l, 0, 0))
    return pl.pallas_call(
        functools.partial(_merge_kernel, seq_len=seq_len),
        grid=(n // tm,),
        in_specs=[
            pl.BlockSpec((tm, D_MODEL), row),
            pl.BlockSpec((None, 8, D_MODEL), lambda i: ((i * tm) // rows_per_mod, 0, 0)),
            pl.BlockSpec((tm, RET_W), row),
            pl.BlockSpec((tm, ATT_Q_W), row),
            pl.BlockSpec((tm, CONV_DIM), row),
            pl.BlockSpec((tm, CONV_DIM), row),
            pl.BlockSpec((BF16_SUBLANES, CONV_DIM), lambda i: (jnp.maximum(i * hb - 1, 0), 0)),
            pl.BlockSpec((BF16_SUBLANES, CONV_DIM), lambda i: (jnp.minimum((i + 1) * hb, last), 0)),
            pl.BlockSpec((tm, 3 * D_MODEL), row),
            pl.BlockSpec((None, 8, CONV_DIM), lambda i: (l, 0, 0)),
            wspec(RET_W), wspec(ATT_Q_W), wspec(CONV_DIM), wspec(D_MODEL),
        ],
        out_specs=pl.BlockSpec((tm, D_MODEL), row),
        out_shape=jax.ShapeDtypeStruct((n, D_MODEL), F32),
        compiler_params=pltpu.CompilerParams(
            dimension_semantics=("arbitrary",), vmem_limit_bytes=VMEM_LIMIT),
        name="merge",
    )(h, mp, ret_o, att_o, cb, u, u, u, gates, conv_w8, w_ret_o, w_att_o, w_conv_o, w_o)


def _rope_tables(n_tok):
    rows = n_tok // GRID_W
    t_row = np.repeat(np.arange(rows, dtype=np.float64), GRID_W)
    t_col = np.tile(np.arange(GRID_W, dtype=np.float64), rows)
    n_freq = HEAD_DIM // 4
    inv = ROPE_THETA ** (-np.arange(n_freq, dtype=np.float64) / n_freq)
    ang = np.concatenate([t_row[:, None] * inv, t_col[:, None] * inv], axis=-1)
    cos, sin = np.cos(ang), np.sin(ang)
    cos64 = np.concatenate([cos, cos], axis=-1)
    sin64 = np.concatenate([-sin, sin], axis=-1)
    return (jnp.asarray(np.tile(cos64, (1, LANES // HEAD_DIM)), F32),
            jnp.asarray(np.tile(sin64, (1, LANES // HEAD_DIM)), F32))


def _mod_pack(mod_l, rows, sub, norm_w_row):
    r0 = rows[0]
    nr = len(rows)
    m = mod_l[r0:r0 + nr, 3 * sub * D_MODEL:3 * (sub + 1) * D_MODEL].reshape(nr, 3, D_MODEL)
    nw = jnp.broadcast_to(norm_w_row[None, None, :], (nr, 1, D_MODEL))
    pad = jnp.zeros((nr, 4, D_MODEL), F32)
    return jnp.concatenate([m, nw, pad], axis=1)


def kernel(x_prompt, x_sample, c, state_ret, cache_k, cache_v, c_ctx, w_ada, b_ada, norm_w, w_ffn_in,
           w_ffn_out, w_in, ret_decay_logit, ret_gn, q_gain, k_gain, conv_w, w_ret_o, w_att_o, w_conv_o,
           w_o):
    n_ctx_b, ctx_len, _ = x_prompt.shape
    n_lat_b, lat_len, _ = x_sample.shape

    w_ffn_in_b = w_ffn_in.astype(BF16)
    w_ffn_out_b = w_ffn_out.astype(BF16)
    w_in_b = w_in.astype(BF16)
    w_ret_o_b = w_ret_o.astype(BF16)
    w_att_o_b = w_att_o.astype(BF16)
    w_conv_o_b = w_conv_o.astype(BF16)
    w_o_b = w_o.astype(BF16)

    q_gain_t = jnp.tile(q_gain, (1, N_HEADS)).reshape(DEPTH, 1, ATT_Q_W)
    k_gain_t = jnp.tile(k_gain, (1, N_KV_HEADS)).reshape(DEPTH, 1, ATT_KV_W)
    conv_w8 = jnp.pad(conv_w, ((0, 0), (0, 8 - conv_w.shape[1]), (0, 0)))
    gid = np.arange(ATT_Q_W) // HEAD_DIM
    bd = jnp.asarray(gid[:, None] == gid[None, :], BF16)
    rope = _rope_tables(lat_len)

    cond8 = jnp.zeros((8, D_MODEL), F32).at[0].set(c_ctx).at[1:1 + n_lat_b].set(c)
    mod = _ada(cond8, w_ada, b_ada)

    ck = cache_k.astype(BF16)
    ck = jnp.broadcast_to(ck[:, :, :, :, None, :], ck.shape[:4] + (GROUP, HEAD_DIM))
    cache_kt = ck.reshape(ck.shape[0], ck.shape[1], ck.shape[2], N_KV_HEADS * KV_LANES)
    cache_vT = jnp.transpose(cache_v.astype(BF16), (0, 1, 3, 4, 2)).reshape(
        cache_v.shape[0], cache_v.shape[1], ATT_KV_W, cache_v.shape[2])

    groups = (
        dict(x=x_prompt.reshape(n_ctx_b * ctx_len, D_MODEL), rows=[0], seq=ctx_len, ctx=True),
        dict(x=x_sample.reshape(n_lat_b * lat_len, D_MODEL), rows=list(range(1, 1 + n_lat_b)),
             seq=lat_len, ctx=False),
    )
    results = []
    for grp in groups:
        h = grp["x"]
        seq = grp["seq"]
        is_ctx = grp["ctx"]
        rpm = h.shape[0] // len(grp["rows"])
        states, keys, values = [], [], []
        for l in range(DEPTH):
            mp = [_mod_pack(mod[l], grp["rows"], s, norm_w[l, s]) for s in range(3)]
            h = _ffn(h, mp[0], rpm, w_ffn_in_b, w_ffn_out_b, l, 0)
            outs = _inproj(h, mp[1], rpm, w_in_b, l, q_gain_t, k_gain_t, bd,
                           None if is_ctx else rope, seq, emit_kv=is_ctx)
            ret_in, qn, kt, vT, cb, u, gates = outs[:7]
            if is_ctx:
                ret_o, st = _retention(ret_in, ret_decay_logit[l], ret_gn[l][None, :], None, seq,
                                       n_seq_blk=8, emit_state=True)
                states.append(st)
                keys.append(outs[7].reshape(n_ctx_b, seq, N_KV_HEADS, HEAD_DIM))
                values.append(outs[8].reshape(n_ctx_b, seq, N_KV_HEADS, HEAD_DIM))
                att_o = _attention(qn, kt, vT, None, seq)
            else:
                (ret_o,) = _retention(ret_in, ret_decay_logit[l], ret_gn[l][None, :], state_ret[:, l],
                                      seq, n_seq_blk=1, emit_state=False)
                att_o = _attention(qn, kt, vT, (cache_kt[:, l], cache_vT[:, l]), seq)
            h = _merge(h, mp[1], rpm, ret_o, att_o, cb, u, gates, conv_w8, w_ret_o_b, w_att_o_b,
                       w_conv_o_b, w_o_b, l, seq)
            h = _ffn(h, mp[2], rpm, w_ffn_in_b, w_ffn_out_b, l, 1)
        results.append((h, states, keys, values))

    (y_ctx, states, keys, values), (y_lat, _, _, _) = results
    y_prompt = y_ctx.reshape(x_prompt.shape)
    y_sample = y_lat.reshape(x_sample.shape)
    new_state_ret = jnp.stack(states, axis=1)
    new_cache_k = jnp.stack(keys, axis=1)
    new_cache_v = jnp.stack(values, axis=1)
    return (y_prompt, y_sample, new_state_ret, new_cache_k, new_cache_v)
```

```python
import functools
import math

import numpy as np
import jax
import jax.numpy as jnp
from jax import lax
from jax.experimental import pallas as pl
from jax.experimental.pallas import tpu as pltpu

D_MODEL = 1024
DEPTH = 2
GRID_W = 64
N_RET_HEADS = 4
RET_DK = 128
RET_DV = 128
RET_CHUNK = 128
N_HEADS = 8
N_KV_HEADS = 2
HEAD_DIM = 64
ROPE_THETA = 10000.0
CONV_DIM = 512
FFN_DIM = 2816
N_MOD = 9
EPS = 1e-6
RET_W = N_RET_HEADS * RET_DK
ATT_Q_W = N_HEADS * HEAD_DIM
ATT_KV_W = N_KV_HEADS * HEAD_DIM
GROUP = N_HEADS // N_KV_HEADS
KV_LANES = GROUP * HEAD_DIM
OFF_RQ, OFF_RK, OFF_RV, OFF_RG = 0, 512, 1024, 1536
OFF_AQ, OFF_AK, OFF_AV = 2048, 2560, 2688
OFF_CB, OFF_CC, OFF_CX = 2816, 3328, 3840
OFF_GATES = 4352

LANES = 128
BF16_SUBLANES = 16
VMEM_LIMIT = 56 * 1024 * 1024

F32 = jnp.float32
BF16 = jnp.bfloat16


def _dot(a, b):
    return jnp.dot(a, b, preferred_element_type=F32)


def _dot_nt(a, b):
    return lax.dot_general(a, b, (((1,), (1,)), ((), ())), preferred_element_type=F32)


def _sigmoid(x):
    return 1.0 / (1.0 + jnp.exp(-x))


def _silu(x):
    return x * _sigmoid(x)


def _mod_norm(x, mp):
    ms = jnp.mean(x * x, axis=-1, keepdims=True)
    y = x * lax.rsqrt(ms + EPS)
    return (y * mp[3:4]) * (1.0 + mp[1:2]) + mp[0:1]


def _resident(shape, index):
    return pl.BlockSpec(shape, index, pipeline_mode=pl.Buffered(1))


ADA_TN = 1152


def _ada_kernel(c_ref, w_ref, b_ref, o_ref):
    a = _silu(c_ref[...]).astype(BF16)
    o_ref[...] = _dot(a, w_ref[...].astype(BF16)) + b_ref[...]


def _ada(cond8, w_ada, b_ada):
    n = N_MOD * D_MODEL
    return pl.pallas_call(
        _ada_kernel,
        grid=(DEPTH, n // ADA_TN),
        in_specs=[
            pl.BlockSpec((8, D_MODEL), lambda l, j: (0, 0)),
            pl.BlockSpec((None, D_MODEL, ADA_TN), lambda l, j: (l, 0, j)),
            pl.BlockSpec((None, 1, ADA_TN), lambda l, j: (l, 0, j)),
        ],
        out_specs=pl.BlockSpec((None, 8, ADA_TN), lambda l, j: (l, 0, j)),
        out_shape=jax.ShapeDtypeStruct((DEPTH, 8, n), F32),
        compiler_params=pltpu.CompilerParams(
            dimension_semantics=("arbitrary", "arbitrary"), vmem_limit_bytes=VMEM_LIMIT),
        name="ada",
    )(cond8, w_ada, b_ada.reshape(DEPTH, 1, n))


FFN_TM = 1024
FFN_FC = 256


def _ffn_kernel(x_ref, mp_ref, wi_ref, wo_ref, o_ref, h_ref):
    x = x_ref[...]
    mp = mp_ref[...]
    xb = _mod_norm(x, mp).astype(BF16)
    for c in range(FFN_DIM // FFN_FC):
        lo = c * FFN_FC
        g = _dot(xb, wi_ref[:, lo:lo + FFN_FC])
        u = _dot(xb, wi_ref[:, FFN_DIM + lo:FFN_DIM + lo + FFN_FC])
        h_ref[:, lo:lo + FFN_FC] = (_silu(g) * u).astype(BF16)
    y = _dot(h_ref[...], wo_ref[...])
    o_ref[...] = x + (0.5 * mp[2:3]) * y


def _ffn(x, mp, rows_per_mod, w_in, w_out, l, j):
    n = x.shape[0]
    tm = FFN_TM
    return pl.pallas_call(
        _ffn_kernel,
        grid=(n // tm,),
        in_specs=[
            pl.BlockSpec((tm, D_MODEL), lambda i: (i, 0)),
            pl.BlockSpec((None, 8, D_MODEL), lambda i: ((i * tm) // rows_per_mod, 0, 0)),
            _resident((None, None, D_MODEL, 2 * FFN_DIM), lambda i: (l, j, 0, 0)),
            _resident((None, None, FFN_DIM, D_MODEL), lambda i: (l, j, 0, 0)),
        ],
        out_specs=pl.BlockSpec((tm, D_MODEL), lambda i: (i, 0)),
        out_shape=jax.ShapeDtypeStruct((n, D_MODEL), F32),
        scratch_shapes=[pltpu.VMEM((tm, FFN_DIM), BF16)],
        compiler_params=pltpu.CompilerParams(
            dimension_semantics=("arbitrary",), vmem_limit_bytes=VMEM_LIMIT),
        name="ffn",
    )(x, mp, w_in, w_out)


INPROJ_TM = 1024
INPROJ_SUB = 256
Q_SCALE = HEAD_DIM ** -0.5 * math.log2(math.e)


def _group_sumsq(a, bd):
    sq = a * a
    hi = sq.astype(BF16)
    lo = (sq - hi.astype(F32)).astype(BF16)
    return _dot(hi, bd) + _dot(lo, bd)


def _group_rms(a, sumsq, gain):
    return (a * lax.rsqrt(sumsq * (1.0 / HEAD_DIM) + EPS)) * gain


def _rope(x, cos, sin_signed):
    half = HEAD_DIM // 2
    reps = x.shape[1] // LANES
    lane = lax.broadcasted_iota(jnp.int32, (x.shape[0], LANES), 1)
    first = (lane % HEAD_DIM) < half
    out = []
    for r in range(reps):
        xs = x[:, r * LANES:(r + 1) * LANES]
        partner = jnp.where(first, pltpu.roll(xs, LANES - half, axis=1), pltpu.roll(xs, half, axis=1))
        out.append(xs * cos + partner * sin_signed)
    return out[0] if reps == 1 else jnp.concatenate(out, axis=1)


def _tile_kv(a):
    lane = lax.broadcasted_iota(jnp.int32, a.shape, 1)
    sw = pltpu.roll(a, HEAD_DIM, axis=1)
    h0 = jnp.where(lane < HEAD_DIM, a, sw)
    h1 = jnp.where(lane < HEAD_DIM, sw, a)
    return jnp.concatenate([h0, h0, h1, h1], axis=1)


def _inproj_kernel(*refs, use_rope, emit_kv):
    x_ref, mp_ref, w_ref, qg_ref, kg_ref, bd_ref = refs[:6]
    pos = 6
    if use_rope:
        cos_ref, sin_ref = refs[pos:pos + 2]
        pos += 2
    ret_ref, q_ref, kt_ref, vT_ref, cb_ref, u_ref = refs[pos:pos + 6]
    pos += 6
    if emit_kv:
        kout_ref, vout_ref = refs[pos:pos + 2]

    mp = mp_ref[...]
    tm = x_ref.shape[0]
    n_sub = tm // INPROJ_SUB
    xbs = [None] * n_sub
    xbs[0] = _mod_norm(x_ref[0:INPROJ_SUB, :], mp).astype(BF16)
    for s in range(n_sub):
        rows = slice(s * INPROJ_SUB, (s + 1) * INPROJ_SUB)
        xb = xbs[s]

        def proj(off, width):
            return _dot(xb, w_ref[:, off:off + width])

        aq = proj(OFF_AQ, ATT_Q_W)
        ret_ref[rows, OFF_RQ:OFF_RQ + RET_W] = proj(OFF_RQ, RET_W).astype(BF16)
        if s + 1 < n_sub:
            nxt = slice((s + 1) * INPROJ_SUB, (s + 2) * INPROJ_SUB)
            xbs[s + 1] = _mod_norm(x_ref[nxt, :], mp).astype(BF16)
        ss_q = _group_sumsq(aq, bd_ref[...])
        ak = proj(OFF_AK, ATT_KV_W)
        v = proj(OFF_AV, ATT_KV_W)
        ret_ref[rows, OFF_RK:OFF_RK + RET_W] = (proj(OFF_RK, RET_W) * (RET_DK ** -0.5)).astype(BF16)

        q = _group_rms(aq, ss_q, qg_ref[...])
        if use_rope:
            q = _rope(q, cos_ref[rows, :], sin_ref[rows, :])
        q_ref[rows, :] = (q * Q_SCALE).astype(BF16)
        ss_k = _group_sumsq(ak, bd_ref[0:ATT_KV_W, 0:ATT_KV_W])
        ret_ref[rows, OFF_RV:OFF_RV + RET_W] = proj(OFF_RV, RET_W).astype(BF16)

        k = _group_rms(ak, ss_k, kg_ref[...])
        if emit_kv:
            kout_ref[rows, :] = k
            vout_ref[rows, :] = v
        if use_rope:
            k = _rope(k, cos_ref[rows, :], sin_ref[rows, :])
        kt_ref[rows, :] = _tile_kv(k).astype(BF16)
        vT_ref[:, rows] = v.T.astype(BF16)

        ret_ref[rows, OFF_RG:OFF_RG + RET_W] = _silu(proj(OFF_RG, RET_W)).astype(BF16)
        cb_ref[rows, :] = proj(OFF_CB, CONV_DIM).astype(BF16)
        u_ref[rows, :] = (proj(OFF_CC, CONV_DIM) * proj(OFF_CX, CONV_DIM)).astype(BF16)


def _inproj(x, mp, rows_per_mod, w_in, l, q_gain_t, k_gain_t, bd, rope, seq_len, emit_kv):
    n = x.shape[0]
    tm = INPROJ_TM
    use_rope = rope is not None
    row = lambda i: (i, 0)
    in_specs = [
        pl.BlockSpec((tm, D_MODEL), row),
        pl.BlockSpec((None, 8, D_MODEL), lambda i: ((i * tm) // rows_per_mod, 0, 0)),
        _resident((None, D_MODEL, OFF_GATES), lambda i: (l, 0, 0)),
        pl.BlockSpec((None, 1, ATT_Q_W), lambda i: (l, 0, 0)),
        pl.BlockSpec((None, 1, ATT_KV_W), lambda i: (l, 0, 0)),
        _resident((ATT_Q_W, ATT_Q_W), lambda i: (0, 0)),
    ]
    args = [x, mp, w_in, q_gain_t, k_gain_t, bd]
    if use_rope:
        tiles_per_seq = seq_len // tm
        in_specs += [pl.BlockSpec((tm, LANES), lambda i: (i % tiles_per_seq, 0))] * 2
        args += list(rope)
    widths = [4 * RET_W, ATT_Q_W, ATT_Q_W, None, CONV_DIM, CONV_DIM]
    out_specs = [pl.BlockSpec((tm, w), row) for w in widths if w]
    out_shape = [jax.ShapeDtypeStruct((n, w), BF16) for w in widths if w]
    out_specs.insert(3, pl.BlockSpec((ATT_KV_W, tm), lambda i: (0, i)))
    out_shape.insert(3, jax.ShapeDtypeStruct((ATT_KV_W, n), BF16))
    if emit_kv:
        out_specs += [pl.BlockSpec((tm, ATT_KV_W), row)] * 2
        out_shape += [jax.ShapeDtypeStruct((n, ATT_KV_W), F32)] * 2
    return pl.pallas_call(
        functools.partial(_inproj_kernel, use_rope=use_rope, emit_kv=emit_kv),
        grid=(n // tm,),
        in_specs=in_specs,
        out_specs=out_specs,
        out_shape=out_shape,
        compiler_params=pltpu.CompilerParams(
            dimension_semantics=("arbitrary",), vmem_limit_bytes=VMEM_LIMIT),
        name="inproj",
    )(*args)


RET_CHUNK_UNROLL = 8
RET_SEQ_UNROLL = 4
TAB_DMASK, TAB_DQF, TAB_DQB, TAB_DKF, TAB_DKB, TAB_DCF, TAB_DCB, N_TAB = range(8)


def _log_sigmoid(x):
    return jnp.minimum(x, 0.0) - jnp.log(1.0 + jnp.exp(-jnp.abs(x)))


def _chunk_rows(start):
    if isinstance(start, int):
        return pl.ds(start, RET_CHUNK)
    return pl.ds(pl.multiple_of(start, RET_CHUNK), RET_CHUNK)


def _retention_kernel(*refs, n_seq, n_chunks, has_s0, emit_state):
    dl_ref, q_ref, k_ref, v_ref, g_ref, gn_ref = refs[:6]
    pos = 6
    if has_s0:
        s0_ref = refs[pos]
        pos += 1
    o_ref = refs[pos]
    pos += 1
    if emit_state:
        st_ref = refs[pos]
        pos += 1
    tab_ref, kv_ref, ent_ref = refs[pos:pos + 3]

    C = RET_CHUNK
    hd = pl.program_id(1)
    row = lax.broadcasted_iota(jnp.int32, (C, C), 0).astype(F32)
    col = lax.broadcasted_iota(jnp.int32, (C, C), 1).astype(F32)
    diff = row - col
    lgf = _log_sigmoid(jnp.full((C, C), dl_ref[0, hd], F32))
    lgb = _log_sigmoid(jnp.full((C, C), dl_ref[1, hd], F32))
    tab_ref[TAB_DMASK] = (jnp.where(diff >= 0, jnp.exp(lgf * jnp.maximum(diff, 0.0)), 0.0)
                          + jnp.where(diff <= 0, jnp.exp(lgb * jnp.maximum(-diff, 0.0)), 0.0))
    tab_ref[TAB_DQF] = jnp.exp(lgf * (row + 1.0))
    tab_ref[TAB_DQB] = jnp.exp(lgb * (C - row))
    tab_ref[TAB_DKF] = jnp.exp(lgf * (C - 1.0 - row))
    tab_ref[TAB_DKB] = jnp.exp(lgb * row)
    tab_ref[TAB_DCF] = jnp.exp(lgf * C)
    tab_ref[TAB_DCB] = jnp.exp(lgb * C)

    def kv_phase(items):
        ops = []
        for r, _ in items:
            rows = _chunk_rows(r)
            k = k_ref[rows, :].astype(F32)
            vT = v_ref[rows, :].astype(F32).T.astype(BF16)
            kk = jnp.concatenate([(k * tab_ref[TAB_DKF]).astype(BF16),
                                  (k * tab_ref[TAB_DKB]).astype(BF16)], axis=1)
            ops.append((vT, kk))
        for (_, i), (vT, kk) in zip(items, ops):
            kv_ref[i] = _dot(vT, kk)

    def scan_phase(s, sc):
        if has_s0:
            st0 = (s0_ref[0].T, s0_ref[1].T)
        else:
            st0 = (jnp.zeros((RET_DV, RET_DK), F32), jnp.zeros((RET_DV, RET_DK), F32))

        def scan_step(t, st):
            st_f, st_b = st
            cb = n_chunks - 1 - t
            ent_ref[sc + t, :, 0:RET_DK] = st_f.astype(BF16)
            ent_ref[sc + cb, :, RET_DK:2 * RET_DK] = st_b.astype(BF16)
            st_f = st_f * tab_ref[TAB_DCF] + kv_ref[sc + t, :, 0:RET_DK]
            st_b = st_b * tab_ref[TAB_DCB] + kv_ref[sc + cb, :, RET_DK:2 * RET_DK]
            return st_f, st_b

        if n_chunks <= RET_CHUNK_UNROLL:
            st = st0
            for t in range(n_chunks):
                st = scan_step(t, st)
        else:
            st = lax.fori_loop(0, n_chunks, scan_step, st0)
        if emit_state:
            st_ref[s, 0] = st[0].T
            st_ref[s, 1] = st[1].T

    def out_phase(items):
        qs = [q_ref[_chunk_rows(r), :] for r, _ in items]
        att = [_dot_nt(q, k_ref[_chunk_rows(r), :]) for q, (r, _) in zip(qs, items)]
        att = [(a * tab_ref[TAB_DMASK]).astype(BF16) for a in att]
        qq = []
        for q in qs:
            qf = q.astype(F32)
            qq.append(jnp.concatenate([(qf * tab_ref[TAB_DQF]).astype(BF16),
                                       (qf * tab_ref[TAB_DQB]).astype(BF16)], axis=1))
        outs = [_dot(a, v_ref[_chunk_rows(r), :]) + _dot_nt(x, ent_ref[i])
                for a, x, (r, i) in zip(att, qq, items)]
        for o, (r, _) in zip(outs, items):
            rows = _chunk_rows(r)
            ms = jnp.mean(o * o, axis=-1, keepdims=True)
            on = (o * lax.rsqrt(ms + EPS)) * gn_ref[...]
            o_ref[rows, :] = (g_ref[rows, :].astype(F32) * on).astype(BF16)

    if n_seq == 1:
        assert n_chunks % RET_CHUNK_UNROLL == 0

        def chunk_group(phase):
            def step(t, carry):
                c0 = t * RET_CHUNK_UNROLL
                phase([((c0 + j) * C, c0 + j) for j in range(RET_CHUNK_UNROLL)])
                return carry
            lax.fori_loop(0, n_chunks // RET_CHUNK_UNROLL, step, 0)

        chunk_group(kv_phase)
        scan_phase(0, 0)
        chunk_group(out_phase)
    else:
        assert n_seq % RET_SEQ_UNROLL == 0 and n_chunks <= RET_CHUNK_UNROLL

        def seq_group(t, carry):
            s0 = t * RET_SEQ_UNROLL
            items = [((s0 + j) * (n_chunks * C) + c * C, j * n_chunks + c)
                     for j in range(RET_SEQ_UNROLL) for c in range(n_chunks)]
            kv_phase(items)
            for j in range(RET_SEQ_UNROLL):
                scan_phase(s0 + j, j * n_chunks)
            out_phase(items)
            return carry

        lax.fori_loop(0, n_seq // RET_SEQ_UNROLL, seq_group, 0)


def _retention(ret_in, decay_logit_l, gn_l, s0, seq_len, n_seq_blk, emit_state):
    n = ret_in.shape[0]
    n_chunks = seq_len // RET_CHUNK
    tb = n_seq_blk * seq_len
    has_s0 = s0 is not None
    n_off = RET_W // RET_DK

    def branch(k):
        return pl.BlockSpec((tb, RET_DK), lambda i, h: (i, k * n_off + h))

    in_specs = [pl.BlockSpec(memory_space=pltpu.SMEM), branch(0), branch(1), branch(2), branch(3),
                pl.BlockSpec((1, RET_DV), lambda i, h: (0, h))]
    args = [decay_logit_l, ret_in, ret_in, ret_in, ret_in, gn_l]
    if has_s0:
        assert n_seq_blk == 1
        in_specs.append(pl.BlockSpec((None, 2, None, RET_DK, RET_DV), lambda i, h: (i, 0, h, 0, 0)))
        args.append(s0)
    out_specs = [pl.BlockSpec((tb, RET_DV), lambda i, h: (i, h))]
    out_shape = [jax.ShapeDtypeStruct((n, RET_W), BF16)]
    if emit_state:
        out_specs.append(pl.BlockSpec((n_seq_blk, 2, None, RET_DK, RET_DV), lambda i, h: (i, 0, h, 0, 0)))
        out_shape.append(jax.ShapeDtypeStruct((n // seq_len, 2, N_RET_HEADS, RET_DK, RET_DV), F32))
    n_slots = n_chunks * (1 if n_seq_blk == 1 else RET_SEQ_UNROLL)
    return pl.pallas_call(
        functools.partial(_retention_kernel, n_seq=n_seq_blk, n_chunks=n_chunks, has_s0=has_s0,
                          emit_state=emit_state),
        grid=(n // tb, N_RET_HEADS),
        in_specs=in_specs,
        out_specs=out_specs,
        out_shape=out_shape,
        scratch_shapes=[
            pltpu.VMEM((N_TAB, RET_CHUNK, RET_CHUNK), F32),
            pltpu.VMEM((n_slots, RET_DV, 2 * RET_DK), F32),
            pltpu.VMEM((n_slots, RET_DV, 2 * RET_DK), BF16),
        ],
        compiler_params=pltpu.CompilerParams(
            dimension_semantics=("arbitrary", "arbitrary"), vmem_limit_bytes=VMEM_LIMIT),
        name="retention",
    )(*args)


ATT_TQ = 1024
ATT_KB = 512
ATT_SHORT_MAX = 512
ATT_RC = 16
NEG_BIG = -1e30
SUM_ROWS = BF16_SUBLANES


def _attention_kernel(*refs, n_cache, n_new):
    q_ref = refs[0]
    pos = 1
    if n_cache:
        kc_ref, vc_ref = refs[pos:pos + 2]
        pos += 2
    kn_ref, vn_ref, o_ref, qs_ref, acc_ref = refs[pos:pos + 5]
    s_slots = refs[pos + 5:pos + 7]
    p_slots = refs[pos + 7:pos + 9]

    tq = q_ref.shape[0]
    nq = GROUP * tq
    qf = q_ref[...].astype(F32)
    head = lax.broadcasted_iota(jnp.int32, (tq, KV_LANES), 1) // HEAD_DIM
    for g in range(GROUP):
        qs_ref[:, g * tq:(g + 1) * tq] = jnp.where(head == g, qf, 0.0).T.astype(BF16)
    acc_ref[...] = jnp.zeros(acc_ref.shape, F32)

    kb = min(ATT_KB, n_new)
    n_cb = 1 if n_cache else 0
    n_nb = n_new // kb
    assert n_new % kb == 0
    n_blk = n_cb + n_nb

    def block(t):
        if isinstance(t, int) and t < n_cb:
            return kc_ref, vc_ref, pl.ds(0, n_cache)
        lo = (t - n_cb) * kb
        return kn_ref, vn_ref, (pl.ds(lo, kb) if isinstance(lo, int) else pl.ds(pl.multiple_of(lo, kb), kb))

    def scores(t, par):
        k_ref, _, keys = block(t)
        s = _dot(k_ref[keys, :], qs_ref[...])
        s_slots[par][0:keys.size, :] = s
        return jnp.max(s, axis=0, keepdims=True)

    def softmax(n_keys, par, m, s_max):
        s_ref, p_ref = s_slots[par], p_slots[par]
        m_new = jnp.maximum(m, s_max)
        m_rows = jnp.broadcast_to(m_new, (ATT_RC, nq))
        for r in range(0, n_keys, ATT_RC):
            p_ref[r:r + ATT_RC, :] = jnp.exp2(s_ref[r:r + ATT_RC, :] - m_rows).astype(BF16)
        return m_new, jnp.exp2(m - m_new)

    def pv(t, par, alpha):
        _, vT_ref, keys = block(t)
        v1 = jnp.concatenate([vT_ref[:, keys], jnp.ones((SUM_ROWS, keys.size), BF16)], axis=0)
        acc_ref[...] = alpha * acc_ref[...] + _dot(v1, p_slots[par][0:keys.size, :])

    def stage(t, par, carry):
        m, alpha, s_max = carry
        next_max = s_max
        if not isinstance(t, int) or t + 1 < n_blk:
            next_max = scores(t + 1, 1 - par)
        if not isinstance(t, int) or t >= 1:
            pv(t - 1, 1 - par, alpha)
        return softmax(block(t)[2].size, par, m, s_max) + (next_max,)

    carry = (jnp.full((1, nq), NEG_BIG, F32), jnp.zeros((1, nq), F32), scores(0, 0))
    lo_t = min(n_cb + 1, n_blk)
    n_loop = max(n_blk - 1 - lo_t, 0)
    if n_loop % 2:
        lo_t += 1
        n_loop -= 1
    for t in range(lo_t):
        carry = stage(t, t % 2, carry)

    def pair(i, carry):
        t = lo_t + 2 * i
        carry = stage(t, lo_t % 2, carry)
        return stage(t + 1, (lo_t + 1) % 2, carry)

    carry = lax.fori_loop(0, n_loop // 2, pair, carry)
    for t in range(lo_t + n_loop, n_blk):
        carry = stage(t, t % 2, carry)
    pv(n_blk - 1, (n_blk - 1) % 2, carry[1])

    acc = acc_ref[...]
    oT = acc[0:HEAD_DIM] / acc[HEAD_DIM:HEAD_DIM + 1]
    o4 = jnp.concatenate([oT[:, g * tq:(g + 1) * tq] for g in range(GROUP)], axis=0)
    o_ref[...] = o4.T.astype(BF16)


ATT_SEQ_PER_STEP = 2


def _attention_short_kernel(q_ref, k_ref, vT_ref, o_ref, *, n_seq):
    t_len = q_ref.shape[0] // n_seq
    head = lax.broadcasted_iota(jnp.int32, (t_len, KV_LANES), 1) // HEAD_DIM
    ones = jnp.ones((SUM_ROWS, t_len), BF16)
    for s in range(n_seq):
        rows = slice(s * t_len, (s + 1) * t_len)
        for kv in range(N_KV_HEADS):
            cols = slice(kv * KV_LANES, (kv + 1) * KV_LANES)
            qf = q_ref[rows, cols].astype(F32)
            qsT = jnp.concatenate([jnp.where(head == g, qf, 0.0).T.astype(BF16) for g in range(GROUP)],
                                  axis=1)
            sT = _dot(k_ref[rows, cols], qsT)
            pT = jnp.exp2(sT - jnp.max(sT, axis=0, keepdims=True)).astype(BF16)
            v1 = jnp.concatenate([vT_ref[kv * HEAD_DIM:(kv + 1) * HEAD_DIM, rows], ones], axis=0)
            acc = _dot(v1, pT)
            oT = acc[0:HEAD_DIM] / acc[HEAD_DIM:HEAD_DIM + 1]
            o4 = jnp.concatenate([oT[:, g * t_len:(g + 1) * t_len] for g in range(GROUP)], axis=0)
            o_ref[rows, cols] = o4.T.astype(BF16)


def _attention_short(q, kt, vT, seq_len):
    n = q.shape[0]
    tb = ATT_SEQ_PER_STEP * seq_len
    return pl.pallas_call(
        functools.partial(_attention_short_kernel, n_seq=ATT_SEQ_PER_STEP),
        grid=(n // tb,),
        in_specs=[pl.BlockSpec((tb, ATT_Q_W), lambda i: (i, 0)),
                  pl.BlockSpec((tb, ATT_Q_W), lambda i: (i, 0)),
                  pl.BlockSpec((ATT_KV_W, tb), lambda i: (0, i))],
        out_specs=pl.BlockSpec((tb, ATT_Q_W), lambda i: (i, 0)),
        out_shape=jax.ShapeDtypeStruct((n, ATT_Q_W), BF16),
        compiler_params=pltpu.CompilerParams(
            dimension_semantics=("arbitrary",), vmem_limit_bytes=VMEM_LIMIT),
        name="attention_short",
    )(q, kt, vT)


def _attention(q, kt, vT, cache, seq_len):
    n = q.shape[0]
    n_batch = n // seq_len
    tq = ATT_TQ
    nq = seq_len // tq
    n_cache = 0 if cache is None else cache[0].shape[1]
    if n_cache == 0 and seq_len <= ATT_SHORT_MAX:
        return _attention_short(q, kt, vT, seq_len)
    blk_rows = max(min(ATT_KB, seq_len), n_cache)
    in_specs = [pl.BlockSpec((tq, KV_LANES), lambda b, kv, i: (b * nq + i, kv))]
    args = [q]
    if n_cache:
        in_specs += [pl.BlockSpec((None, n_cache, KV_LANES), lambda b, kv, i: (b, 0, kv)),
                     pl.BlockSpec((None, HEAD_DIM, n_cache), lambda b, kv, i: (b, kv, 0))]
        args += list(cache)
    in_specs += [pl.BlockSpec((seq_len, KV_LANES), lambda b, kv, i: (b, kv)),
                 pl.BlockSpec((HEAD_DIM, seq_len), lambda b, kv, i: (kv, b))]
    args += [kt, vT]
    return pl.pallas_call(
        functools.partial(_attention_kernel, n_cache=n_cache, n_new=seq_len),
        grid=(n_batch, N_KV_HEADS, nq),
        in_specs=in_specs,
        out_specs=pl.BlockSpec((tq, KV_LANES), lambda b, kv, i: (b * nq + i, kv)),
        out_shape=jax.ShapeDtypeStruct((n, ATT_Q_W), BF16),
        scratch_shapes=[
            pltpu.VMEM((KV_LANES, GROUP * tq), BF16),
            pltpu.VMEM((HEAD_DIM + SUM_ROWS, GROUP * tq), F32),
            pltpu.VMEM((blk_rows, GROUP * tq), F32),
            pltpu.VMEM((blk_rows, GROUP * tq), F32),
            pltpu.VMEM((blk_rows, GROUP * tq), BF16),
            pltpu.VMEM((blk_rows, GROUP * tq), BF16),
        ],
        compiler_params=pltpu.CompilerParams(
            dimension_semantics=("arbitrary", "arbitrary", "arbitrary"), vmem_limit_bytes=VMEM_LIMIT),
        name="attention",
    )(*args)


MERGE_TM = 1024
MERGE_SUB = 256


def _merge_kernel(h_ref, mp_ref, ro_ref, ao_ref, cb_ref, u_ref, up_ref, un_ref, cw_ref,
                  wg_ref, wr_ref, wa_ref, wc_ref, wo_ref, o_ref, *, seq_len):
    tm = h_ref.shape[0]
    sub = MERGE_SUB
    n_sub = tm // sub
    i = pl.program_id(0)
    cw = cw_ref[...]
    r = lax.broadcasted_iota(jnp.int32, (sub, CONV_DIM), 0)
    edge = BF16_SUBLANES

    def branches(s):
        r0 = s * sub
        rows = slice(r0, r0 + sub)
        yr = _dot(ro_ref[rows, :], wr_ref[...])
        ya = _dot(ao_ref[rows, :], wa_ref[...])
        u = u_ref[rows, :].astype(F32)
        before = up_ref if s == 0 else u_ref.at[r0 - edge:r0, :]
        after = un_ref if s == n_sub - 1 else u_ref.at[r0 + sub:r0 + sub + edge, :]
        prev_row = before[edge - 1:edge, :].astype(F32)
        next_row = after[0:1, :].astype(F32)
        t = (i * tm + r0 + r) % seq_len
        u_prev = jnp.where(r == 0, prev_row, pltpu.roll(u, 1, axis=0))
        u_next = jnp.where(r == sub - 1, next_row, pltpu.roll(u, sub - 1, axis=0))
        u_prev = jnp.where(t == 0, 0.0, u_prev)
        u_next = jnp.where(t == seq_len - 1, 0.0, u_next)
        conv = u_prev * cw[0:1] + u * cw[1:2] + u_next * cw[2:3]
        yc = _dot((cb_ref[rows, :].astype(F32) * conv).astype(BF16), wc_ref[...])
        xb = _mod_norm(h_ref[rows, :], mp_ref[...]).astype(BF16)
        merged = _sigmoid(_dot(xb, wg_ref[:, 0:D_MODEL])) * yr
        merged = merged + _sigmoid(_dot(xb, wg_ref[:, D_MODEL:2 * D_MODEL])) * ya
        merged = merged + _sigmoid(_dot(xb, wg_ref[:, 2 * D_MODEL:3 * D_MODEL])) * yc
        return merged.astype(BF16)

    def project(s, merged):
        rows = slice(s * sub, (s + 1) * sub)
        o_ref[rows, :] = h_ref[rows, :] + mp_ref[2:3, :] * _dot(merged, wo_ref[...])

    pending = branches(0)
    for s in range(1, n_sub):
        nxt = branches(s)
        project(s - 1, pending)
        pending = nxt
    project(n_sub - 1, pending)


def _merge(h, mp, rows_per_mod, ret_o, att_o, cb, u, conv_w8, w_gates, w_ret_o, w_att_o, w_conv_o, w_o,
           l, seq_len):
    n = h.shape[0]
    tm = MERGE_TM
    hb = tm // BF16_SUBLANES
    last = n // BF16_SUBLANES - 1
    row = lambda i: (i, 0)
    wspec = lambda k: _resident((None, k, D_MODEL), lambda i: (l, 0, 0))
    return pl.pallas_call(
        functools.partial(_merge_kernel, seq_len=seq_len),
        grid=(n // tm,),
        in_specs=[
            pl.BlockSpec((tm, D_MODEL), row),
            pl.BlockSpec((None, 8, D_MODEL), lambda i: ((i * tm) // rows_per_mod, 0, 0)),
            pl.BlockSpec((tm, RET_W), row),
            pl.BlockSpec((tm, ATT_Q_W), row),
            pl.BlockSpec((tm, CONV_DIM), row),
            pl.BlockSpec((tm, CONV_DIM), row),
            pl.BlockSpec((BF16_SUBLANES, CONV_DIM), lambda i: (jnp.maximum(i * hb - 1, 0), 0)),
            pl.BlockSpec((BF16_SUBLANES, CONV_DIM), lambda i: (jnp.minimum((i + 1) * hb, last), 0)),
            pl.BlockSpec((None, 8, CONV_DIM), lambda i: (l, 0, 0)),
            _resident((None, D_MODEL, 3 * D_MODEL), lambda i: (l, 0, 0)),
            wspec(RET_W), wspec(ATT_Q_W), wspec(CONV_DIM), wspec(D_MODEL),
        ],
        out_specs=pl.BlockSpec((tm, D_MODEL), row),
        out_shape=jax.ShapeDtypeStruct((n, D_MODEL), F32),
        compiler_params=pltpu.CompilerParams(
            dimension_semantics=("arbitrary",), vmem_limit_bytes=VMEM_LIMIT),
        name="merge",
    )(h, mp, ret_o, att_o, cb, u, u, u, conv_w8, w_gates, w_ret_o, w_att_o, w_conv_o, w_o)


def _rope_tables(n_tok):
    rows = n_tok // GRID_W
    t_row = np.repeat(np.arange(rows, dtype=np.float64), GRID_W)
    t_col = np.tile(np.arange(GRID_W, dtype=np.float64), rows)
    n_freq = HEAD_DIM // 4
    inv = ROPE_THETA ** (-np.arange(n_freq, dtype=np.float64) / n_freq)
    ang = np.concatenate([t_row[:, None] * inv, t_col[:, None] * inv], axis=-1)
    cos, sin = np.cos(ang), np.sin(ang)
    cos64 = np.concatenate([cos, cos], axis=-1)
    sin64 = np.concatenate([-sin, sin], axis=-1)
    return (jnp.asarray(np.tile(cos64, (1, LANES // HEAD_DIM)), F32),
            jnp.asarray(np.tile(sin64, (1, LANES // HEAD_DIM)), F32))


def _mod_pack(mod_l, rows, sub, norm_w_row):
    r0 = rows[0]
    nr = len(rows)
    m = mod_l[r0:r0 + nr, 3 * sub * D_MODEL:3 * (sub + 1) * D_MODEL].reshape(nr, 3, D_MODEL)
    nw = jnp.broadcast_to(norm_w_row[None, None, :], (nr, 1, D_MODEL))
    pad = jnp.zeros((nr, 4, D_MODEL), F32)
    return jnp.concatenate([m, nw, pad], axis=1)


def kernel(x_prompt, x_sample, c, state_ret, cache_k, cache_v, c_ctx, w_ada, b_ada, norm_w, w_ffn_in,
           w_ffn_out, w_in, ret_decay_logit, ret_gn, q_gain, k_gain, conv_w, w_ret_o, w_att_o, w_conv_o,
           w_o):
    n_ctx_b, ctx_len, _ = x_prompt.shape
    n_lat_b, lat_len, _ = x_sample.shape

    w_ffn_in_b = w_ffn_in.astype(BF16)
    w_ffn_out_b = w_ffn_out.astype(BF16)
    w_in_b = w_in[:, :, :OFF_GATES].astype(BF16)
    w_gates_b = w_in[:, :, OFF_GATES:].astype(BF16)
    w_ret_o_b = w_ret_o.astype(BF16)
    w_att_o_b = w_att_o.astype(BF16)
    w_conv_o_b = w_conv_o.astype(BF16)
    w_o_b = w_o.astype(BF16)

    q_gain_t = jnp.tile(q_gain, (1, N_HEADS)).reshape(DEPTH, 1, ATT_Q_W)
    k_gain_t = jnp.tile(k_gain, (1, N_KV_HEADS)).reshape(DEPTH, 1, ATT_KV_W)
    conv_w8 = jnp.pad(conv_w, ((0, 0), (0, 8 - conv_w.shape[1]), (0, 0)))
    gid = np.arange(ATT_Q_W) // HEAD_DIM
    bd = jnp.asarray(gid[:, None] == gid[None, :], BF16)
    rope = _rope_tables(lat_len)

    cond8 = jnp.zeros((8, D_MODEL), F32).at[0].set(c_ctx).at[1:1 + n_lat_b].set(c)
    mod = _ada(cond8, w_ada, b_ada)

    ck = cache_k.astype(BF16)
    ck = jnp.broadcast_to(ck[:, :, :, :, None, :], ck.shape[:4] + (GROUP, HEAD_DIM))
    cache_kt = ck.reshape(ck.shape[0], ck.shape[1], ck.shape[2], N_KV_HEADS * KV_LANES)
    cache_vT = jnp.transpose(cache_v.astype(BF16), (0, 1, 3, 4, 2)).reshape(
        cache_v.shape[0], cache_v.shape[1], ATT_KV_W, cache_v.shape[2])

    groups = (
        dict(x=x_prompt.reshape(n_ctx_b * ctx_len, D_MODEL), rows=[0], seq=ctx_len, ctx=True),
        dict(x=x_sample.reshape(n_lat_b * lat_len, D_MODEL), rows=list(range(1, 1 + n_lat_b)),
             seq=lat_len, ctx=False),
    )
    results = []
    for grp in groups:
        h = grp["x"]
        seq = grp["seq"]
        is_ctx = grp["ctx"]
        rpm = h.shape[0] // len(grp["rows"])
        states, keys, values = [], [], []
        for l in range(DEPTH):
            mp = [_mod_pack(mod[l], grp["rows"], s, norm_w[l, s]) for s in range(3)]
            h = _ffn(h, mp[0], rpm, w_ffn_in_b, w_ffn_out_b, l, 0)
            outs = _inproj(h, mp[1], rpm, w_in_b, l, q_gain_t, k_gain_t, bd,
                           None if is_ctx else rope, seq, emit_kv=is_ctx)
            ret_in, qn, kt, vT, cb, u = outs[:6]
            if is_ctx:
                ret_o, st = _retention(ret_in, ret_decay_logit[l], ret_gn[l][None, :], None, seq,
                                       n_seq_blk=8, emit_state=True)
                states.append(st)
                keys.append(outs[6].reshape(n_ctx_b, seq, N_KV_HEADS, HEAD_DIM))
                values.append(outs[7].reshape(n_ctx_b, seq, N_KV_HEADS, HEAD_DIM))
                att_o = _attention(qn, kt, vT, None, seq)
            else:
                (ret_o,) = _retention(ret_in, ret_decay_logit[l], ret_gn[l][None, :], state_ret[:, l],
                                      seq, n_seq_blk=1, emit_state=False)
                att_o = _attention(qn, kt, vT, (cache_kt[:, l], cache_vT[:, l]), seq)
            h = _merge(h, mp[1], rpm, ret_o, att_o, cb, u, conv_w8, w_gates_b, w_ret_o_b, w_att_o_b,
                       w_conv_o_b, w_o_b, l, seq)
            h = _ffn(h, mp[2], rpm, w_ffn_in_b, w_ffn_out_b, l, 1)
        results.append((h, states, keys, values))

    (y_ctx, states, keys, values), (y_lat, _, _, _) = results
    y_prompt = y_ctx.reshape(x_prompt.shape)
    y_sample = y_lat.reshape(x_sample.shape)
    new_state_ret = jnp.stack(states, axis=1)
    new_cache_k = jnp.stack(keys, axis=1)
    new_cache_v = jnp.stack(values, axis=1)
    return (y_prompt, y_sample, new_state_ret, new_cache_k, new_cache_v)
```

```python
import functools
import math

import numpy as np
import jax
import jax.numpy as jnp
from jax import lax
from jax.experimental import pallas as pl
from jax.experimental.pallas import tpu as pltpu

D_MODEL = 1024
DEPTH = 2
GRID_W = 64
N_RET_HEADS = 4
RET_DK = 128
RET_DV = 128
RET_CHUNK = 128
N_HEADS = 8
N_KV_HEADS = 2
HEAD_DIM = 64
ROPE_THETA = 10000.0
CONV_DIM = 512
FFN_DIM = 2816
N_MOD = 9
EPS = 1e-6
RET_W = N_RET_HEADS * RET_DK
ATT_Q_W = N_HEADS * HEAD_DIM
ATT_KV_W = N_KV_HEADS * HEAD_DIM
GROUP = N_HEADS // N_KV_HEADS
KV_LANES = GROUP * HEAD_DIM
OFF_RQ, OFF_RK, OFF_RV, OFF_RG = 0, 512, 1024, 1536
OFF_AQ, OFF_AK, OFF_AV = 2048, 2560, 2688
OFF_CB, OFF_CC, OFF_CX = 2816, 3328, 3840
OFF_GATES = 4352

LANES = 128
BF16_SUBLANES = 16
VMEM_LIMIT = 56 * 1024 * 1024

F32 = jnp.float32
BF16 = jnp.bfloat16


def _dot(a, b):
    return jnp.dot(a, b, preferred_element_type=F32)


def _dot_nt(a, b):
    return lax.dot_general(a, b, (((1,), (1,)), ((), ())), preferred_element_type=F32)


def _sigmoid(x):
    return 1.0 / (1.0 + jnp.exp(-x))


def _silu(x):
    return x * _sigmoid(x)


def _mod_norm(x, mp):
    ms = jnp.mean(x * x, axis=-1, keepdims=True)
    y = x * lax.rsqrt(ms + EPS)
    return (y * mp[3:4]) * (1.0 + mp[1:2]) + mp[0:1]


def _resident(shape, index):
    return pl.BlockSpec(shape, index, pipeline_mode=pl.Buffered(1))


ADA_TN = 1152


def _ada_kernel(c_ref, w_ref, b_ref, o_ref):
    a = _silu(c_ref[...]).astype(BF16)
    o_ref[...] = _dot(a, w_ref[...].astype(BF16)) + b_ref[...]


def _ada(cond8, w_ada, b_ada):
    n = N_MOD * D_MODEL
    return pl.pallas_call(
        _ada_kernel,
        grid=(DEPTH, n // ADA_TN),
        in_specs=[
            pl.BlockSpec((8, D_MODEL), lambda l, j: (0, 0)),
            pl.BlockSpec((None, D_MODEL, ADA_TN), lambda l, j: (l, 0, j)),
            pl.BlockSpec((None, 1, ADA_TN), lambda l, j: (l, 0, j)),
        ],
        out_specs=pl.BlockSpec((None, 8, ADA_TN), lambda l, j: (l, 0, j)),
        out_shape=jax.ShapeDtypeStruct((DEPTH, 8, n), F32),
        compiler_params=pltpu.CompilerParams(
            dimension_semantics=("arbitrary", "arbitrary"), vmem_limit_bytes=VMEM_LIMIT),
        name="ada",
    )(cond8, w_ada, b_ada.reshape(DEPTH, 1, n))


FFN_TM = 1024
FFN_FC = 256


def _ffn_kernel(x_ref, mp_ref, wi_ref, wo_ref, o_ref, h_ref):
    x = x_ref[...]
    mp = mp_ref[...]
    xb = _mod_norm(x, mp).astype(BF16)
    for c in range(FFN_DIM // FFN_FC):
        lo = c * FFN_FC
        g = _dot(xb, wi_ref[:, lo:lo + FFN_FC])
        u = _dot(xb, wi_ref[:, FFN_DIM + lo:FFN_DIM + lo + FFN_FC])
        h_ref[:, lo:lo + FFN_FC] = (_silu(g) * u).astype(BF16)
    y = _dot(h_ref[...], wo_ref[...])
    o_ref[...] = x + (0.5 * mp[2:3]) * y


def _ffn(x, mp, rows_per_mod, w_in, w_out, l, j):
    n = x.shape[0]
    tm = FFN_TM
    return pl.pallas_call(
        _ffn_kernel,
        grid=(n // tm,),
        in_specs=[
            pl.BlockSpec((tm, D_MODEL), lambda i: (i, 0)),
            pl.BlockSpec((None, 8, D_MODEL), lambda i: ((i * tm) // rows_per_mod, 0, 0)),
            _resident((None, None, D_MODEL, 2 * FFN_DIM), lambda i: (l, j, 0, 0)),
            _resident((None, None, FFN_DIM, D_MODEL), lambda i: (l, j, 0, 0)),
        ],
        out_specs=pl.BlockSpec((tm, D_MODEL), lambda i: (i, 0)),
        out_shape=jax.ShapeDtypeStruct((n, D_MODEL), F32),
        scratch_shapes=[pltpu.VMEM((tm, FFN_DIM), BF16)],
        compiler_params=pltpu.CompilerParams(
            dimension_semantics=("arbitrary",), vmem_limit_bytes=VMEM_LIMIT),
        name="ffn",
    )(x, mp, w_in, w_out)


INPROJ_TM = 1024
INPROJ_SUB = 512
Q_SCALE = HEAD_DIM ** -0.5 * math.log2(math.e)


def _group_sumsq(a, bd):
    sq = a * a
    hi = sq.astype(BF16)
    lo = (sq - hi.astype(F32)).astype(BF16)
    return _dot(hi, bd) + _dot(lo, bd)


def _group_rms(a, sumsq, gain):
    return (a * lax.rsqrt(sumsq * (1.0 / HEAD_DIM) + EPS)) * gain


def _rope(x, cos, sin_signed):
    half = HEAD_DIM // 2
    reps = x.shape[1] // LANES
    lane = lax.broadcasted_iota(jnp.int32, (x.shape[0], LANES), 1)
    first = (lane % HEAD_DIM) < half
    out = []
    for r in range(reps):
        xs = x[:, r * LANES:(r + 1) * LANES]
        partner = jnp.where(first, pltpu.roll(xs, LANES - half, axis=1), pltpu.roll(xs, half, axis=1))
        out.append(xs * cos + partner * sin_signed)
    return out[0] if reps == 1 else jnp.concatenate(out, axis=1)


def _tile_kv(a):
    lane = lax.broadcasted_iota(jnp.int32, a.shape, 1)
    sw = pltpu.roll(a, HEAD_DIM, axis=1)
    h0 = jnp.where(lane < HEAD_DIM, a, sw)
    h1 = jnp.where(lane < HEAD_DIM, sw, a)
    return jnp.concatenate([h0, h0, h1, h1], axis=1)


def _inproj_kernel(*refs, use_rope, emit_kv):
    x_ref, mp_ref, w_ref, qg_ref, kg_ref, bd_ref = refs[:6]
    pos = 6
    if use_rope:
        cos_ref, sin_ref = refs[pos:pos + 2]
        pos += 2
    ret_ref, q_ref, kt_ref, vT_ref, cb_ref, u_ref = refs[pos:pos + 6]
    pos += 6
    if emit_kv:
        kout_ref, vout_ref = refs[pos:pos + 2]

    mp = mp_ref[...]
    tm = x_ref.shape[0]
    n_sub = tm // INPROJ_SUB
    xbs = [None] * n_sub
    xbs[0] = _mod_norm(x_ref[0:INPROJ_SUB, :], mp).astype(BF16)
    for s in range(n_sub):
        rows = slice(s * INPROJ_SUB, (s + 1) * INPROJ_SUB)
        xb = xbs[s]

        def proj(off, width):
            return _dot(xb, w_ref[:, off:off + width])

        aq = proj(OFF_AQ, ATT_Q_W)
        ret_ref[rows, OFF_RQ:OFF_RQ + RET_W] = proj(OFF_RQ, RET_W).astype(BF16)
        if s + 1 < n_sub:
            nxt = slice((s + 1) * INPROJ_SUB, (s + 2) * INPROJ_SUB)
            xbs[s + 1] = _mod_norm(x_ref[nxt, :], mp).astype(BF16)
        ss_q = _group_sumsq(aq, bd_ref[...])
        ak = proj(OFF_AK, ATT_KV_W)
        v = proj(OFF_AV, ATT_KV_W)
        ret_ref[rows, OFF_RK:OFF_RK + RET_W] = (proj(OFF_RK, RET_W) * (RET_DK ** -0.5)).astype(BF16)

        q = _group_rms(aq, ss_q, qg_ref[...])
        if use_rope:
            q = _rope(q, cos_ref[rows, :], sin_ref[rows, :])
        q_ref[rows, :] = (q * Q_SCALE).astype(BF16)
        ss_k = _group_sumsq(ak, bd_ref[0:ATT_KV_W, 0:ATT_KV_W])
        ret_ref[rows, OFF_RV:OFF_RV + RET_W] = proj(OFF_RV, RET_W).astype(BF16)

        k = _group_rms(ak, ss_k, kg_ref[...])
        if emit_kv:
            kout_ref[rows, :] = k
            vout_ref[rows, :] = v
        if use_rope:
            k = _rope(k, cos_ref[rows, :], sin_ref[rows, :])
        kt_ref[rows, :] = _tile_kv(k).astype(BF16)
        vT_ref[:, rows] = v.T.astype(BF16)

        ret_ref[rows, OFF_RG:OFF_RG + RET_W] = _silu(proj(OFF_RG, RET_W)).astype(BF16)
        cb_ref[rows, :] = proj(OFF_CB, CONV_DIM).astype(BF16)
        u_ref[rows, :] = (proj(OFF_CC, CONV_DIM) * proj(OFF_CX, CONV_DIM)).astype(BF16)


def _inproj(x, mp, rows_per_mod, w_in, l, q_gain_t, k_gain_t, bd, rope, seq_len, emit_kv):
    n = x.shape[0]
    tm = INPROJ_TM
    use_rope = rope is not None
    row = lambda i: (i, 0)
    in_specs = [
        pl.BlockSpec((tm, D_MODEL), row),
        pl.BlockSpec((None, 8, D_MODEL), lambda i: ((i * tm) // rows_per_mod, 0, 0)),
        _resident((None, D_MODEL, OFF_GATES), lambda i: (l, 0, 0)),
        pl.BlockSpec((None, 1, ATT_Q_W), lambda i: (l, 0, 0)),
        pl.BlockSpec((None, 1, ATT_KV_W), lambda i: (l, 0, 0)),
        _resident((ATT_Q_W, ATT_Q_W), lambda i: (0, 0)),
    ]
    args = [x, mp, w_in, q_gain_t, k_gain_t, bd]
    if use_rope:
        tiles_per_seq = seq_len // tm
        in_specs += [pl.BlockSpec((tm, LANES), lambda i: (i % tiles_per_seq, 0))] * 2
        args += list(rope)
    widths = [4 * RET_W, ATT_Q_W, ATT_Q_W, None, CONV_DIM, CONV_DIM]
    out_specs = [pl.BlockSpec((tm, w), row) for w in widths if w]
    out_shape = [jax.ShapeDtypeStruct((n, w), BF16) for w in widths if w]
    out_specs.insert(3, pl.BlockSpec((ATT_KV_W, tm), lambda i: (0, i)))
    out_shape.insert(3, jax.ShapeDtypeStruct((ATT_KV_W, n), BF16))
    if emit_kv:
        out_specs += [pl.BlockSpec((tm, ATT_KV_W), row)] * 2
        out_shape += [jax.ShapeDtypeStruct((n, ATT_KV_W), F32)] * 2
    return pl.pallas_call(
        functools.partial(_inproj_kernel, use_rope=use_rope, emit_kv=emit_kv),
        grid=(n // tm,),
        in_specs=in_specs,
        out_specs=out_specs,
        out_shape=out_shape,
        compiler_params=pltpu.CompilerParams(
            dimension_semantics=("arbitrary",), vmem_limit_bytes=VMEM_LIMIT),
        name="inproj",
    )(*args)


RET_CHUNK_UNROLL = 8
RET_SEQ_UNROLL = 4
TAB_DMASK, TAB_DQF, TAB_DQB, TAB_DKF, TAB_DKB, TAB_DCF, TAB_DCB, N_TAB = range(8)


def _log_sigmoid(x):
    return jnp.minimum(x, 0.0) - jnp.log(1.0 + jnp.exp(-jnp.abs(x)))


def _chunk_rows(start):
    if isinstance(start, int):
        return pl.ds(start, RET_CHUNK)
    return pl.ds(pl.multiple_of(start, RET_CHUNK), RET_CHUNK)


def _retention_kernel(*refs, n_seq, n_chunks, has_s0, emit_state):
    dl_ref, q_ref, k_ref, v_ref, g_ref, gn_ref = refs[:6]
    pos = 6
    if has_s0:
        s0_ref = refs[pos]
        pos += 1
    o_ref = refs[pos]
    pos += 1
    if emit_state:
        st_ref = refs[pos]
        pos += 1
    tab_ref, kv_ref, ent_ref = refs[pos:pos + 3]

    C = RET_CHUNK
    hd = pl.program_id(1)
    row = lax.broadcasted_iota(jnp.int32, (C, C), 0).astype(F32)
    col = lax.broadcasted_iota(jnp.int32, (C, C), 1).astype(F32)
    diff = row - col
    lgf = _log_sigmoid(jnp.full((C, C), dl_ref[0, hd], F32))
    lgb = _log_sigmoid(jnp.full((C, C), dl_ref[1, hd], F32))
    tab_ref[TAB_DMASK] = (jnp.where(diff >= 0, jnp.exp(lgf * jnp.maximum(diff, 0.0)), 0.0)
                          + jnp.where(diff <= 0, jnp.exp(lgb * jnp.maximum(-diff, 0.0)), 0.0))
    tab_ref[TAB_DQF] = jnp.exp(lgf * (row + 1.0))
    tab_ref[TAB_DQB] = jnp.exp(lgb * (C - row))
    tab_ref[TAB_DKF] = jnp.exp(lgf * (C - 1.0 - row))
    tab_ref[TAB_DKB] = jnp.exp(lgb * row)
    tab_ref[TAB_DCF] = jnp.exp(lgf * C)
    tab_ref[TAB_DCB] = jnp.exp(lgb * C)

    def kv_phase(items):
        ops = []
        for r, _ in items:
            rows = _chunk_rows(r)
            k = k_ref[rows, :].astype(F32)
            vT = v_ref[rows, :].astype(F32).T.astype(BF16)
            kk = jnp.concatenate([(k * tab_ref[TAB_DKF]).astype(BF16),
                                  (k * tab_ref[TAB_DKB]).astype(BF16)], axis=1)
            ops.append((vT, kk))
        for (_, i), (vT, kk) in zip(items, ops):
            kv_ref[i] = _dot(vT, kk)

    def scan_phase(s, sc):
        if has_s0:
            st0 = (s0_ref[0].T, s0_ref[1].T)
        else:
            st0 = (jnp.zeros((RET_DV, RET_DK), F32), jnp.zeros((RET_DV, RET_DK), F32))

        def scan_step(t, st):
            st_f, st_b = st
            cb = n_chunks - 1 - t
            ent_ref[sc + t, :, 0:RET_DK] = st_f.astype(BF16)
            ent_ref[sc + cb, :, RET_DK:2 * RET_DK] = st_b.astype(BF16)
            st_f = st_f * tab_ref[TAB_DCF] + kv_ref[sc + t, :, 0:RET_DK]
            st_b = st_b * tab_ref[TAB_DCB] + kv_ref[sc + cb, :, RET_DK:2 * RET_DK]
            return st_f, st_b

        if n_chunks <= RET_CHUNK_UNROLL:
            st = st0
            for t in range(n_chunks):
                st = scan_step(t, st)
        else:
            st = lax.fori_loop(0, n_chunks, scan_step, st0)
        if emit_state:
            st_ref[s, 0] = st[0].T
            st_ref[s, 1] = st[1].T

    def out_phase(items):
        qs = [q_ref[_chunk_rows(r), :] for r, _ in items]
        att = [_dot_nt(q, k_ref[_chunk_rows(r), :]) for q, (r, _) in zip(qs, items)]
        att = [(a * tab_ref[TAB_DMASK]).astype(BF16) for a in att]
        qq = []
        for q in qs:
            qf = q.astype(F32)
            qq.append(jnp.concatenate([(qf * tab_ref[TAB_DQF]).astype(BF16),
                                       (qf * tab_ref[TAB_DQB]).astype(BF16)], axis=1))
        outs = [_dot(a, v_ref[_chunk_rows(r), :]) + _dot_nt(x, ent_ref[i])
                for a, x, (r, i) in zip(att, qq, items)]
        for o, (r, _) in zip(outs, items):
            rows = _chunk_rows(r)
            ms = jnp.mean(o * o, axis=-1, keepdims=True)
            on = (o * lax.rsqrt(ms + EPS)) * gn_ref[...]
            o_ref[rows, :] = (g_ref[rows, :].astype(F32) * on).astype(BF16)

    if n_seq == 1:
        assert n_chunks % RET_CHUNK_UNROLL == 0

        def chunk_group(phase):
            def step(t, carry):
                c0 = t * RET_CHUNK_UNROLL
                phase([((c0 + j) * C, c0 + j) for j in range(RET_CHUNK_UNROLL)])
                return carry
            lax.fori_loop(0, n_chunks // RET_CHUNK_UNROLL, step, 0)

        chunk_group(kv_phase)
        scan_phase(0, 0)
        chunk_group(out_phase)
    else:
        assert n_seq % RET_SEQ_UNROLL == 0 and n_chunks <= RET_CHUNK_UNROLL

        def seq_group(t, carry):
            s0 = t * RET_SEQ_UNROLL
            items = [((s0 + j) * (n_chunks * C) + c * C, j * n_chunks + c)
                     for j in range(RET_SEQ_UNROLL) for c in range(n_chunks)]
            kv_phase(items)
            for j in range(RET_SEQ_UNROLL):
                scan_phase(s0 + j, j * n_chunks)
            out_phase(items)
            return carry

        lax.fori_loop(0, n_seq // RET_SEQ_UNROLL, seq_group, 0)


def _retention(ret_in, decay_logit_l, gn_l, s0, seq_len, n_seq_blk, emit_state):
    n = ret_in.shape[0]
    n_chunks = seq_len // RET_CHUNK
    tb = n_seq_blk * seq_len
    has_s0 = s0 is not None
    n_off = RET_W // RET_DK

    def branch(k):
        return pl.BlockSpec((tb, RET_DK), lambda i, h: (i, k * n_off + h))

    in_specs = [pl.BlockSpec(memory_space=pltpu.SMEM), branch(0), branch(1), branch(2), branch(3),
                pl.BlockSpec((1, RET_DV), lambda i, h: (0, h))]
    args = [decay_logit_l, ret_in, ret_in, ret_in, ret_in, gn_l]
    if has_s0:
        assert n_seq_blk == 1
        in_specs.append(pl.BlockSpec((None, 2, None, RET_DK, RET_DV), lambda i, h: (i, 0, h, 0, 0)))
        args.append(s0)
    out_specs = [pl.BlockSpec((tb, RET_DV), lambda i, h: (i, h))]
    out_shape = [jax.ShapeDtypeStruct((n, RET_W), BF16)]
    if emit_state:
        out_specs.append(pl.BlockSpec((n_seq_blk, 2, None, RET_DK, RET_DV), lambda i, h: (i, 0, h, 0, 0)))
        out_shape.append(jax.ShapeDtypeStruct((n // seq_len, 2, N_RET_HEADS, RET_DK, RET_DV), F32))
    n_slots = n_chunks * (1 if n_seq_blk == 1 else RET_SEQ_UNROLL)
    return pl.pallas_call(
        functools.partial(_retention_kernel, n_seq=n_seq_blk, n_chunks=n_chunks, has_s0=has_s0,
                          emit_state=emit_state),
        grid=(n // tb, N_RET_HEADS),
        in_specs=in_specs,
        out_specs=out_specs,
        out_shape=out_shape,
        scratch_shapes=[
            pltpu.VMEM((N_TAB, RET_CHUNK, RET_CHUNK), F32),
            pltpu.VMEM((n_slots, RET_DV, 2 * RET_DK), F32),
            pltpu.VMEM((n_slots, RET_DV, 2 * RET_DK), BF16),
        ],
        compiler_params=pltpu.CompilerParams(
            dimension_semantics=("arbitrary", "arbitrary"), vmem_limit_bytes=VMEM_LIMIT),
        name="retention",
    )(*args)


ATT_TQ = 1024
ATT_KB = 512
ATT_SHORT_MAX = 512
ATT_RC = 16
NEG_BIG = -1e30
SUM_ROWS = BF16_SUBLANES


def _attention_kernel(*refs, n_cache, n_new):
    q_ref = refs[0]
    pos = 1
    if n_cache:
        kc_ref, vc_ref = refs[pos:pos + 2]
        pos += 2
    kn_ref, vn_ref, o_ref, qs_ref, acc_ref = refs[pos:pos + 5]
    s_slots = refs[pos + 5:pos + 7]
    p_slots = refs[pos + 7:pos + 9]

    tq = q_ref.shape[0]
    nq = GROUP * tq
    qf = q_ref[...].astype(F32)
    head = lax.broadcasted_iota(jnp.int32, (tq, KV_LANES), 1) // HEAD_DIM
    for g in range(GROUP):
        qs_ref[:, g * tq:(g + 1) * tq] = jnp.where(head == g, qf, 0.0).T.astype(BF16)
    acc_ref[...] = jnp.zeros(acc_ref.shape, F32)

    kb = min(ATT_KB, n_new)
    n_cb = 1 if n_cache else 0
    n_nb = n_new // kb
    assert n_new % kb == 0
    n_blk = n_cb + n_nb

    def block(t):
        if isinstance(t, int) and t < n_cb:
            return kc_ref, vc_ref, pl.ds(0, n_cache)
        lo = (t - n_cb) * kb
        return kn_ref, vn_ref, (pl.ds(lo, kb) if isinstance(lo, int) else pl.ds(pl.multiple_of(lo, kb), kb))

    def scores(t, par):
        k_ref, _, keys = block(t)
        s = _dot(k_ref[keys, :], qs_ref[...])
        s_slots[par][0:keys.size, :] = s
        return jnp.max(s, axis=0, keepdims=True)

    def softmax(n_keys, par, m, s_max):
        s_ref, p_ref = s_slots[par], p_slots[par]
        m_new = jnp.maximum(m, s_max)
        m_rows = jnp.broadcast_to(m_new, (ATT_RC, nq))
        for r in range(0, n_keys, ATT_RC):
            p_ref[r:r + ATT_RC, :] = jnp.exp2(s_ref[r:r + ATT_RC, :] - m_rows).astype(BF16)
        return m_new, jnp.exp2(m - m_new)

    def pv(t, par, alpha):
        _, vT_ref, keys = block(t)
        v1 = jnp.concatenate([vT_ref[:, keys], jnp.ones((SUM_ROWS, keys.size), BF16)], axis=0)
        acc_ref[...] = alpha * acc_ref[...] + _dot(v1, p_slots[par][0:keys.size, :])

    def stage(t, par, carry):
        m, alpha, s_max = carry
        next_max = s_max
        if not isinstance(t, int) or t + 1 < n_blk:
            next_max = scores(t + 1, 1 - par)
        if not isinstance(t, int) or t >= 1:
            pv(t - 1, 1 - par, alpha)
        return softmax(block(t)[2].size, par, m, s_max) + (next_max,)

    carry = (jnp.full((1, nq), NEG_BIG, F32), jnp.zeros((1, nq), F32), scores(0, 0))
    lo_t = min(n_cb + 1, n_blk)
    n_loop = max(n_blk - 1 - lo_t, 0)
    if n_loop % 2:
        lo_t += 1
        n_loop -= 1
    for t in range(lo_t):
        carry = stage(t, t % 2, carry)

    def pair(i, carry):
        t = lo_t + 2 * i
        carry = stage(t, lo_t % 2, carry)
        return stage(t + 1, (lo_t + 1) % 2, carry)

    carry = lax.fori_loop(0, n_loop // 2, pair, carry)
    for t in range(lo_t + n_loop, n_blk):
        carry = stage(t, t % 2, carry)
    pv(n_blk - 1, (n_blk - 1) % 2, carry[1])

    acc = acc_ref[...]
    oT = acc[0:HEAD_DIM] / acc[HEAD_DIM:HEAD_DIM + 1]
    o4 = jnp.concatenate([oT[:, g * tq:(g + 1) * tq] for g in range(GROUP)], axis=0)
    o_ref[...] = o4.T.astype(BF16)


ATT_SEQ_PER_STEP = 2


def _attention_short_kernel(q_ref, k_ref, vT_ref, o_ref, *, n_seq):
    t_len = q_ref.shape[0] // n_seq
    head = lax.broadcasted_iota(jnp.int32, (t_len, KV_LANES), 1) // HEAD_DIM
    ones = jnp.ones((SUM_ROWS, t_len), BF16)
    for s in range(n_seq):
        rows = slice(s * t_len, (s + 1) * t_len)
        for kv in range(N_KV_HEADS):
            cols = slice(kv * KV_LANES, (kv + 1) * KV_LANES)
            qf = q_ref[rows, cols].astype(F32)
            qsT = jnp.concatenate([jnp.where(head == g, qf, 0.0).T.astype(BF16) for g in range(GROUP)],
                                  axis=1)
            sT = _dot(k_ref[rows, cols], qsT)
            pT = jnp.exp2(sT - jnp.max(sT, axis=0, keepdims=True)).astype(BF16)
            v1 = jnp.concatenate([vT_ref[kv * HEAD_DIM:(kv + 1) * HEAD_DIM, rows], ones], axis=0)
            acc = _dot(v1, pT)
            oT = acc[0:HEAD_DIM] / acc[HEAD_DIM:HEAD_DIM + 1]
            o4 = jnp.concatenate([oT[:, g * t_len:(g + 1) * t_len] for g in range(GROUP)], axis=0)
            o_ref[rows, cols] = o4.T.astype(BF16)


def _attention_short(q, kt, vT, seq_len):
    n = q.shape[0]
    tb = ATT_SEQ_PER_STEP * seq_len
    return pl.pallas_call(
        functools.partial(_attention_short_kernel, n_seq=ATT_SEQ_PER_STEP),
        grid=(n // tb,),
        in_specs=[pl.BlockSpec((tb, ATT_Q_W), lambda i: (i, 0)),
                  pl.BlockSpec((tb, ATT_Q_W), lambda i: (i, 0)),
                  pl.BlockSpec((ATT_KV_W, tb), lambda i: (0, i))],
        out_specs=pl.BlockSpec((tb, ATT_Q_W), lambda i: (i, 0)),
        out_shape=jax.ShapeDtypeStruct((n, ATT_Q_W), BF16),
        compiler_params=pltpu.CompilerParams(
            dimension_semantics=("arbitrary",), vmem_limit_bytes=VMEM_LIMIT),
        name="attention_short",
    )(q, kt, vT)


def _attention(q, kt, vT, cache, seq_len):
    n = q.shape[0]
    n_batch = n // seq_len
    tq = ATT_TQ
    nq = seq_len // tq
    n_cache = 0 if cache is None else cache[0].shape[1]
    if n_cache == 0 and seq_len <= ATT_SHORT_MAX:
        return _attention_short(q, kt, vT, seq_len)
    blk_rows = max(min(ATT_KB, seq_len), n_cache)
    in_specs = [pl.BlockSpec((tq, KV_LANES), lambda b, kv, i: (b * nq + i, kv))]
    args = [q]
    if n_cache:
        in_specs += [pl.BlockSpec((None, n_cache, KV_LANES), lambda b, kv, i: (b, 0, kv)),
                     pl.BlockSpec((None, HEAD_DIM, n_cache), lambda b, kv, i: (b, kv, 0))]
        args += list(cache)
    in_specs += [pl.BlockSpec((seq_len, KV_LANES), lambda b, kv, i: (b, kv)),
                 pl.BlockSpec((HEAD_DIM, seq_len), lambda b, kv, i: (kv, b))]
    args += [kt, vT]
    return pl.pallas_call(
        functools.partial(_attention_kernel, n_cache=n_cache, n_new=seq_len),
        grid=(n_batch, N_KV_HEADS, nq),
        in_specs=in_specs,
        out_specs=pl.BlockSpec((tq, KV_LANES), lambda b, kv, i: (b * nq + i, kv)),
        out_shape=jax.ShapeDtypeStruct((n, ATT_Q_W), BF16),
        scratch_shapes=[
            pltpu.VMEM((KV_LANES, GROUP * tq), BF16),
            pltpu.VMEM((HEAD_DIM + SUM_ROWS, GROUP * tq), F32),
            pltpu.VMEM((blk_rows, GROUP * tq), F32),
            pltpu.VMEM((blk_rows, GROUP * tq), F32),
            pltpu.VMEM((blk_rows, GROUP * tq), BF16),
            pltpu.VMEM((blk_rows, GROUP * tq), BF16),
        ],
        compiler_params=pltpu.CompilerParams(
            dimension_semantics=("arbitrary", "arbitrary", "arbitrary"), vmem_limit_bytes=VMEM_LIMIT),
        name="attention",
    )(*args)


MERGE_TM = 1024
MERGE_SUB = 512


def _merge_kernel(h_ref, mp_ref, ro_ref, ao_ref, cb_ref, u_ref, up_ref, un_ref, cw_ref,
                  wg_ref, wr_ref, wa_ref, wc_ref, wo_ref, o_ref, *, seq_len):
    tm = h_ref.shape[0]
    sub = MERGE_SUB
    n_sub = tm // sub
    i = pl.program_id(0)
    cw = cw_ref[...]
    r = lax.broadcasted_iota(jnp.int32, (sub, CONV_DIM), 0)
    edge = BF16_SUBLANES

    def branches(s):
        r0 = s * sub
        rows = slice(r0, r0 + sub)
        yr = _dot(ro_ref[rows, :], wr_ref[...])
        ya = _dot(ao_ref[rows, :], wa_ref[...])
        u = u_ref[rows, :].astype(F32)
        before = up_ref if s == 0 else u_ref.at[r0 - edge:r0, :]
        after = un_ref if s == n_sub - 1 else u_ref.at[r0 + sub:r0 + sub + edge, :]
        prev_row = before[edge - 1:edge, :].astype(F32)
        next_row = after[0:1, :].astype(F32)
        t = (i * tm + r0 + r) % seq_len
        u_prev = jnp.where(r == 0, prev_row, pltpu.roll(u, 1, axis=0))
        u_next = jnp.where(r == sub - 1, next_row, pltpu.roll(u, sub - 1, axis=0))
        u_prev = jnp.where(t == 0, 0.0, u_prev)
        u_next = jnp.where(t == seq_len - 1, 0.0, u_next)
        conv = u_prev * cw[0:1] + u * cw[1:2] + u_next * cw[2:3]
        yc = _dot((cb_ref[rows, :].astype(F32) * conv).astype(BF16), wc_ref[...])
        xb = _mod_norm(h_ref[rows, :], mp_ref[...]).astype(BF16)
        merged = _sigmoid(_dot(xb, wg_ref[:, 0:D_MODEL])) * yr
        merged = merged + _sigmoid(_dot(xb, wg_ref[:, D_MODEL:2 * D_MODEL])) * ya
        merged = merged + _sigmoid(_dot(xb, wg_ref[:, 2 * D_MODEL:3 * D_MODEL])) * yc
        return merged.astype(BF16)

    def project(s, merged):
        rows = slice(s * sub, (s + 1) * sub)
        o_ref[rows, :] = h_ref[rows, :] + mp_ref[2:3, :] * _dot(merged, wo_ref[...])

    pending = branches(0)
    for s in range(1, n_sub):
        nxt = branches(s)
        project(s - 1, pending)
        pending = nxt
    project(n_sub - 1, pending)


def _merge(h, mp, rows_per_mod, ret_o, att_o, cb, u, conv_w8, w_gates, w_ret_o, w_att_o, w_conv_o, w_o,
           l, seq_len):
    n = h.shape[0]
    tm = MERGE_TM
    hb = tm // BF16_SUBLANES
    last = n // BF16_SUBLANES - 1
    row = lambda i: (i, 0)
    wspec = lambda k: _resident((None, k, D_MODEL), lambda i: (l, 0, 0))
    return pl.pallas_call(
        functools.partial(_merge_kernel, seq_len=seq_len),
        grid=(n // tm,),
        in_specs=[
            pl.BlockSpec((tm, D_MODEL), row),
            pl.BlockSpec((None, 8, D_MODEL), lambda i: ((i * tm) // rows_per_mod, 0, 0)),
            pl.BlockSpec((tm, RET_W), row),
            pl.BlockSpec((tm, ATT_Q_W), row),
            pl.BlockSpec((tm, CONV_DIM), row),
            pl.BlockSpec((tm, CONV_DIM), row),
            pl.BlockSpec((BF16_SUBLANES, CONV_DIM), lambda i: (jnp.maximum(i * hb - 1, 0), 0)),
            pl.BlockSpec((BF16_SUBLANES, CONV_DIM), lambda i: (jnp.minimum((i + 1) * hb, last), 0)),
            pl.BlockSpec((None, 8, CONV_DIM), lambda i: (l, 0, 0)),
            _resident((None, D_MODEL, 3 * D_MODEL), lambda i: (l, 0, 0)),
            wspec(RET_W), wspec(ATT_Q_W), wspec(CONV_DIM), wspec(D_MODEL),
        ],
        out_specs=pl.BlockSpec((tm, D_MODEL), row),
        out_shape=jax.ShapeDtypeStruct((n, D_MODEL), F32),
        compiler_params=pltpu.CompilerParams(
            dimension_semantics=("arbitrary",), vmem_limit_bytes=VMEM_LIMIT),
        name="merge",
    )(h, mp, ret_o, att_o, cb, u, u, u, conv_w8, w_gates, w_ret_o, w_att_o, w_conv_o, w_o)


def _rope_tables(n_tok):
    rows = n_tok // GRID_W
    t_row = np.repeat(np.arange(rows, dtype=np.float64), GRID_W)
    t_col = np.tile(np.arange(GRID_W, dtype=np.float64), rows)
    n_freq = HEAD_DIM // 4
    inv = ROPE_THETA ** (-np.arange(n_freq, dtype=np.float64) / n_freq)
    ang = np.concatenate([t_row[:, None] * inv, t_col[:, None] * inv], axis=-1)
    cos, sin = np.cos(ang), np.sin(ang)
    cos64 = np.concatenate([cos, cos], axis=-1)
    sin64 = np.concatenate([-sin, sin], axis=-1)
    return (jnp.asarray(np.tile(cos64, (1, LANES // HEAD_DIM)), F32),
            jnp.asarray(np.tile(sin64, (1, LANES // HEAD_DIM)), F32))


def _mod_pack(mod_l, rows, sub, norm_w_row):
    r0 = rows[0]
    nr = len(rows)
    m = mod_l[r0:r0 + nr, 3 * sub * D_MODEL:3 * (sub + 1) * D_MODEL].reshape(nr, 3, D_MODEL)
    nw = jnp.broadcast_to(norm_w_row[None, None, :], (nr, 1, D_MODEL))
    pad = jnp.zeros((nr, 4, D_MODEL), F32)
    return jnp.concatenate([m, nw, pad], axis=1)


def kernel(x_prompt, x_sample, c, state_ret, cache_k, cache_v, c_ctx, w_ada, b_ada, norm_w, w_ffn_in,
           w_ffn_out, w_in, ret_decay_logit, ret_gn, q_gain, k_gain, conv_w, w_ret_o, w_att_o, w_conv_o,
           w_o):
    n_ctx_b, ctx_len, _ = x_prompt.shape
    n_lat_b, lat_len, _ = x_sample.shape

    w_ffn_in_b = w_ffn_in.astype(BF16)
    w_ffn_out_b = w_ffn_out.astype(BF16)
    w_in_b = w_in[:, :, :OFF_GATES].astype(BF16)
    w_gates_b = w_in[:, :, OFF_GATES:].astype(BF16)
    w_ret_o_b = w_ret_o.astype(BF16)
    w_att_o_b = w_att_o.astype(BF16)
    w_conv_o_b = w_conv_o.astype(BF16)
    w_o_b = w_o.astype(BF16)

    q_gain_t = jnp.tile(q_gain, (1, N_HEADS)).reshape(DEPTH, 1, ATT_Q_W)
    k_gain_t = jnp.tile(k_gain, (1, N_KV_HEADS)).reshape(DEPTH, 1, ATT_KV_W)
    conv_w8 = jnp.pad(conv_w, ((0, 0), (0, 8 - conv_w.shape[1]), (0, 0)))
    gid = np.arange(ATT_Q_W) // HEAD_DIM
    bd = jnp.asarray(gid[:, None] == gid[None, :], BF16)
    rope = _rope_tables(lat_len)

    cond8 = jnp.zeros((8, D_MODEL), F32).at[0].set(c_ctx).at[1:1 + n_lat_b].set(c)
    mod = _ada(cond8, w_ada, b_ada)

    ck = cache_k.astype(BF16)
    ck = jnp.broadcast_to(ck[:, :, :, :, None, :], ck.shape[:4] + (GROUP, HEAD_DIM))
    cache_kt = ck.reshape(ck.shape[0], ck.shape[1], ck.shape[2], N_KV_HEADS * KV_LANES)
    cache_vT = jnp.transpose(cache_v.astype(BF16), (0, 1, 3, 4, 2)).reshape(
        cache_v.shape[0], cache_v.shape[1], ATT_KV_W, cache_v.shape[2])

    groups = (
        dict(x=x_prompt.reshape(n_ctx_b * ctx_len, D_MODEL), rows=[0], seq=ctx_len, ctx=True),
        dict(x=x_sample.reshape(n_lat_b * lat_len, D_MODEL), rows=list(range(1, 1 + n_lat_b)),
             seq=lat_len, ctx=False),
    )
    results = []
    for grp in groups:
        h = grp["x"]
        seq = grp["seq"]
        is_ctx = grp["ctx"]
        rpm = h.shape[0] // len(grp["rows"])
        states, keys, values = [], [], []
        for l in range(DEPTH):
            mp = [_mod_pack(mod[l], grp["rows"], s, norm_w[l, s]) for s in range(3)]
            h = _ffn(h, mp[0], rpm, w_ffn_in_b, w_ffn_out_b, l, 0)
            outs = _inproj(h, mp[1], rpm, w_in_b, l, q_gain_t, k_gain_t, bd,
                           None if is_ctx else rope, seq, emit_kv=is_ctx)
            ret_in, qn, kt, vT, cb, u = outs[:6]
            if is_ctx:
                ret_o, st = _retention(ret_in, ret_decay_logit[l], ret_gn[l][None, :], None, seq,
                                       n_seq_blk=8, emit_state=True)
                states.append(st)
                keys.append(outs[6].reshape(n_ctx_b, seq, N_KV_HEADS, HEAD_DIM))
                values.append(outs[7].reshape(n_ctx_b, seq, N_KV_HEADS, HEAD_DIM))
                att_o = _attention(qn, kt, vT, None, seq)
            else:
                (ret_o,) = _retention(ret_in, ret_decay_logit[l], ret_gn[l][None, :], state_ret[:, l],
                                      seq, n_seq_blk=1, emit_state=False)
                att_o = _attention(qn, kt, vT, (cache_kt[:, l], cache_vT[:, l]), seq)
            h = _merge(h, mp[1], rpm, ret_o, att_o, cb, u, conv_w8, w_gates_b, w_ret_o_b, w_att_o_b,
                       w_conv_o_b, w_o_b, l, seq)
            h = _ffn(h, mp[2], rpm, w_ffn_in_b, w_ffn_out_b, l, 1)
        results.append((h, states, keys, values))

    (y_ctx, states, keys, values), (y_lat, _, _, _) = results
    y_prompt = y_ctx.reshape(x_prompt.shape)
    y_sample = y_lat.reshape(x_sample.shape)
    new_state_ret = jnp.stack(states, axis=1)
    new_cache_k = jnp.stack(keys, axis=1)
    new_cache_v = jnp.stack(values, axis=1)
    return (y_prompt, y_sample, new_state_ret, new_cache_k, new_cache_v)
```

```python
import functools
import math

import numpy as np
import jax
import jax.numpy as jnp
from jax import lax
from jax.experimental import pallas as pl
from jax.experimental.pallas import tpu as pltpu

D_MODEL = 1024
DEPTH = 2
GRID_W = 64
N_RET_HEADS = 4
RET_DK = 128
RET_DV = 128
RET_CHUNK = 128
N_HEADS = 8
N_KV_HEADS = 2
HEAD_DIM = 64
ROPE_THETA = 10000.0
CONV_DIM = 512
FFN_DIM = 2816
N_MOD = 9
EPS = 1e-6
RET_W = N_RET_HEADS * RET_DK
ATT_Q_W = N_HEADS * HEAD_DIM
ATT_KV_W = N_KV_HEADS * HEAD_DIM
GROUP = N_HEADS // N_KV_HEADS
KV_LANES = GROUP * HEAD_DIM
OFF_RQ, OFF_RK, OFF_RV, OFF_RG = 0, 512, 1024, 1536
OFF_AQ, OFF_AK, OFF_AV = 2048, 2560, 2688
OFF_CB, OFF_CC, OFF_CX = 2816, 3328, 3840
OFF_GATES = 4352

LANES = 128
BF16_SUBLANES = 16
VMEM_LIMIT = 56 * 1024 * 1024

F32 = jnp.float32
BF16 = jnp.bfloat16


def _dot(a, b):
    return jnp.dot(a, b, preferred_element_type=F32)


def _dot_nt(a, b):
    return lax.dot_general(a, b, (((1,), (1,)), ((), ())), preferred_element_type=F32)


def _sigmoid(x):
    return 1.0 / (1.0 + jnp.exp(-x))


def _silu(x):
    return x * _sigmoid(x)


def _mod_norm(x, mp):
    ms = jnp.mean(x * x, axis=-1, keepdims=True)
    y = x * lax.rsqrt(ms + EPS)
    return (y * mp[3:4]) * (1.0 + mp[1:2]) + mp[0:1]


def _resident(shape, index):
    return pl.BlockSpec(shape, index, pipeline_mode=pl.Buffered(1))


ADA_TN = 2304


def _ada_kernel(c_ref, w_ref, b_ref, o_ref):
    a = _silu(c_ref[...]).astype(BF16)
    o_ref[...] = _dot(a, w_ref[...].astype(BF16)) + b_ref[...]


def _ada(cond8, w_ada, b_ada):
    n = N_MOD * D_MODEL
    return pl.pallas_call(
        _ada_kernel,
        grid=(DEPTH, n // ADA_TN),
        in_specs=[
            pl.BlockSpec((8, D_MODEL), lambda l, j: (0, 0)),
            pl.BlockSpec((None, D_MODEL, ADA_TN), lambda l, j: (l, 0, j)),
            pl.BlockSpec((None, 1, ADA_TN), lambda l, j: (l, 0, j)),
        ],
        out_specs=pl.BlockSpec((None, 8, ADA_TN), lambda l, j: (l, 0, j)),
        out_shape=jax.ShapeDtypeStruct((DEPTH, 8, n), F32),
        compiler_params=pltpu.CompilerParams(
            dimension_semantics=("arbitrary", "arbitrary"), vmem_limit_bytes=VMEM_LIMIT),
        name="ada",
    )(cond8, w_ada, b_ada.reshape(DEPTH, 1, n))


FFN_TM = 1024
FFN_FC = 256


def _ffn_kernel(x_ref, mp_ref, wi_ref, wo_ref, o_ref, h_ref):
    x = x_ref[...]
    mp = mp_ref[...]
    xb = _mod_norm(x, mp).astype(BF16)
    for c in range(FFN_DIM // FFN_FC):
        lo = c * FFN_FC
        g = _dot(xb, wi_ref[:, lo:lo + FFN_FC])
        u = _dot(xb, wi_ref[:, FFN_DIM + lo:FFN_DIM + lo + FFN_FC])
        h_ref[:, lo:lo + FFN_FC] = (_silu(g) * u).astype(BF16)
    y = _dot(h_ref[...], wo_ref[...])
    o_ref[...] = x + (0.5 * mp[2:3]) * y


def _ffn(x, mp, rows_per_mod, w_in, w_out, l, j):
    n = x.shape[0]
    tm = FFN_TM
    return pl.pallas_call(
        _ffn_kernel,
        grid=(n // tm,),
        in_specs=[
            pl.BlockSpec((tm, D_MODEL), lambda i: (i, 0)),
            pl.BlockSpec((None, 8, D_MODEL), lambda i: ((i * tm) // rows_per_mod, 0, 0)),
            _resident((None, None, D_MODEL, 2 * FFN_DIM), lambda i: (l, j, 0, 0)),
            _resident((None, None, FFN_DIM, D_MODEL), lambda i: (l, j, 0, 0)),
        ],
        out_specs=pl.BlockSpec((tm, D_MODEL), lambda i: (i, 0)),
        out_shape=jax.ShapeDtypeStruct((n, D_MODEL), F32),
        scratch_shapes=[pltpu.VMEM((tm, FFN_DIM), BF16)],
        compiler_params=pltpu.CompilerParams(
            dimension_semantics=("arbitrary",), vmem_limit_bytes=VMEM_LIMIT),
        name="ffn",
    )(x, mp, w_in, w_out)


INPROJ_TM = 1024
INPROJ_SUB = 512
Q_SCALE = HEAD_DIM ** -0.5 * math.log2(math.e)


def _group_sumsq(a, bd):
    sq = a * a
    hi = sq.astype(BF16)
    lo = (sq - hi.astype(F32)).astype(BF16)
    return _dot(hi, bd) + _dot(lo, bd)


def _group_rms(a, sumsq, gain):
    return (a * lax.rsqrt(sumsq * (1.0 / HEAD_DIM) + EPS)) * gain


def _rope(x, cos, sin_signed):
    half = HEAD_DIM // 2
    reps = x.shape[1] // LANES
    lane = lax.broadcasted_iota(jnp.int32, (x.shape[0], LANES), 1)
    first = (lane % HEAD_DIM) < half
    out = []
    for r in range(reps):
        xs = x[:, r * LANES:(r + 1) * LANES]
        partner = jnp.where(first, pltpu.roll(xs, LANES - half, axis=1), pltpu.roll(xs, half, axis=1))
        out.append(xs * cos + partner * sin_signed)
    return out[0] if reps == 1 else jnp.concatenate(out, axis=1)


def _tile_kv(a):
    lane = lax.broadcasted_iota(jnp.int32, a.shape, 1)
    sw = pltpu.roll(a, HEAD_DIM, axis=1)
    h0 = jnp.where(lane < HEAD_DIM, a, sw)
    h1 = jnp.where(lane < HEAD_DIM, sw, a)
    return jnp.concatenate([h0, h0, h1, h1], axis=1)


def _inproj_kernel(*refs, use_rope, emit_kv):
    x_ref, mp_ref, w_ref, qg_ref, kg_ref, bd_ref = refs[:6]
    pos = 6
    if use_rope:
        cos_ref, sin_ref = refs[pos:pos + 2]
        pos += 2
    ret_ref, q_ref, kt_ref, vT_ref, cb_ref, u_ref = refs[pos:pos + 6]
    pos += 6
    if emit_kv:
        kout_ref, vout_ref = refs[pos:pos + 2]

    mp = mp_ref[...]
    tm = x_ref.shape[0]
    n_sub = tm // INPROJ_SUB
    xbs = [None] * n_sub
    xbs[0] = _mod_norm(x_ref[0:INPROJ_SUB, :], mp).astype(BF16)
    for s in range(n_sub):
        rows = slice(s * INPROJ_SUB, (s + 1) * INPROJ_SUB)
        xb = xbs[s]

        def proj(off, width):
            return _dot(xb, w_ref[:, off:off + width])

        aq = proj(OFF_AQ, ATT_Q_W)
        ret_ref[rows, OFF_RQ:OFF_RQ + RET_W] = proj(OFF_RQ, RET_W).astype(BF16)
        if s + 1 < n_sub:
            nxt = slice((s + 1) * INPROJ_SUB, (s + 2) * INPROJ_SUB)
            xbs[s + 1] = _mod_norm(x_ref[nxt, :], mp).astype(BF16)
        ss_q = _group_sumsq(aq, bd_ref[...])
        ak = proj(OFF_AK, ATT_KV_W)
        v = proj(OFF_AV, ATT_KV_W)
        ret_ref[rows, OFF_RK:OFF_RK + RET_W] = (proj(OFF_RK, RET_W) * (RET_DK ** -0.5)).astype(BF16)

        q = _group_rms(aq, ss_q, qg_ref[...])
        if use_rope:
            q = _rope(q, cos_ref[rows, :], sin_ref[rows, :])
        q_ref[rows, :] = (q * Q_SCALE).astype(BF16)
        ss_k = _group_sumsq(ak, bd_ref[0:ATT_KV_W, 0:ATT_KV_W])
        ret_ref[rows, OFF_RV:OFF_RV + RET_W] = proj(OFF_RV, RET_W).astype(BF16)

        k = _group_rms(ak, ss_k, kg_ref[...])
        if emit_kv:
            kout_ref[rows, :] = k
            vout_ref[rows, :] = v
        if use_rope:
            k = _rope(k, cos_ref[rows, :], sin_ref[rows, :])
        kt_ref[rows, :] = _tile_kv(k).astype(BF16)
        vT_ref[:, rows] = v.T.astype(BF16)

        ret_ref[rows, OFF_RG:OFF_RG + RET_W] = _silu(proj(OFF_RG, RET_W)).astype(BF16)
        cb_ref[rows, :] = proj(OFF_CB, CONV_DIM).astype(BF16)
        u_ref[rows, :] = (proj(OFF_CC, CONV_DIM) * proj(OFF_CX, CONV_DIM)).astype(BF16)


def _inproj(x, mp, rows_per_mod, w_in, l, q_gain_t, k_gain_t, bd, rope, seq_len, emit_kv):
    n = x.shape[0]
    tm = INPROJ_TM
    use_rope = rope is not None
    row = lambda i: (i, 0)
    in_specs = [
        pl.BlockSpec((tm, D_MODEL), row),
        pl.BlockSpec((None, 8, D_MODEL), lambda i: ((i * tm) // rows_per_mod, 0, 0)),
        _resident((None, D_MODEL, OFF_GATES), lambda i: (l, 0, 0)),
        pl.BlockSpec((None, 1, ATT_Q_W), lambda i: (l, 0, 0)),
        pl.BlockSpec((None, 1, ATT_KV_W), lambda i: (l, 0, 0)),
        _resident((ATT_Q_W, ATT_Q_W), lambda i: (0, 0)),
    ]
    args = [x, mp, w_in, q_gain_t, k_gain_t, bd]
    if use_rope:
        tiles_per_seq = seq_len // tm
        in_specs += [pl.BlockSpec((tm, LANES), lambda i: (i % tiles_per_seq, 0))] * 2
        args += list(rope)
    widths = [4 * RET_W, ATT_Q_W, ATT_Q_W, None, CONV_DIM, CONV_DIM]
    out_specs = [pl.BlockSpec((tm, w), row) for w in widths if w]
    out_shape = [jax.ShapeDtypeStruct((n, w), BF16) for w in widths if w]
    out_specs.insert(3, pl.BlockSpec((ATT_KV_W, tm), lambda i: (0, i)))
    out_shape.insert(3, jax.ShapeDtypeStruct((ATT_KV_W, n), BF16))
    if emit_kv:
        out_specs += [pl.BlockSpec((tm, ATT_KV_W), row)] * 2
        out_shape += [jax.ShapeDtypeStruct((n, ATT_KV_W), F32)] * 2
    return pl.pallas_call(
        functools.partial(_inproj_kernel, use_rope=use_rope, emit_kv=emit_kv),
        grid=(n // tm,),
        in_specs=in_specs,
        out_specs=out_specs,
        out_shape=out_shape,
        compiler_params=pltpu.CompilerParams(
            dimension_semantics=("arbitrary",), vmem_limit_bytes=VMEM_LIMIT),
        name="inproj",
    )(*args)


RET_CHUNK_UNROLL = 8
RET_SEQ_UNROLL = 4
TAB_DMASK, TAB_DQF, TAB_DQB, TAB_DKF, TAB_DKB, TAB_DCF, TAB_DCB, N_TAB = range(8)


def _log_sigmoid(x):
    return jnp.minimum(x, 0.0) - jnp.log(1.0 + jnp.exp(-jnp.abs(x)))


def _chunk_rows(start):
    if isinstance(start, int):
        return pl.ds(start, RET_CHUNK)
    return pl.ds(pl.multiple_of(start, RET_CHUNK), RET_CHUNK)


def _retention_kernel(*refs, n_seq, n_chunks, has_s0, emit_state):
    dl_ref, q_ref, k_ref, v_ref, g_ref, gn_ref = refs[:6]
    pos = 6
    if has_s0:
        s0_ref = refs[pos]
        pos += 1
    o_ref = refs[pos]
    pos += 1
    if emit_state:
        st_ref = refs[pos]
        pos += 1
    tab_ref, kv_ref, ent_ref = refs[pos:pos + 3]

    C = RET_CHUNK
    hd = pl.program_id(1)
    row = lax.broadcasted_iota(jnp.int32, (C, C), 0).astype(F32)
    col = lax.broadcasted_iota(jnp.int32, (C, C), 1).astype(F32)
    diff = row - col
    lgf = _log_sigmoid(jnp.full((C, C), dl_ref[0, hd], F32))
    lgb = _log_sigmoid(jnp.full((C, C), dl_ref[1, hd], F32))
    tab_ref[TAB_DMASK] = (jnp.where(diff >= 0, jnp.exp(lgf * jnp.maximum(diff, 0.0)), 0.0)
                          + jnp.where(diff <= 0, jnp.exp(lgb * jnp.maximum(-diff, 0.0)), 0.0))
    tab_ref[TAB_DQF] = jnp.exp(lgf * (row + 1.0))
    tab_ref[TAB_DQB] = jnp.exp(lgb * (C - row))
    tab_ref[TAB_DKF] = jnp.exp(lgf * (C - 1.0 - row))
    tab_ref[TAB_DKB] = jnp.exp(lgb * row)
    tab_ref[TAB_DCF] = jnp.exp(lgf * C)
    tab_ref[TAB_DCB] = jnp.exp(lgb * C)

    def kv_phase(items):
        ops = []
        for r, _ in items:
            rows = _chunk_rows(r)
            k = k_ref[rows, :].astype(F32)
            vT = v_ref[rows, :].astype(F32).T.astype(BF16)
            kk = jnp.concatenate([(k * tab_ref[TAB_DKF]).astype(BF16),
                                  (k * tab_ref[TAB_DKB]).astype(BF16)], axis=1)
            ops.append((vT, kk))
        for (_, i), (vT, kk) in zip(items, ops):
            kv_ref[i] = _dot(vT, kk)

    def scan_phase(s, sc):
        if has_s0:
            st0 = (s0_ref[0].T, s0_ref[1].T)
        else:
            st0 = (jnp.zeros((RET_DV, RET_DK), F32), jnp.zeros((RET_DV, RET_DK), F32))

        def scan_step(t, st):
            st_f, st_b = st
            cb = n_chunks - 1 - t
            ent_ref[sc + t, :, 0:RET_DK] = st_f.astype(BF16)
            ent_ref[sc + cb, :, RET_DK:2 * RET_DK] = st_b.astype(BF16)
            st_f = st_f * tab_ref[TAB_DCF] + kv_ref[sc + t, :, 0:RET_DK]
            st_b = st_b * tab_ref[TAB_DCB] + kv_ref[sc + cb, :, RET_DK:2 * RET_DK]
            return st_f, st_b

        if n_chunks <= RET_CHUNK_UNROLL:
            st = st0
            for t in range(n_chunks):
                st = scan_step(t, st)
        else:
            st = lax.fori_loop(0, n_chunks, scan_step, st0)
        if emit_state:
            st_ref[s, 0] = st[0].T
            st_ref[s, 1] = st[1].T

    def out_phase(items):
        qs = [q_ref[_chunk_rows(r), :] for r, _ in items]
        att = [_dot_nt(q, k_ref[_chunk_rows(r), :]) for q, (r, _) in zip(qs, items)]
        att = [(a * tab_ref[TAB_DMASK]).astype(BF16) for a in att]
        qq = []
        for q in qs:
            qf = q.astype(F32)
            qq.append(jnp.concatenate([(qf * tab_ref[TAB_DQF]).astype(BF16),
                                       (qf * tab_ref[TAB_DQB]).astype(BF16)], axis=1))
        outs = [_dot(a, v_ref[_chunk_rows(r), :]) + _dot_nt(x, ent_ref[i])
                for a, x, (r, i) in zip(att, qq, items)]
        for o, (r, _) in zip(outs, items):
            rows = _chunk_rows(r)
            ms = jnp.mean(o * o, axis=-1, keepdims=True)
            on = (o * lax.rsqrt(ms + EPS)) * gn_ref[...]
            o_ref[rows, :] = (g_ref[rows, :].astype(F32) * on).astype(BF16)

    if n_seq == 1:
        assert n_chunks % RET_CHUNK_UNROLL == 0

        def chunk_group(phase):
            def step(t, carry):
                c0 = t * RET_CHUNK_UNROLL
                phase([((c0 + j) * C, c0 + j) for j in range(RET_CHUNK_UNROLL)])
                return carry
            lax.fori_loop(0, n_chunks // RET_CHUNK_UNROLL, step, 0)

        chunk_group(kv_phase)
        scan_phase(0, 0)
        chunk_group(out_phase)
    else:
        assert n_seq % RET_SEQ_UNROLL == 0 and n_chunks <= RET_CHUNK_UNROLL

        def seq_group(t, carry):
            s0 = t * RET_SEQ_UNROLL
            items = [((s0 + j) * (n_chunks * C) + c * C, j * n_chunks + c)
                     for j in range(RET_SEQ_UNROLL) for c in range(n_chunks)]
            kv_phase(items)
            for j in range(RET_SEQ_UNROLL):
                scan_phase(s0 + j, j * n_chunks)
            out_phase(items)
            return carry

        lax.fori_loop(0, n_seq // RET_SEQ_UNROLL, seq_group, 0)


def _retention(ret_in, decay_logit_l, gn_l, s0, seq_len, n_seq_blk, emit_state):
    n = ret_in.shape[0]
    n_chunks = seq_len // RET_CHUNK
    tb = n_seq_blk * seq_len
    has_s0 = s0 is not None
    n_off = RET_W // RET_DK

    def branch(k):
        return pl.BlockSpec((tb, RET_DK), lambda i, h: (i, k * n_off + h))

    in_specs = [pl.BlockSpec(memory_space=pltpu.SMEM), branch(0), branch(1), branch(2), branch(3),
                pl.BlockSpec((1, RET_DV), lambda i, h: (0, h))]
    args = [decay_logit_l, ret_in, ret_in, ret_in, ret_in, gn_l]
    if has_s0:
        assert n_seq_blk == 1
        in_specs.append(pl.BlockSpec((None, 2, None, RET_DK, RET_DV), lambda i, h: (i, 0, h, 0, 0)))
        args.append(s0)
    out_specs = [pl.BlockSpec((tb, RET_DV), lambda i, h: (i, h))]
    out_shape = [jax.ShapeDtypeStruct((n, RET_W), BF16)]
    if emit_state:
        out_specs.append(pl.BlockSpec((n_seq_blk, 2, None, RET_DK, RET_DV), lambda i, h: (i, 0, h, 0, 0)))
        out_shape.append(jax.ShapeDtypeStruct((n // seq_len, 2, N_RET_HEADS, RET_DK, RET_DV), F32))
    n_slots = n_chunks * (1 if n_seq_blk == 1 else RET_SEQ_UNROLL)
    return pl.pallas_call(
        functools.partial(_retention_kernel, n_seq=n_seq_blk, n_chunks=n_chunks, has_s0=has_s0,
                          emit_state=emit_state),
        grid=(n // tb, N_RET_HEADS),
        in_specs=in_specs,
        out_specs=out_specs,
        out_shape=out_shape,
        scratch_shapes=[
            pltpu.VMEM((N_TAB, RET_CHUNK, RET_CHUNK), F32),
            pltpu.VMEM((n_slots, RET_DV, 2 * RET_DK), F32),
            pltpu.VMEM((n_slots, RET_DV, 2 * RET_DK), BF16),
        ],
        compiler_params=pltpu.CompilerParams(
            dimension_semantics=("arbitrary", "arbitrary"), vmem_limit_bytes=VMEM_LIMIT),
        name="retention",
    )(*args)


ATT_TQ = 1024
ATT_KB = 512
ATT_SHORT_MAX = 512
ATT_RC = 16
NEG_BIG = -1e30
SUM_ROWS = BF16_SUBLANES


def _attention_kernel(*refs, n_cache, n_new):
    q_ref = refs[0]
    pos = 1
    if n_cache:
        kc_ref, vc_ref = refs[pos:pos + 2]
        pos += 2
    kn_ref, vn_ref, o_ref, qs_ref, acc_ref = refs[pos:pos + 5]
    s_slots = refs[pos + 5:pos + 7]
    p_slots = refs[pos + 7:pos + 9]

    tq = q_ref.shape[0]
    nq = GROUP * tq
    qf = q_ref[...].astype(F32)
    head = lax.broadcasted_iota(jnp.int32, (tq, KV_LANES), 1) // HEAD_DIM
    for g in range(GROUP):
        qs_ref[:, g * tq:(g + 1) * tq] = jnp.where(head == g, qf, 0.0).T.astype(BF16)
    acc_ref[...] = jnp.zeros(acc_ref.shape, F32)

    kb = min(ATT_KB, n_new)
    n_cb = 1 if n_cache else 0
    n_nb = n_new // kb
    assert n_new % kb == 0
    n_blk = n_cb + n_nb

    def block(t):
        if isinstance(t, int) and t < n_cb:
            return kc_ref, vc_ref, pl.ds(0, n_cache)
        lo = (t - n_cb) * kb
        return kn_ref, vn_ref, (pl.ds(lo, kb) if isinstance(lo, int) else pl.ds(pl.multiple_of(lo, kb), kb))

    def scores(t, par):
        k_ref, _, keys = block(t)
        s = _dot(k_ref[keys, :], qs_ref[...])
        s_slots[par][0:keys.size, :] = s
        return jnp.max(s, axis=0, keepdims=True)

    def softmax(n_keys, par, m, s_max):
        s_ref, p_ref = s_slots[par], p_slots[par]
        m_new = jnp.maximum(m, s_max)
        m_rows = jnp.broadcast_to(m_new, (ATT_RC, nq))
        for r in range(0, n_keys, ATT_RC):
            p_ref[r:r + ATT_RC, :] = jnp.exp2(s_ref[r:r + ATT_RC, :] - m_rows).astype(BF16)
        return m_new, jnp.exp2(m - m_new)

    def pv(t, par, alpha):
        _, vT_ref, keys = block(t)
        v1 = jnp.concatenate([vT_ref[:, keys], jnp.ones((SUM_ROWS, keys.size), BF16)], axis=0)
        acc_ref[...] = alpha * acc_ref[...] + _dot(v1, p_slots[par][0:keys.size, :])

    def stage(t, par, carry):
        m, alpha, s_max = carry
        next_max = s_max
        if not isinstance(t, int) or t + 1 < n_blk:
            next_max = scores(t + 1, 1 - par)
        if not isinstance(t, int) or t >= 1:
            pv(t - 1, 1 - par, alpha)
        return softmax(block(t)[2].size, par, m, s_max) + (next_max,)

    carry = (jnp.full((1, nq), NEG_BIG, F32), jnp.zeros((1, nq), F32), scores(0, 0))
    lo_t = min(n_cb + 1, n_blk)
    n_loop = max(n_blk - 1 - lo_t, 0)
    if n_loop % 2:
        lo_t += 1
        n_loop -= 1
    for t in range(lo_t):
        carry = stage(t, t % 2, carry)

    def pair(i, carry):
        t = lo_t + 2 * i
        carry = stage(t, lo_t % 2, carry)
        return stage(t + 1, (lo_t + 1) % 2, carry)

    carry = lax.fori_loop(0, n_loop // 2, pair, carry)
    for t in range(lo_t + n_loop, n_blk):
        carry = stage(t, t % 2, carry)
    pv(n_blk - 1, (n_blk - 1) % 2, carry[1])

    acc = acc_ref[...]
    oT = acc[0:HEAD_DIM] / acc[HEAD_DIM:HEAD_DIM + 1]
    o4 = jnp.concatenate([oT[:, g * tq:(g + 1) * tq] for g in range(GROUP)], axis=0)
    o_ref[...] = o4.T.astype(BF16)


ATT_SEQ_PER_STEP = 4


def _attention_short_kernel(q_ref, k_ref, vT_ref, o_ref, *, n_seq):
    t_len = q_ref.shape[0] // n_seq
    head = lax.broadcasted_iota(jnp.int32, (t_len, KV_LANES), 1) // HEAD_DIM
    ones = jnp.ones((SUM_ROWS, t_len), BF16)
    for s in range(n_seq):
        rows = slice(s * t_len, (s + 1) * t_len)
        for kv in range(N_KV_HEADS):
            cols = slice(kv * KV_LANES, (kv + 1) * KV_LANES)
            qf = q_ref[rows, cols].astype(F32)
            qsT = jnp.concatenate([jnp.where(head == g, qf, 0.0).T.astype(BF16) for g in range(GROUP)],
                                  axis=1)
            sT = _dot(k_ref[rows, cols], qsT)
            pT = jnp.exp2(sT - jnp.max(sT, axis=0, keepdims=True)).astype(BF16)
            v1 = jnp.concatenate([vT_ref[kv * HEAD_DIM:(kv + 1) * HEAD_DIM, rows], ones], axis=0)
            acc = _dot(v1, pT)
            oT = acc[0:HEAD_DIM] / acc[HEAD_DIM:HEAD_DIM + 1]
            o4 = jnp.concatenate([oT[:, g * t_len:(g + 1) * t_len] for g in range(GROUP)], axis=0)
            o_ref[rows, cols] = o4.T.astype(BF16)


def _attention_short(q, kt, vT, seq_len):
    n = q.shape[0]
    tb = ATT_SEQ_PER_STEP * seq_len
    return pl.pallas_call(
        functools.partial(_attention_short_kernel, n_seq=ATT_SEQ_PER_STEP),
        grid=(n // tb,),
        in_specs=[pl.BlockSpec((tb, ATT_Q_W), lambda i: (i, 0)),
                  pl.BlockSpec((tb, ATT_Q_W), lambda i: (i, 0)),
                  pl.BlockSpec((ATT_KV_W, tb), lambda i: (0, i))],
        out_specs=pl.BlockSpec((tb, ATT_Q_W), lambda i: (i, 0)),
        out_shape=jax.ShapeDtypeStruct((n, ATT_Q_W), BF16),
        compiler_params=pltpu.CompilerParams(
            dimension_semantics=("arbitrary",), vmem_limit_bytes=VMEM_LIMIT),
        name="attention_short",
    )(q, kt, vT)


def _attention(q, kt, vT, cache, seq_len):
    n = q.shape[0]
    n_batch = n // seq_len
    tq = ATT_TQ
    nq = seq_len // tq
    n_cache = 0 if cache is None else cache[0].shape[1]
    if n_cache == 0 and seq_len <= ATT_SHORT_MAX:
        return _attention_short(q, kt, vT, seq_len)
    blk_rows = max(min(ATT_KB, seq_len), n_cache)
    in_specs = [pl.BlockSpec((tq, KV_LANES), lambda b, kv, i: (b * nq + i, kv))]
    args = [q]
    if n_cache:
        in_specs += [pl.BlockSpec((None, n_cache, KV_LANES), lambda b, kv, i: (b, 0, kv)),
                     pl.BlockSpec((None, HEAD_DIM, n_cache), lambda b, kv, i: (b, kv, 0))]
        args += list(cache)
    in_specs += [pl.BlockSpec((seq_len, KV_LANES), lambda b, kv, i: (b, kv)),
                 pl.BlockSpec((HEAD_DIM, seq_len), lambda b, kv, i: (kv, b))]
    args += [kt, vT]
    return pl.pallas_call(
        functools.partial(_attention_kernel, n_cache=n_cache, n_new=seq_len),
        grid=(n_batch, N_KV_HEADS, nq),
        in_specs=in_specs,
        out_specs=pl.BlockSpec((tq, KV_LANES), lambda b, kv, i: (b * nq + i, kv)),
        out_shape=jax.ShapeDtypeStruct((n, ATT_Q_W), BF16),
        scratch_shapes=[
            pltpu.VMEM((KV_LANES, GROUP * tq), BF16),
            pltpu.VMEM((HEAD_DIM + SUM_ROWS, GROUP * tq), F32),
            pltpu.VMEM((blk_rows, GROUP * tq), F32),
            pltpu.VMEM((blk_rows, GROUP * tq), F32),
            pltpu.VMEM((blk_rows, GROUP * tq), BF16),
            pltpu.VMEM((blk_rows, GROUP * tq), BF16),
        ],
        compiler_params=pltpu.CompilerParams(
            dimension_semantics=("arbitrary", "arbitrary", "arbitrary"), vmem_limit_bytes=VMEM_LIMIT),
        name="attention",
    )(*args)


MERGE_TM = 1024
MERGE_SUB = 256


def _merge_kernel(h_ref, mp_ref, ro_ref, ao_ref, cb_ref, u_ref, up_ref, un_ref, cw_ref,
                  wg_ref, wr_ref, wa_ref, wc_ref, wo_ref, o_ref, *, seq_len):
    tm = h_ref.shape[0]
    sub = MERGE_SUB
    n_sub = tm // sub
    i = pl.program_id(0)
    cw = cw_ref[...]
    r = lax.broadcasted_iota(jnp.int32, (sub, CONV_DIM), 0)
    edge = BF16_SUBLANES

    def branches(s):
        r0 = s * sub
        rows = slice(r0, r0 + sub)
        yr = _dot(ro_ref[rows, :], wr_ref[...])
        ya = _dot(ao_ref[rows, :], wa_ref[...])
        u = u_ref[rows, :].astype(F32)
        before = up_ref if s == 0 else u_ref.at[r0 - edge:r0, :]
        after = un_ref if s == n_sub - 1 else u_ref.at[r0 + sub:r0 + sub + edge, :]
        prev_row = before[edge - 1:edge, :].astype(F32)
        next_row = after[0:1, :].astype(F32)
        t = (i * tm + r0 + r) % seq_len
        u_prev = jnp.where(r == 0, prev_row, pltpu.roll(u, 1, axis=0))
        u_next = jnp.where(r == sub - 1, next_row, pltpu.roll(u, sub - 1, axis=0))
        u_prev = jnp.where(t == 0, 0.0, u_prev)
        u_next = jnp.where(t == seq_len - 1, 0.0, u_next)
        conv = u_prev * cw[0:1] + u * cw[1:2] + u_next * cw[2:3]
        yc = _dot((cb_ref[rows, :].astype(F32) * conv).astype(BF16), wc_ref[...])
        xb = _mod_norm(h_ref[rows, :], mp_ref[...]).astype(BF16)
        merged = _sigmoid(_dot(xb, wg_ref[:, 0:D_MODEL])) * yr
        merged = merged + _sigmoid(_dot(xb, wg_ref[:, D_MODEL:2 * D_MODEL])) * ya
        merged = merged + _sigmoid(_dot(xb, wg_ref[:, 2 * D_MODEL:3 * D_MODEL])) * yc
        return merged.astype(BF16)

    def project(s, merged):
        rows = slice(s * sub, (s + 1) * sub)
        o_ref[rows, :] = h_ref[rows, :] + mp_ref[2:3, :] * _dot(merged, wo_ref[...])

    pending = branches(0)
    for s in range(1, n_sub):
        nxt = branches(s)
        project(s - 1, pending)
        pending = nxt
    project(n_sub - 1, pending)


def _merge(h, mp, rows_per_mod, ret_o, att_o, cb, u, conv_w8, w_gates, w_ret_o, w_att_o, w_conv_o, w_o,
           l, seq_len):
    n = h.shape[0]
    tm = MERGE_TM
    hb = tm // BF16_SUBLANES
    last = n // BF16_SUBLANES - 1
    row = lambda i: (i, 0)
    wspec = lambda k: _resident((None, k, D_MODEL), lambda i: (l, 0, 0))
    return pl.pallas_call(
        functools.partial(_merge_kernel, seq_len=seq_len),
        grid=(n // tm,),
        in_specs=[
            pl.BlockSpec((tm, D_MODEL), row),
            pl.BlockSpec((None, 8, D_MODEL), lambda i: ((i * tm) // rows_per_mod, 0, 0)),
            pl.BlockSpec((tm, RET_W), row),
            pl.BlockSpec((tm, ATT_Q_W), row),
            pl.BlockSpec((tm, CONV_DIM), row),
            pl.BlockSpec((tm, CONV_DIM), row),
            pl.BlockSpec((BF16_SUBLANES, CONV_DIM), lambda i: (jnp.maximum(i * hb - 1, 0), 0)),
            pl.BlockSpec((BF16_SUBLANES, CONV_DIM), lambda i: (jnp.minimum((i + 1) * hb, last), 0)),
            pl.BlockSpec((None, 8, CONV_DIM), lambda i: (l, 0, 0)),
            _resident((None, D_MODEL, 3 * D_MODEL), lambda i: (l, 0, 0)),
            wspec(RET_W), wspec(ATT_Q_W), wspec(CONV_DIM), wspec(D_MODEL),
        ],
        out_specs=pl.BlockSpec((tm, D_MODEL), row),
        out_shape=jax.ShapeDtypeStruct((n, D_MODEL), F32),
        compiler_params=pltpu.CompilerParams(
            dimension_semantics=("arbitrary",), vmem_limit_bytes=VMEM_LIMIT),
        name="merge",
    )(h, mp, ret_o, att_o, cb, u, u, u, conv_w8, w_gates, w_ret_o, w_att_o, w_conv_o, w_o)


def _rope_tables(n_tok):
    rows = n_tok // GRID_W
    t_row = np.repeat(np.arange(rows, dtype=np.float64), GRID_W)
    t_col = np.tile(np.arange(GRID_W, dtype=np.float64), rows)
    n_freq = HEAD_DIM // 4
    inv = ROPE_THETA ** (-np.arange(n_freq, dtype=np.float64) / n_freq)
    ang = np.concatenate([t_row[:, None] * inv, t_col[:, None] * inv], axis=-1)
    cos, sin = np.cos(ang), np.sin(ang)
    cos64 = np.concatenate([cos, cos], axis=-1)
    sin64 = np.concatenate([-sin, sin], axis=-1)
    return (jnp.asarray(np.tile(cos64, (1, LANES // HEAD_DIM)), F32),
            jnp.asarray(np.tile(sin64, (1, LANES // HEAD_DIM)), F32))


def _mod_pack(mod_l, rows, sub, norm_w_row):
    r0 = rows[0]
    nr = len(rows)
    m = mod_l[r0:r0 + nr, 3 * sub * D_MODEL:3 * (sub + 1) * D_MODEL].reshape(nr, 3, D_MODEL)
    nw = jnp.broadcast_to(norm_w_row[None, None, :], (nr, 1, D_MODEL))
    pad = jnp.zeros((nr, 4, D_MODEL), F32)
    return jnp.concatenate([m, nw, pad], axis=1)


def kernel(x_prompt, x_sample, c, state_ret, cache_k, cache_v, c_ctx, w_ada, b_ada, norm_w, w_ffn_in,
           w_ffn_out, w_in, ret_decay_logit, ret_gn, q_gain, k_gain, conv_w, w_ret_o, w_att_o, w_conv_o,
           w_o):
    n_ctx_b, ctx_len, _ = x_prompt.shape
    n_lat_b, lat_len, _ = x_sample.shape

    w_ffn_in_b = w_ffn_in.astype(BF16)
    w_ffn_out_b = w_ffn_out.astype(BF16)
    w_in_b = w_in[:, :, :OFF_GATES].astype(BF16)
    w_gates_b = w_in[:, :, OFF_GATES:].astype(BF16)
    w_ret_o_b = w_ret_o.astype(BF16)
    w_att_o_b = w_att_o.astype(BF16)
    w_conv_o_b = w_conv_o.astype(BF16)
    w_o_b = w_o.astype(BF16)

    q_gain_t = jnp.tile(q_gain, (1, N_HEADS)).reshape(DEPTH, 1, ATT_Q_W)
    k_gain_t = jnp.tile(k_gain, (1, N_KV_HEADS)).reshape(DEPTH, 1, ATT_KV_W)
    conv_w8 = jnp.pad(conv_w, ((0, 0), (0, 8 - conv_w.shape[1]), (0, 0)))
    gid = np.arange(ATT_Q_W) // HEAD_DIM
    bd = jnp.asarray(gid[:, None] == gid[None, :], BF16)
    rope = _rope_tables(lat_len)

    cond8 = jnp.zeros((8, D_MODEL), F32).at[0].set(c_ctx).at[1:1 + n_lat_b].set(c)
    mod = _ada(cond8, w_ada, b_ada)

    ck = cache_k.astype(BF16)
    ck = jnp.broadcast_to(ck[:, :, :, :, None, :], ck.shape[:4] + (GROUP, HEAD_DIM))
    cache_kt = ck.reshape(ck.shape[0], ck.shape[1], ck.shape[2], N_KV_HEADS * KV_LANES)
    cache_vT = jnp.transpose(cache_v.astype(BF16), (0, 1, 3, 4, 2)).reshape(
        cache_v.shape[0], cache_v.shape[1], ATT_KV_W, cache_v.shape[2])

    groups = (
        dict(x=x_prompt.reshape(n_ctx_b * ctx_len, D_MODEL), rows=[0], seq=ctx_len, ctx=True),
        dict(x=x_sample.reshape(n_lat_b * lat_len, D_MODEL), rows=list(range(1, 1 + n_lat_b)),
             seq=lat_len, ctx=False),
    )
    results = []
    for grp in groups:
        h = grp["x"]
        seq = grp["seq"]
        is_ctx = grp["ctx"]
        rpm = h.shape[0] // len(grp["rows"])
        states, keys, values = [], [], []
        for l in range(DEPTH):
            mp = [_mod_pack(mod[l], grp["rows"], s, norm_w[l, s]) for s in range(3)]
            h = _ffn(h, mp[0], rpm, w_ffn_in_b, w_ffn_out_b, l, 0)
            outs = _inproj(h, mp[1], rpm, w_in_b, l, q_gain_t, k_gain_t, bd,
                           None if is_ctx else rope, seq, emit_kv=is_ctx)
            ret_in, qn, kt, vT, cb, u = outs[:6]
            if is_ctx:
                ret_o, st = _retention(ret_in, ret_decay_logit[l], ret_gn[l][None, :], None, seq,
                                       n_seq_blk=8, emit_state=True)
                states.append(st)
                keys.append(outs[6].reshape(n_ctx_b, seq, N_KV_HEADS, HEAD_DIM))
                values.append(outs[7].reshape(n_ctx_b, seq, N_KV_HEADS, HEAD_DIM))
                att_o = _attention(qn, kt, vT, None, seq)
            else:
                (ret_o,) = _retention(ret_in, ret_decay_logit[l], ret_gn[l][None, :], state_ret[:, l],
                                      seq, n_seq_blk=1, emit_state=False)
                att_o = _attention(qn, kt, vT, (cache_kt[:, l], cache_vT[:, l]), seq)
            h = _merge(h, mp[1], rpm, ret_o, att_o, cb, u, conv_w8, w_gates_b, w_ret_o_b, w_att_o_b,
                       w_conv_o_b, w_o_b, l, seq)
            h = _ffn(h, mp[2], rpm, w_ffn_in_b, w_ffn_out_b, l, 1)
        results.append((h, states, keys, values))

    (y_ctx, states, keys, values), (y_lat, _, _, _) = results
    y_prompt = y_ctx.reshape(x_prompt.shape)
    y_sample = y_lat.reshape(x_sample.shape)
    new_state_ret = jnp.stack(states, axis=1)
    new_cache_k = jnp.stack(keys, axis=1)
    new_cache_v = jnp.stack(values, axis=1)
    return (y_prompt, y_sample, new_state_ret, new_cache_k, new_cache_v)
```

```python
import functools
import math

import numpy as np
import jax
import jax.numpy as jnp
from jax import lax
from jax.experimental import pallas as pl
from jax.experimental.pallas import tpu as pltpu

D_MODEL = 1024
DEPTH = 2
GRID_W = 64
N_RET_HEADS = 4
RET_DK = 128
RET_DV = 128
RET_CHUNK = 128
N_HEADS = 8
N_KV_HEADS = 2
HEAD_DIM = 64
ROPE_THETA = 10000.0
CONV_DIM = 512
FFN_DIM = 2816
N_MOD = 9
EPS = 1e-6
RET_W = N_RET_HEADS * RET_DK
ATT_Q_W = N_HEADS * HEAD_DIM
ATT_KV_W = N_KV_HEADS * HEAD_DIM
GROUP = N_HEADS // N_KV_HEADS
KV_LANES = GROUP * HEAD_DIM
OFF_RQ, OFF_RK, OFF_RV, OFF_RG = 0, 512, 1024, 1536
OFF_AQ, OFF_AK, OFF_AV = 2048, 2560, 2688
OFF_CB, OFF_CC, OFF_CX = 2816, 3328, 3840
OFF_GATES = 4352

LANES = 128
BF16_SUBLANES = 16
VMEM_LIMIT = 56 * 1024 * 1024

F32 = jnp.float32
BF16 = jnp.bfloat16


def _dot(a, b):
    return jnp.dot(a, b, preferred_element_type=F32)


def _dot_nt(a, b):
    return lax.dot_general(a, b, (((1,), (1,)), ((), ())), preferred_element_type=F32)


def _sigmoid(x):
    return 1.0 / (1.0 + jnp.exp(-x))


def _silu(x):
    return x * _sigmoid(x)


def _mod_norm(x, mp):
    ms = jnp.mean(x * x, axis=-1, keepdims=True)
    y = x * lax.rsqrt(ms + EPS)
    return (y * mp[3:4]) * (1.0 + mp[1:2]) + mp[0:1]


def _resident(shape, index):
    return pl.BlockSpec(shape, index, pipeline_mode=pl.Buffered(1))


ADA_TN = 2304


def _ada_kernel(c_ref, w_ref, b_ref, o_ref):
    a = _silu(c_ref[...]).astype(BF16)
    o_ref[...] = _dot(a, w_ref[...].astype(BF16)) + b_ref[...]


def _ada(cond8, w_ada, b_ada):
    n = N_MOD * D_MODEL
    return pl.pallas_call(
        _ada_kernel,
        grid=(DEPTH, n // ADA_TN),
        in_specs=[
            pl.BlockSpec((8, D_MODEL), lambda l, j: (0, 0)),
            pl.BlockSpec((None, D_MODEL, ADA_TN), lambda l, j: (l, 0, j)),
            pl.BlockSpec((None, 1, ADA_TN), lambda l, j: (l, 0, j)),
        ],
        out_specs=pl.BlockSpec((None, 8, ADA_TN), lambda l, j: (l, 0, j)),
        out_shape=jax.ShapeDtypeStruct((DEPTH, 8, n), F32),
        compiler_params=pltpu.CompilerParams(
            dimension_semantics=("arbitrary", "arbitrary"), vmem_limit_bytes=VMEM_LIMIT),
        name="ada",
    )(cond8, w_ada, b_ada.reshape(DEPTH, 1, n))


FFN_TM = 1024
FFN_FC = 256


def _ffn_kernel(x_ref, mp_ref, wi_ref, wo_ref, o_ref, h_ref):
    x = x_ref[...]
    mp = mp_ref[...]
    xb = _mod_norm(x, mp).astype(BF16)
    for c in range(FFN_DIM // FFN_FC):
        lo = c * FFN_FC
        g = _dot(xb, wi_ref[:, lo:lo + FFN_FC])
        u = _dot(xb, wi_ref[:, FFN_DIM + lo:FFN_DIM + lo + FFN_FC])
        h_ref[:, lo:lo + FFN_FC] = (_silu(g) * u).astype(BF16)
    y = _dot(h_ref[...], wo_ref[...])
    o_ref[...] = x + (0.5 * mp[2:3]) * y


def _ffn(x, mp, rows_per_mod, w_in, w_out, l, j):
    n = x.shape[0]
    tm = FFN_TM
    return pl.pallas_call(
        _ffn_kernel,
        grid=(n // tm,),
        in_specs=[
            pl.BlockSpec((tm, D_MODEL), lambda i: (i, 0)),
            pl.BlockSpec((None, 8, D_MODEL), lambda i: ((i * tm) // rows_per_mod, 0, 0)),
            _resident((None, None, D_MODEL, 2 * FFN_DIM), lambda i: (l, j, 0, 0)),
            _resident((None, None, FFN_DIM, D_MODEL), lambda i: (l, j, 0, 0)),
        ],
        out_specs=pl.BlockSpec((tm, D_MODEL), lambda i: (i, 0)),
        out_shape=jax.ShapeDtypeStruct((n, D_MODEL), F32),
        scratch_shapes=[pltpu.VMEM((tm, FFN_DIM), BF16)],
        compiler_params=pltpu.CompilerParams(
            dimension_semantics=("arbitrary",), vmem_limit_bytes=VMEM_LIMIT),
        name="ffn",
    )(x, mp, w_in, w_out)


INPROJ_TM = 1024
INPROJ_SUB = 512
Q_SCALE = HEAD_DIM ** -0.5 * math.log2(math.e)


def _group_sumsq(a, bd):
    sq = a * a
    hi = sq.astype(BF16)
    lo = (sq - hi.astype(F32)).astype(BF16)
    return _dot(hi, bd) + _dot(lo, bd)


def _group_rms(a, sumsq, gain):
    return (a * lax.rsqrt(sumsq * (1.0 / HEAD_DIM) + EPS)) * gain


def _rope(x, cos, sin_signed):
    half = HEAD_DIM // 2
    reps = x.shape[1] // LANES
    lane = lax.broadcasted_iota(jnp.int32, (x.shape[0], LANES), 1)
    first = (lane % HEAD_DIM) < half
    out = []
    for r in range(reps):
        xs = x[:, r * LANES:(r + 1) * LANES]
        partner = jnp.where(first, pltpu.roll(xs, LANES - half, axis=1), pltpu.roll(xs, half, axis=1))
        out.append(xs * cos + partner * sin_signed)
    return out[0] if reps == 1 else jnp.concatenate(out, axis=1)


def _tile_kv(a):
    lane = lax.broadcasted_iota(jnp.int32, a.shape, 1)
    sw = pltpu.roll(a, HEAD_DIM, axis=1)
    h0 = jnp.where(lane < HEAD_DIM, a, sw)
    h1 = jnp.where(lane < HEAD_DIM, sw, a)
    return jnp.concatenate([h0, h0, h1, h1], axis=1)


def _inproj_kernel(*refs, use_rope, emit_kv):
    x_ref, mp_ref, w_ref, qg_ref, kg_ref, bd_ref = refs[:6]
    pos = 6
    if use_rope:
        cos_ref, sin_ref = refs[pos:pos + 2]
        pos += 2
    ret_ref, q_ref, kt_ref, vT_ref, cb_ref, u_ref = refs[pos:pos + 6]
    pos += 6
    if emit_kv:
        kout_ref, vout_ref = refs[pos:pos + 2]

    mp = mp_ref[...]
    tm = x_ref.shape[0]
    n_sub = tm // INPROJ_SUB
    xbs = [None] * n_sub
    xbs[0] = _mod_norm(x_ref[0:INPROJ_SUB, :], mp).astype(BF16)
    for s in range(n_sub):
        rows = slice(s * INPROJ_SUB, (s + 1) * INPROJ_SUB)
        xb = xbs[s]

        def proj(off, width):
            return _dot(xb, w_ref[:, off:off + width])

        aq = proj(OFF_AQ, ATT_Q_W)
        ret_ref[rows, OFF_RQ:OFF_RQ + RET_W] = proj(OFF_RQ, RET_W).astype(BF16)
        if s + 1 < n_sub:
            nxt = slice((s + 1) * INPROJ_SUB, (s + 2) * INPROJ_SUB)
            xbs[s + 1] = _mod_norm(x_ref[nxt, :], mp).astype(BF16)
        ss_q = _group_sumsq(aq, bd_ref[...])
        ak = proj(OFF_AK, ATT_KV_W)
        v = proj(OFF_AV, ATT_KV_W)
        ret_ref[rows, OFF_RK:OFF_RK + RET_W] = (proj(OFF_RK, RET_W) * (RET_DK ** -0.5)).astype(BF16)

        q = _group_rms(aq, ss_q, qg_ref[...])
        if use_rope:
            q = _rope(q, cos_ref[rows, :], sin_ref[rows, :])
        q_ref[rows, :] = (q * Q_SCALE).astype(BF16)
        ss_k = _group_sumsq(ak, bd_ref[0:ATT_KV_W, 0:ATT_KV_W])
        ret_ref[rows, OFF_RV:OFF_RV + RET_W] = proj(OFF_RV, RET_W).astype(BF16)

        k = _group_rms(ak, ss_k, kg_ref[...])
        if emit_kv:
            kout_ref[rows, :] = k
            vout_ref[rows, :] = v
        if use_rope:
            k = _rope(k, cos_ref[rows, :], sin_ref[rows, :])
        kt_ref[rows, :] = _tile_kv(k).astype(BF16)
        vT_ref[:, rows] = v.T.astype(BF16)

        ret_ref[rows, OFF_RG:OFF_RG + RET_W] = _silu(proj(OFF_RG, RET_W)).astype(BF16)
        cb_ref[rows, :] = proj(OFF_CB, CONV_DIM).astype(BF16)
        u_ref[rows, :] = (proj(OFF_CC, CONV_DIM) * proj(OFF_CX, CONV_DIM)).astype(BF16)


def _inproj(x, mp, rows_per_mod, w_in, l, q_gain_t, k_gain_t, bd, rope, seq_len, emit_kv):
    n = x.shape[0]
    tm = INPROJ_TM
    use_rope = rope is not None
    row = lambda i: (i, 0)
    in_specs = [
        pl.BlockSpec((tm, D_MODEL), row),
        pl.BlockSpec((None, 8, D_MODEL), lambda i: ((i * tm) // rows_per_mod, 0, 0)),
        _resident((None, D_MODEL, OFF_GATES), lambda i: (l, 0, 0)),
        pl.BlockSpec((None, 1, ATT_Q_W), lambda i: (l, 0, 0)),
        pl.BlockSpec((None, 1, ATT_KV_W), lambda i: (l, 0, 0)),
        _resident((ATT_Q_W, ATT_Q_W), lambda i: (0, 0)),
    ]
    args = [x, mp, w_in, q_gain_t, k_gain_t, bd]
    if use_rope:
        tiles_per_seq = seq_len // tm
        in_specs += [pl.BlockSpec((tm, LANES), lambda i: (i % tiles_per_seq, 0))] * 2
        args += list(rope)
    widths = [4 * RET_W, ATT_Q_W, ATT_Q_W, None, CONV_DIM, CONV_DIM]
    out_specs = [pl.BlockSpec((tm, w), row) for w in widths if w]
    out_shape = [jax.ShapeDtypeStruct((n, w), BF16) for w in widths if w]
    out_specs.insert(3, pl.BlockSpec((ATT_KV_W, tm), lambda i: (0, i)))
    out_shape.insert(3, jax.ShapeDtypeStruct((ATT_KV_W, n), BF16))
    if emit_kv:
        out_specs += [pl.BlockSpec((tm, ATT_KV_W), row)] * 2
        out_shape += [jax.ShapeDtypeStruct((n, ATT_KV_W), F32)] * 2
    return pl.pallas_call(
        functools.partial(_inproj_kernel, use_rope=use_rope, emit_kv=emit_kv),
        grid=(n // tm,),
        in_specs=in_specs,
        out_specs=out_specs,
        out_shape=out_shape,
        compiler_params=pltpu.CompilerParams(
            dimension_semantics=("arbitrary",), vmem_limit_bytes=VMEM_LIMIT),
        name="inproj",
    )(*args)


RET_GROUP = 8
RET_SCAN_UNROLL = 2
TAB_DMASK, TAB_DQF, TAB_DQB, TAB_DKF, TAB_DKB, TAB_DCF, TAB_DCB, N_TAB = range(8)


def _log_sigmoid(x):
    return jnp.minimum(x, 0.0) - jnp.log(1.0 + jnp.exp(-jnp.abs(x)))


def _chunk_rows(start):
    if isinstance(start, int):
        return pl.ds(start, RET_CHUNK)
    return pl.ds(pl.multiple_of(start, RET_CHUNK), RET_CHUNK)


def _retention_kernel(*refs, n_seq, n_chunks, has_s0, emit_state):
    dl_ref, q_ref, k_ref, v_ref, g_ref, gn_ref = refs[:6]
    pos = 6
    if has_s0:
        s0_ref = refs[pos]
        pos += 1
    o_ref = refs[pos]
    pos += 1
    if emit_state:
        st_ref = refs[pos]
        pos += 1
    tab_ref, kv_ref, ent_ref = refs[pos:pos + 3]

    C = RET_CHUNK
    hd = pl.program_id(1)
    row = lax.broadcasted_iota(jnp.int32, (C, C), 0).astype(F32)
    col = lax.broadcasted_iota(jnp.int32, (C, C), 1).astype(F32)
    diff = row - col
    lgf = _log_sigmoid(jnp.full((C, C), dl_ref[0, hd], F32))
    lgb = _log_sigmoid(jnp.full((C, C), dl_ref[1, hd], F32))
    tab_ref[TAB_DMASK] = (jnp.where(diff >= 0, jnp.exp(lgf * jnp.maximum(diff, 0.0)), 0.0)
                          + jnp.where(diff <= 0, jnp.exp(lgb * jnp.maximum(-diff, 0.0)), 0.0))
    tab_ref[TAB_DQF] = jnp.exp(lgf * (row + 1.0))
    tab_ref[TAB_DQB] = jnp.exp(lgb * (C - row))
    tab_ref[TAB_DKF] = jnp.exp(lgf * (C - 1.0 - row))
    tab_ref[TAB_DKB] = jnp.exp(lgb * row)
    tab_ref[TAB_DCF] = jnp.exp(lgf * C)
    tab_ref[TAB_DCB] = jnp.exp(lgb * C)

    def kv_phase(items):
        ops = []
        for r, _ in items:
            rows = _chunk_rows(r)
            k = k_ref[rows, :].astype(F32)
            vT = v_ref[rows, :].astype(F32).T.astype(BF16)
            kk = jnp.concatenate([(k * tab_ref[TAB_DKF]).astype(BF16),
                                  (k * tab_ref[TAB_DKB]).astype(BF16)], axis=1)
            ops.append((vT, kk))
        for (_, i), (vT, kk) in zip(items, ops):
            kv_ref[i] = _dot(vT, kk)

    def scan_phase(s, sc):
        if has_s0:
            st0 = (s0_ref[0].T, s0_ref[1].T)
        else:
            st0 = (jnp.zeros((RET_DV, RET_DK), F32), jnp.zeros((RET_DV, RET_DK), F32))

        def scan_step(t, st):
            st_f, st_b = st
            cb = n_chunks - 1 - t
            ent_ref[sc + t, :, 0:RET_DK] = st_f.astype(BF16)
            ent_ref[sc + cb, :, RET_DK:2 * RET_DK] = st_b.astype(BF16)
            st_f = st_f * tab_ref[TAB_DCF] + kv_ref[sc + t, :, 0:RET_DK]
            st_b = st_b * tab_ref[TAB_DCB] + kv_ref[sc + cb, :, RET_DK:2 * RET_DK]
            return st_f, st_b

        if n_chunks <= RET_SCAN_UNROLL:
            st = st0
            for t in range(n_chunks):
                st = scan_step(t, st)
        else:
            st = lax.fori_loop(0, n_chunks, scan_step, st0)
        if emit_state:
            st_ref[s, 0] = st[0].T
            st_ref[s, 1] = st[1].T

    def score_stage(items):
        qs = [q_ref[_chunk_rows(r), :] for r, _ in items]
        att = [_dot_nt(q, k_ref[_chunk_rows(r), :]) for q, (r, _) in zip(qs, items)]
        att = [(a * tab_ref[TAB_DMASK]).astype(BF16) for a in att]
        qq = []
        for q in qs:
            qf = q.astype(F32)
            qq.append(jnp.concatenate([(qf * tab_ref[TAB_DQF]).astype(BF16),
                                       (qf * tab_ref[TAB_DQB]).astype(BF16)], axis=1))
        return att, qq

    def product_stage(items, att, qq):
        return [_dot(a, v_ref[_chunk_rows(r), :]) + _dot_nt(x, ent_ref[i])
                for a, x, (r, i) in zip(att, qq, items)]

    def norm_stage(items, outs):
        for o, (r, _) in zip(outs, items):
            rows = _chunk_rows(r)
            ms = jnp.mean(o * o, axis=-1, keepdims=True)
            on = (o * lax.rsqrt(ms + EPS)) * gn_ref[...]
            o_ref[rows, :] = (g_ref[rows, :].astype(F32) * on).astype(BF16)

    items = [(s * n_chunks * C + c * C, s * n_chunks + c) for s in range(n_seq) for c in range(n_chunks)]
    assert len(items) % RET_GROUP == 0
    groups = [items[i:i + RET_GROUP] for i in range(0, len(items), RET_GROUP)]
    for grp in groups:
        kv_phase(grp)
    for s in range(n_seq):
        scan_phase(s, s * n_chunks)
    scored, products = {}, {}
    for t in range(len(groups) + 2):
        if t < len(groups):
            scored[t] = score_stage(groups[t])
        if 0 <= t - 1 < len(groups):
            products[t - 1] = product_stage(groups[t - 1], *scored.pop(t - 1))
        if 0 <= t - 2 < len(groups):
            norm_stage(groups[t - 2], products.pop(t - 2))


def _retention(ret_in, decay_logit_l, gn_l, s0, seq_len, n_seq_blk, emit_state):
    n = ret_in.shape[0]
    n_chunks = seq_len // RET_CHUNK
    tb = n_seq_blk * seq_len
    has_s0 = s0 is not None
    n_off = RET_W // RET_DK

    def branch(k):
        return pl.BlockSpec((tb, RET_DK), lambda i, h: (i, k * n_off + h))

    in_specs = [pl.BlockSpec(memory_space=pltpu.SMEM), branch(0), branch(1), branch(2), branch(3),
                pl.BlockSpec((1, RET_DV), lambda i, h: (0, h))]
    args = [decay_logit_l, ret_in, ret_in, ret_in, ret_in, gn_l]
    if has_s0:
        assert n_seq_blk == 1
        in_specs.append(pl.BlockSpec((None, 2, None, RET_DK, RET_DV), lambda i, h: (i, 0, h, 0, 0)))
        args.append(s0)
    out_specs = [pl.BlockSpec((tb, RET_DV), lambda i, h: (i, h))]
    out_shape = [jax.ShapeDtypeStruct((n, RET_W), BF16)]
    if emit_state:
        out_specs.append(pl.BlockSpec((n_seq_blk, 2, None, RET_DK, RET_DV), lambda i, h: (i, 0, h, 0, 0)))
        out_shape.append(jax.ShapeDtypeStruct((n // seq_len, 2, N_RET_HEADS, RET_DK, RET_DV), F32))
    n_slots = n_chunks * n_seq_blk
    return pl.pallas_call(
        functools.partial(_retention_kernel, n_seq=n_seq_blk, n_chunks=n_chunks, has_s0=has_s0,
                          emit_state=emit_state),
        grid=(n // tb, N_RET_HEADS),
        in_specs=in_specs,
        out_specs=out_specs,
        out_shape=out_shape,
        scratch_shapes=[
            pltpu.VMEM((N_TAB, RET_CHUNK, RET_CHUNK), F32),
            pltpu.VMEM((n_slots, RET_DV, 2 * RET_DK), F32),
            pltpu.VMEM((n_slots, RET_DV, 2 * RET_DK), BF16),
        ],
        compiler_params=pltpu.CompilerParams(
            dimension_semantics=("arbitrary", "arbitrary"), vmem_limit_bytes=VMEM_LIMIT),
        name="retention",
    )(*args)


ATT_TQ = 1024
ATT_KB = 512
ATT_SHORT_MAX = 512
ATT_RC = 16
NEG_BIG = -1e30
SUM_ROWS = BF16_SUBLANES


def _attention_kernel(*refs, n_cache, n_new):
    q_ref = refs[0]
    pos = 1
    if n_cache:
        kc_ref, vc_ref = refs[pos:pos + 2]
        pos += 2
    kn_ref, vn_ref, o_ref, qs_ref, acc_ref = refs[pos:pos + 5]
    s_slots = refs[pos + 5:pos + 7]
    p_slots = refs[pos + 7:pos + 9]

    tq = q_ref.shape[0]
    nq = GROUP * tq
    qf = q_ref[...].astype(F32)
    head = lax.broadcasted_iota(jnp.int32, (tq, KV_LANES), 1) // HEAD_DIM
    for g in range(GROUP):
        qs_ref[:, g * tq:(g + 1) * tq] = jnp.where(head == g, qf, 0.0).T.astype(BF16)
    acc_ref[...] = jnp.zeros(acc_ref.shape, F32)

    kb = min(ATT_KB, n_new)
    n_cb = 1 if n_cache else 0
    n_nb = n_new // kb
    assert n_new % kb == 0
    n_blk = n_cb + n_nb

    def block(t):
        if isinstance(t, int) and t < n_cb:
            return kc_ref, vc_ref, pl.ds(0, n_cache)
        lo = (t - n_cb) * kb
        return kn_ref, vn_ref, (pl.ds(lo, kb) if isinstance(lo, int) else pl.ds(pl.multiple_of(lo, kb), kb))

    def scores(t, par):
        k_ref, _, keys = block(t)
        s = _dot(k_ref[keys, :], qs_ref[...])
        s_slots[par][0:keys.size, :] = s
        return jnp.max(s, axis=0, keepdims=True)

    def softmax(n_keys, par, m, s_max):
        s_ref, p_ref = s_slots[par], p_slots[par]
        m_new = jnp.maximum(m, s_max)
        m_rows = jnp.broadcast_to(m_new, (ATT_RC, nq))
        for r in range(0, n_keys, ATT_RC):
            p_ref[r:r + ATT_RC, :] = jnp.exp2(s_ref[r:r + ATT_RC, :] - m_rows).astype(BF16)
        return m_new, jnp.exp2(m - m_new)

    def pv(t, par, alpha):
        _, vT_ref, keys = block(t)
        v1 = jnp.concatenate([vT_ref[:, keys], jnp.ones((SUM_ROWS, keys.size), BF16)], axis=0)
        acc_ref[...] = alpha * acc_ref[...] + _dot(v1, p_slots[par][0:keys.size, :])

    def stage(t, par, carry):
        m, alpha, s_max = carry
        next_max = s_max
        if not isinstance(t, int) or t + 1 < n_blk:
            next_max = scores(t + 1, 1 - par)
        if not isinstance(t, int) or t >= 1:
            pv(t - 1, 1 - par, alpha)
        return softmax(block(t)[2].size, par, m, s_max) + (next_max,)

    carry = (jnp.full((1, nq), NEG_BIG, F32), jnp.zeros((1, nq), F32), scores(0, 0))
    lo_t = min(n_cb + 1, n_blk)
    n_loop = max(n_blk - 1 - lo_t, 0)
    if n_loop % 2:
        lo_t += 1
        n_loop -= 1
    for t in range(lo_t):
        carry = stage(t, t % 2, carry)

    def pair(i, carry):
        t = lo_t + 2 * i
        carry = stage(t, lo_t % 2, carry)
        return stage(t + 1, (lo_t + 1) % 2, carry)

    carry = lax.fori_loop(0, n_loop // 2, pair, carry)
    for t in range(lo_t + n_loop, n_blk):
        carry = stage(t, t % 2, carry)
    pv(n_blk - 1, (n_blk - 1) % 2, carry[1])

    acc = acc_ref[...]
    oT = acc[0:HEAD_DIM] / acc[HEAD_DIM:HEAD_DIM + 1]
    o4 = jnp.concatenate([oT[:, g * tq:(g + 1) * tq] for g in range(GROUP)], axis=0)
    o_ref[...] = o4.T.astype(BF16)


ATT_SEQ_PER_STEP = 4


def _attention_short_kernel(q_ref, k_ref, vT_ref, o_ref, *, n_seq):
    t_len = q_ref.shape[0] // n_seq
    head = lax.broadcasted_iota(jnp.int32, (t_len, KV_LANES), 1) // HEAD_DIM
    ones = jnp.ones((SUM_ROWS, t_len), BF16)
    for s in range(n_seq):
        rows = slice(s * t_len, (s + 1) * t_len)
        for kv in range(N_KV_HEADS):
            cols = slice(kv * KV_LANES, (kv + 1) * KV_LANES)
            qf = q_ref[rows, cols].astype(F32)
            qsT = jnp.concatenate([jnp.where(head == g, qf, 0.0).T.astype(BF16) for g in range(GROUP)],
                                  axis=1)
            sT = _dot(k_ref[rows, cols], qsT)
            pT = jnp.exp2(sT - jnp.max(sT, axis=0, keepdims=True)).astype(BF16)
            v1 = jnp.concatenate([vT_ref[kv * HEAD_DIM:(kv + 1) * HEAD_DIM, rows], ones], axis=0)
            acc = _dot(v1, pT)
            oT = acc[0:HEAD_DIM] / acc[HEAD_DIM:HEAD_DIM + 1]
            o4 = jnp.concatenate([oT[:, g * t_len:(g + 1) * t_len] for g in range(GROUP)], axis=0)
            o_ref[rows, cols] = o4.T.astype(BF16)


def _attention_short(q, kt, vT, seq_len):
    n = q.shape[0]
    tb = ATT_SEQ_PER_STEP * seq_len
    return pl.pallas_call(
        functools.partial(_attention_short_kernel, n_seq=ATT_SEQ_PER_STEP),
        grid=(n // tb,),
        in_specs=[pl.BlockSpec((tb, ATT_Q_W), lambda i: (i, 0)),
                  pl.BlockSpec((tb, ATT_Q_W), lambda i: (i, 0)),
                  pl.BlockSpec((ATT_KV_W, tb), lambda i: (0, i))],
        out_specs=pl.BlockSpec((tb, ATT_Q_W), lambda i: (i, 0)),
        out_shape=jax.ShapeDtypeStruct((n, ATT_Q_W), BF16),
        compiler_params=pltpu.CompilerParams(
            dimension_semantics=("arbitrary",), vmem_limit_bytes=VMEM_LIMIT),
        name="attention_short",
    )(q, kt, vT)


def _attention(q, kt, vT, cache, seq_len):
    n = q.shape[0]
    n_batch = n // seq_len
    tq = ATT_TQ
    nq = seq_len // tq
    n_cache = 0 if cache is None else cache[0].shape[1]
    if n_cache == 0 and seq_len <= ATT_SHORT_MAX:
        return _attention_short(q, kt, vT, seq_len)
    blk_rows = max(min(ATT_KB, seq_len), n_cache)
    in_specs = [pl.BlockSpec((tq, KV_LANES), lambda b, kv, i: (b * nq + i, kv))]
    args = [q]
    if n_cache:
        in_specs += [pl.BlockSpec((None, n_cache, KV_LANES), lambda b, kv, i: (b, 0, kv)),
                     pl.BlockSpec((None, HEAD_DIM, n_cache), lambda b, kv, i: (b, kv, 0))]
        args += list(cache)
    in_specs += [pl.BlockSpec((seq_len, KV_LANES), lambda b, kv, i: (b, kv)),
                 pl.BlockSpec((HEAD_DIM, seq_len), lambda b, kv, i: (kv, b))]
    args += [kt, vT]
    return pl.pallas_call(
        functools.partial(_attention_kernel, n_cache=n_cache, n_new=seq_len),
        grid=(n_batch, N_KV_HEADS, nq),
        in_specs=in_specs,
        out_specs=pl.BlockSpec((tq, KV_LANES), lambda b, kv, i: (b * nq + i, kv)),
        out_shape=jax.ShapeDtypeStruct((n, ATT_Q_W), BF16),
        scratch_shapes=[
            pltpu.VMEM((KV_LANES, GROUP * tq), BF16),
            pltpu.VMEM((HEAD_DIM + SUM_ROWS, GROUP * tq), F32),
            pltpu.VMEM((blk_rows, GROUP * tq), F32),
            pltpu.VMEM((blk_rows, GROUP * tq), F32),
            pltpu.VMEM((blk_rows, GROUP * tq), BF16),
            pltpu.VMEM((blk_rows, GROUP * tq), BF16),
        ],
        compiler_params=pltpu.CompilerParams(
            dimension_semantics=("arbitrary", "arbitrary", "arbitrary"), vmem_limit_bytes=VMEM_LIMIT),
        name="attention",
    )(*args)


MERGE_TM = 1024
MERGE_SUB = 256


def _merge_kernel(h_ref, mp_ref, ro_ref, ao_ref, cb_ref, u_ref, up_ref, un_ref, cw_ref,
                  wg_ref, wr_ref, wa_ref, wc_ref, wo_ref, o_ref, *, seq_len):
    tm = h_ref.shape[0]
    sub = MERGE_SUB
    n_sub = tm // sub
    i = pl.program_id(0)
    cw = cw_ref[...]
    r = lax.broadcasted_iota(jnp.int32, (sub, CONV_DIM), 0)
    edge = BF16_SUBLANES

    def branches(s):
        r0 = s * sub
        rows = slice(r0, r0 + sub)
        yr = _dot(ro_ref[rows, :], wr_ref[...])
        ya = _dot(ao_ref[rows, :], wa_ref[...])
        u = u_ref[rows, :].astype(F32)
        before = up_ref if s == 0 else u_ref.at[r0 - edge:r0, :]
        after = un_ref if s == n_sub - 1 else u_ref.at[r0 + sub:r0 + sub + edge, :]
        prev_row = before[edge - 1:edge, :].astype(F32)
        next_row = after[0:1, :].astype(F32)
        t = (i * tm + r0 + r) % seq_len
        u_prev = jnp.where(r == 0, prev_row, pltpu.roll(u, 1, axis=0))
        u_next = jnp.where(r == sub - 1, next_row, pltpu.roll(u, sub - 1, axis=0))
        u_prev = jnp.where(t == 0, 0.0, u_prev)
        u_next = jnp.where(t == seq_len - 1, 0.0, u_next)
        conv = u_prev * cw[0:1] + u * cw[1:2] + u_next * cw[2:3]
        yc = _dot((cb_ref[rows, :].astype(F32) * conv).astype(BF16), wc_ref[...])
        xb = _mod_norm(h_ref[rows, :], mp_ref[...]).astype(BF16)
        merged = _sigmoid(_dot(xb, wg_ref[:, 0:D_MODEL])) * yr
        merged = merged + _sigmoid(_dot(xb, wg_ref[:, D_MODEL:2 * D_MODEL])) * ya
        merged = merged + _sigmoid(_dot(xb, wg_ref[:, 2 * D_MODEL:3 * D_MODEL])) * yc
        return merged.astype(BF16)

    def project(s, merged):
        rows = slice(s * sub, (s + 1) * sub)
        o_ref[rows, :] = h_ref[rows, :] + mp_ref[2:3, :] * _dot(merged, wo_ref[...])

    pending = branches(0)
    for s in range(1, n_sub):
        nxt = branches(s)
        project(s - 1, pending)
        pending = nxt
    project(n_sub - 1, pending)


def _merge(h, mp, rows_per_mod, ret_o, att_o, cb, u, conv_w8, w_gates, w_ret_o, w_att_o, w_conv_o, w_o,
           l, seq_len):
    n = h.shape[0]
    tm = MERGE_TM
    hb = tm // BF16_SUBLANES
    last = n // BF16_SUBLANES - 1
    row = lambda i: (i, 0)
    wspec = lambda k: _resident((None, k, D_MODEL), lambda i: (l, 0, 0))
    return pl.pallas_call(
        functools.partial(_merge_kernel, seq_len=seq_len),
        grid=(n // tm,),
        in_specs=[
            pl.BlockSpec((tm, D_MODEL), row),
            pl.BlockSpec((None, 8, D_MODEL), lambda i: ((i * tm) // rows_per_mod, 0, 0)),
            pl.BlockSpec((tm, RET_W), row),
            pl.BlockSpec((tm, ATT_Q_W), row),
            pl.BlockSpec((tm, CONV_DIM), row),
            pl.BlockSpec((tm, CONV_DIM), row),
            pl.BlockSpec((BF16_SUBLANES, CONV_DIM), lambda i: (jnp.maximum(i * hb - 1, 0), 0)),
            pl.BlockSpec((BF16_SUBLANES, CONV_DIM), lambda i: (jnp.minimum((i + 1) * hb, last), 0)),
            pl.BlockSpec((None, 8, CONV_DIM), lambda i: (l, 0, 0)),
            _resident((None, D_MODEL, 3 * D_MODEL), lambda i: (l, 0, 0)),
            wspec(RET_W), wspec(ATT_Q_W), wspec(CONV_DIM), wspec(D_MODEL),
        ],
        out_specs=pl.BlockSpec((tm, D_MODEL), row),
        out_shape=jax.ShapeDtypeStruct((n, D_MODEL), F32),
        compiler_params=pltpu.CompilerParams(
            dimension_semantics=("arbitrary",), vmem_limit_bytes=VMEM_LIMIT),
        name="merge",
    )(h, mp, ret_o, att_o, cb, u, u, u, conv_w8, w_gates, w_ret_o, w_att_o, w_conv_o, w_o)


def _rope_tables(n_tok):
    rows = n_tok // GRID_W
    t_row = np.repeat(np.arange(rows, dtype=np.float64), GRID_W)
    t_col = np.tile(np.arange(GRID_W, dtype=np.float64), rows)
    n_freq = HEAD_DIM // 4
    inv = ROPE_THETA ** (-np.arange(n_freq, dtype=np.float64) / n_freq)
    ang = np.concatenate([t_row[:, None] * inv, t_col[:, None] * inv], axis=-1)
    cos, sin = np.cos(ang), np.sin(ang)
    cos64 = np.concatenate([cos, cos], axis=-1)
    sin64 = np.concatenate([-sin, sin], axis=-1)
    return (jnp.asarray(np.tile(cos64, (1, LANES // HEAD_DIM)), F32),
            jnp.asarray(np.tile(sin64, (1, LANES // HEAD_DIM)), F32))


def _mod_pack(mod_l, rows, sub, norm_w_row):
    r0 = rows[0]
    nr = len(rows)
    m = mod_l[r0:r0 + nr, 3 * sub * D_MODEL:3 * (sub + 1) * D_MODEL].reshape(nr, 3, D_MODEL)
    nw = jnp.broadcast_to(norm_w_row[None, None, :], (nr, 1, D_MODEL))
    pad = jnp.zeros((nr, 4, D_MODEL), F32)
    return jnp.concatenate([m, nw, pad], axis=1)


def kernel(x_prompt, x_sample, c, state_ret, cache_k, cache_v, c_ctx, w_ada, b_ada, norm_w, w_ffn_in,
           w_ffn_out, w_in, ret_decay_logit, ret_gn, q_gain, k_gain, conv_w, w_ret_o, w_att_o, w_conv_o,
           w_o):
    n_ctx_b, ctx_len, _ = x_prompt.shape
    n_lat_b, lat_len, _ = x_sample.shape

    w_ffn_in_b = w_ffn_in.astype(BF16)
    w_ffn_out_b = w_ffn_out.astype(BF16)
    w_in_b = w_in[:, :, :OFF_GATES].astype(BF16)
    w_gates_b = w_in[:, :, OFF_GATES:].astype(BF16)
    w_ret_o_b = w_ret_o.astype(BF16)
    w_att_o_b = w_att_o.astype(BF16)
    w_conv_o_b = w_conv_o.astype(BF16)
    w_o_b = w_o.astype(BF16)

    q_gain_t = jnp.tile(q_gain, (1, N_HEADS)).reshape(DEPTH, 1, ATT_Q_W)
    k_gain_t = jnp.tile(k_gain, (1, N_KV_HEADS)).reshape(DEPTH, 1, ATT_KV_W)
    conv_w8 = jnp.pad(conv_w, ((0, 0), (0, 8 - conv_w.shape[1]), (0, 0)))
    gid = np.arange(ATT_Q_W) // HEAD_DIM
    bd = jnp.asarray(gid[:, None] == gid[None, :], BF16)
    rope = _rope_tables(lat_len)

    cond8 = jnp.zeros((8, D_MODEL), F32).at[0].set(c_ctx).at[1:1 + n_lat_b].set(c)
    mod = _ada(cond8, w_ada, b_ada)

    ck = cache_k.astype(BF16)
    ck = jnp.broadcast_to(ck[:, :, :, :, None, :], ck.shape[:4] + (GROUP, HEAD_DIM))
    cache_kt = ck.reshape(ck.shape[0], ck.shape[1], ck.shape[2], N_KV_HEADS * KV_LANES)
    cache_vT = jnp.transpose(cache_v.astype(BF16), (0, 1, 3, 4, 2)).reshape(
        cache_v.shape[0], cache_v.shape[1], ATT_KV_W, cache_v.shape[2])

    groups = (
        dict(x=x_prompt.reshape(n_ctx_b * ctx_len, D_MODEL), rows=[0], seq=ctx_len, ctx=True),
        dict(x=x_sample.reshape(n_lat_b * lat_len, D_MODEL), rows=list(range(1, 1 + n_lat_b)),
             seq=lat_len, ctx=False),
    )
    results = []
    for grp in groups:
        h = grp["x"]
        seq = grp["seq"]
        is_ctx = grp["ctx"]
        rpm = h.shape[0] // len(grp["rows"])
        states, keys, values = [], [], []
        for l in range(DEPTH):
            mp = [_mod_pack(mod[l], grp["rows"], s, norm_w[l, s]) for s in range(3)]
            h = _ffn(h, mp[0], rpm, w_ffn_in_b, w_ffn_out_b, l, 0)
            outs = _inproj(h, mp[1], rpm, w_in_b, l, q_gain_t, k_gain_t, bd,
                           None if is_ctx else rope, seq, emit_kv=is_ctx)
            ret_in, qn, kt, vT, cb, u = outs[:6]
            if is_ctx:
                ret_o, st = _retention(ret_in, ret_decay_logit[l], ret_gn[l][None, :], None, seq,
                                       n_seq_blk=16, emit_state=True)
                states.append(st)
                keys.append(outs[6].reshape(n_ctx_b, seq, N_KV_HEADS, HEAD_DIM))
                values.append(outs[7].reshape(n_ctx_b, seq, N_KV_HEADS, HEAD_DIM))
                att_o = _attention(qn, kt, vT, None, seq)
            else:
                (ret_o,) = _retention(ret_in, ret_decay_logit[l], ret_gn[l][None, :], state_ret[:, l],
                                      seq, n_seq_blk=1, emit_state=False)
                att_o = _attention(qn, kt, vT, (cache_kt[:, l], cache_vT[:, l]), seq)
            h = _merge(h, mp[1], rpm, ret_o, att_o, cb, u, conv_w8, w_gates_b, w_ret_o_b, w_att_o_b,
                       w_conv_o_b, w_o_b, l, seq)
            h = _ffn(h, mp[2], rpm, w_ffn_in_b, w_ffn_out_b, l, 1)
        results.append((h, states, keys, values))

    (y_ctx, states, keys, values), (y_lat, _, _, _) = results
    y_prompt = y_ctx.reshape(x_prompt.shape)
    y_sample = y_lat.reshape(x_sample.shape)
    new_state_ret = jnp.stack(states, axis=1)
    new_cache_k = jnp.stack(keys, axis=1)
    new_cache_v = jnp.stack(values, axis=1)
    return (y_prompt, y_sample, new_state_ret, new_cache_k, new_cache_v)
```

```python
import functools
import math

import numpy as np
import jax
import jax.numpy as jnp
from jax import lax
from jax.experimental import pallas as pl
from jax.experimental.pallas import tpu as pltpu

D_MODEL = 1024
DEPTH = 2
GRID_W = 64
N_RET_HEADS = 4
RET_DK = 128
RET_DV = 128
RET_CHUNK = 128
N_HEADS = 8
N_KV_HEADS = 2
HEAD_DIM = 64
ROPE_THETA = 10000.0
CONV_DIM = 512
FFN_DIM = 2816
N_MOD = 9
EPS = 1e-6
RET_W = N_RET_HEADS * RET_DK
ATT_Q_W = N_HEADS * HEAD_DIM
ATT_KV_W = N_KV_HEADS * HEAD_DIM
GROUP = N_HEADS // N_KV_HEADS
KV_LANES = GROUP * HEAD_DIM
OFF_RQ, OFF_RK, OFF_RV, OFF_RG = 0, 512, 1024, 1536
OFF_AQ, OFF_AK, OFF_AV = 2048, 2560, 2688
OFF_CB, OFF_CC, OFF_CX = 2816, 3328, 3840
OFF_GATES = 4352

LANES = 128
BF16_SUBLANES = 16
VMEM_LIMIT = 56 * 1024 * 1024

F32 = jnp.float32
BF16 = jnp.bfloat16


def _dot(a, b):
    return jnp.dot(a, b, preferred_element_type=F32)


def _dot_nt(a, b):
    return lax.dot_general(a, b, (((1,), (1,)), ((), ())), preferred_element_type=F32)


def _sigmoid(x):
    return 1.0 / (1.0 + jnp.exp(-x))


def _silu(x):
    return x * _sigmoid(x)


def _mod_norm(x, mp):
    ms = jnp.mean(x * x, axis=-1, keepdims=True)
    y = x * lax.rsqrt(ms + EPS)
    return (y * mp[3:4]) * (1.0 + mp[1:2]) + mp[0:1]


def _resident(shape, index):
    return pl.BlockSpec(shape, index, pipeline_mode=pl.Buffered(1))


ADA_TN = 2304


def _ada_kernel(c_ref, w_ref, b_ref, o_ref):
    a = _silu(c_ref[...]).astype(BF16)
    o_ref[...] = _dot(a, w_ref[...].astype(BF16)) + b_ref[...]


def _ada(cond8, w_ada, b_ada):
    n = N_MOD * D_MODEL
    return pl.pallas_call(
        _ada_kernel,
        grid=(DEPTH, n // ADA_TN),
        in_specs=[
            pl.BlockSpec((8, D_MODEL), lambda l, j: (0, 0)),
            pl.BlockSpec((None, D_MODEL, ADA_TN), lambda l, j: (l, 0, j)),
            pl.BlockSpec((None, 1, ADA_TN), lambda l, j: (l, 0, j)),
        ],
        out_specs=pl.BlockSpec((None, 8, ADA_TN), lambda l, j: (l, 0, j)),
        out_shape=jax.ShapeDtypeStruct((DEPTH, 8, n), F32),
        compiler_params=pltpu.CompilerParams(
            dimension_semantics=("arbitrary", "arbitrary"), vmem_limit_bytes=VMEM_LIMIT),
        name="ada",
    )(cond8, w_ada, b_ada.reshape(DEPTH, 1, n))


FFN_TM = 1024
FFN_FC = 256


def _ffn_kernel(x_ref, mp_ref, wi_ref, wo_ref, o_ref, h_ref):
    x = x_ref[...]
    mp = mp_ref[...]
    xb = _mod_norm(x, mp).astype(BF16)
    for c in range(FFN_DIM // FFN_FC):
        lo = c * FFN_FC
        g = _dot(xb, wi_ref[:, lo:lo + FFN_FC])
        u = _dot(xb, wi_ref[:, FFN_DIM + lo:FFN_DIM + lo + FFN_FC])
        h_ref[:, lo:lo + FFN_FC] = (_silu(g) * u).astype(BF16)
    y = _dot(h_ref[...], wo_ref[...])
    o_ref[...] = x + (0.5 * mp[2:3]) * y


def _ffn(x, mp, rows_per_mod, w_in, w_out, l, j):
    n = x.shape[0]
    tm = FFN_TM
    return pl.pallas_call(
        _ffn_kernel,
        grid=(n // tm,),
        in_specs=[
            pl.BlockSpec((tm, D_MODEL), lambda i: (i, 0)),
            pl.BlockSpec((None, 8, D_MODEL), lambda i: ((i * tm) // rows_per_mod, 0, 0)),
            _resident((None, None, D_MODEL, 2 * FFN_DIM), lambda i: (l, j, 0, 0)),
            _resident((None, None, FFN_DIM, D_MODEL), lambda i: (l, j, 0, 0)),
        ],
        out_specs=pl.BlockSpec((tm, D_MODEL), lambda i: (i, 0)),
        out_shape=jax.ShapeDtypeStruct((n, D_MODEL), F32),
        scratch_shapes=[pltpu.VMEM((tm, FFN_DIM), BF16)],
        compiler_params=pltpu.CompilerParams(
            dimension_semantics=("arbitrary",), vmem_limit_bytes=VMEM_LIMIT),
        name="ffn",
    )(x, mp, w_in, w_out)


INPROJ_TM = 1024
INPROJ_SUB = 512
Q_SCALE = HEAD_DIM ** -0.5 * math.log2(math.e)


def _group_sumsq(a, bd):
    sq = a * a
    hi = sq.astype(BF16)
    lo = (sq - hi.astype(F32)).astype(BF16)
    return _dot(hi, bd) + _dot(lo, bd)


def _group_rms(a, sumsq, gain):
    return (a * lax.rsqrt(sumsq * (1.0 / HEAD_DIM) + EPS)) * gain


def _rope(x, cos, sin_signed):
    half = HEAD_DIM // 2
    reps = x.shape[1] // LANES
    lane = lax.broadcasted_iota(jnp.int32, (x.shape[0], LANES), 1)
    first = (lane % HEAD_DIM) < half
    out = []
    for r in range(reps):
        xs = x[:, r * LANES:(r + 1) * LANES]
        partner = jnp.where(first, pltpu.roll(xs, LANES - half, axis=1), pltpu.roll(xs, half, axis=1))
        out.append(xs * cos + partner * sin_signed)
    return out[0] if reps == 1 else jnp.concatenate(out, axis=1)


def _tile_kv(a):
    lane = lax.broadcasted_iota(jnp.int32, a.shape, 1)
    sw = pltpu.roll(a, HEAD_DIM, axis=1)
    h0 = jnp.where(lane < HEAD_DIM, a, sw)
    h1 = jnp.where(lane < HEAD_DIM, sw, a)
    return jnp.concatenate([h0, h0, h1, h1], axis=1)


def _inproj_kernel(*refs, use_rope, emit_kv):
    x_ref, mp_ref, w_ref, qg_ref, kg_ref, bd_ref = refs[:6]
    pos = 6
    if use_rope:
        cos_ref, sin_ref = refs[pos:pos + 2]
        pos += 2
    ret_ref, q_ref, kt_ref, vT_ref, cb_ref, u_ref = refs[pos:pos + 6]
    pos += 6
    if emit_kv:
        kout_ref, vout_ref = refs[pos:pos + 2]

    mp = mp_ref[...]
    tm = x_ref.shape[0]
    n_sub = tm // INPROJ_SUB
    xbs = [None] * n_sub
    xbs[0] = _mod_norm(x_ref[0:INPROJ_SUB, :], mp).astype(BF16)
    for s in range(n_sub):
        rows = slice(s * INPROJ_SUB, (s + 1) * INPROJ_SUB)
        xb = xbs[s]

        def proj(off, width):
            return _dot(xb, w_ref[:, off:off + width])

        aq = proj(OFF_AQ, ATT_Q_W)
        ret_ref[rows, OFF_RQ:OFF_RQ + RET_W] = proj(OFF_RQ, RET_W).astype(BF16)
        if s + 1 < n_sub:
            nxt = slice((s + 1) * INPROJ_SUB, (s + 2) * INPROJ_SUB)
            xbs[s + 1] = _mod_norm(x_ref[nxt, :], mp).astype(BF16)
        ss_q = _group_sumsq(aq, bd_ref[...])
        ak = proj(OFF_AK, ATT_KV_W)
        v = proj(OFF_AV, ATT_KV_W)
        ret_ref[rows, OFF_RK:OFF_RK + RET_W] = (proj(OFF_RK, RET_W) * (RET_DK ** -0.5)).astype(BF16)

        q = _group_rms(aq, ss_q, qg_ref[...])
        if use_rope:
            q = _rope(q, cos_ref[rows, :], sin_ref[rows, :])
        q_ref[rows, :] = (q * Q_SCALE).astype(BF16)
        ss_k = _group_sumsq(ak, bd_ref[0:ATT_KV_W, 0:ATT_KV_W])
        ret_ref[rows, OFF_RV:OFF_RV + RET_W] = proj(OFF_RV, RET_W).astype(BF16)

        k = _group_rms(ak, ss_k, kg_ref[...])
        if emit_kv:
            kout_ref[rows, :] = k
            vout_ref[rows, :] = v
        if use_rope:
            k = _rope(k, cos_ref[rows, :], sin_ref[rows, :])
        kt_ref[rows, :] = _tile_kv(k).astype(BF16)
        vT_ref[:, rows] = v.T.astype(BF16)

        ret_ref[rows, OFF_RG:OFF_RG + RET_W] = _silu(proj(OFF_RG, RET_W)).astype(BF16)
        cb_ref[rows, :] = proj(OFF_CB, CONV_DIM).astype(BF16)
        u_ref[rows, :] = (proj(OFF_CC, CONV_DIM) * proj(OFF_CX, CONV_DIM)).astype(BF16)


def _inproj(x, mp, rows_per_mod, w_in, l, q_gain_t, k_gain_t, bd, rope, seq_len, emit_kv):
    n = x.shape[0]
    tm = INPROJ_TM
    use_rope = rope is not None
    row = lambda i: (i, 0)
    in_specs = [
        pl.BlockSpec((tm, D_MODEL), row),
        pl.BlockSpec((None, 8, D_MODEL), lambda i: ((i * tm) // rows_per_mod, 0, 0)),
        _resident((None, D_MODEL, OFF_GATES), lambda i: (l, 0, 0)),
        pl.BlockSpec((None, 1, ATT_Q_W), lambda i: (l, 0, 0)),
        pl.BlockSpec((None, 1, ATT_KV_W), lambda i: (l, 0, 0)),
        _resident((ATT_Q_W, ATT_Q_W), lambda i: (0, 0)),
    ]
    args = [x, mp, w_in, q_gain_t, k_gain_t, bd]
    if use_rope:
        tiles_per_seq = seq_len // tm
        in_specs += [pl.BlockSpec((tm, LANES), lambda i: (i % tiles_per_seq, 0))] * 2
        args += list(rope)
    widths = [4 * RET_W, ATT_Q_W, ATT_Q_W, None, CONV_DIM, CONV_DIM]
    out_specs = [pl.BlockSpec((tm, w), row) for w in widths if w]
    out_shape = [jax.ShapeDtypeStruct((n, w), BF16) for w in widths if w]
    out_specs.insert(3, pl.BlockSpec((ATT_KV_W, tm), lambda i: (0, i)))
    out_shape.insert(3, jax.ShapeDtypeStruct((ATT_KV_W, n), BF16))
    if emit_kv:
        out_specs += [pl.BlockSpec((tm, ATT_KV_W), row)] * 2
        out_shape += [jax.ShapeDtypeStruct((n, ATT_KV_W), F32)] * 2
    return pl.pallas_call(
        functools.partial(_inproj_kernel, use_rope=use_rope, emit_kv=emit_kv),
        grid=(n // tm,),
        in_specs=in_specs,
        out_specs=out_specs,
        out_shape=out_shape,
        compiler_params=pltpu.CompilerParams(
            dimension_semantics=("arbitrary",), vmem_limit_bytes=VMEM_LIMIT),
        name="inproj",
    )(*args)


RET_GROUP = 8
RET_SCAN_UNROLL = 2
TAB_DMASK, TAB_DQF, TAB_DQB, TAB_DKF, TAB_DKB, TAB_DCF, TAB_DCB, N_TAB = range(8)


def _log_sigmoid(x):
    return jnp.minimum(x, 0.0) - jnp.log(1.0 + jnp.exp(-jnp.abs(x)))


def _chunk_rows(start):
    if isinstance(start, int):
        return pl.ds(start, RET_CHUNK)
    return pl.ds(pl.multiple_of(start, RET_CHUNK), RET_CHUNK)


def _retention_kernel(*refs, n_seq, n_chunks, has_s0, emit_state):
    dl_ref, q_ref, k_ref, v_ref, g_ref, gn_ref = refs[:6]
    pos = 6
    if has_s0:
        s0_ref = refs[pos]
        pos += 1
    o_ref = refs[pos]
    pos += 1
    if emit_state:
        st_ref = refs[pos]
        pos += 1
    tab_ref, kv_ref, ent_ref = refs[pos:pos + 3]

    C = RET_CHUNK
    hd = pl.program_id(1)
    row = lax.broadcasted_iota(jnp.int32, (C, C), 0).astype(F32)
    col = lax.broadcasted_iota(jnp.int32, (C, C), 1).astype(F32)
    diff = row - col
    lgf = _log_sigmoid(jnp.full((C, C), dl_ref[0, hd], F32))
    lgb = _log_sigmoid(jnp.full((C, C), dl_ref[1, hd], F32))
    tab_ref[TAB_DMASK] = (jnp.where(diff >= 0, jnp.exp(lgf * jnp.maximum(diff, 0.0)), 0.0)
                          + jnp.where(diff <= 0, jnp.exp(lgb * jnp.maximum(-diff, 0.0)), 0.0))
    tab_ref[TAB_DQF] = jnp.exp(lgf * (row + 1.0))
    tab_ref[TAB_DQB] = jnp.exp(lgb * (C - row))
    tab_ref[TAB_DKF] = jnp.exp(lgf * (C - 1.0 - row))
    tab_ref[TAB_DKB] = jnp.exp(lgb * row)
    tab_ref[TAB_DCF] = jnp.exp(lgf * C)
    tab_ref[TAB_DCB] = jnp.exp(lgb * C)

    def kv_phase(items):
        ops = []
        for r, _ in items:
            rows = _chunk_rows(r)
            k = k_ref[rows, :].astype(F32)
            vT = v_ref[rows, :].astype(F32).T.astype(BF16)
            kk = jnp.concatenate([(k * tab_ref[TAB_DKF]).astype(BF16),
                                  (k * tab_ref[TAB_DKB]).astype(BF16)], axis=1)
            ops.append((vT, kk))
        for (_, i), (vT, kk) in zip(items, ops):
            kv_ref[i] = _dot(vT, kk)

    def scan_phase(s, sc):
        if has_s0:
            st0 = (s0_ref[0].T, s0_ref[1].T)
        else:
            st0 = (jnp.zeros((RET_DV, RET_DK), F32), jnp.zeros((RET_DV, RET_DK), F32))

        def scan_step(t, st):
            st_f, st_b = st
            cb = n_chunks - 1 - t
            ent_ref[sc + t, :, 0:RET_DK] = st_f.astype(BF16)
            ent_ref[sc + cb, :, RET_DK:2 * RET_DK] = st_b.astype(BF16)
            st_f = st_f * tab_ref[TAB_DCF] + kv_ref[sc + t, :, 0:RET_DK]
            st_b = st_b * tab_ref[TAB_DCB] + kv_ref[sc + cb, :, RET_DK:2 * RET_DK]
            return st_f, st_b

        if n_chunks <= RET_SCAN_UNROLL:
            st = st0
            for t in range(n_chunks):
                st = scan_step(t, st)
        else:
            st = lax.fori_loop(0, n_chunks, scan_step, st0)
        if emit_state:
            st_ref[s, 0] = st[0].T
            st_ref[s, 1] = st[1].T

    def score_stage(items):
        qs = [q_ref[_chunk_rows(r), :] for r, _ in items]
        att = [_dot_nt(q, k_ref[_chunk_rows(r), :]) for q, (r, _) in zip(qs, items)]
        att = [(a * tab_ref[TAB_DMASK]).astype(BF16) for a in att]
        qq = []
        for q in qs:
            qf = q.astype(F32)
            qq.append(jnp.concatenate([(qf * tab_ref[TAB_DQF]).astype(BF16),
                                       (qf * tab_ref[TAB_DQB]).astype(BF16)], axis=1))
        return att, qq

    def product_stage(items, att, qq):
        return [_dot(a, v_ref[_chunk_rows(r), :]) + _dot_nt(x, ent_ref[i])
                for a, x, (r, i) in zip(att, qq, items)]

    def norm_stage(items, outs):
        for o, (r, _) in zip(outs, items):
            rows = _chunk_rows(r)
            ms = jnp.mean(o * o, axis=-1, keepdims=True)
            on = (o * lax.rsqrt(ms + EPS)) * gn_ref[...]
            o_ref[rows, :] = (g_ref[rows, :].astype(F32) * on).astype(BF16)

    items = [(s * n_chunks * C + c * C, s * n_chunks + c) for s in range(n_seq) for c in range(n_chunks)]
    assert len(items) % RET_GROUP == 0
    groups = [items[i:i + RET_GROUP] for i in range(0, len(items), RET_GROUP)]
    for grp in groups:
        kv_phase(grp)
    for s in range(n_seq):
        scan_phase(s, s * n_chunks)
    scored, products = {}, {}
    for t in range(len(groups) + 2):
        if t < len(groups):
            scored[t] = score_stage(groups[t])
        if 0 <= t - 1 < len(groups):
            products[t - 1] = product_stage(groups[t - 1], *scored.pop(t - 1))
        if 0 <= t - 2 < len(groups):
            norm_stage(groups[t - 2], products.pop(t - 2))


def _retention(ret_in, decay_logit_l, gn_l, s0, seq_len, n_seq_blk, emit_state):
    n = ret_in.shape[0]
    n_chunks = seq_len // RET_CHUNK
    tb = n_seq_blk * seq_len
    has_s0 = s0 is not None
    n_off = RET_W // RET_DK

    def branch(k):
        return pl.BlockSpec((tb, RET_DK), lambda i, h: (i, k * n_off + h))

    in_specs = [pl.BlockSpec(memory_space=pltpu.SMEM), branch(0), branch(1), branch(2), branch(3),
                pl.BlockSpec((1, RET_DV), lambda i, h: (0, h))]
    args = [decay_logit_l, ret_in, ret_in, ret_in, ret_in, gn_l]
    if has_s0:
        assert n_seq_blk == 1
        in_specs.append(pl.BlockSpec((None, 2, None, RET_DK, RET_DV), lambda i, h: (i, 0, h, 0, 0)))
        args.append(s0)
    out_specs = [pl.BlockSpec((tb, RET_DV), lambda i, h: (i, h))]
    out_shape = [jax.ShapeDtypeStruct((n, RET_W), BF16)]
    if emit_state:
        out_specs.append(pl.BlockSpec((n_seq_blk, 2, None, RET_DK, RET_DV), lambda i, h: (i, 0, h, 0, 0)))
        out_shape.append(jax.ShapeDtypeStruct((n // seq_len, 2, N_RET_HEADS, RET_DK, RET_DV), F32))
    n_slots = n_chunks * n_seq_blk
    return pl.pallas_call(
        functools.partial(_retention_kernel, n_seq=n_seq_blk, n_chunks=n_chunks, has_s0=has_s0,
                          emit_state=emit_state),
        grid=(n // tb, N_RET_HEADS),
        in_specs=in_specs,
        out_specs=out_specs,
        out_shape=out_shape,
        scratch_shapes=[
            pltpu.VMEM((N_TAB, RET_CHUNK, RET_CHUNK), F32),
            pltpu.VMEM((n_slots, RET_DV, 2 * RET_DK), F32),
            pltpu.VMEM((n_slots, RET_DV, 2 * RET_DK), BF16),
        ],
        compiler_params=pltpu.CompilerParams(
            dimension_semantics=("arbitrary", "arbitrary"), vmem_limit_bytes=VMEM_LIMIT),
        name="retention",
    )(*args)


ATT_TQ = 1024
ATT_KB = 512
ATT_SHORT_MAX = 512
ATT_RC = 16
NEG_BIG = -1e30
SUM_ROWS = BF16_SUBLANES


def _attention_kernel(*refs, n_cache, n_new):
    q_ref = refs[0]
    pos = 1
    if n_cache:
        kc_ref, vc_ref = refs[pos:pos + 2]
        pos += 2
    kn_ref, vn_ref, o_ref, qs_ref, acc_ref = refs[pos:pos + 5]
    s_slots = refs[pos + 5:pos + 7]
    p_slots = refs[pos + 7:pos + 9]

    tq = q_ref.shape[0]
    nq = GROUP * tq
    qf = q_ref[...].astype(F32)
    head = lax.broadcasted_iota(jnp.int32, (tq, KV_LANES), 1) // HEAD_DIM
    for g in range(GROUP):
        qs_ref[:, g * tq:(g + 1) * tq] = jnp.where(head == g, qf, 0.0).T.astype(BF16)
    acc_ref[...] = jnp.zeros(acc_ref.shape, F32)

    kb = min(ATT_KB, n_new)
    n_cb = 1 if n_cache else 0
    n_nb = n_new // kb
    assert n_new % kb == 0
    n_blk = n_cb + n_nb

    def block(t):
        if isinstance(t, int) and t < n_cb:
            return kc_ref, vc_ref, pl.ds(0, n_cache)
        lo = (t - n_cb) * kb
        return kn_ref, vn_ref, (pl.ds(lo, kb) if isinstance(lo, int) else pl.ds(pl.multiple_of(lo, kb), kb))

    def scores(t, par):
        k_ref, _, keys = block(t)
        s = _dot(k_ref[keys, :], qs_ref[...])
        s_slots[par][0:keys.size, :] = s
        return jnp.max(s, axis=0, keepdims=True)

    def softmax(n_keys, par, m, s_max):
        s_ref, p_ref = s_slots[par], p_slots[par]
        m_new = jnp.maximum(m, s_max)
        m_rows = jnp.broadcast_to(m_new, (ATT_RC, nq))
        for r in range(0, n_keys, ATT_RC):
            p_ref[r:r + ATT_RC, :] = jnp.exp2(s_ref[r:r + ATT_RC, :] - m_rows).astype(BF16)
        return m_new, jnp.exp2(m - m_new)

    def pv(t, par, alpha):
        _, vT_ref, keys = block(t)
        v1 = jnp.concatenate([vT_ref[:, keys], jnp.ones((SUM_ROWS, keys.size), BF16)], axis=0)
        acc_ref[...] = alpha * acc_ref[...] + _dot(v1, p_slots[par][0:keys.size, :])

    def stage(t, par, carry):
        m, alpha, s_max = carry
        next_max = s_max
        if not isinstance(t, int) or t + 1 < n_blk:
            next_max = scores(t + 1, 1 - par)
        if not isinstance(t, int) or t >= 1:
            pv(t - 1, 1 - par, alpha)
        return softmax(block(t)[2].size, par, m, s_max) + (next_max,)

    carry = (jnp.full((1, nq), NEG_BIG, F32), jnp.zeros((1, nq), F32), scores(0, 0))
    lo_t = min(n_cb + 1, n_blk)
    n_loop = max(n_blk - 1 - lo_t, 0)
    if n_loop % 2:
        lo_t += 1
        n_loop -= 1
    for t in range(lo_t):
        carry = stage(t, t % 2, carry)

    def pair(i, carry):
        t = lo_t + 2 * i
        carry = stage(t, lo_t % 2, carry)
        return stage(t + 1, (lo_t + 1) % 2, carry)

    carry = lax.fori_loop(0, n_loop // 2, pair, carry)
    for t in range(lo_t + n_loop, n_blk):
        carry = stage(t, t % 2, carry)
    pv(n_blk - 1, (n_blk - 1) % 2, carry[1])

    acc = acc_ref[...]
    oT = acc[0:HEAD_DIM] / acc[HEAD_DIM:HEAD_DIM + 1]
    o4 = jnp.concatenate([oT[:, g * tq:(g + 1) * tq] for g in range(GROUP)], axis=0)
    o_ref[...] = o4.T.astype(BF16)


ATT_SEQ_PER_STEP = 4


def _attention_short_kernel(q_ref, k_ref, vT_ref, o_ref, *, n_seq):
    t_len = q_ref.shape[0] // n_seq
    head = lax.broadcasted_iota(jnp.int32, (t_len, KV_LANES), 1) // HEAD_DIM
    ones = jnp.ones((SUM_ROWS, t_len), BF16)
    items = [(slice(s * t_len, (s + 1) * t_len), kv) for s in range(n_seq) for kv in range(N_KV_HEADS)]

    def cols(kv):
        return slice(kv * KV_LANES, (kv + 1) * KV_LANES)

    def score_stage(item):
        rows, kv = item
        qf = q_ref[rows, cols(kv)].astype(F32)
        qsT = jnp.concatenate([jnp.where(head == g, qf, 0.0).T.astype(BF16) for g in range(GROUP)],
                              axis=1)
        return _dot(k_ref[rows, cols(kv)], qsT)

    def softmax_stage(sT):
        return jnp.exp2(sT - jnp.max(sT, axis=0, keepdims=True)).astype(BF16)

    def value_stage(item, pT):
        rows, kv = item
        v1 = jnp.concatenate([vT_ref[kv * HEAD_DIM:(kv + 1) * HEAD_DIM, rows], ones], axis=0)
        return _dot(v1, pT)

    def out_stage(item, acc):
        rows, kv = item
        oT = acc[0:HEAD_DIM] / acc[HEAD_DIM:HEAD_DIM + 1]
        o4 = jnp.concatenate([oT[:, g * t_len:(g + 1) * t_len] for g in range(GROUP)], axis=0)
        o_ref[rows, cols(kv)] = o4.T.astype(BF16)

    n = len(items)
    scores, probs, accs = {}, {}, {}
    for t in range(n + 3):
        if t < n:
            scores[t] = score_stage(items[t])
        if 0 <= t - 1 < n:
            probs[t - 1] = softmax_stage(scores.pop(t - 1))
        if 0 <= t - 2 < n:
            accs[t - 2] = value_stage(items[t - 2], probs.pop(t - 2))
        if 0 <= t - 3 < n:
            out_stage(items[t - 3], accs.pop(t - 3))


def _attention_short(q, kt, vT, seq_len):
    n = q.shape[0]
    tb = ATT_SEQ_PER_STEP * seq_len
    return pl.pallas_call(
        functools.partial(_attention_short_kernel, n_seq=ATT_SEQ_PER_STEP),
        grid=(n // tb,),
        in_specs=[pl.BlockSpec((tb, ATT_Q_W), lambda i: (i, 0)),
                  pl.BlockSpec((tb, ATT_Q_W), lambda i: (i, 0)),
                  pl.BlockSpec((ATT_KV_W, tb), lambda i: (0, i))],
        out_specs=pl.BlockSpec((tb, ATT_Q_W), lambda i: (i, 0)),
        out_shape=jax.ShapeDtypeStruct((n, ATT_Q_W), BF16),
        compiler_params=pltpu.CompilerParams(
            dimension_semantics=("arbitrary",), vmem_limit_bytes=VMEM_LIMIT),
        name="attention_short",
    )(q, kt, vT)


def _attention(q, kt, vT, cache, seq_len):
    n = q.shape[0]
    n_batch = n // seq_len
    tq = ATT_TQ
    nq = seq_len // tq
    n_cache = 0 if cache is None else cache[0].shape[1]
    if n_cache == 0 and seq_len <= ATT_SHORT_MAX:
        return _attention_short(q, kt, vT, seq_len)
    blk_rows = max(min(ATT_KB, seq_len), n_cache)
    in_specs = [pl.BlockSpec((tq, KV_LANES), lambda b, kv, i: (b * nq + i, kv))]
    args = [q]
    if n_cache:
        in_specs += [pl.BlockSpec((None, n_cache, KV_LANES), lambda b, kv, i: (b, 0, kv)),
                     pl.BlockSpec((None, HEAD_DIM, n_cache), lambda b, kv, i: (b, kv, 0))]
        args += list(cache)
    in_specs += [pl.BlockSpec((seq_len, KV_LANES), lambda b, kv, i: (b, kv)),
                 pl.BlockSpec((HEAD_DIM, seq_len), lambda b, kv, i: (kv, b))]
    args += [kt, vT]
    return pl.pallas_call(
        functools.partial(_attention_kernel, n_cache=n_cache, n_new=seq_len),
        grid=(n_batch, N_KV_HEADS, nq),
        in_specs=in_specs,
        out_specs=pl.BlockSpec((tq, KV_LANES), lambda b, kv, i: (b * nq + i, kv)),
        out_shape=jax.ShapeDtypeStruct((n, ATT_Q_W), BF16),
        scratch_shapes=[
            pltpu.VMEM((KV_LANES, GROUP * tq), BF16),
            pltpu.VMEM((HEAD_DIM + SUM_ROWS, GROUP * tq), F32),
            pltpu.VMEM((blk_rows, GROUP * tq), F32),
            pltpu.VMEM((blk_rows, GROUP * tq), F32),
            pltpu.VMEM((blk_rows, GROUP * tq), BF16),
            pltpu.VMEM((blk_rows, GROUP * tq), BF16),
        ],
        compiler_params=pltpu.CompilerParams(
            dimension_semantics=("arbitrary", "arbitrary", "arbitrary"), vmem_limit_bytes=VMEM_LIMIT),
        name="attention",
    )(*args)


MERGE_TM = 1024
MERGE_SUB = 256


def _merge_kernel(h_ref, mp_ref, ro_ref, ao_ref, cb_ref, u_ref, up_ref, un_ref, cw_ref,
                  wg_ref, wr_ref, wa_ref, wc_ref, wo_ref, o_ref, *, seq_len):
    tm = h_ref.shape[0]
    sub = MERGE_SUB
    n_sub = tm // sub
    i = pl.program_id(0)
    cw = cw_ref[...]
    r = lax.broadcasted_iota(jnp.int32, (sub, CONV_DIM), 0)
    edge = BF16_SUBLANES

    def branches(s):
        r0 = s * sub
        rows = slice(r0, r0 + sub)
        yr = _dot(ro_ref[rows, :], wr_ref[...])
        ya = _dot(ao_ref[rows, :], wa_ref[...])
        u = u_ref[rows, :].astype(F32)
        before = up_ref if s == 0 else u_ref.at[r0 - edge:r0, :]
        after = un_ref if s == n_sub - 1 else u_ref.at[r0 + sub:r0 + sub + edge, :]
        prev_row = before[edge - 1:edge, :].astype(F32)
        next_row = after[0:1, :].astype(F32)
        t = (i * tm + r0 + r) % seq_len
        u_prev = jnp.where(r == 0, prev_row, pltpu.roll(u, 1, axis=0))
        u_next = jnp.where(r == sub - 1, next_row, pltpu.roll(u, sub - 1, axis=0))
        u_prev = jnp.where(t == 0, 0.0, u_prev)
        u_next = jnp.where(t == seq_len - 1, 0.0, u_next)
        conv = u_prev * cw[0:1] + u * cw[1:2] + u_next * cw[2:3]
        yc = _dot((cb_ref[rows, :].astype(F32) * conv).astype(BF16), wc_ref[...])
        xb = _mod_norm(h_ref[rows, :], mp_ref[...]).astype(BF16)
        merged = _sigmoid(_dot(xb, wg_ref[:, 0:D_MODEL])) * yr
        merged = merged + _sigmoid(_dot(xb, wg_ref[:, D_MODEL:2 * D_MODEL])) * ya
        merged = merged + _sigmoid(_dot(xb, wg_ref[:, 2 * D_MODEL:3 * D_MODEL])) * yc
        return merged.astype(BF16)

    def project(s, merged):
        rows = slice(s * sub, (s + 1) * sub)
        o_ref[rows, :] = h_ref[rows, :] + mp_ref[2:3, :] * _dot(merged, wo_ref[...])

    pending = branches(0)
    for s in range(1, n_sub):
        nxt = branches(s)
        project(s - 1, pending)
        pending = nxt
    project(n_sub - 1, pending)


def _merge(h, mp, rows_per_mod, ret_o, att_o, cb, u, conv_w8, w_gates, w_ret_o, w_att_o, w_conv_o, w_o,
           l, seq_len):
    n = h.shape[0]
    tm = MERGE_TM
    hb = tm // BF16_SUBLANES
    last = n // BF16_SUBLANES - 1
    row = lambda i: (i, 0)
    wspec = lambda k: _resident((None, k, D_MODEL), lambda i: (l, 0, 0))
    return pl.pallas_call(
        functools.partial(_merge_kernel, seq_len=seq_len),
        grid=(n // tm,),
        in_specs=[
            pl.BlockSpec((tm, D_MODEL), row),
            pl.BlockSpec((None, 8, D_MODEL), lambda i: ((i * tm) // rows_per_mod, 0, 0)),
            pl.BlockSpec((tm, RET_W), row),
            pl.BlockSpec((tm, ATT_Q_W), row),
            pl.BlockSpec((tm, CONV_DIM), row),
            pl.BlockSpec((tm, CONV_DIM), row),
            pl.BlockSpec((BF16_SUBLANES, CONV_DIM), lambda i: (jnp.maximum(i * hb - 1, 0), 0)),
            pl.BlockSpec((BF16_SUBLANES, CONV_DIM), lambda i: (jnp.minimum((i + 1) * hb, last), 0)),
            pl.BlockSpec((None, 8, CONV_DIM), lambda i: (l, 0, 0)),
            _resident((None, D_MODEL, 3 * D_MODEL), lambda i: (l, 0, 0)),
            wspec(RET_W), wspec(ATT_Q_W), wspec(CONV_DIM), wspec(D_MODEL),
        ],
        out_specs=pl.BlockSpec((tm, D_MODEL), row),
        out_shape=jax.ShapeDtypeStruct((n, D_MODEL), F32),
        compiler_params=pltpu.CompilerParams(
            dimension_semantics=("arbitrary",), vmem_limit_bytes=VMEM_LIMIT),
        name="merge",
    )(h, mp, ret_o, att_o, cb, u, u, u, conv_w8, w_gates, w_ret_o, w_att_o, w_conv_o, w_o)


def _rope_tables(n_tok):
    rows = n_tok // GRID_W
    t_row = np.repeat(np.arange(rows, dtype=np.float64), GRID_W)
    t_col = np.tile(np.arange(GRID_W, dtype=np.float64), rows)
    n_freq = HEAD_DIM // 4
    inv = ROPE_THETA ** (-np.arange(n_freq, dtype=np.float64) / n_freq)
    ang = np.concatenate([t_row[:, None] * inv, t_col[:, None] * inv], axis=-1)
    cos, sin = np.cos(ang), np.sin(ang)
    cos64 = np.concatenate([cos, cos], axis=-1)
    sin64 = np.concatenate([-sin, sin], axis=-1)
    return (jnp.asarray(np.tile(cos64, (1, LANES // HEAD_DIM)), F32),
            jnp.asarray(np.tile(sin64, (1, LANES // HEAD_DIM)), F32))


def _mod_pack(mod_l, rows, sub, norm_w_row):
    r0 = rows[0]
    nr = len(rows)
    m = mod_l[r0:r0 + nr, 3 * sub * D_MODEL:3 * (sub + 1) * D_MODEL].reshape(nr, 3, D_MODEL)
    nw = jnp.broadcast_to(norm_w_row[None, None, :], (nr, 1, D_MODEL))
    pad = jnp.zeros((nr, 4, D_MODEL), F32)
    return jnp.concatenate([m, nw, pad], axis=1)


def kernel(x_prompt, x_sample, c, state_ret, cache_k, cache_v, c_ctx, w_ada, b_ada, norm_w, w_ffn_in,
           w_ffn_out, w_in, ret_decay_logit, ret_gn, q_gain, k_gain, conv_w, w_ret_o, w_att_o, w_conv_o,
           w_o):
    n_ctx_b, ctx_len, _ = x_prompt.shape
    n_lat_b, lat_len, _ = x_sample.shape

    w_ffn_in_b = w_ffn_in.astype(BF16)
    w_ffn_out_b = w_ffn_out.astype(BF16)
    w_in_b = w_in[:, :, :OFF_GATES].astype(BF16)
    w_gates_b = w_in[:, :, OFF_GATES:].astype(BF16)
    w_ret_o_b = w_ret_o.astype(BF16)
    w_att_o_b = w_att_o.astype(BF16)
    w_conv_o_b = w_conv_o.astype(BF16)
    w_o_b = w_o.astype(BF16)

    q_gain_t = jnp.tile(q_gain, (1, N_HEADS)).reshape(DEPTH, 1, ATT_Q_W)
    k_gain_t = jnp.tile(k_gain, (1, N_KV_HEADS)).reshape(DEPTH, 1, ATT_KV_W)
    conv_w8 = jnp.pad(conv_w, ((0, 0), (0, 8 - conv_w.shape[1]), (0, 0)))
    gid = np.arange(ATT_Q_W) // HEAD_DIM
    bd = jnp.asarray(gid[:, None] == gid[None, :], BF16)
    rope = _rope_tables(lat_len)

    cond8 = jnp.zeros((8, D_MODEL), F32).at[0].set(c_ctx).at[1:1 + n_lat_b].set(c)
    mod = _ada(cond8, w_ada, b_ada)

    ck = cache_k.astype(BF16)
    ck = jnp.broadcast_to(ck[:, :, :, :, None, :], ck.shape[:4] + (GROUP, HEAD_DIM))
    cache_kt = ck.reshape(ck.shape[0], ck.shape[1], ck.shape[2], N_KV_HEADS * KV_LANES)
    cache_vT = jnp.transpose(cache_v.astype(BF16), (0, 1, 3, 4, 2)).reshape(
        cache_v.shape[0], cache_v.shape[1], ATT_KV_W, cache_v.shape[2])

    groups = (
        dict(x=x_prompt.reshape(n_ctx_b * ctx_len, D_MODEL), rows=[0], seq=ctx_len, ctx=True),
        dict(x=x_sample.reshape(n_lat_b * lat_len, D_MODEL), rows=list(range(1, 1 + n_lat_b)),
             seq=lat_len, ctx=False),
    )
    results = []
    for grp in groups:
        h = grp["x"]
        seq = grp["seq"]
        is_ctx = grp["ctx"]
        rpm = h.shape[0] // len(grp["rows"])
        states, keys, values = [], [], []
        for l in range(DEPTH):
            mp = [_mod_pack(mod[l], grp["rows"], s, norm_w[l, s]) for s in range(3)]
            h = _ffn(h, mp[0], rpm, w_ffn_in_b, w_ffn_out_b, l, 0)
            outs = _inproj(h, mp[1], rpm, w_in_b, l, q_gain_t, k_gain_t, bd,
                           None if is_ctx else rope, seq, emit_kv=is_ctx)
            ret_in, qn, kt, vT, cb, u = outs[:6]
            if is_ctx:
                ret_o, st = _retention(ret_in, ret_decay_logit[l], ret_gn[l][None, :], None, seq,
                                       n_seq_blk=16, emit_state=True)
                states.append(st)
                keys.append(outs[6].reshape(n_ctx_b, seq, N_KV_HEADS, HEAD_DIM))
                values.append(outs[7].reshape(n_ctx_b, seq, N_KV_HEADS, HEAD_DIM))
                att_o = _attention(qn, kt, vT, None, seq)
            else:
                (ret_o,) = _retention(ret_in, ret_decay_logit[l], ret_gn[l][None, :], state_ret[:, l],
                                      seq, n_seq_blk=1, emit_state=False)
                att_o = _attention(qn, kt, vT, (cache_kt[:, l], cache_vT[:, l]), seq)
            h = _merge(h, mp[1], rpm, ret_o, att_o, cb, u, conv_w8, w_gates_b, w_ret_o_b, w_att_o_b,
                       w_conv_o_b, w_o_b, l, seq)
            h = _ffn(h, mp[2], rpm, w_ffn_in_b, w_ffn_out_b, l, 1)
        results.append((h, states, keys, values))

    (y_ctx, states, keys, values), (y_lat, _, _, _) = results
    y_prompt = y_ctx.reshape(x_prompt.shape)
    y_sample = y_lat.reshape(x_sample.shape)
    new_state_ret = jnp.stack(states, axis=1)
    new_cache_k = jnp.stack(keys, axis=1)
    new_cache_v = jnp.stack(values, axis=1)
    return (y_prompt, y_sample, new_state_ret, new_cache_k, new_cache_v)
```

```python
import functools
import math

import numpy as np
import jax
import jax.numpy as jnp
from jax import lax
from jax.experimental import pallas as pl
from jax.experimental.pallas import tpu as pltpu

D_MODEL = 1024
DEPTH = 2
GRID_W = 64
N_RET_HEADS = 4
RET_DK = 128
RET_DV = 128
RET_CHUNK = 128
N_HEADS = 8
N_KV_HEADS = 2
HEAD_DIM = 64
ROPE_THETA = 10000.0
CONV_DIM = 512
FFN_DIM = 2816
N_MOD = 9
EPS = 1e-6
RET_W = N_RET_HEADS * RET_DK
ATT_Q_W = N_HEADS * HEAD_DIM
ATT_KV_W = N_KV_HEADS * HEAD_DIM
GROUP = N_HEADS // N_KV_HEADS
KV_LANES = GROUP * HEAD_DIM
OFF_RQ, OFF_RK, OFF_RV, OFF_RG = 0, 512, 1024, 1536
OFF_AQ, OFF_AK, OFF_AV = 2048, 2560, 2688
OFF_CB, OFF_CC, OFF_CX = 2816, 3328, 3840
OFF_GATES = 4352

LANES = 128
BF16_SUBLANES = 16
VMEM_LIMIT = 56 * 1024 * 1024

F32 = jnp.float32
BF16 = jnp.bfloat16


def _dot(a, b):
    return jnp.dot(a, b, preferred_element_type=F32)


def _dot_nt(a, b):
    return lax.dot_general(a, b, (((1,), (1,)), ((), ())), preferred_element_type=F32)


def _sigmoid(x):
    return 1.0 / (1.0 + jnp.exp(-x))


def _silu(x):
    return x * _sigmoid(x)


def _mod_norm(x, mp):
    ms = jnp.mean(x * x, axis=-1, keepdims=True)
    y = x * lax.rsqrt(ms + EPS)
    return (y * mp[3:4]) * (1.0 + mp[1:2]) + mp[0:1]


def _resident(shape, index):
    return pl.BlockSpec(shape, index, pipeline_mode=pl.Buffered(1))


ADA_TN = 2304


def _ada_kernel(c_ref, w_ref, b_ref, o_ref):
    a = _silu(c_ref[...]).astype(BF16)
    o_ref[...] = _dot(a, w_ref[...].astype(BF16)) + b_ref[...]


def _ada(cond8, w_ada, b_ada):
    n = N_MOD * D_MODEL
    return pl.pallas_call(
        _ada_kernel,
        grid=(DEPTH, n // ADA_TN),
        in_specs=[
            pl.BlockSpec((8, D_MODEL), lambda l, j: (0, 0)),
            pl.BlockSpec((None, D_MODEL, ADA_TN), lambda l, j: (l, 0, j)),
            pl.BlockSpec((None, 1, ADA_TN), lambda l, j: (l, 0, j)),
        ],
        out_specs=pl.BlockSpec((None, 8, ADA_TN), lambda l, j: (l, 0, j)),
        out_shape=jax.ShapeDtypeStruct((DEPTH, 8, n), F32),
        compiler_params=pltpu.CompilerParams(
            dimension_semantics=("arbitrary", "arbitrary"), vmem_limit_bytes=VMEM_LIMIT),
        name="ada",
    )(cond8, w_ada, b_ada.reshape(DEPTH, 1, n))


FFN_TM = 1024
FFN_FC = 256


def _ffn_kernel(x_ref, mp_ref, wi_ref, wo_ref, o_ref, h_ref):
    x = x_ref[...]
    mp = mp_ref[...]
    xb = _mod_norm(x, mp).astype(BF16)
    for c in range(FFN_DIM // FFN_FC):
        lo = c * FFN_FC
        g = _dot(xb, wi_ref[:, lo:lo + FFN_FC])
        u = _dot(xb, wi_ref[:, FFN_DIM + lo:FFN_DIM + lo + FFN_FC])
        h_ref[:, lo:lo + FFN_FC] = (_silu(g) * u).astype(BF16)
    y = _dot(h_ref[...], wo_ref[...])
    o_ref[...] = x + (0.5 * mp[2:3]) * y


def _ffn(x, mp, rows_per_mod, w_in, w_out, l, j):
    n = x.shape[0]
    tm = FFN_TM
    return pl.pallas_call(
        _ffn_kernel,
        grid=(n // tm,),
        in_specs=[
            pl.BlockSpec((tm, D_MODEL), lambda i: (i, 0)),
            pl.BlockSpec((None, 8, D_MODEL), lambda i: ((i * tm) // rows_per_mod, 0, 0)),
            _resident((None, None, D_MODEL, 2 * FFN_DIM), lambda i: (l, j, 0, 0)),
            _resident((None, None, FFN_DIM, D_MODEL), lambda i: (l, j, 0, 0)),
        ],
        out_specs=pl.BlockSpec((tm, D_MODEL), lambda i: (i, 0)),
        out_shape=jax.ShapeDtypeStruct((n, D_MODEL), F32),
        scratch_shapes=[pltpu.VMEM((tm, FFN_DIM), BF16)],
        compiler_params=pltpu.CompilerParams(
            dimension_semantics=("arbitrary",), vmem_limit_bytes=VMEM_LIMIT),
        name="ffn",
    )(x, mp, w_in, w_out)


INPROJ_TM = 1024
INPROJ_SUB = 512
Q_SCALE = HEAD_DIM ** -0.5 * math.log2(math.e)


def _group_sumsq(a, bd):
    sq = a * a
    hi = sq.astype(BF16)
    lo = (sq - hi.astype(F32)).astype(BF16)
    return _dot(hi, bd) + _dot(lo, bd)


def _group_rms(a, sumsq, gain):
    return (a * lax.rsqrt(sumsq * (1.0 / HEAD_DIM) + EPS)) * gain


def _rope(x, cos, sin_signed):
    half = HEAD_DIM // 2
    reps = x.shape[1] // LANES
    lane = lax.broadcasted_iota(jnp.int32, (x.shape[0], LANES), 1)
    first = (lane % HEAD_DIM) < half
    out = []
    for r in range(reps):
        xs = x[:, r * LANES:(r + 1) * LANES]
        partner = jnp.where(first, pltpu.roll(xs, LANES - half, axis=1), pltpu.roll(xs, half, axis=1))
        out.append(xs * cos + partner * sin_signed)
    return out[0] if reps == 1 else jnp.concatenate(out, axis=1)


def _tile_kv(a):
    lane = lax.broadcasted_iota(jnp.int32, a.shape, 1)
    sw = pltpu.roll(a, HEAD_DIM, axis=1)
    h0 = jnp.where(lane < HEAD_DIM, a, sw)
    h1 = jnp.where(lane < HEAD_DIM, sw, a)
    return jnp.concatenate([h0, h0, h1, h1], axis=1)


def _inproj_kernel(*refs, use_rope, emit_kv):
    x_ref, mp_ref, w_ref, qg_ref, kg_ref, bd_ref = refs[:6]
    pos = 6
    if use_rope:
        cos_ref, sin_ref = refs[pos:pos + 2]
        pos += 2
    ret_ref, q_ref, kt_ref, vT_ref, cb_ref, u_ref = refs[pos:pos + 6]
    pos += 6
    if emit_kv:
        kout_ref, vout_ref = refs[pos:pos + 2]

    mp = mp_ref[...]
    tm = x_ref.shape[0]
    n_sub = tm // INPROJ_SUB
    xbs = [None] * n_sub
    xbs[0] = _mod_norm(x_ref[0:INPROJ_SUB, :], mp).astype(BF16)
    for s in range(n_sub):
        rows = slice(s * INPROJ_SUB, (s + 1) * INPROJ_SUB)
        xb = xbs[s]

        def proj(off, width):
            return _dot(xb, w_ref[:, off:off + width])

        aq = proj(OFF_AQ, ATT_Q_W)
        ret_ref[rows, OFF_RQ:OFF_RQ + RET_W] = proj(OFF_RQ, RET_W).astype(BF16)
        if s + 1 < n_sub:
            nxt = slice((s + 1) * INPROJ_SUB, (s + 2) * INPROJ_SUB)
            xbs[s + 1] = _mod_norm(x_ref[nxt, :], mp).astype(BF16)
        ss_q = _group_sumsq(aq, bd_ref[...])
        ak = proj(OFF_AK, ATT_KV_W)
        v = proj(OFF_AV, ATT_KV_W)
        ret_ref[rows, OFF_RK:OFF_RK + RET_W] = (proj(OFF_RK, RET_W) * (RET_DK ** -0.5)).astype(BF16)

        q = _group_rms(aq, ss_q, qg_ref[...])
        if use_rope:
            q = _rope(q, cos_ref[rows, :], sin_ref[rows, :])
        q_ref[rows, :] = (q * Q_SCALE).astype(BF16)
        ss_k = _group_sumsq(ak, bd_ref[0:ATT_KV_W, 0:ATT_KV_W])
        ret_ref[rows, OFF_RV:OFF_RV + RET_W] = proj(OFF_RV, RET_W).astype(BF16)

        k = _group_rms(ak, ss_k, kg_ref[...])
        if emit_kv:
            kout_ref[rows, :] = k
            vout_ref[rows, :] = v
        if use_rope:
            k = _rope(k, cos_ref[rows, :], sin_ref[rows, :])
        kt_ref[rows, :] = _tile_kv(k).astype(BF16)
        vT_ref[:, rows] = v.T.astype(BF16)

        ret_ref[rows, OFF_RG:OFF_RG + RET_W] = _silu(proj(OFF_RG, RET_W)).astype(BF16)
        cb_ref[rows, :] = proj(OFF_CB, CONV_DIM).astype(BF16)
        u_ref[rows, :] = (proj(OFF_CC, CONV_DIM) * proj(OFF_CX, CONV_DIM)).astype(BF16)


def _inproj(x, mp, rows_per_mod, w_in, l, q_gain_t, k_gain_t, bd, rope, seq_len, emit_kv):
    n = x.shape[0]
    tm = INPROJ_TM
    use_rope = rope is not None
    row = lambda i: (i, 0)
    in_specs = [
        pl.BlockSpec((tm, D_MODEL), row),
        pl.BlockSpec((None, 8, D_MODEL), lambda i: ((i * tm) // rows_per_mod, 0, 0)),
        _resident((None, D_MODEL, OFF_GATES), lambda i: (l, 0, 0)),
        pl.BlockSpec((None, 1, ATT_Q_W), lambda i: (l, 0, 0)),
        pl.BlockSpec((None, 1, ATT_KV_W), lambda i: (l, 0, 0)),
        _resident((ATT_Q_W, ATT_Q_W), lambda i: (0, 0)),
    ]
    args = [x, mp, w_in, q_gain_t, k_gain_t, bd]
    if use_rope:
        tiles_per_seq = seq_len // tm
        in_specs += [pl.BlockSpec((tm, LANES), lambda i: (i % tiles_per_seq, 0))] * 2
        args += list(rope)
    widths = [4 * RET_W, ATT_Q_W, ATT_Q_W, None, CONV_DIM, CONV_DIM]
    out_specs = [pl.BlockSpec((tm, w), row) for w in widths if w]
    out_shape = [jax.ShapeDtypeStruct((n, w), BF16) for w in widths if w]
    out_specs.insert(3, pl.BlockSpec((ATT_KV_W, tm), lambda i: (0, i)))
    out_shape.insert(3, jax.ShapeDtypeStruct((ATT_KV_W, n), BF16))
    if emit_kv:
        out_specs += [pl.BlockSpec((tm, ATT_KV_W), row)] * 2
        out_shape += [jax.ShapeDtypeStruct((n, ATT_KV_W), F32)] * 2
    return pl.pallas_call(
        functools.partial(_inproj_kernel, use_rope=use_rope, emit_kv=emit_kv),
        grid=(n // tm,),
        in_specs=in_specs,
        out_specs=out_specs,
        out_shape=out_shape,
        compiler_params=pltpu.CompilerParams(
            dimension_semantics=("arbitrary",), vmem_limit_bytes=VMEM_LIMIT),
        name="inproj",
    )(*args)


RET_GROUP = 8
RET_SCAN_UNROLL = 2
TAB_DMASK, TAB_DQF, TAB_DQB, TAB_DKF, TAB_DKB, TAB_DCF, TAB_DCB, N_TAB = range(8)


def _log_sigmoid(x):
    return jnp.minimum(x, 0.0) - jnp.log(1.0 + jnp.exp(-jnp.abs(x)))


def _chunk_rows(start):
    if isinstance(start, int):
        return pl.ds(start, RET_CHUNK)
    return pl.ds(pl.multiple_of(start, RET_CHUNK), RET_CHUNK)


def _retention_kernel(*refs, n_seq, n_chunks, has_s0, emit_state):
    dl_ref, q_ref, k_ref, v_ref, g_ref, gn_ref = refs[:6]
    pos = 6
    if has_s0:
        s0_ref = refs[pos]
        pos += 1
    o_ref = refs[pos]
    pos += 1
    if emit_state:
        st_ref = refs[pos]
        pos += 1
    tab_ref, kv_ref, ent_ref = refs[pos:pos + 3]

    C = RET_CHUNK
    hd = pl.program_id(1)
    row = lax.broadcasted_iota(jnp.int32, (C, C), 0).astype(F32)
    col = lax.broadcasted_iota(jnp.int32, (C, C), 1).astype(F32)
    diff = row - col
    lgf = _log_sigmoid(jnp.full((C, C), dl_ref[0, hd], F32))
    lgb = _log_sigmoid(jnp.full((C, C), dl_ref[1, hd], F32))
    tab_ref[TAB_DMASK] = (jnp.where(diff >= 0, jnp.exp(lgf * jnp.maximum(diff, 0.0)), 0.0)
                          + jnp.where(diff <= 0, jnp.exp(lgb * jnp.maximum(-diff, 0.0)), 0.0))
    tab_ref[TAB_DQF] = jnp.exp(lgf * (row + 1.0))
    tab_ref[TAB_DQB] = jnp.exp(lgb * (C - row))
    tab_ref[TAB_DKF] = jnp.exp(lgf * (C - 1.0 - row))
    tab_ref[TAB_DKB] = jnp.exp(lgb * row)
    tab_ref[TAB_DCF] = jnp.exp(lgf * C)
    tab_ref[TAB_DCB] = jnp.exp(lgb * C)

    def kv_phase(items):
        ops = []
        for r, _ in items:
            rows = _chunk_rows(r)
            k = k_ref[rows, :].astype(F32)
            vT = v_ref[rows, :].astype(F32).T.astype(BF16)
            kk = jnp.concatenate([(k * tab_ref[TAB_DKF]).astype(BF16),
                                  (k * tab_ref[TAB_DKB]).astype(BF16)], axis=1)
            ops.append((vT, kk))
        for (_, i), (vT, kk) in zip(items, ops):
            kv_ref[i] = _dot(vT, kk)

    def scan_phase(s, sc):
        if has_s0:
            st0 = (s0_ref[0].T, s0_ref[1].T)
        else:
            st0 = (jnp.zeros((RET_DV, RET_DK), F32), jnp.zeros((RET_DV, RET_DK), F32))

        def scan_step(t, st):
            st_f, st_b = st
            cb = n_chunks - 1 - t
            ent_ref[sc + t, :, 0:RET_DK] = st_f.astype(BF16)
            ent_ref[sc + cb, :, RET_DK:2 * RET_DK] = st_b.astype(BF16)
            st_f = st_f * tab_ref[TAB_DCF] + kv_ref[sc + t, :, 0:RET_DK]
            st_b = st_b * tab_ref[TAB_DCB] + kv_ref[sc + cb, :, RET_DK:2 * RET_DK]
            return st_f, st_b

        if n_chunks <= RET_SCAN_UNROLL:
            st = st0
            for t in range(n_chunks):
                st = scan_step(t, st)
        else:
            st = lax.fori_loop(0, n_chunks, scan_step, st0)
        if emit_state:
            st_ref[s, 0] = st[0].T
            st_ref[s, 1] = st[1].T

    def score_stage(items):
        qs = [q_ref[_chunk_rows(r), :] for r, _ in items]
        att = [_dot_nt(q, k_ref[_chunk_rows(r), :]) for q, (r, _) in zip(qs, items)]
        att = [(a * tab_ref[TAB_DMASK]).astype(BF16) for a in att]
        qq = []
        for q in qs:
            qf = q.astype(F32)
            qq.append(jnp.concatenate([(qf * tab_ref[TAB_DQF]).astype(BF16),
                                       (qf * tab_ref[TAB_DQB]).astype(BF16)], axis=1))
        return att, qq

    def product_stage(items, att, qq):
        return [_dot(a, v_ref[_chunk_rows(r), :]) + _dot_nt(x, ent_ref[i])
                for a, x, (r, i) in zip(att, qq, items)]

    def norm_stage(items, outs):
        for o, (r, _) in zip(outs, items):
            rows = _chunk_rows(r)
            ms = jnp.mean(o * o, axis=-1, keepdims=True)
            on = (o * lax.rsqrt(ms + EPS)) * gn_ref[...]
            o_ref[rows, :] = (g_ref[rows, :].astype(F32) * on).astype(BF16)

    items = [(s * n_chunks * C + c * C, s * n_chunks + c) for s in range(n_seq) for c in range(n_chunks)]
    assert len(items) % RET_GROUP == 0
    groups = [items[i:i + RET_GROUP] for i in range(0, len(items), RET_GROUP)]
    for grp in groups:
        kv_phase(grp)
    for s in range(n_seq):
        scan_phase(s, s * n_chunks)
    scored, products = {}, {}
    for t in range(len(groups) + 2):
        if t < len(groups):
            scored[t] = score_stage(groups[t])
        if 0 <= t - 1 < len(groups):
            products[t - 1] = product_stage(groups[t - 1], *scored.pop(t - 1))
        if 0 <= t - 2 < len(groups):
            norm_stage(groups[t - 2], products.pop(t - 2))


def _retention(ret_in, decay_logit_l, gn_l, s0, seq_len, n_seq_blk, emit_state):
    n = ret_in.shape[0]
    n_chunks = seq_len // RET_CHUNK
    tb = n_seq_blk * seq_len
    has_s0 = s0 is not None
    n_off = RET_W // RET_DK

    def branch(k):
        return pl.BlockSpec((tb, RET_DK), lambda i, h: (i, k * n_off + h))

    in_specs = [pl.BlockSpec(memory_space=pltpu.SMEM), branch(0), branch(1), branch(2), branch(3),
                pl.BlockSpec((1, RET_DV), lambda i, h: (0, h))]
    args = [decay_logit_l, ret_in, ret_in, ret_in, ret_in, gn_l]
    if has_s0:
        assert n_seq_blk == 1
        in_specs.append(pl.BlockSpec((None, 2, None, RET_DK, RET_DV), lambda i, h: (i, 0, h, 0, 0)))
        args.append(s0)
    out_specs = [pl.BlockSpec((tb, RET_DV), lambda i, h: (i, h))]
    out_shape = [jax.ShapeDtypeStruct((n, RET_W), BF16)]
    if emit_state:
        out_specs.append(pl.BlockSpec((n_seq_blk, 2, None, RET_DK, RET_DV), lambda i, h: (i, 0, h, 0, 0)))
        out_shape.append(jax.ShapeDtypeStruct((n // seq_len, 2, N_RET_HEADS, RET_DK, RET_DV), F32))
    n_slots = n_chunks * n_seq_blk
    return pl.pallas_call(
        functools.partial(_retention_kernel, n_seq=n_seq_blk, n_chunks=n_chunks, has_s0=has_s0,
                          emit_state=emit_state),
        grid=(n // tb, N_RET_HEADS),
        in_specs=in_specs,
        out_specs=out_specs,
        out_shape=out_shape,
        scratch_shapes=[
            pltpu.VMEM((N_TAB, RET_CHUNK, RET_CHUNK), F32),
            pltpu.VMEM((n_slots, RET_DV, 2 * RET_DK), F32),
            pltpu.VMEM((n_slots, RET_DV, 2 * RET_DK), BF16),
        ],
        compiler_params=pltpu.CompilerParams(
            dimension_semantics=("arbitrary", "arbitrary"), vmem_limit_bytes=VMEM_LIMIT),
        name="retention",
    )(*args)


ATT_TQ = 1024
ATT_KB = 512
ATT_SHORT_MAX = 512
ATT_RC = 16
NEG_BIG = -1e30
SUM_ROWS = BF16_SUBLANES


def _attention_kernel(*refs, n_cache, n_new):
    q_ref = refs[0]
    pos = 1
    if n_cache:
        kc_ref, vc_ref = refs[pos:pos + 2]
        pos += 2
    kn_ref, vn_ref, o_ref, qs_ref, acc_ref = refs[pos:pos + 5]
    s_slots = refs[pos + 5:pos + 7]
    p_slots = refs[pos + 7:pos + 9]

    tq = q_ref.shape[0]
    nq = GROUP * tq
    qf = q_ref[...].astype(F32)
    head = lax.broadcasted_iota(jnp.int32, (tq, KV_LANES), 1) // HEAD_DIM
    for g in range(GROUP):
        qs_ref[:, g * tq:(g + 1) * tq] = jnp.where(head == g, qf, 0.0).T.astype(BF16)
    acc_ref[...] = jnp.zeros(acc_ref.shape, F32)

    kb = min(ATT_KB, n_new)
    n_cb = 1 if n_cache else 0
    n_nb = n_new // kb
    assert n_new % kb == 0
    n_blk = n_cb + n_nb

    def block(t):
        if isinstance(t, int) and t < n_cb:
            return kc_ref, vc_ref, pl.ds(0, n_cache)
        lo = (t - n_cb) * kb
        return kn_ref, vn_ref, (pl.ds(lo, kb) if isinstance(lo, int) else pl.ds(pl.multiple_of(lo, kb), kb))

    def scores(t, par):
        k_ref, _, keys = block(t)
        s = _dot(k_ref[keys, :], qs_ref[...])
        s_slots[par][0:keys.size, :] = s
        return jnp.max(s, axis=0, keepdims=True)

    def softmax(n_keys, par, m, s_max):
        s_ref, p_ref = s_slots[par], p_slots[par]
        m_new = jnp.maximum(m, s_max)
        m_rows = jnp.broadcast_to(m_new, (ATT_RC, nq))
        for r in range(0, n_keys, ATT_RC):
            p_ref[r:r + ATT_RC, :] = jnp.exp2(s_ref[r:r + ATT_RC, :] - m_rows).astype(BF16)
        return m_new, jnp.exp2(m - m_new)

    def pv(t, par, alpha):
        _, vT_ref, keys = block(t)
        v1 = jnp.concatenate([vT_ref[:, keys], jnp.ones((SUM_ROWS, keys.size), BF16)], axis=0)
        acc_ref[...] = alpha * acc_ref[...] + _dot(v1, p_slots[par][0:keys.size, :])

    def stage(t, par, carry):
        m, alpha, s_max = carry
        next_max = s_max
        if not isinstance(t, int) or t + 1 < n_blk:
            next_max = scores(t + 1, 1 - par)
        if not isinstance(t, int) or t >= 1:
            pv(t - 1, 1 - par, alpha)
        return softmax(block(t)[2].size, par, m, s_max) + (next_max,)

    carry = (jnp.full((1, nq), NEG_BIG, F32), jnp.zeros((1, nq), F32), scores(0, 0))
    lo_t = min(n_cb + 1, n_blk)
    n_loop = max(n_blk - 1 - lo_t, 0)
    if n_loop % 2:
        lo_t += 1
        n_loop -= 1
    for t in range(lo_t):
        carry = stage(t, t % 2, carry)

    def pair(i, carry):
        t = lo_t + 2 * i
        carry = stage(t, lo_t % 2, carry)
        return stage(t + 1, (lo_t + 1) % 2, carry)

    carry = lax.fori_loop(0, n_loop // 2, pair, carry)
    for t in range(lo_t + n_loop, n_blk):
        carry = stage(t, t % 2, carry)
    pv(n_blk - 1, (n_blk - 1) % 2, carry[1])

    acc = acc_ref[...]
    oT = acc[0:HEAD_DIM] / acc[HEAD_DIM:HEAD_DIM + 1]
    o4 = jnp.concatenate([oT[:, g * tq:(g + 1) * tq] for g in range(GROUP)], axis=0)
    o_ref[...] = o4.T.astype(BF16)


ATT_SEQ_PER_STEP = 4


def _attention_short_kernel(q_ref, k_ref, vT_ref, o_ref, *, n_seq):
    t_len = q_ref.shape[0] // n_seq
    head = lax.broadcasted_iota(jnp.int32, (t_len, KV_LANES), 1) // HEAD_DIM
    ones = jnp.ones((SUM_ROWS, t_len), BF16)
    items = [(slice(s * t_len, (s + 1) * t_len), kv) for s in range(n_seq) for kv in range(N_KV_HEADS)]

    def cols(kv):
        return slice(kv * KV_LANES, (kv + 1) * KV_LANES)

    def score_stage(item):
        rows, kv = item
        qf = q_ref[rows, cols(kv)].astype(F32)
        qsT = jnp.concatenate([jnp.where(head == g, qf, 0.0).T.astype(BF16) for g in range(GROUP)],
                              axis=1)
        return _dot(k_ref[rows, cols(kv)], qsT)

    def softmax_stage(sT):
        return jnp.exp2(sT - jnp.max(sT, axis=0, keepdims=True)).astype(BF16)

    def value_stage(item, pT):
        rows, kv = item
        v1 = jnp.concatenate([vT_ref[kv * HEAD_DIM:(kv + 1) * HEAD_DIM, rows], ones], axis=0)
        return _dot(v1, pT)

    def out_stage(item, acc):
        rows, kv = item
        oT = acc[0:HEAD_DIM] / acc[HEAD_DIM:HEAD_DIM + 1]
        o4 = jnp.concatenate([oT[:, g * t_len:(g + 1) * t_len] for g in range(GROUP)], axis=0)
        o_ref[rows, cols(kv)] = o4.T.astype(BF16)

    n = len(items)
    scores, probs, accs = {}, {}, {}
    for t in range(n + 3):
        if t < n:
            scores[t] = score_stage(items[t])
        if 0 <= t - 1 < n:
            probs[t - 1] = softmax_stage(scores.pop(t - 1))
        if 0 <= t - 2 < n:
            accs[t - 2] = value_stage(items[t - 2], probs.pop(t - 2))
        if 0 <= t - 3 < n:
            out_stage(items[t - 3], accs.pop(t - 3))


def _attention_short(q, kt, vT, seq_len):
    n = q.shape[0]
    tb = ATT_SEQ_PER_STEP * seq_len
    return pl.pallas_call(
        functools.partial(_attention_short_kernel, n_seq=ATT_SEQ_PER_STEP),
        grid=(n // tb,),
        in_specs=[pl.BlockSpec((tb, ATT_Q_W), lambda i: (i, 0)),
                  pl.BlockSpec((tb, ATT_Q_W), lambda i: (i, 0)),
                  pl.BlockSpec((ATT_KV_W, tb), lambda i: (0, i))],
        out_specs=pl.BlockSpec((tb, ATT_Q_W), lambda i: (i, 0)),
        out_shape=jax.ShapeDtypeStruct((n, ATT_Q_W), BF16),
        compiler_params=pltpu.CompilerParams(
            dimension_semantics=("arbitrary",), vmem_limit_bytes=VMEM_LIMIT),
        name="attention_short",
    )(q, kt, vT)


def _attention(q, kt, vT, cache, seq_len):
    n = q.shape[0]
    n_batch = n // seq_len
    tq = ATT_TQ
    nq = seq_len // tq
    n_cache = 0 if cache is None else cache[0].shape[1]
    if n_cache == 0 and seq_len <= ATT_SHORT_MAX:
        return _attention_short(q, kt, vT, seq_len)
    blk_rows = max(min(ATT_KB, seq_len), n_cache)
    in_specs = [pl.BlockSpec((tq, KV_LANES), lambda b, kv, i: (b * nq + i, kv))]
    args = [q]
    if n_cache:
        in_specs += [pl.BlockSpec((None, n_cache, KV_LANES), lambda b, kv, i: (b, 0, kv)),
                     pl.BlockSpec((None, HEAD_DIM, n_cache), lambda b, kv, i: (b, kv, 0))]
        args += list(cache)
    in_specs += [pl.BlockSpec((seq_len, KV_LANES), lambda b, kv, i: (b, kv)),
                 pl.BlockSpec((HEAD_DIM, seq_len), lambda b, kv, i: (kv, b))]
    args += [kt, vT]
    return pl.pallas_call(
        functools.partial(_attention_kernel, n_cache=n_cache, n_new=seq_len),
        grid=(n_batch, N_KV_HEADS, nq),
        in_specs=in_specs,
        out_specs=pl.BlockSpec((tq, KV_LANES), lambda b, kv, i: (b * nq + i, kv)),
        out_shape=jax.ShapeDtypeStruct((n, ATT_Q_W), BF16),
        scratch_shapes=[
            pltpu.VMEM((KV_LANES, GROUP * tq), BF16),
            pltpu.VMEM((HEAD_DIM + SUM_ROWS, GROUP * tq), F32),
            pltpu.VMEM((blk_rows, GROUP * tq), F32),
            pltpu.VMEM((blk_rows, GROUP * tq), F32),
            pltpu.VMEM((blk_rows, GROUP * tq), BF16),
            pltpu.VMEM((blk_rows, GROUP * tq), BF16),
        ],
        compiler_params=pltpu.CompilerParams(
            dimension_semantics=("arbitrary", "arbitrary", "arbitrary"), vmem_limit_bytes=VMEM_LIMIT),
        name="attention",
    )(*args)


MERGE_TM = 1024
MERGE_SUB = 256


def _merge_kernel(h_ref, mp_ref, ro_ref, ao_ref, cb_ref, u_ref, up_ref, un_ref, cw_ref,
                  wg_ref, wr_ref, wa_ref, wc_ref, wo_ref, o_ref, *, seq_len):
    tm = h_ref.shape[0]
    sub = MERGE_SUB
    n_sub = tm // sub
    i = pl.program_id(0)
    cw = cw_ref[...]
    r = lax.broadcasted_iota(jnp.int32, (sub, CONV_DIM), 0)
    edge = BF16_SUBLANES

    def branches(s):
        r0 = s * sub
        rows = slice(r0, r0 + sub)
        yr = _dot(ro_ref[rows, :], wr_ref[...])
        ya = _dot(ao_ref[rows, :], wa_ref[...])
        u = u_ref[rows, :].astype(F32)
        before = up_ref if s == 0 else u_ref.at[r0 - edge:r0, :]
        after = un_ref if s == n_sub - 1 else u_ref.at[r0 + sub:r0 + sub + edge, :]
        prev_row = before[edge - 1:edge, :].astype(F32)
        next_row = after[0:1, :].astype(F32)
        t = (i * tm + r0 + r) % seq_len
        u_prev = jnp.where(r == 0, prev_row, pltpu.roll(u, 1, axis=0))
        u_next = jnp.where(r == sub - 1, next_row, pltpu.roll(u, sub - 1, axis=0))
        u_prev = jnp.where(t == 0, 0.0, u_prev)
        u_next = jnp.where(t == seq_len - 1, 0.0, u_next)
        conv = u_prev * cw[0:1] + u * cw[1:2] + u_next * cw[2:3]
        yc = _dot((cb_ref[rows, :].astype(F32) * conv).astype(BF16), wc_ref[...])
        xb = _mod_norm(h_ref[rows, :], mp_ref[...]).astype(BF16)
        merged = _sigmoid(_dot(xb, wg_ref[0, :,0:D_MODEL])) * yr
        merged = merged + _sigmoid(_dot(xb, wg_ref[0, :,D_MODEL:2 * D_MODEL])) * ya
        merged = merged + _sigmoid(_dot(xb, wg_ref[0, :,2 * D_MODEL:3 * D_MODEL])) * yc
        return merged.astype(BF16)

    def project(s, merged):
        rows = slice(s * sub, (s + 1) * sub)
        o_ref[rows, :] = h_ref[rows, :] + mp_ref[2:3, :] * _dot(merged, wo_ref[...])

    pending = branches(0)
    for s in range(1, n_sub):
        nxt = branches(s)
        project(s - 1, pending)
        pending = nxt
    project(n_sub - 1, pending)


def _merge(h, mp, rows_per_mod, ret_o, att_o, cb, u, conv_w8, w_gates, w_ret_o, w_att_o, w_conv_o, w_o,
           l, seq_len):
    n = h.shape[0]
    tm = MERGE_TM
    hb = tm // BF16_SUBLANES
    last = n // BF16_SUBLANES - 1
    row = lambda i: (i, 0)
    wspec = lambda k: _resident((None, k, D_MODEL), lambda i: (l, 0, 0))
    return pl.pallas_call(
        functools.partial(_merge_kernel, seq_len=seq_len),
        grid=(n // tm,),
        in_specs=[
            pl.BlockSpec((tm, D_MODEL), row),
            pl.BlockSpec((None, 8, D_MODEL), lambda i: ((i * tm) // rows_per_mod, 0, 0)),
            pl.BlockSpec((tm, RET_W), row),
            pl.BlockSpec((tm, ATT_Q_W), row),
            pl.BlockSpec((tm, CONV_DIM), row),
            pl.BlockSpec((tm, CONV_DIM), row),
            pl.BlockSpec((BF16_SUBLANES, CONV_DIM), lambda i: (jnp.maximum(i * hb - 1, 0), 0)),
            pl.BlockSpec((BF16_SUBLANES, CONV_DIM), lambda i: (jnp.minimum((i + 1) * hb, last), 0)),
            pl.BlockSpec((None, 8, CONV_DIM), lambda i: (l, 0, 0)),
            _resident((pl.Element(1), pl.Element(D_MODEL), pl.Element(3 * D_MODEL)),
                      lambda i: (l, 0, OFF_GATES)),
            wspec(RET_W), wspec(ATT_Q_W), wspec(CONV_DIM), wspec(D_MODEL),
        ],
        out_specs=pl.BlockSpec((tm, D_MODEL), row),
        out_shape=jax.ShapeDtypeStruct((n, D_MODEL), F32),
        compiler_params=pltpu.CompilerParams(
            dimension_semantics=("arbitrary",), vmem_limit_bytes=VMEM_LIMIT),
        name="merge",
    )(h, mp, ret_o, att_o, cb, u, u, u, conv_w8, w_gates, w_ret_o, w_att_o, w_conv_o, w_o)


def _rope_tables(n_tok):
    rows = n_tok // GRID_W
    t_row = np.repeat(np.arange(rows, dtype=np.float64), GRID_W)
    t_col = np.tile(np.arange(GRID_W, dtype=np.float64), rows)
    n_freq = HEAD_DIM // 4
    inv = ROPE_THETA ** (-np.arange(n_freq, dtype=np.float64) / n_freq)
    ang = np.concatenate([t_row[:, None] * inv, t_col[:, None] * inv], axis=-1)
    cos, sin = np.cos(ang), np.sin(ang)
    cos64 = np.concatenate([cos, cos], axis=-1)
    sin64 = np.concatenate([-sin, sin], axis=-1)
    return (jnp.asarray(np.tile(cos64, (1, LANES // HEAD_DIM)), F32),
            jnp.asarray(np.tile(sin64, (1, LANES // HEAD_DIM)), F32))


def _mod_pack(mod_l, rows, sub, norm_w_row):
    r0 = rows[0]
    nr = len(rows)
    m = mod_l[r0:r0 + nr, 3 * sub * D_MODEL:3 * (sub + 1) * D_MODEL].reshape(nr, 3, D_MODEL)
    nw = jnp.broadcast_to(norm_w_row[None, None, :], (nr, 1, D_MODEL))
    pad = jnp.zeros((nr, 4, D_MODEL), F32)
    return jnp.concatenate([m, nw, pad], axis=1)


def kernel(x_prompt, x_sample, c, state_ret, cache_k, cache_v, c_ctx, w_ada, b_ada, norm_w, w_ffn_in,
           w_ffn_out, w_in, ret_decay_logit, ret_gn, q_gain, k_gain, conv_w, w_ret_o, w_att_o, w_conv_o,
           w_o):
    n_ctx_b, ctx_len, _ = x_prompt.shape
    n_lat_b, lat_len, _ = x_sample.shape

    w_ffn_in_b = w_ffn_in.astype(BF16)
    w_ffn_out_b = w_ffn_out.astype(BF16)
    w_in_b = w_in.astype(BF16)
    w_gates_b = w_in_b
    w_ret_o_b = w_ret_o.astype(BF16)
    w_att_o_b = w_att_o.astype(BF16)
    w_conv_o_b = w_conv_o.astype(BF16)
    w_o_b = w_o.astype(BF16)

    q_gain_t = jnp.tile(q_gain, (1, N_HEADS)).reshape(DEPTH, 1, ATT_Q_W)
    k_gain_t = jnp.tile(k_gain, (1, N_KV_HEADS)).reshape(DEPTH, 1, ATT_KV_W)
    conv_w8 = jnp.pad(conv_w, ((0, 0), (0, 8 - conv_w.shape[1]), (0, 0)))
    gid = np.arange(ATT_Q_W) // HEAD_DIM
    bd = jnp.asarray(gid[:, None] == gid[None, :], BF16)
    rope = _rope_tables(lat_len)

    cond8 = jnp.zeros((8, D_MODEL), F32).at[0].set(c_ctx).at[1:1 + n_lat_b].set(c)
    mod = _ada(cond8, w_ada, b_ada)

    ck = cache_k.astype(BF16)
    ck = jnp.broadcast_to(ck[:, :, :, :, None, :], ck.shape[:4] + (GROUP, HEAD_DIM))
    cache_kt = ck.reshape(ck.shape[0], ck.shape[1], ck.shape[2], N_KV_HEADS * KV_LANES)
    cache_vT = jnp.transpose(cache_v.astype(BF16), (0, 1, 3, 4, 2)).reshape(
        cache_v.shape[0], cache_v.shape[1], ATT_KV_W, cache_v.shape[2])

    groups = (
        dict(x=x_prompt.reshape(n_ctx_b * ctx_len, D_MODEL), rows=[0], seq=ctx_len, ctx=True),
        dict(x=x_sample.reshape(n_lat_b * lat_len, D_MODEL), rows=list(range(1, 1 + n_lat_b)),
             seq=lat_len, ctx=False),
    )
    results = []
    for grp in groups:
        h = grp["x"]
        seq = grp["seq"]
        is_ctx = grp["ctx"]
        rpm = h.shape[0] // len(grp["rows"])
        states, keys, values = [], [], []
        for l in range(DEPTH):
            mp = [_mod_pack(mod[l], grp["rows"], s, norm_w[l, s]) for s in range(3)]
            h = _ffn(h, mp[0], rpm, w_ffn_in_b, w_ffn_out_b, l, 0)
            outs = _inproj(h, mp[1], rpm, w_in_b, l, q_gain_t, k_gain_t, bd,
                           None if is_ctx else rope, seq, emit_kv=is_ctx)
            ret_in, qn, kt, vT, cb, u = outs[:6]
            if is_ctx:
                ret_o, st = _retention(ret_in, ret_decay_logit[l], ret_gn[l][None, :], None, seq,
                                       n_seq_blk=16, emit_state=True)
                states.append(st)
                keys.append(outs[6].reshape(n_ctx_b, seq, N_KV_HEADS, HEAD_DIM))
                values.append(outs[7].reshape(n_ctx_b, seq, N_KV_HEADS, HEAD_DIM))
                att_o = _attention(qn, kt, vT, None, seq)
            else:
                (ret_o,) = _retention(ret_in, ret_decay_logit[l], ret_gn[l][None, :], state_ret[:, l],
                                      seq, n_seq_blk=1, emit_state=False)
                att_o = _attention(qn, kt, vT, (cache_kt[:, l], cache_vT[:, l]), seq)
            h = _merge(h, mp[1], rpm, ret_o, att_o, cb, u, conv_w8, w_gates_b, w_ret_o_b, w_att_o_b,
                       w_conv_o_b, w_o_b, l, seq)
            h = _ffn(h, mp[2], rpm, w_ffn_in_b, w_ffn_out_b, l, 1)
        results.append((h, states, keys, values))

    (y_ctx, states, keys, values), (y_lat, _, _, _) = results
    y_prompt = y_ctx.reshape(x_prompt.shape)
    y_sample = y_lat.reshape(x_sample.shape)
    new_state_ret = jnp.stack(states, axis=1)
    new_cache_k = jnp.stack(keys, axis=1)
    new_cache_v = jnp.stack(values, axis=1)
    return (y_prompt, y_sample, new_state_ret, new_cache_k, new_cache_v)
```

```python
import functools
import math

import numpy as np
import jax
import jax.numpy as jnp
from jax import lax
from jax.experimental import pallas as pl
from jax.experimental.pallas import tpu as pltpu

D_MODEL = 1024
DEPTH = 2
GRID_W = 64
N_RET_HEADS = 4
RET_DK = 128
RET_DV = 128
RET_CHUNK = 128
N_HEADS = 8
N_KV_HEADS = 2
HEAD_DIM = 64
ROPE_THETA = 10000.0
CONV_DIM = 512
FFN_DIM = 2816
N_MOD = 9
EPS = 1e-6
RET_W = N_RET_HEADS * RET_DK
ATT_Q_W = N_HEADS * HEAD_DIM
ATT_KV_W = N_KV_HEADS * HEAD_DIM
GROUP = N_HEADS // N_KV_HEADS
KV_LANES = GROUP * HEAD_DIM
OFF_RQ, OFF_RK, OFF_RV, OFF_RG = 0, 512, 1024, 1536
OFF_AQ, OFF_AK, OFF_AV = 2048, 2560, 2688
OFF_CB, OFF_CC, OFF_CX = 2816, 3328, 3840
OFF_GATES = 4352

LANES = 128
BF16_SUBLANES = 16
VMEM_LIMIT = 56 * 1024 * 1024

F32 = jnp.float32
BF16 = jnp.bfloat16


def _dot(a, b):
    return jnp.dot(a, b, preferred_element_type=F32)


def _dot_nt(a, b):
    return lax.dot_general(a, b, (((1,), (1,)), ((), ())), preferred_element_type=F32)


def _sigmoid(x):
    return 1.0 / (1.0 + jnp.exp(-x))


def _silu(x):
    return x * _sigmoid(x)


def _mod_norm(x, mp):
    ms = jnp.mean(x * x, axis=-1, keepdims=True)
    y = x * lax.rsqrt(ms + EPS)
    return (y * mp[3:4]) * (1.0 + mp[1:2]) + mp[0:1]


def _resident(shape, index):
    return pl.BlockSpec(shape, index, pipeline_mode=pl.Buffered(1))


ADA_TN = 2304


def _ada_kernel(c_ref, w_ref, b_ref, o_ref):
    a = _silu(c_ref[...]).astype(BF16)
    o_ref[...] = _dot(a, w_ref[...].astype(BF16)) + b_ref[...]


def _ada(cond8, w_ada, b_ada):
    n = N_MOD * D_MODEL
    return pl.pallas_call(
        _ada_kernel,
        grid=(DEPTH, n // ADA_TN),
        in_specs=[
            pl.BlockSpec((8, D_MODEL), lambda l, j: (0, 0)),
            pl.BlockSpec((None, D_MODEL, ADA_TN), lambda l, j: (l, 0, j)),
            pl.BlockSpec((None, 1, ADA_TN), lambda l, j: (l, 0, j)),
        ],
        out_specs=pl.BlockSpec((None, 8, ADA_TN), lambda l, j: (l, 0, j)),
        out_shape=jax.ShapeDtypeStruct((DEPTH, 8, n), F32),
        compiler_params=pltpu.CompilerParams(
            dimension_semantics=("arbitrary", "arbitrary"), vmem_limit_bytes=VMEM_LIMIT),
        name="ada",
    )(cond8, w_ada, b_ada.reshape(DEPTH, 1, n))


FFN_TM = 1024
FFN_FC = 256


def _ffn_tile(x, mp, wi_ref, wo_ref, h_ref):
    xb = _mod_norm(x, mp).astype(BF16)
    for c in range(FFN_DIM // FFN_FC):
        lo = c * FFN_FC
        g = _dot(xb, wi_ref[:, lo:lo + FFN_FC])
        u = _dot(xb, wi_ref[:, FFN_DIM + lo:FFN_DIM + lo + FFN_FC])
        h_ref[:, lo:lo + FFN_FC] = (_silu(g) * u).astype(BF16)
    y = _dot(h_ref[...], wo_ref[...])
    return x + (0.5 * mp[2:3]) * y


def _ffn_kernel(x_ref, mp_ref, wi_ref, wo_ref, o_ref, h_ref):
    o_ref[...] = _ffn_tile(x_ref[...], mp_ref[...], wi_ref, wo_ref, h_ref)


def _ffn(x, mp, rows_per_mod, w_in, w_out, l, j):
    n = x.shape[0]
    tm = FFN_TM
    return pl.pallas_call(
        _ffn_kernel,
        grid=(n // tm,),
        in_specs=[
            pl.BlockSpec((tm, D_MODEL), lambda i: (i, 0)),
            pl.BlockSpec((None, 8, D_MODEL), lambda i: ((i * tm) // rows_per_mod, 0, 0)),
            _resident((None, None, D_MODEL, 2 * FFN_DIM), lambda i: (l, j, 0, 0)),
            _resident((None, None, FFN_DIM, D_MODEL), lambda i: (l, j, 0, 0)),
        ],
        out_specs=pl.BlockSpec((tm, D_MODEL), lambda i: (i, 0)),
        out_shape=jax.ShapeDtypeStruct((n, D_MODEL), F32),
        scratch_shapes=[pltpu.VMEM((tm, FFN_DIM), BF16)],
        compiler_params=pltpu.CompilerParams(
            dimension_semantics=("arbitrary",), vmem_limit_bytes=VMEM_LIMIT),
        name="ffn",
    )(x, mp, w_in, w_out)


INPROJ_TM = 1024
INPROJ_SUB = 512
Q_SCALE = HEAD_DIM ** -0.5 * math.log2(math.e)


def _group_sumsq(a, bd):
    sq = a * a
    hi = sq.astype(BF16)
    lo = (sq - hi.astype(F32)).astype(BF16)
    return _dot(hi, bd) + _dot(lo, bd)


def _group_rms(a, sumsq, gain):
    return (a * lax.rsqrt(sumsq * (1.0 / HEAD_DIM) + EPS)) * gain


def _rope(x, cos, sin_signed):
    half = HEAD_DIM // 2
    reps = x.shape[1] // LANES
    lane = lax.broadcasted_iota(jnp.int32, (x.shape[0], LANES), 1)
    first = (lane % HEAD_DIM) < half
    out = []
    for r in range(reps):
        xs = x[:, r * LANES:(r + 1) * LANES]
        partner = jnp.where(first, pltpu.roll(xs, LANES - half, axis=1), pltpu.roll(xs, half, axis=1))
        out.append(xs * cos + partner * sin_signed)
    return out[0] if reps == 1 else jnp.concatenate(out, axis=1)


def _tile_kv(a):
    lane = lax.broadcasted_iota(jnp.int32, a.shape, 1)
    sw = pltpu.roll(a, HEAD_DIM, axis=1)
    h0 = jnp.where(lane < HEAD_DIM, a, sw)
    h1 = jnp.where(lane < HEAD_DIM, sw, a)
    return jnp.concatenate([h0, h0, h1, h1], axis=1)


def _inproj_kernel(*refs, use_rope, emit_kv):
    x_ref, mp_ref, w_ref, qg_ref, kg_ref, bd_ref = refs[:6]
    pos = 6
    if use_rope:
        cos_ref, sin_ref = refs[pos:pos + 2]
        pos += 2
    ret_ref, q_ref, kt_ref, vT_ref, cb_ref, u_ref = refs[pos:pos + 6]
    pos += 6
    if emit_kv:
        kout_ref, vout_ref = refs[pos:pos + 2]

    mp = mp_ref[...]
    tm = x_ref.shape[0]
    n_sub = tm // INPROJ_SUB
    xbs = [None] * n_sub
    xbs[0] = _mod_norm(x_ref[0:INPROJ_SUB, :], mp).astype(BF16)
    for s in range(n_sub):
        rows = slice(s * INPROJ_SUB, (s + 1) * INPROJ_SUB)
        xb = xbs[s]

        def proj(off, width):
            return _dot(xb, w_ref[:, off:off + width])

        aq = proj(OFF_AQ, ATT_Q_W)
        ret_ref[rows, OFF_RQ:OFF_RQ + RET_W] = proj(OFF_RQ, RET_W).astype(BF16)
        if s + 1 < n_sub:
            nxt = slice((s + 1) * INPROJ_SUB, (s + 2) * INPROJ_SUB)
            xbs[s + 1] = _mod_norm(x_ref[nxt, :], mp).astype(BF16)
        ss_q = _group_sumsq(aq, bd_ref[...])
        ak = proj(OFF_AK, ATT_KV_W)
        v = proj(OFF_AV, ATT_KV_W)
        ret_ref[rows, OFF_RK:OFF_RK + RET_W] = (proj(OFF_RK, RET_W) * (RET_DK ** -0.5)).astype(BF16)

        q = _group_rms(aq, ss_q, qg_ref[...])
        if use_rope:
            q = _rope(q, cos_ref[rows, :], sin_ref[rows, :])
        q_ref[rows, :] = (q * Q_SCALE).astype(BF16)
        ss_k = _group_sumsq(ak, bd_ref[0:ATT_KV_W, 0:ATT_KV_W])
        ret_ref[rows, OFF_RV:OFF_RV + RET_W] = proj(OFF_RV, RET_W).astype(BF16)

        k = _group_rms(ak, ss_k, kg_ref[...])
        if emit_kv:
            kout_ref[rows, :] = k
            vout_ref[rows, :] = v
        if use_rope:
            k = _rope(k, cos_ref[rows, :], sin_ref[rows, :])
        kt_ref[rows, :] = _tile_kv(k).astype(BF16)
        vT_ref[:, rows] = v.T.astype(BF16)

        ret_ref[rows, OFF_RG:OFF_RG + RET_W] = _silu(proj(OFF_RG, RET_W)).astype(BF16)
        cb_ref[rows, :] = proj(OFF_CB, CONV_DIM).astype(BF16)
        u_ref[rows, :] = (proj(OFF_CC, CONV_DIM) * proj(OFF_CX, CONV_DIM)).astype(BF16)


def _inproj(x, mp, rows_per_mod, w_in, l, q_gain_t, k_gain_t, bd, rope, seq_len, emit_kv):
    n = x.shape[0]
    tm = INPROJ_TM
    use_rope = rope is not None
    row = lambda i: (i, 0)
    in_specs = [
        pl.BlockSpec((tm, D_MODEL), row),
        pl.BlockSpec((None, 8, D_MODEL), lambda i: ((i * tm) // rows_per_mod, 0, 0)),
        _resident((None, D_MODEL, OFF_GATES), lambda i: (l, 0, 0)),
        pl.BlockSpec((None, 1, ATT_Q_W), lambda i: (l, 0, 0)),
        pl.BlockSpec((None, 1, ATT_KV_W), lambda i: (l, 0, 0)),
        _resident((ATT_Q_W, ATT_Q_W), lambda i: (0, 0)),
    ]
    args = [x, mp, w_in, q_gain_t, k_gain_t, bd]
    if use_rope:
        tiles_per_seq = seq_len // tm
        in_specs += [pl.BlockSpec((tm, LANES), lambda i: (i % tiles_per_seq, 0))] * 2
        args += list(rope)
    widths = [4 * RET_W, ATT_Q_W, ATT_Q_W, None, CONV_DIM, CONV_DIM]
    out_specs = [pl.BlockSpec((tm, w), row) for w in widths if w]
    out_shape = [jax.ShapeDtypeStruct((n, w), BF16) for w in widths if w]
    out_specs.insert(3, pl.BlockSpec((ATT_KV_W, tm), lambda i: (0, i)))
    out_shape.insert(3, jax.ShapeDtypeStruct((ATT_KV_W, n), BF16))
    if emit_kv:
        out_specs += [pl.BlockSpec((tm, ATT_KV_W), row)] * 2
        out_shape += [jax.ShapeDtypeStruct((n, ATT_KV_W), F32)] * 2
    return pl.pallas_call(
        functools.partial(_inproj_kernel, use_rope=use_rope, emit_kv=emit_kv),
        grid=(n // tm,),
        in_specs=in_specs,
        out_specs=out_specs,
        out_shape=out_shape,
        compiler_params=pltpu.CompilerParams(
            dimension_semantics=("arbitrary",), vmem_limit_bytes=VMEM_LIMIT),
        name="inproj",
    )(*args)


RET_GROUP = 8
RET_SCAN_UNROLL = 2
TAB_DMASK, TAB_DQF, TAB_DQB, TAB_DKF, TAB_DKB, TAB_DCF, TAB_DCB, N_TAB = range(8)


def _log_sigmoid(x):
    return jnp.minimum(x, 0.0) - jnp.log(1.0 + jnp.exp(-jnp.abs(x)))


def _chunk_rows(start):
    if isinstance(start, int):
        return pl.ds(start, RET_CHUNK)
    return pl.ds(pl.multiple_of(start, RET_CHUNK), RET_CHUNK)


def _retention_kernel(*refs, n_seq, n_chunks, has_s0, emit_state):
    dl_ref, q_ref, k_ref, v_ref, g_ref, gn_ref = refs[:6]
    pos = 6
    if has_s0:
        s0_ref = refs[pos]
        pos += 1
    o_ref = refs[pos]
    pos += 1
    if emit_state:
        st_ref = refs[pos]
        pos += 1
    tab_ref, kv_ref, ent_ref = refs[pos:pos + 3]

    C = RET_CHUNK
    hd = pl.program_id(1)
    row = lax.broadcasted_iota(jnp.int32, (C, C), 0).astype(F32)
    col = lax.broadcasted_iota(jnp.int32, (C, C), 1).astype(F32)
    diff = row - col
    lgf = _log_sigmoid(jnp.full((C, C), dl_ref[0, hd], F32))
    lgb = _log_sigmoid(jnp.full((C, C), dl_ref[1, hd], F32))
    tab_ref[TAB_DMASK] = (jnp.where(diff >= 0, jnp.exp(lgf * jnp.maximum(diff, 0.0)), 0.0)
                          + jnp.where(diff <= 0, jnp.exp(lgb * jnp.maximum(-diff, 0.0)), 0.0))
    tab_ref[TAB_DQF] = jnp.exp(lgf * (row + 1.0))
    tab_ref[TAB_DQB] = jnp.exp(lgb * (C - row))
    tab_ref[TAB_DKF] = jnp.exp(lgf * (C - 1.0 - row))
    tab_ref[TAB_DKB] = jnp.exp(lgb * row)
    tab_ref[TAB_DCF] = jnp.exp(lgf * C)
    tab_ref[TAB_DCB] = jnp.exp(lgb * C)

    def kv_phase(items):
        ops = []
        for r, _ in items:
            rows = _chunk_rows(r)
            k = k_ref[rows, :].astype(F32)
            vT = v_ref[rows, :].astype(F32).T.astype(BF16)
            kk = jnp.concatenate([(k * tab_ref[TAB_DKF]).astype(BF16),
                                  (k * tab_ref[TAB_DKB]).astype(BF16)], axis=1)
            ops.append((vT, kk))
        for (_, i), (vT, kk) in zip(items, ops):
            kv_ref[i] = _dot(vT, kk)

    def scan_phase(s, sc):
        if has_s0:
            st0 = (s0_ref[0].T, s0_ref[1].T)
        else:
            st0 = (jnp.zeros((RET_DV, RET_DK), F32), jnp.zeros((RET_DV, RET_DK), F32))

        def scan_step(t, st):
            st_f, st_b = st
            cb = n_chunks - 1 - t
            ent_ref[sc + t, :, 0:RET_DK] = st_f.astype(BF16)
            ent_ref[sc + cb, :, RET_DK:2 * RET_DK] = st_b.astype(BF16)
            st_f = st_f * tab_ref[TAB_DCF] + kv_ref[sc + t, :, 0:RET_DK]
            st_b = st_b * tab_ref[TAB_DCB] + kv_ref[sc + cb, :, RET_DK:2 * RET_DK]
            return st_f, st_b

        if n_chunks <= RET_SCAN_UNROLL:
            st = st0
            for t in range(n_chunks):
                st = scan_step(t, st)
        else:
            st = lax.fori_loop(0, n_chunks, scan_step, st0)
        if emit_state:
            st_ref[s, 0] = st[0].T
            st_ref[s, 1] = st[1].T

    def score_stage(items):
        qs = [q_ref[_chunk_rows(r), :] for r, _ in items]
        att = [_dot_nt(q, k_ref[_chunk_rows(r), :]) for q, (r, _) in zip(qs, items)]
        att = [(a * tab_ref[TAB_DMASK]).astype(BF16) for a in att]
        qq = []
        for q in qs:
            qf = q.astype(F32)
            qq.append(jnp.concatenate([(qf * tab_ref[TAB_DQF]).astype(BF16),
                                       (qf * tab_ref[TAB_DQB]).astype(BF16)], axis=1))
        return att, qq

    def product_stage(items, att, qq):
        return [_dot(a, v_ref[_chunk_rows(r), :]) + _dot_nt(x, ent_ref[i])
                for a, x, (r, i) in zip(att, qq, items)]

    def norm_stage(items, outs):
        for o, (r, _) in zip(outs, items):
            rows = _chunk_rows(r)
            ms = jnp.mean(o * o, axis=-1, keepdims=True)
            on = (o * lax.rsqrt(ms + EPS)) * gn_ref[...]
            o_ref[rows, :] = (g_ref[rows, :].astype(F32) * on).astype(BF16)

    items = [(s * n_chunks * C + c * C, s * n_chunks + c) for s in range(n_seq) for c in range(n_chunks)]
    assert len(items) % RET_GROUP == 0
    groups = [items[i:i + RET_GROUP] for i in range(0, len(items), RET_GROUP)]
    for grp in groups:
        kv_phase(grp)
    for s in range(n_seq):
        scan_phase(s, s * n_chunks)
    scored, products = {}, {}
    for t in range(len(groups) + 2):
        if t < len(groups):
            scored[t] = score_stage(groups[t])
        if 0 <= t - 1 < len(groups):
            products[t - 1] = product_stage(groups[t - 1], *scored.pop(t - 1))
        if 0 <= t - 2 < len(groups):
            norm_stage(groups[t - 2], products.pop(t - 2))


def _retention(ret_in, decay_logit_l, gn_l, s0, seq_len, n_seq_blk, emit_state):
    n = ret_in.shape[0]
    n_chunks = seq_len // RET_CHUNK
    tb = n_seq_blk * seq_len
    has_s0 = s0 is not None
    n_off = RET_W // RET_DK

    def branch(k):
        return pl.BlockSpec((tb, RET_DK), lambda i, h: (i, k * n_off + h))

    in_specs = [pl.BlockSpec(memory_space=pltpu.SMEM), branch(0), branch(1), branch(2), branch(3),
                pl.BlockSpec((1, RET_DV), lambda i, h: (0, h))]
    args = [decay_logit_l, ret_in, ret_in, ret_in, ret_in, gn_l]
    if has_s0:
        assert n_seq_blk == 1
        in_specs.append(pl.BlockSpec((None, 2, None, RET_DK, RET_DV), lambda i, h: (i, 0, h, 0, 0)))
        args.append(s0)
    out_specs = [pl.BlockSpec((tb, RET_DV), lambda i, h: (i, h))]
    out_shape = [jax.ShapeDtypeStruct((n, RET_W), BF16)]
    if emit_state:
        out_specs.append(pl.BlockSpec((n_seq_blk, 2, None, RET_DK, RET_DV), lambda i, h: (i, 0, h, 0, 0)))
        out_shape.append(jax.ShapeDtypeStruct((n // seq_len, 2, N_RET_HEADS, RET_DK, RET_DV), F32))
    n_slots = n_chunks * n_seq_blk
    return pl.pallas_call(
        functools.partial(_retention_kernel, n_seq=n_seq_blk, n_chunks=n_chunks, has_s0=has_s0,
                          emit_state=emit_state),
        grid=(n // tb, N_RET_HEADS),
        in_specs=in_specs,
        out_specs=out_specs,
        out_shape=out_shape,
        scratch_shapes=[
            pltpu.VMEM((N_TAB, RET_CHUNK, RET_CHUNK), F32),
            pltpu.VMEM((n_slots, RET_DV, 2 * RET_DK), F32),
            pltpu.VMEM((n_slots, RET_DV, 2 * RET_DK), BF16),
        ],
        compiler_params=pltpu.CompilerParams(
            dimension_semantics=("arbitrary", "arbitrary"), vmem_limit_bytes=VMEM_LIMIT),
        name="retention",
    )(*args)


ATT_TQ = 1024
ATT_KB = 512
ATT_SHORT_MAX = 512
ATT_RC = 16
NEG_BIG = -1e30
SUM_ROWS = BF16_SUBLANES


def _attention_kernel(*refs, n_cache, n_new):
    q_ref = refs[0]
    pos = 1
    if n_cache:
        kc_ref, vc_ref = refs[pos:pos + 2]
        pos += 2
    kn_ref, vn_ref, o_ref, qs_ref, acc_ref = refs[pos:pos + 5]
    s_slots = refs[pos + 5:pos + 7]
    p_slots = refs[pos + 7:pos + 9]

    tq = q_ref.shape[0]
    nq = GROUP * tq
    qf = q_ref[...].astype(F32)
    head = lax.broadcasted_iota(jnp.int32, (tq, KV_LANES), 1) // HEAD_DIM
    for g in range(GROUP):
        qs_ref[:, g * tq:(g + 1) * tq] = jnp.where(head == g, qf, 0.0).T.astype(BF16)
    acc_ref[...] = jnp.zeros(acc_ref.shape, F32)

    kb = min(ATT_KB, n_new)
    n_cb = 1 if n_cache else 0
    n_nb = n_new // kb
    assert n_new % kb == 0
    n_blk = n_cb + n_nb

    def block(t):
        if isinstance(t, int) and t < n_cb:
            return kc_ref, vc_ref, pl.ds(0, n_cache)
        lo = (t - n_cb) * kb
        return kn_ref, vn_ref, (pl.ds(lo, kb) if isinstance(lo, int) else pl.ds(pl.multiple_of(lo, kb), kb))

    def scores(t, par):
        k_ref, _, keys = block(t)
        s = _dot(k_ref[keys, :], qs_ref[...])
        s_slots[par][0:keys.size, :] = s
        return jnp.max(s, axis=0, keepdims=True)

    def softmax(n_keys, par, m, s_max):
        s_ref, p_ref = s_slots[par], p_slots[par]
        m_new = jnp.maximum(m, s_max)
        m_rows = jnp.broadcast_to(m_new, (ATT_RC, nq))
        for r in range(0, n_keys, ATT_RC):
            p_ref[r:r + ATT_RC, :] = jnp.exp2(s_ref[r:r + ATT_RC, :] - m_rows).astype(BF16)
        return m_new, jnp.exp2(m - m_new)

    def pv(t, par, alpha):
        _, vT_ref, keys = block(t)
        v1 = jnp.concatenate([vT_ref[:, keys], jnp.ones((SUM_ROWS, keys.size), BF16)], axis=0)
        acc_ref[...] = alpha * acc_ref[...] + _dot(v1, p_slots[par][0:keys.size, :])

    def stage(t, par, carry):
        m, alpha, s_max = carry
        next_max = s_max
        if not isinstance(t, int) or t + 1 < n_blk:
            next_max = scores(t + 1, 1 - par)
        if not isinstance(t, int) or t >= 1:
            pv(t - 1, 1 - par, alpha)
        return softmax(block(t)[2].size, par, m, s_max) + (next_max,)

    carry = (jnp.full((1, nq), NEG_BIG, F32), jnp.zeros((1, nq), F32), scores(0, 0))
    lo_t = min(n_cb + 1, n_blk)
    n_loop = max(n_blk - 1 - lo_t, 0)
    if n_loop % 2:
        lo_t += 1
        n_loop -= 1
    for t in range(lo_t):
        carry = stage(t, t % 2, carry)

    def pair(i, carry):
        t = lo_t + 2 * i
        carry = stage(t, lo_t % 2, carry)
        return stage(t + 1, (lo_t + 1) % 2, carry)

    carry = lax.fori_loop(0, n_loop // 2, pair, carry)
    for t in range(lo_t + n_loop, n_blk):
        carry = stage(t, t % 2, carry)
    pv(n_blk - 1, (n_blk - 1) % 2, carry[1])

    acc = acc_ref[...]
    oT = acc[0:HEAD_DIM] / acc[HEAD_DIM:HEAD_DIM + 1]
    o4 = jnp.concatenate([oT[:, g * tq:(g + 1) * tq] for g in range(GROUP)], axis=0)
    o_ref[...] = o4.T.astype(BF16)


ATT_SEQ_PER_STEP = 4


def _attention_short_kernel(q_ref, k_ref, vT_ref, o_ref, *, n_seq):
    t_len = q_ref.shape[0] // n_seq
    head = lax.broadcasted_iota(jnp.int32, (t_len, KV_LANES), 1) // HEAD_DIM
    ones = jnp.ones((SUM_ROWS, t_len), BF16)
    items = [(slice(s * t_len, (s + 1) * t_len), kv) for s in range(n_seq) for kv in range(N_KV_HEADS)]

    def cols(kv):
        return slice(kv * KV_LANES, (kv + 1) * KV_LANES)

    def score_stage(item):
        rows, kv = item
        qf = q_ref[rows, cols(kv)].astype(F32)
        qsT = jnp.concatenate([jnp.where(head == g, qf, 0.0).T.astype(BF16) for g in range(GROUP)],
                              axis=1)
        return _dot(k_ref[rows, cols(kv)], qsT)

    def softmax_stage(sT):
        return jnp.exp2(sT - jnp.max(sT, axis=0, keepdims=True)).astype(BF16)

    def value_stage(item, pT):
        rows, kv = item
        v1 = jnp.concatenate([vT_ref[kv * HEAD_DIM:(kv + 1) * HEAD_DIM, rows], ones], axis=0)
        return _dot(v1, pT)

    def out_stage(item, acc):
        rows, kv = item
        oT = acc[0:HEAD_DIM] / acc[HEAD_DIM:HEAD_DIM + 1]
        o4 = jnp.concatenate([oT[:, g * t_len:(g + 1) * t_len] for g in range(GROUP)], axis=0)
        o_ref[rows, cols(kv)] = o4.T.astype(BF16)

    n = len(items)
    scores, probs, accs = {}, {}, {}
    for t in range(n + 3):
        if t < n:
            scores[t] = score_stage(items[t])
        if 0 <= t - 1 < n:
            probs[t - 1] = softmax_stage(scores.pop(t - 1))
        if 0 <= t - 2 < n:
            accs[t - 2] = value_stage(items[t - 2], probs.pop(t - 2))
        if 0 <= t - 3 < n:
            out_stage(items[t - 3], accs.pop(t - 3))


def _attention_short(q, kt, vT, seq_len):
    n = q.shape[0]
    tb = ATT_SEQ_PER_STEP * seq_len
    return pl.pallas_call(
        functools.partial(_attention_short_kernel, n_seq=ATT_SEQ_PER_STEP),
        grid=(n // tb,),
        in_specs=[pl.BlockSpec((tb, ATT_Q_W), lambda i: (i, 0)),
                  pl.BlockSpec((tb, ATT_Q_W), lambda i: (i, 0)),
                  pl.BlockSpec((ATT_KV_W, tb), lambda i: (0, i))],
        out_specs=pl.BlockSpec((tb, ATT_Q_W), lambda i: (i, 0)),
        out_shape=jax.ShapeDtypeStruct((n, ATT_Q_W), BF16),
        compiler_params=pltpu.CompilerParams(
            dimension_semantics=("arbitrary",), vmem_limit_bytes=VMEM_LIMIT),
        name="attention_short",
    )(q, kt, vT)


def _attention(q, kt, vT, cache, seq_len):
    n = q.shape[0]
    n_batch = n // seq_len
    tq = ATT_TQ
    nq = seq_len // tq
    n_cache = 0 if cache is None else cache[0].shape[1]
    if n_cache == 0 and seq_len <= ATT_SHORT_MAX:
        return _attention_short(q, kt, vT, seq_len)
    blk_rows = max(min(ATT_KB, seq_len), n_cache)
    in_specs = [pl.BlockSpec((tq, KV_LANES), lambda b, kv, i: (b * nq + i, kv))]
    args = [q]
    if n_cache:
        in_specs += [pl.BlockSpec((None, n_cache, KV_LANES), lambda b, kv, i: (b, 0, kv)),
                     pl.BlockSpec((None, HEAD_DIM, n_cache), lambda b, kv, i: (b, kv, 0))]
        args += list(cache)
    in_specs += [pl.BlockSpec((seq_len, KV_LANES), lambda b, kv, i: (b, kv)),
                 pl.BlockSpec((HEAD_DIM, seq_len), lambda b, kv, i: (kv, b))]
    args += [kt, vT]
    return pl.pallas_call(
        functools.partial(_attention_kernel, n_cache=n_cache, n_new=seq_len),
        grid=(n_batch, N_KV_HEADS, nq),
        in_specs=in_specs,
        out_specs=pl.BlockSpec((tq, KV_LANES), lambda b, kv, i: (b * nq + i, kv)),
        out_shape=jax.ShapeDtypeStruct((n, ATT_Q_W), BF16),
        scratch_shapes=[
            pltpu.VMEM((KV_LANES, GROUP * tq), BF16),
            pltpu.VMEM((HEAD_DIM + SUM_ROWS, GROUP * tq), F32),
            pltpu.VMEM((blk_rows, GROUP * tq), F32),
            pltpu.VMEM((blk_rows, GROUP * tq), F32),
            pltpu.VMEM((blk_rows, GROUP * tq), BF16),
            pltpu.VMEM((blk_rows, GROUP * tq), BF16),
        ],
        compiler_params=pltpu.CompilerParams(
            dimension_semantics=("arbitrary", "arbitrary", "arbitrary"), vmem_limit_bytes=VMEM_LIMIT),
        name="attention",
    )(*args)


MERGE_TM = 512
MERGE_SUB = 256


def _merge_kernel(h_ref, mp_ref, mpf_ref, ro_ref, ao_ref, cb_ref, u_ref, up_ref, un_ref, cw_ref,
                  wg_ref, wr_ref, wa_ref, wc_ref, wo_ref, wfi_ref, wfo_ref, o_ref, mid_ref, hid_ref,
                  *, seq_len):
    tm = h_ref.shape[0]
    sub = MERGE_SUB
    n_sub = tm // sub
    i = pl.program_id(0)
    cw = cw_ref[...]
    r = lax.broadcasted_iota(jnp.int32, (sub, CONV_DIM), 0)
    edge = BF16_SUBLANES

    def branches(s):
        r0 = s * sub
        rows = slice(r0, r0 + sub)
        yr = _dot(ro_ref[rows, :], wr_ref[...])
        ya = _dot(ao_ref[rows, :], wa_ref[...])
        u = u_ref[rows, :].astype(F32)
        before = up_ref if s == 0 else u_ref.at[r0 - edge:r0, :]
        after = un_ref if s == n_sub - 1 else u_ref.at[r0 + sub:r0 + sub + edge, :]
        prev_row = before[edge - 1:edge, :].astype(F32)
        next_row = after[0:1, :].astype(F32)
        t = (i * tm + r0 + r) % seq_len
        u_prev = jnp.where(r == 0, prev_row, pltpu.roll(u, 1, axis=0))
        u_next = jnp.where(r == sub - 1, next_row, pltpu.roll(u, sub - 1, axis=0))
        u_prev = jnp.where(t == 0, 0.0, u_prev)
        u_next = jnp.where(t == seq_len - 1, 0.0, u_next)
        conv = u_prev * cw[0:1] + u * cw[1:2] + u_next * cw[2:3]
        yc = _dot((cb_ref[rows, :].astype(F32) * conv).astype(BF16), wc_ref[...])
        xb = _mod_norm(h_ref[rows, :], mp_ref[...]).astype(BF16)
        merged = _sigmoid(_dot(xb, wg_ref[0, :,0:D_MODEL])) * yr
        merged = merged + _sigmoid(_dot(xb, wg_ref[0, :,D_MODEL:2 * D_MODEL])) * ya
        merged = merged + _sigmoid(_dot(xb, wg_ref[0, :,2 * D_MODEL:3 * D_MODEL])) * yc
        return merged.astype(BF16)

    def project(s, merged):
        rows = slice(s * sub, (s + 1) * sub)
        mid_ref[rows, :] = h_ref[rows, :] + mp_ref[2:3, :] * _dot(merged, wo_ref[...])

    pending = branches(0)
    for s in range(1, n_sub):
        nxt = branches(s)
        project(s - 1, pending)
        pending = nxt
    project(n_sub - 1, pending)
    o_ref[...] = _ffn_tile(mid_ref[...], mpf_ref[...], wfi_ref, wfo_ref, hid_ref)


def _merge_ffn(h, mp, mp_ffn, rows_per_mod, ret_o, att_o, cb, u, conv_w8, w_gates, w_ret_o, w_att_o,
               w_conv_o, w_o, w_ffn_in, w_ffn_out, l, seq_len):
    n = h.shape[0]
    tm = MERGE_TM
    hb = tm // BF16_SUBLANES
    last = n // BF16_SUBLANES - 1
    row = lambda i: (i, 0)
    wspec = lambda k: _resident((None, k, D_MODEL), lambda i: (l, 0, 0))
    mod_row = lambda i: ((i * tm) // rows_per_mod, 0, 0)
    return pl.pallas_call(
        functools.partial(_merge_kernel, seq_len=seq_len),
        grid=(n // tm,),
        in_specs=[
            pl.BlockSpec((tm, D_MODEL), row),
            pl.BlockSpec((None, 8, D_MODEL), mod_row),
            pl.BlockSpec((None, 8, D_MODEL), mod_row),
            pl.BlockSpec((tm, RET_W), row),
            pl.BlockSpec((tm, ATT_Q_W), row),
            pl.BlockSpec((tm, CONV_DIM), row),
            pl.BlockSpec((tm, CONV_DIM), row),
            pl.BlockSpec((BF16_SUBLANES, CONV_DIM), lambda i: (jnp.maximum(i * hb - 1, 0), 0)),
            pl.BlockSpec((BF16_SUBLANES, CONV_DIM), lambda i: (jnp.minimum((i + 1) * hb, last), 0)),
            pl.BlockSpec((None, 8, CONV_DIM), lambda i: (l, 0, 0)),
            _resident((pl.Element(1), pl.Element(D_MODEL), pl.Element(3 * D_MODEL)),
                      lambda i: (l, 0, OFF_GATES)),
            wspec(RET_W), wspec(ATT_Q_W), wspec(CONV_DIM), wspec(D_MODEL),
            _resident((None, None, D_MODEL, 2 * FFN_DIM), lambda i: (l, 1, 0, 0)),
            _resident((None, None, FFN_DIM, D_MODEL), lambda i: (l, 1, 0, 0)),
        ],
        out_specs=pl.BlockSpec((tm, D_MODEL), row),
        out_shape=jax.ShapeDtypeStruct((n, D_MODEL), F32),
        scratch_shapes=[pltpu.VMEM((tm, D_MODEL), F32), pltpu.VMEM((tm, FFN_DIM), BF16)],
        compiler_params=pltpu.CompilerParams(
            dimension_semantics=("arbitrary",), vmem_limit_bytes=VMEM_LIMIT),
        name="merge_ffn",
    )(h, mp, mp_ffn, ret_o, att_o, cb, u, u, u, conv_w8, w_gates, w_ret_o, w_att_o, w_conv_o, w_o,
      w_ffn_in, w_ffn_out)


def _rope_tables(n_tok):
    rows = n_tok // GRID_W
    t_row = np.repeat(np.arange(rows, dtype=np.float64), GRID_W)
    t_col = np.tile(np.arange(GRID_W, dtype=np.float64), rows)
    n_freq = HEAD_DIM // 4
    inv = ROPE_THETA ** (-np.arange(n_freq, dtype=np.float64) / n_freq)
    ang = np.concatenate([t_row[:, None] * inv, t_col[:, None] * inv], axis=-1)
    cos, sin = np.cos(ang), np.sin(ang)
    cos64 = np.concatenate([cos, cos], axis=-1)
    sin64 = np.concatenate([-sin, sin], axis=-1)
    return (jnp.asarray(np.tile(cos64, (1, LANES // HEAD_DIM)), F32),
            jnp.asarray(np.tile(sin64, (1, LANES // HEAD_DIM)), F32))


def _mod_pack(mod_l, rows, sub, norm_w_row):
    r0 = rows[0]
    nr = len(rows)
    m = mod_l[r0:r0 + nr, 3 * sub * D_MODEL:3 * (sub + 1) * D_MODEL].reshape(nr, 3, D_MODEL)
    nw = jnp.broadcast_to(norm_w_row[None, None, :], (nr, 1, D_MODEL))
    pad = jnp.zeros((nr, 4, D_MODEL), F32)
    return jnp.concatenate([m, nw, pad], axis=1)


def kernel(x_prompt, x_sample, c, state_ret, cache_k, cache_v, c_ctx, w_ada, b_ada, norm_w, w_ffn_in,
           w_ffn_out, w_in, ret_decay_logit, ret_gn, q_gain, k_gain, conv_w, w_ret_o, w_att_o, w_conv_o,
           w_o):
    n_ctx_b, ctx_len, _ = x_prompt.shape
    n_lat_b, lat_len, _ = x_sample.shape

    w_ffn_in_b = w_ffn_in.astype(BF16)
    w_ffn_out_b = w_ffn_out.astype(BF16)
    w_in_b = w_in.astype(BF16)
    w_gates_b = w_in_b
    w_ret_o_b = w_ret_o.astype(BF16)
    w_att_o_b = w_att_o.astype(BF16)
    w_conv_o_b = w_conv_o.astype(BF16)
    w_o_b = w_o.astype(BF16)

    q_gain_t = jnp.tile(q_gain, (1, N_HEADS)).reshape(DEPTH, 1, ATT_Q_W)
    k_gain_t = jnp.tile(k_gain, (1, N_KV_HEADS)).reshape(DEPTH, 1, ATT_KV_W)
    conv_w8 = jnp.pad(conv_w, ((0, 0), (0, 8 - conv_w.shape[1]), (0, 0)))
    gid = np.arange(ATT_Q_W) // HEAD_DIM
    bd = jnp.asarray(gid[:, None] == gid[None, :], BF16)
    rope = _rope_tables(lat_len)

    cond8 = jnp.zeros((8, D_MODEL), F32).at[0].set(c_ctx).at[1:1 + n_lat_b].set(c)
    mod = _ada(cond8, w_ada, b_ada)

    ck = cache_k.astype(BF16)
    ck = jnp.broadcast_to(ck[:, :, :, :, None, :], ck.shape[:4] + (GROUP, HEAD_DIM))
    cache_kt = ck.reshape(ck.shape[0], ck.shape[1], ck.shape[2], N_KV_HEADS * KV_LANES)
    cache_vT = jnp.transpose(cache_v.astype(BF16), (0, 1, 3, 4, 2)).reshape(
        cache_v.shape[0], cache_v.shape[1], ATT_KV_W, cache_v.shape[2])

    groups = (
        dict(x=x_prompt.reshape(n_ctx_b * ctx_len, D_MODEL), rows=[0], seq=ctx_len, ctx=True),
        dict(x=x_sample.reshape(n_lat_b * lat_len, D_MODEL), rows=list(range(1, 1 + n_lat_b)),
             seq=lat_len, ctx=False),
    )
    results = []
    for grp in groups:
        h = grp["x"]
        seq = grp["seq"]
        is_ctx = grp["ctx"]
        rpm = h.shape[0] // len(grp["rows"])
        states, keys, values = [], [], []
        for l in range(DEPTH):
            mp = [_mod_pack(mod[l], grp["rows"], s, norm_w[l, s]) for s in range(3)]
            h = _ffn(h, mp[0], rpm, w_ffn_in_b, w_ffn_out_b, l, 0)
            outs = _inproj(h, mp[1], rpm, w_in_b, l, q_gain_t, k_gain_t, bd,
                           None if is_ctx else rope, seq, emit_kv=is_ctx)
            ret_in, qn, kt, vT, cb, u = outs[:6]
            if is_ctx:
                ret_o, st = _retention(ret_in, ret_decay_logit[l], ret_gn[l][None, :], None, seq,
                                       n_seq_blk=16, emit_state=True)
                states.append(st)
                keys.append(outs[6].reshape(n_ctx_b, seq, N_KV_HEADS, HEAD_DIM))
                values.append(outs[7].reshape(n_ctx_b, seq, N_KV_HEADS, HEAD_DIM))
                att_o = _attention(qn, kt, vT, None, seq)
            else:
                (ret_o,) = _retention(ret_in, ret_decay_logit[l], ret_gn[l][None, :], state_ret[:, l],
                                      seq, n_seq_blk=1, emit_state=False)
                att_o = _attention(qn, kt, vT, (cache_kt[:, l], cache_vT[:, l]), seq)
            h = _merge_ffn(h, mp[1], mp[2], rpm, ret_o, att_o, cb, u, conv_w8, w_gates_b, w_ret_o_b,
                           w_att_o_b, w_conv_o_b, w_o_b, w_ffn_in_b, w_ffn_out_b, l, seq)
        results.append((h, states, keys, values))

    (y_ctx, states, keys, values), (y_lat, _, _, _) = results
    y_prompt = y_ctx.reshape(x_prompt.shape)
    y_sample = y_lat.reshape(x_sample.shape)
    new_state_ret = jnp.stack(states, axis=1)
    new_cache_k = jnp.stack(keys, axis=1)
    new_cache_v = jnp.stack(values, axis=1)
    return (y_prompt, y_sample, new_state_ret, new_cache_k, new_cache_v)
```

```python
import functools
import math

import numpy as np
import jax
import jax.numpy as jnp
from jax import lax
from jax.experimental import pallas as pl
from jax.experimental.pallas import tpu as pltpu

D_MODEL = 1024
DEPTH = 2
GRID_W = 64
N_RET_HEADS = 4
RET_DK = 128
RET_DV = 128
RET_CHUNK = 128
N_HEADS = 8
N_KV_HEADS = 2
HEAD_DIM = 64
ROPE_THETA = 10000.0
CONV_DIM = 512
FFN_DIM = 2816
N_MOD = 9
EPS = 1e-6
RET_W = N_RET_HEADS * RET_DK
ATT_Q_W = N_HEADS * HEAD_DIM
ATT_KV_W = N_KV_HEADS * HEAD_DIM
GROUP = N_HEADS // N_KV_HEADS
KV_LANES = GROUP * HEAD_DIM
OFF_RQ, OFF_RK, OFF_RV, OFF_RG = 0, 512, 1024, 1536
OFF_AQ, OFF_AK, OFF_AV = 2048, 2560, 2688
OFF_CB, OFF_CC, OFF_CX = 2816, 3328, 3840
OFF_GATES = 4352

LANES = 128
BF16_SUBLANES = 16
VMEM_LIMIT = 56 * 1024 * 1024

F32 = jnp.float32
BF16 = jnp.bfloat16


def _dot(a, b):
    return jnp.dot(a, b, preferred_element_type=F32)


def _dot_nt(a, b):
    return lax.dot_general(a, b, (((1,), (1,)), ((), ())), preferred_element_type=F32)


def _sigmoid(x):
    return 1.0 / (1.0 + jnp.exp(-x))


def _silu(x):
    return x * _sigmoid(x)


def _mod_norm(x, mp):
    ms = jnp.mean(x * x, axis=-1, keepdims=True)
    y = x * lax.rsqrt(ms + EPS)
    return (y * mp[3:4]) * (1.0 + mp[1:2]) + mp[0:1]


def _resident(shape, index):
    return pl.BlockSpec(shape, index, pipeline_mode=pl.Buffered(1))


ADA_TN = 2304


def _ada_kernel(c_ref, w_ref, b_ref, o_ref):
    a = _silu(c_ref[...]).astype(BF16)
    o_ref[...] = _dot(a, w_ref[...].astype(BF16)) + b_ref[...]


def _ada(cond8, w_ada, b_ada):
    n = N_MOD * D_MODEL
    return pl.pallas_call(
        _ada_kernel,
        grid=(DEPTH, n // ADA_TN),
        in_specs=[
            pl.BlockSpec((8, D_MODEL), lambda l, j: (0, 0)),
            pl.BlockSpec((None, D_MODEL, ADA_TN), lambda l, j: (l, 0, j)),
            pl.BlockSpec((None, 1, ADA_TN), lambda l, j: (l, 0, j)),
        ],
        out_specs=pl.BlockSpec((None, 8, ADA_TN), lambda l, j: (l, 0, j)),
        out_shape=jax.ShapeDtypeStruct((DEPTH, 8, n), F32),
        compiler_params=pltpu.CompilerParams(
            dimension_semantics=("arbitrary", "arbitrary"), vmem_limit_bytes=VMEM_LIMIT),
        name="ada",
    )(cond8, w_ada, b_ada.reshape(DEPTH, 1, n))


FFN_FC = 256


def _ffn_tile(x, mp, wi_ref, wo_ref, h_ref):
    xb = _mod_norm(x, mp).astype(BF16)
    for c in range(FFN_DIM // FFN_FC):
        lo = c * FFN_FC
        g = _dot(xb, wi_ref[:, lo:lo + FFN_FC])
        u = _dot(xb, wi_ref[:, FFN_DIM + lo:FFN_DIM + lo + FFN_FC])
        h_ref[:, lo:lo + FFN_FC] = (_silu(g) * u).astype(BF16)
    y = _dot(h_ref[...], wo_ref[...])
    return x + (0.5 * mp[2:3]) * y


INPROJ_TM = 512
INPROJ_SUB = 256
Q_SCALE = HEAD_DIM ** -0.5 * math.log2(math.e)


def _group_sumsq(a, bd):
    sq = a * a
    hi = sq.astype(BF16)
    lo = (sq - hi.astype(F32)).astype(BF16)
    return _dot(hi, bd) + _dot(lo, bd)


def _group_rms(a, sumsq, gain):
    return (a * lax.rsqrt(sumsq * (1.0 / HEAD_DIM) + EPS)) * gain


def _rope(x, cos, sin_signed):
    half = HEAD_DIM // 2
    reps = x.shape[1] // LANES
    lane = lax.broadcasted_iota(jnp.int32, (x.shape[0], LANES), 1)
    first = (lane % HEAD_DIM) < half
    out = []
    for r in range(reps):
        xs = x[:, r * LANES:(r + 1) * LANES]
        partner = jnp.where(first, pltpu.roll(xs, LANES - half, axis=1), pltpu.roll(xs, half, axis=1))
        out.append(xs * cos + partner * sin_signed)
    return out[0] if reps == 1 else jnp.concatenate(out, axis=1)


def _tile_kv(a):
    lane = lax.broadcasted_iota(jnp.int32, a.shape, 1)
    sw = pltpu.roll(a, HEAD_DIM, axis=1)
    h0 = jnp.where(lane < HEAD_DIM, a, sw)
    h1 = jnp.where(lane < HEAD_DIM, sw, a)
    return jnp.concatenate([h0, h0, h1, h1], axis=1)


def _inproj_kernel(*refs, use_rope, emit_kv):
    xin_ref, mpf_ref, wfi_ref, wfo_ref, mp_ref, w_ref, qg_ref, kg_ref, bd_ref = refs[:9]
    pos = 9
    if use_rope:
        cos_ref, sin_ref = refs[pos:pos + 2]
        pos += 2
    x_ref, ret_ref, q_ref, kt_ref, vT_ref, cb_ref, u_ref = refs[pos:pos + 7]
    pos += 7
    if emit_kv:
        kout_ref, vout_ref = refs[pos:pos + 2]
    hid_ref = refs[-1]

    x_ref[...] = _ffn_tile(xin_ref[...], mpf_ref[...], wfi_ref, wfo_ref, hid_ref)
    mp = mp_ref[...]
    tm = x_ref.shape[0]
    n_sub = tm // INPROJ_SUB
    xbs = [None] * n_sub
    xbs[0] = _mod_norm(x_ref[0:INPROJ_SUB, :], mp).astype(BF16)
    for s in range(n_sub):
        rows = slice(s * INPROJ_SUB, (s + 1) * INPROJ_SUB)
        xb = xbs[s]

        def proj(off, width):
            return _dot(xb, w_ref[:, off:off + width])

        aq = proj(OFF_AQ, ATT_Q_W)
        ret_ref[rows, OFF_RQ:OFF_RQ + RET_W] = proj(OFF_RQ, RET_W).astype(BF16)
        if s + 1 < n_sub:
            nxt = slice((s + 1) * INPROJ_SUB, (s + 2) * INPROJ_SUB)
            xbs[s + 1] = _mod_norm(x_ref[nxt, :], mp).astype(BF16)
        ss_q = _group_sumsq(aq, bd_ref[...])
        ak = proj(OFF_AK, ATT_KV_W)
        v = proj(OFF_AV, ATT_KV_W)
        ret_ref[rows, OFF_RK:OFF_RK + RET_W] = (proj(OFF_RK, RET_W) * (RET_DK ** -0.5)).astype(BF16)

        q = _group_rms(aq, ss_q, qg_ref[...])
        if use_rope:
            q = _rope(q, cos_ref[rows, :], sin_ref[rows, :])
        q_ref[rows, :] = (q * Q_SCALE).astype(BF16)
        ss_k = _group_sumsq(ak, bd_ref[0:ATT_KV_W, 0:ATT_KV_W])
        ret_ref[rows, OFF_RV:OFF_RV + RET_W] = proj(OFF_RV, RET_W).astype(BF16)

        k = _group_rms(ak, ss_k, kg_ref[...])
        if emit_kv:
            kout_ref[rows, :] = k
            vout_ref[rows, :] = v
        if use_rope:
            k = _rope(k, cos_ref[rows, :], sin_ref[rows, :])
        kt_ref[rows, :] = _tile_kv(k).astype(BF16)
        vT_ref[:, rows] = v.T.astype(BF16)

        ret_ref[rows, OFF_RG:OFF_RG + RET_W] = _silu(proj(OFF_RG, RET_W)).astype(BF16)
        cb_ref[rows, :] = proj(OFF_CB, CONV_DIM).astype(BF16)
        u_ref[rows, :] = (proj(OFF_CC, CONV_DIM) * proj(OFF_CX, CONV_DIM)).astype(BF16)


def _ffn_inproj(x, mp_ffn, mp, rows_per_mod, w_ffn_in, w_ffn_out, w_in, l, q_gain_t, k_gain_t, bd, rope,
                seq_len, emit_kv):
    n = x.shape[0]
    tm = INPROJ_TM
    use_rope = rope is not None
    row = lambda i: (i, 0)
    mod_row = lambda i: ((i * tm) // rows_per_mod, 0, 0)
    in_specs = [
        pl.BlockSpec((tm, D_MODEL), row),
        pl.BlockSpec((None, 8, D_MODEL), mod_row),
        _resident((None, None, D_MODEL, 2 * FFN_DIM), lambda i: (l, 0, 0, 0)),
        _resident((None, None, FFN_DIM, D_MODEL), lambda i: (l, 0, 0, 0)),
        pl.BlockSpec((None, 8, D_MODEL), mod_row),
        _resident((None, D_MODEL, OFF_GATES), lambda i: (l, 0, 0)),
        pl.BlockSpec((None, 1, ATT_Q_W), lambda i: (l, 0, 0)),
        pl.BlockSpec((None, 1, ATT_KV_W), lambda i: (l, 0, 0)),
        _resident((ATT_Q_W, ATT_Q_W), lambda i: (0, 0)),
    ]
    args = [x, mp_ffn, w_ffn_in, w_ffn_out, mp, w_in, q_gain_t, k_gain_t, bd]
    if use_rope:
        tiles_per_seq = seq_len // tm
        in_specs += [pl.BlockSpec((tm, LANES), lambda i: (i % tiles_per_seq, 0))] * 2
        args += list(rope)
    widths = [4 * RET_W, ATT_Q_W, ATT_Q_W, None, CONV_DIM, CONV_DIM]
    out_specs = [pl.BlockSpec((tm, w), row) for w in widths if w]
    out_shape = [jax.ShapeDtypeStruct((n, w), BF16) for w in widths if w]
    out_specs.insert(3, pl.BlockSpec((ATT_KV_W, tm), lambda i: (0, i)))
    out_shape.insert(3, jax.ShapeDtypeStruct((ATT_KV_W, n), BF16))
    out_specs.insert(0, pl.BlockSpec((tm, D_MODEL), row))
    out_shape.insert(0, jax.ShapeDtypeStruct((n, D_MODEL), F32))
    if emit_kv:
        out_specs += [pl.BlockSpec((tm, ATT_KV_W), row)] * 2
        out_shape += [jax.ShapeDtypeStruct((n, ATT_KV_W), F32)] * 2
    return pl.pallas_call(
        functools.partial(_inproj_kernel, use_rope=use_rope, emit_kv=emit_kv),
        grid=(n // tm,),
        in_specs=in_specs,
        out_specs=out_specs,
        out_shape=out_shape,
        scratch_shapes=[pltpu.VMEM((tm, FFN_DIM), BF16)],
        compiler_params=pltpu.CompilerParams(
            dimension_semantics=("arbitrary",), vmem_limit_bytes=VMEM_LIMIT),
        name="ffn_inproj",
    )(*args)


RET_GROUP = 8
RET_SCAN_UNROLL = 2
TAB_DMASK, TAB_DQF, TAB_DQB, TAB_DKF, TAB_DKB, TAB_DCF, TAB_DCB, N_TAB = range(8)


def _log_sigmoid(x):
    return jnp.minimum(x, 0.0) - jnp.log(1.0 + jnp.exp(-jnp.abs(x)))


def _chunk_rows(start):
    if isinstance(start, int):
        return pl.ds(start, RET_CHUNK)
    return pl.ds(pl.multiple_of(start, RET_CHUNK), RET_CHUNK)


def _retention_kernel(*refs, n_seq, n_chunks, has_s0, emit_state):
    dl_ref, q_ref, k_ref, v_ref, g_ref, gn_ref = refs[:6]
    pos = 6
    if has_s0:
        s0_ref = refs[pos]
        pos += 1
    o_ref = refs[pos]
    pos += 1
    if emit_state:
        st_ref = refs[pos]
        pos += 1
    tab_ref, kv_ref, ent_ref = refs[pos:pos + 3]

    C = RET_CHUNK
    hd = pl.program_id(1)
    row = lax.broadcasted_iota(jnp.int32, (C, C), 0).astype(F32)
    col = lax.broadcasted_iota(jnp.int32, (C, C), 1).astype(F32)
    diff = row - col
    lgf = _log_sigmoid(jnp.full((C, C), dl_ref[0, hd], F32))
    lgb = _log_sigmoid(jnp.full((C, C), dl_ref[1, hd], F32))
    tab_ref[TAB_DMASK] = (jnp.where(diff >= 0, jnp.exp(lgf * jnp.maximum(diff, 0.0)), 0.0)
                          + jnp.where(diff <= 0, jnp.exp(lgb * jnp.maximum(-diff, 0.0)), 0.0))
    tab_ref[TAB_DQF] = jnp.exp(lgf * (row + 1.0))
    tab_ref[TAB_DQB] = jnp.exp(lgb * (C - row))
    tab_ref[TAB_DKF] = jnp.exp(lgf * (C - 1.0 - row))
    tab_ref[TAB_DKB] = jnp.exp(lgb * row)
    tab_ref[TAB_DCF] = jnp.exp(lgf * C)
    tab_ref[TAB_DCB] = jnp.exp(lgb * C)

    def kv_phase(items):
        ops = []
        for r, _ in items:
            rows = _chunk_rows(r)
            k = k_ref[rows, :].astype(F32)
            vT = v_ref[rows, :].astype(F32).T.astype(BF16)
            kk = jnp.concatenate([(k * tab_ref[TAB_DKF]).astype(BF16),
                                  (k * tab_ref[TAB_DKB]).astype(BF16)], axis=1)
            ops.append((vT, kk))
        for (_, i), (vT, kk) in zip(items, ops):
            kv_ref[i] = _dot(vT, kk)

    def scan_phase(s, sc):
        if has_s0:
            st0 = (s0_ref[0].T, s0_ref[1].T)
        else:
            st0 = (jnp.zeros((RET_DV, RET_DK), F32), jnp.zeros((RET_DV, RET_DK), F32))

        def scan_step(t, st):
            st_f, st_b = st
            cb = n_chunks - 1 - t
            ent_ref[sc + t, :, 0:RET_DK] = st_f.astype(BF16)
            ent_ref[sc + cb, :, RET_DK:2 * RET_DK] = st_b.astype(BF16)
            st_f = st_f * tab_ref[TAB_DCF] + kv_ref[sc + t, :, 0:RET_DK]
            st_b = st_b * tab_ref[TAB_DCB] + kv_ref[sc + cb, :, RET_DK:2 * RET_DK]
            return st_f, st_b

        if n_chunks <= RET_SCAN_UNROLL:
            st = st0
            for t in range(n_chunks):
                st = scan_step(t, st)
        else:
            st = lax.fori_loop(0, n_chunks, scan_step, st0)
        if emit_state:
            st_ref[s, 0] = st[0].T
            st_ref[s, 1] = st[1].T

    def score_stage(items):
        qs = [q_ref[_chunk_rows(r), :] for r, _ in items]
        att = [_dot_nt(q, k_ref[_chunk_rows(r), :]) for q, (r, _) in zip(qs, items)]
        att = [(a * tab_ref[TAB_DMASK]).astype(BF16) for a in att]
        qq = []
        for q in qs:
            qf = q.astype(F32)
            qq.append(jnp.concatenate([(qf * tab_ref[TAB_DQF]).astype(BF16),
                                       (qf * tab_ref[TAB_DQB]).astype(BF16)], axis=1))
        return att, qq

    def product_stage(items, att, qq):
        return [_dot(a, v_ref[_chunk_rows(r), :]) + _dot_nt(x, ent_ref[i])
                for a, x, (r, i) in zip(att, qq, items)]

    def norm_stage(items, outs):
        for o, (r, _) in zip(outs, items):
            rows = _chunk_rows(r)
            ms = jnp.mean(o * o, axis=-1, keepdims=True)
            on = (o * lax.rsqrt(ms + EPS)) * gn_ref[...]
            o_ref[rows, :] = (g_ref[rows, :].astype(F32) * on).astype(BF16)

    items = [(s * n_chunks * C + c * C, s * n_chunks + c) for s in range(n_seq) for c in range(n_chunks)]
    assert len(items) % RET_GROUP == 0
    groups = [items[i:i + RET_GROUP] for i in range(0, len(items), RET_GROUP)]
    for grp in groups:
        kv_phase(grp)
    for s in range(n_seq):
        scan_phase(s, s * n_chunks)
    scored, products = {}, {}
    for t in range(len(groups) + 2):
        if t < len(groups):
            scored[t] = score_stage(groups[t])
        if 0 <= t - 1 < len(groups):
            products[t - 1] = product_stage(groups[t - 1], *scored.pop(t - 1))
        if 0 <= t - 2 < len(groups):
            norm_stage(groups[t - 2], products.pop(t - 2))


def _retention(ret_in, decay_logit_l, gn_l, s0, seq_len, n_seq_blk, emit_state):
    n = ret_in.shape[0]
    n_chunks = seq_len // RET_CHUNK
    tb = n_seq_blk * seq_len
    has_s0 = s0 is not None
    n_off = RET_W // RET_DK

    def branch(k):
        return pl.BlockSpec((tb, RET_DK), lambda i, h: (i, k * n_off + h))

    in_specs = [pl.BlockSpec(memory_space=pltpu.SMEM), branch(0), branch(1), branch(2), branch(3),
                pl.BlockSpec((1, RET_DV), lambda i, h: (0, h))]
    args = [decay_logit_l, ret_in, ret_in, ret_in, ret_in, gn_l]
    if has_s0:
        assert n_seq_blk == 1
        in_specs.append(pl.BlockSpec((None, 2, None, RET_DK, RET_DV), lambda i, h: (i, 0, h, 0, 0)))
        args.append(s0)
    out_specs = [pl.BlockSpec((tb, RET_DV), lambda i, h: (i, h))]
    out_shape = [jax.ShapeDtypeStruct((n, RET_W), BF16)]
    if emit_state:
        out_specs.append(pl.BlockSpec((n_seq_blk, 2, None, RET_DK, RET_DV), lambda i, h: (i, 0, h, 0, 0)))
        out_shape.append(jax.ShapeDtypeStruct((n // seq_len, 2, N_RET_HEADS, RET_DK, RET_DV), F32))
    n_slots = n_chunks * n_seq_blk
    return pl.pallas_call(
        functools.partial(_retention_kernel, n_seq=n_seq_blk, n_chunks=n_chunks, has_s0=has_s0,
                          emit_state=emit_state),
        grid=(n // tb, N_RET_HEADS),
        in_specs=in_specs,
        out_specs=out_specs,
        out_shape=out_shape,
        scratch_shapes=[
            pltpu.VMEM((N_TAB, RET_CHUNK, RET_CHUNK), F32),
            pltpu.VMEM((n_slots, RET_DV, 2 * RET_DK), F32),
            pltpu.VMEM((n_slots, RET_DV, 2 * RET_DK), BF16),
        ],
        compiler_params=pltpu.CompilerParams(
            dimension_semantics=("arbitrary", "arbitrary"), vmem_limit_bytes=VMEM_LIMIT),
        name="retention",
    )(*args)


ATT_TQ = 1024
ATT_KB = 512
ATT_SHORT_MAX = 512
ATT_RC = 16
NEG_BIG = -1e30
SUM_ROWS = BF16_SUBLANES


def _attention_kernel(*refs, n_cache, n_new):
    q_ref = refs[0]
    pos = 1
    if n_cache:
        kc_ref, vc_ref = refs[pos:pos + 2]
        pos += 2
    kn_ref, vn_ref, o_ref, qs_ref, acc_ref = refs[pos:pos + 5]
    s_slots = refs[pos + 5:pos + 7]
    p_slots = refs[pos + 7:pos + 9]

    tq = q_ref.shape[0]
    nq = GROUP * tq
    qf = q_ref[...].astype(F32)
    head = lax.broadcasted_iota(jnp.int32, (tq, KV_LANES), 1) // HEAD_DIM
    for g in range(GROUP):
        qs_ref[:, g * tq:(g + 1) * tq] = jnp.where(head == g, qf, 0.0).T.astype(BF16)
    acc_ref[...] = jnp.zeros(acc_ref.shape, F32)

    kb = min(ATT_KB, n_new)
    n_cb = 1 if n_cache else 0
    n_nb = n_new // kb
    assert n_new % kb == 0
    n_blk = n_cb + n_nb

    def block(t):
        if isinstance(t, int) and t < n_cb:
            return kc_ref, vc_ref, pl.ds(0, n_cache)
        lo = (t - n_cb) * kb
        return kn_ref, vn_ref, (pl.ds(lo, kb) if isinstance(lo, int) else pl.ds(pl.multiple_of(lo, kb), kb))

    def scores(t, par):
        k_ref, _, keys = block(t)
        s = _dot(k_ref[keys, :], qs_ref[...])
        s_slots[par][0:keys.size, :] = s
        return jnp.max(s, axis=0, keepdims=True)

    def softmax(n_keys, par, m, s_max):
        s_ref, p_ref = s_slots[par], p_slots[par]
        m_new = jnp.maximum(m, s_max)
        m_rows = jnp.broadcast_to(m_new, (ATT_RC, nq))
        for r in range(0, n_keys, ATT_RC):
            p_ref[r:r + ATT_RC, :] = jnp.exp2(s_ref[r:r + ATT_RC, :] - m_rows).astype(BF16)
        return m_new, jnp.exp2(m - m_new)

    def pv(t, par, alpha):
        _, vT_ref, keys = block(t)
        v1 = jnp.concatenate([vT_ref[:, keys], jnp.ones((SUM_ROWS, keys.size), BF16)], axis=0)
        acc_ref[...] = alpha * acc_ref[...] + _dot(v1, p_slots[par][0:keys.size, :])

    def stage(t, par, carry):
        m, alpha, s_max = carry
        next_max = s_max
        if not isinstance(t, int) or t + 1 < n_blk:
            next_max = scores(t + 1, 1 - par)
        if not isinstance(t, int) or t >= 1:
            pv(t - 1, 1 - par, alpha)
        return softmax(block(t)[2].size, par, m, s_max) + (next_max,)

    carry = (jnp.full((1, nq), NEG_BIG, F32), jnp.zeros((1, nq), F32), scores(0, 0))
    lo_t = min(n_cb + 1, n_blk)
    n_loop = max(n_blk - 1 - lo_t, 0)
    if n_loop % 2:
        lo_t += 1
        n_loop -= 1
    for t in range(lo_t):
        carry = stage(t, t % 2, carry)

    def pair(i, carry):
        t = lo_t + 2 * i
        carry = stage(t, lo_t % 2, carry)
        return stage(t + 1, (lo_t + 1) % 2, carry)

    carry = lax.fori_loop(0, n_loop // 2, pair, carry)
    for t in range(lo_t + n_loop, n_blk):
        carry = stage(t, t % 2, carry)
    pv(n_blk - 1, (n_blk - 1) % 2, carry[1])

    acc = acc_ref[...]
    oT = acc[0:HEAD_DIM] / acc[HEAD_DIM:HEAD_DIM + 1]
    o4 = jnp.concatenate([oT[:, g * tq:(g + 1) * tq] for g in range(GROUP)], axis=0)
    o_ref[...] = o4.T.astype(BF16)


ATT_SEQ_PER_STEP = 4


def _attention_short_kernel(q_ref, k_ref, vT_ref, o_ref, *, n_seq):
    t_len = q_ref.shape[0] // n_seq
    head = lax.broadcasted_iota(jnp.int32, (t_len, KV_LANES), 1) // HEAD_DIM
    ones = jnp.ones((SUM_ROWS, t_len), BF16)
    items = [(slice(s * t_len, (s + 1) * t_len), kv) for s in range(n_seq) for kv in range(N_KV_HEADS)]

    def cols(kv):
        return slice(kv * KV_LANES, (kv + 1) * KV_LANES)

    def score_stage(item):
        rows, kv = item
        qf = q_ref[rows, cols(kv)].astype(F32)
        qsT = jnp.concatenate([jnp.where(head == g, qf, 0.0).T.astype(BF16) for g in range(GROUP)],
                              axis=1)
        return _dot(k_ref[rows, cols(kv)], qsT)

    def softmax_stage(sT):
        return jnp.exp2(sT - jnp.max(sT, axis=0, keepdims=True)).astype(BF16)

    def value_stage(item, pT):
        rows, kv = item
        v1 = jnp.concatenate([vT_ref[kv * HEAD_DIM:(kv + 1) * HEAD_DIM, rows], ones], axis=0)
        return _dot(v1, pT)

    def out_stage(item, acc):
        rows, kv = item
        oT = acc[0:HEAD_DIM] / acc[HEAD_DIM:HEAD_DIM + 1]
        o4 = jnp.concatenate([oT[:, g * t_len:(g + 1) * t_len] for g in range(GROUP)], axis=0)
        o_ref[rows, cols(kv)] = o4.T.astype(BF16)

    n = len(items)
    scores, probs, accs = {}, {}, {}
    for t in range(n + 3):
        if t < n:
            scores[t] = score_stage(items[t])
        if 0 <= t - 1 < n:
            probs[t - 1] = softmax_stage(scores.pop(t - 1))
        if 0 <= t - 2 < n:
            accs[t - 2] = value_stage(items[t - 2], probs.pop(t - 2))
        if 0 <= t - 3 < n:
            out_stage(items[t - 3], accs.pop(t - 3))


def _attention_short(q, kt, vT, seq_len):
    n = q.shape[0]
    tb = ATT_SEQ_PER_STEP * seq_len
    return pl.pallas_call(
        functools.partial(_attention_short_kernel, n_seq=ATT_SEQ_PER_STEP),
        grid=(n // tb,),
        in_specs=[pl.BlockSpec((tb, ATT_Q_W), lambda i: (i, 0)),
                  pl.BlockSpec((tb, ATT_Q_W), lambda i: (i, 0)),
                  pl.BlockSpec((ATT_KV_W, tb), lambda i: (0, i))],
        out_specs=pl.BlockSpec((tb, ATT_Q_W), lambda i: (i, 0)),
        out_shape=jax.ShapeDtypeStruct((n, ATT_Q_W), BF16),
        compiler_params=pltpu.CompilerParams(
            dimension_semantics=("arbitrary",), vmem_limit_bytes=VMEM_LIMIT),
        name="attention_short",
    )(q, kt, vT)


def _attention(q, kt, vT, cache, seq_len):
    n = q.shape[0]
    n_batch = n // seq_len
    tq = ATT_TQ
    nq = seq_len // tq
    n_cache = 0 if cache is None else cache[0].shape[1]
    if n_cache == 0 and seq_len <= ATT_SHORT_MAX:
        return _attention_short(q, kt, vT, seq_len)
    blk_rows = max(min(ATT_KB, seq_len), n_cache)
    in_specs = [pl.BlockSpec((tq, KV_LANES), lambda b, kv, i: (b * nq + i, kv))]
    args = [q]
    if n_cache:
        in_specs += [pl.BlockSpec((None, n_cache, KV_LANES), lambda b, kv, i: (b, 0, kv)),
                     pl.BlockSpec((None, HEAD_DIM, n_cache), lambda b, kv, i: (b, kv, 0))]
        args += list(cache)
    in_specs += [pl.BlockSpec((seq_len, KV_LANES), lambda b, kv, i: (b, kv)),
                 pl.BlockSpec((HEAD_DIM, seq_len), lambda b, kv, i: (kv, b))]
    args += [kt, vT]
    return pl.pallas_call(
        functools.partial(_attention_kernel, n_cache=n_cache, n_new=seq_len),
        grid=(n_batch, N_KV_HEADS, nq),
        in_specs=in_specs,
        out_specs=pl.BlockSpec((tq, KV_LANES), lambda b, kv, i: (b * nq + i, kv)),
        out_shape=jax.ShapeDtypeStruct((n, ATT_Q_W), BF16),
        scratch_shapes=[
            pltpu.VMEM((KV_LANES, GROUP * tq), BF16),
            pltpu.VMEM((HEAD_DIM + SUM_ROWS, GROUP * tq), F32),
            pltpu.VMEM((blk_rows, GROUP * tq), F32),
            pltpu.VMEM((blk_rows, GROUP * tq), F32),
            pltpu.VMEM((blk_rows, GROUP * tq), BF16),
            pltpu.VMEM((blk_rows, GROUP * tq), BF16),
        ],
        compiler_params=pltpu.CompilerParams(
            dimension_semantics=("arbitrary", "arbitrary", "arbitrary"), vmem_limit_bytes=VMEM_LIMIT),
        name="attention",
    )(*args)


MERGE_TM = 512
MERGE_SUB = 256


def _merge_kernel(h_ref, mp_ref, mpf_ref, ro_ref, ao_ref, cb_ref, u_ref, up_ref, un_ref, cw_ref,
                  wg_ref, wr_ref, wa_ref, wc_ref, wo_ref, wfi_ref, wfo_ref, o_ref, mid_ref, hid_ref,
                  *, seq_len):
    tm = h_ref.shape[0]
    sub = MERGE_SUB
    n_sub = tm // sub
    i = pl.program_id(0)
    cw = cw_ref[...]
    r = lax.broadcasted_iota(jnp.int32, (sub, CONV_DIM), 0)
    edge = BF16_SUBLANES

    def branches(s):
        r0 = s * sub
        rows = slice(r0, r0 + sub)
        yr = _dot(ro_ref[rows, :], wr_ref[...])
        ya = _dot(ao_ref[rows, :], wa_ref[...])
        u = u_ref[rows, :].astype(F32)
        before = up_ref if s == 0 else u_ref.at[r0 - edge:r0, :]
        after = un_ref if s == n_sub - 1 else u_ref.at[r0 + sub:r0 + sub + edge, :]
        prev_row = before[edge - 1:edge, :].astype(F32)
        next_row = after[0:1, :].astype(F32)
        t = (i * tm + r0 + r) % seq_len
        u_prev = jnp.where(r == 0, prev_row, pltpu.roll(u, 1, axis=0))
        u_next = jnp.where(r == sub - 1, next_row, pltpu.roll(u, sub - 1, axis=0))
        u_prev = jnp.where(t == 0, 0.0, u_prev)
        u_next = jnp.where(t == seq_len - 1, 0.0, u_next)
        conv = u_prev * cw[0:1] + u * cw[1:2] + u_next * cw[2:3]
        yc = _dot((cb_ref[rows, :].astype(F32) * conv).astype(BF16), wc_ref[...])
        xb = _mod_norm(h_ref[rows, :], mp_ref[...]).astype(BF16)
        merged = _sigmoid(_dot(xb, wg_ref[0, :,0:D_MODEL])) * yr
        merged = merged + _sigmoid(_dot(xb, wg_ref[0, :,D_MODEL:2 * D_MODEL])) * ya
        merged = merged + _sigmoid(_dot(xb, wg_ref[0, :,2 * D_MODEL:3 * D_MODEL])) * yc
        return merged.astype(BF16)

    def project(s, merged):
        rows = slice(s * sub, (s + 1) * sub)
        mid_ref[rows, :] = h_ref[rows, :] + mp_ref[2:3, :] * _dot(merged, wo_ref[...])

    pending = branches(0)
    for s in range(1, n_sub):
        nxt = branches(s)
        project(s - 1, pending)
        pending = nxt
    project(n_sub - 1, pending)
    o_ref[...] = _ffn_tile(mid_ref[...], mpf_ref[...], wfi_ref, wfo_ref, hid_ref)


def _merge_ffn(h, mp, mp_ffn, rows_per_mod, ret_o, att_o, cb, u, conv_w8, w_gates, w_ret_o, w_att_o,
               w_conv_o, w_o, w_ffn_in, w_ffn_out, l, seq_len):
    n = h.shape[0]
    tm = MERGE_TM
    hb = tm // BF16_SUBLANES
    last = n // BF16_SUBLANES - 1
    row = lambda i: (i, 0)
    wspec = lambda k: _resident((None, k, D_MODEL), lambda i: (l, 0, 0))
    mod_row = lambda i: ((i * tm) // rows_per_mod, 0, 0)
    return pl.pallas_call(
        functools.partial(_merge_kernel, seq_len=seq_len),
        grid=(n // tm,),
        in_specs=[
            pl.BlockSpec((tm, D_MODEL), row),
            pl.BlockSpec((None, 8, D_MODEL), mod_row),
            pl.BlockSpec((None, 8, D_MODEL), mod_row),
            pl.BlockSpec((tm, RET_W), row),
            pl.BlockSpec((tm, ATT_Q_W), row),
            pl.BlockSpec((tm, CONV_DIM), row),
            pl.BlockSpec((tm, CONV_DIM), row),
            pl.BlockSpec((BF16_SUBLANES, CONV_DIM), lambda i: (jnp.maximum(i * hb - 1, 0), 0)),
            pl.BlockSpec((BF16_SUBLANES, CONV_DIM), lambda i: (jnp.minimum((i + 1) * hb, last), 0)),
            pl.BlockSpec((None, 8, CONV_DIM), lambda i: (l, 0, 0)),
            _resident((pl.Element(1), pl.Element(D_MODEL), pl.Element(3 * D_MODEL)),
                      lambda i: (l, 0, OFF_GATES)),
            wspec(RET_W), wspec(ATT_Q_W), wspec(CONV_DIM), wspec(D_MODEL),
            _resident((None, None, D_MODEL, 2 * FFN_DIM), lambda i: (l, 1, 0, 0)),
            _resident((None, None, FFN_DIM, D_MODEL), lambda i: (l, 1, 0, 0)),
        ],
        out_specs=pl.BlockSpec((tm, D_MODEL), row),
        out_shape=jax.ShapeDtypeStruct((n, D_MODEL), F32),
        scratch_shapes=[pltpu.VMEM((tm, D_MODEL), F32), pltpu.VMEM((tm, FFN_DIM), BF16)],
        compiler_params=pltpu.CompilerParams(
            dimension_semantics=("arbitrary",), vmem_limit_bytes=VMEM_LIMIT),
        name="merge_ffn",
    )(h, mp, mp_ffn, ret_o, att_o, cb, u, u, u, conv_w8, w_gates, w_ret_o, w_att_o, w_conv_o, w_o,
      w_ffn_in, w_ffn_out)


def _rope_tables(n_tok):
    rows = n_tok // GRID_W
    t_row = np.repeat(np.arange(rows, dtype=np.float64), GRID_W)
    t_col = np.tile(np.arange(GRID_W, dtype=np.float64), rows)
    n_freq = HEAD_DIM // 4
    inv = ROPE_THETA ** (-np.arange(n_freq, dtype=np.float64) / n_freq)
    ang = np.concatenate([t_row[:, None] * inv, t_col[:, None] * inv], axis=-1)
    cos, sin = np.cos(ang), np.sin(ang)
    cos64 = np.concatenate([cos, cos], axis=-1)
    sin64 = np.concatenate([-sin, sin], axis=-1)
    return (jnp.asarray(np.tile(cos64, (1, LANES // HEAD_DIM)), F32),
            jnp.asarray(np.tile(sin64, (1, LANES // HEAD_DIM)), F32))


def _mod_pack(mod_l, rows, sub, norm_w_row):
    r0 = rows[0]
    nr = len(rows)
    m = mod_l[r0:r0 + nr, 3 * sub * D_MODEL:3 * (sub + 1) * D_MODEL].reshape(nr, 3, D_MODEL)
    nw = jnp.broadcast_to(norm_w_row[None, None, :], (nr, 1, D_MODEL))
    pad = jnp.zeros((nr, 4, D_MODEL), F32)
    return jnp.concatenate([m, nw, pad], axis=1)


def kernel(x_prompt, x_sample, c, state_ret, cache_k, cache_v, c_ctx, w_ada, b_ada, norm_w, w_ffn_in,
           w_ffn_out, w_in, ret_decay_logit, ret_gn, q_gain, k_gain, conv_w, w_ret_o, w_att_o, w_conv_o,
           w_o):
    n_ctx_b, ctx_len, _ = x_prompt.shape
    n_lat_b, lat_len, _ = x_sample.shape

    w_ffn_in_b = w_ffn_in.astype(BF16)
    w_ffn_out_b = w_ffn_out.astype(BF16)
    w_in_b = w_in.astype(BF16)
    w_gates_b = w_in_b
    w_ret_o_b = w_ret_o.astype(BF16)
    w_att_o_b = w_att_o.astype(BF16)
    w_conv_o_b = w_conv_o.astype(BF16)
    w_o_b = w_o.astype(BF16)

    q_gain_t = jnp.tile(q_gain, (1, N_HEADS)).reshape(DEPTH, 1, ATT_Q_W)
    k_gain_t = jnp.tile(k_gain, (1, N_KV_HEADS)).reshape(DEPTH, 1, ATT_KV_W)
    conv_w8 = jnp.pad(conv_w, ((0, 0), (0, 8 - conv_w.shape[1]), (0, 0)))
    gid = np.arange(ATT_Q_W) // HEAD_DIM
    bd = jnp.asarray(gid[:, None] == gid[None, :], BF16)
    rope = _rope_tables(lat_len)

    cond8 = jnp.zeros((8, D_MODEL), F32).at[0].set(c_ctx).at[1:1 + n_lat_b].set(c)
    mod = _ada(cond8, w_ada, b_ada)

    ck = cache_k.astype(BF16)
    ck = jnp.broadcast_to(ck[:, :, :, :, None, :], ck.shape[:4] + (GROUP, HEAD_DIM))
    cache_kt = ck.reshape(ck.shape[0], ck.shape[1], ck.shape[2], N_KV_HEADS * KV_LANES)
    cache_vT = jnp.transpose(cache_v.astype(BF16), (0, 1, 3, 4, 2)).reshape(
        cache_v.shape[0], cache_v.shape[1], ATT_KV_W, cache_v.shape[2])

    groups = (
        dict(x=x_prompt.reshape(n_ctx_b * ctx_len, D_MODEL), rows=[0], seq=ctx_len, ctx=True),
        dict(x=x_sample.reshape(n_lat_b * lat_len, D_MODEL), rows=list(range(1, 1 + n_lat_b)),
             seq=lat_len, ctx=False),
    )
    results = []
    for grp in groups:
        h = grp["x"]
        seq = grp["seq"]
        is_ctx = grp["ctx"]
        rpm = h.shape[0] // len(grp["rows"])
        states, keys, values = [], [], []
        for l in range(DEPTH):
            mp = [_mod_pack(mod[l], grp["rows"], s, norm_w[l, s]) for s in range(3)]
            outs = _ffn_inproj(h, mp[0], mp[1], rpm, w_ffn_in_b, w_ffn_out_b, w_in_b, l, q_gain_t,
                               k_gain_t, bd, None if is_ctx else rope, seq, emit_kv=is_ctx)
            h, ret_in, qn, kt, vT, cb, u = outs[:7]
            if is_ctx:
                ret_o, st = _retention(ret_in, ret_decay_logit[l], ret_gn[l][None, :], None, seq,
                                       n_seq_blk=16, emit_state=True)
                states.append(st)
                keys.append(outs[7].reshape(n_ctx_b, seq, N_KV_HEADS, HEAD_DIM))
                values.append(outs[8].reshape(n_ctx_b, seq, N_KV_HEADS, HEAD_DIM))
                att_o = _attention(qn, kt, vT, None, seq)
            else:
                (ret_o,) = _retention(ret_in, ret_decay_logit[l], ret_gn[l][None, :], state_ret[:, l],
                                      seq, n_seq_blk=1, emit_state=False)
                att_o = _attention(qn, kt, vT, (cache_kt[:, l], cache_vT[:, l]), seq)
            h = _merge_ffn(h, mp[1], mp[2], rpm, ret_o, att_o, cb, u, conv_w8, w_gates_b, w_ret_o_b,
                           w_att_o_b, w_conv_o_b, w_o_b, w_ffn_in_b, w_ffn_out_b, l, seq)
        results.append((h, states, keys, values))

    (y_ctx, states, keys, values), (y_lat, _, _, _) = results
    y_prompt = y_ctx.reshape(x_prompt.shape)
    y_sample = y_lat.reshape(x_sample.shape)
    new_state_ret = jnp.stack(states, axis=1)
    new_cache_k = jnp.stack(keys, axis=1)
    new_cache_v = jnp.stack(values, axis=1)
    return (y_prompt, y_sample, new_state_ret, new_cache_k, new_cache_v)
```

```python
import functools
import math

import numpy as np
import jax
import jax.numpy as jnp
from jax import lax
from jax.experimental import pallas as pl
from jax.experimental.pallas import tpu as pltpu

D_MODEL = 1024
DEPTH = 2
GRID_W = 64
N_RET_HEADS = 4
RET_DK = 128
RET_DV = 128
RET_CHUNK = 128
N_HEADS = 8
N_KV_HEADS = 2
HEAD_DIM = 64
ROPE_THETA = 10000.0
CONV_DIM = 512
FFN_DIM = 2816
N_MOD = 9
EPS = 1e-6
RET_W = N_RET_HEADS * RET_DK
ATT_Q_W = N_HEADS * HEAD_DIM
ATT_KV_W = N_KV_HEADS * HEAD_DIM
GROUP = N_HEADS // N_KV_HEADS
KV_LANES = GROUP * HEAD_DIM
OFF_RQ, OFF_RK, OFF_RV, OFF_RG = 0, 512, 1024, 1536
OFF_AQ, OFF_AK, OFF_AV = 2048, 2560, 2688
OFF_CB, OFF_CC, OFF_CX = 2816, 3328, 3840
OFF_GATES = 4352

LANES = 128
BF16_SUBLANES = 16
VMEM_LIMIT = 56 * 1024 * 1024

F32 = jnp.float32
BF16 = jnp.bfloat16


def _dot(a, b):
    return jnp.dot(a, b, preferred_element_type=F32)


def _dot_nt(a, b):
    return lax.dot_general(a, b, (((1,), (1,)), ((), ())), preferred_element_type=F32)


def _sigmoid(x):
    return 1.0 / (1.0 + jnp.exp(-x))


def _silu(x):
    return x * _sigmoid(x)


def _mod_norm(x, mp):
    ms = jnp.mean(x * x, axis=-1, keepdims=True)
    y = x * lax.rsqrt(ms + EPS)
    return (y * mp[3:4]) * (1.0 + mp[1:2]) + mp[0:1]


def _resident(shape, index):
    return pl.BlockSpec(shape, index, pipeline_mode=pl.Buffered(1))


ADA_TN = 2304


def _ada_kernel(c_ref, w_ref, b_ref, o_ref):
    a = _silu(c_ref[...]).astype(BF16)
    o_ref[...] = _dot(a, w_ref[...].astype(BF16)) + b_ref[...]


def _ada(cond8, w_ada, b_ada):
    n = N_MOD * D_MODEL
    return pl.pallas_call(
        _ada_kernel,
        grid=(DEPTH, n // ADA_TN),
        in_specs=[
            pl.BlockSpec((8, D_MODEL), lambda l, j: (0, 0)),
            pl.BlockSpec((None, D_MODEL, ADA_TN), lambda l, j: (l, 0, j)),
            pl.BlockSpec((None, 1, ADA_TN), lambda l, j: (l, 0, j)),
        ],
        out_specs=pl.BlockSpec((None, 8, ADA_TN), lambda l, j: (l, 0, j)),
        out_shape=jax.ShapeDtypeStruct((DEPTH, 8, n), F32),
        compiler_params=pltpu.CompilerParams(
            dimension_semantics=("arbitrary", "arbitrary"), vmem_limit_bytes=VMEM_LIMIT),
        name="ada",
    )(cond8, w_ada, b_ada.reshape(DEPTH, 1, n))


FFN_FC = 256


def _ffn_tile(x, mp, wi_ref, wo_ref, h_ref):
    xb = _mod_norm(x, mp).astype(BF16)
    for c in range(FFN_DIM // FFN_FC):
        lo = c * FFN_FC
        g = _dot(xb, wi_ref[:, lo:lo + FFN_FC])
        u = _dot(xb, wi_ref[:, FFN_DIM + lo:FFN_DIM + lo + FFN_FC])
        h_ref[:, lo:lo + FFN_FC] = (_silu(g) * u).astype(BF16)
    y = _dot(h_ref[...], wo_ref[...])
    return x + (0.5 * mp[2:3]) * y


INPROJ_TM = 512
INPROJ_SUB = 256
Q_SCALE = HEAD_DIM ** -0.5 * math.log2(math.e)


def _group_sumsq(a, bd):
    sq = a * a
    hi = sq.astype(BF16)
    lo = (sq - hi.astype(F32)).astype(BF16)
    return _dot(hi, bd) + _dot(lo, bd)


def _group_rms(a, sumsq, gain):
    return (a * lax.rsqrt(sumsq * (1.0 / HEAD_DIM) + EPS)) * gain


def _rope(x, cos, sin_signed):
    half = HEAD_DIM // 2
    reps = x.shape[1] // LANES
    lane = lax.broadcasted_iota(jnp.int32, (x.shape[0], LANES), 1)
    first = (lane % HEAD_DIM) < half
    out = []
    for r in range(reps):
        xs = x[:, r * LANES:(r + 1) * LANES]
        partner = jnp.where(first, pltpu.roll(xs, LANES - half, axis=1), pltpu.roll(xs, half, axis=1))
        out.append(xs * cos + partner * sin_signed)
    return out[0] if reps == 1 else jnp.concatenate(out, axis=1)


def _tile_kv(a):
    lane = lax.broadcasted_iota(jnp.int32, a.shape, 1)
    sw = pltpu.roll(a, HEAD_DIM, axis=1)
    h0 = jnp.where(lane < HEAD_DIM, a, sw)
    h1 = jnp.where(lane < HEAD_DIM, sw, a)
    return jnp.concatenate([h0, h0, h1, h1], axis=1)


def _inproj_kernel(*refs, use_rope, emit_kv):
    xin_ref, mpf_ref, wfi_ref, wfo_ref, mp_ref, w_ref, qg_ref, kg_ref, bd_ref = refs[:9]
    pos = 9
    if use_rope:
        cos_ref, sin_ref = refs[pos:pos + 2]
        pos += 2
    x_ref, ret_ref, q_ref, kt_ref, vT_ref, cb_ref, u_ref = refs[pos:pos + 7]
    pos += 7
    if emit_kv:
        kout_ref, vout_ref = refs[pos:pos + 2]
    hid_ref = refs[-1]

    x_ref[...] = _ffn_tile(xin_ref[...], mpf_ref[...], wfi_ref, wfo_ref, hid_ref)
    mp = mp_ref[...]
    tm = x_ref.shape[0]
    n_sub = tm // INPROJ_SUB
    xbs = [None] * n_sub
    xbs[0] = _mod_norm(x_ref[0:INPROJ_SUB, :], mp).astype(BF16)
    for s in range(n_sub):
        rows = slice(s * INPROJ_SUB, (s + 1) * INPROJ_SUB)
        xb = xbs[s]

        def proj(off, width):
            return _dot(xb, w_ref[:, off:off + width])

        aq = proj(OFF_AQ, ATT_Q_W)
        ret_ref[rows, OFF_RQ:OFF_RQ + RET_W] = proj(OFF_RQ, RET_W).astype(BF16)
        if s + 1 < n_sub:
            nxt = slice((s + 1) * INPROJ_SUB, (s + 2) * INPROJ_SUB)
            xbs[s + 1] = _mod_norm(x_ref[nxt, :], mp).astype(BF16)
        ss_q = _group_sumsq(aq, bd_ref[...])
        ak = proj(OFF_AK, ATT_KV_W)
        v = proj(OFF_AV, ATT_KV_W)
        ret_ref[rows, OFF_RK:OFF_RK + RET_W] = (proj(OFF_RK, RET_W) * (RET_DK ** -0.5)).astype(BF16)

        q = _group_rms(aq, ss_q, qg_ref[...])
        if use_rope:
            q = _rope(q, cos_ref[rows, :], sin_ref[rows, :])
        q_ref[rows, :] = (q * Q_SCALE).astype(BF16)
        ss_k = _group_sumsq(ak, bd_ref[0:ATT_KV_W, 0:ATT_KV_W])
        ret_ref[rows, OFF_RV:OFF_RV + RET_W] = proj(OFF_RV, RET_W).astype(BF16)

        k = _group_rms(ak, ss_k, kg_ref[...])
        if emit_kv:
            kout_ref[rows, :] = k
            vout_ref[rows, :] = v
        if use_rope:
            k = _rope(k, cos_ref[rows, :], sin_ref[rows, :])
        kt_ref[rows, :] = _tile_kv(k).astype(BF16)
        vT_ref[:, rows] = v.T.astype(BF16)

        ret_ref[rows, OFF_RG:OFF_RG + RET_W] = _silu(proj(OFF_RG, RET_W)).astype(BF16)
        cb_ref[rows, :] = proj(OFF_CB, CONV_DIM).astype(BF16)
        u_ref[rows, :] = (proj(OFF_CC, CONV_DIM) * proj(OFF_CX, CONV_DIM)).astype(BF16)


def _ffn_inproj(x, mp_ffn, mp, rows_per_mod, w_ffn_in, w_ffn_out, w_in, l, q_gain_t, k_gain_t, bd, rope,
                seq_len, emit_kv):
    n = x.shape[0]
    tm = INPROJ_TM
    use_rope = rope is not None
    row = lambda i: (i, 0)
    mod_row = lambda i: ((i * tm) // rows_per_mod, 0, 0)
    in_specs = [
        pl.BlockSpec((tm, D_MODEL), row),
        pl.BlockSpec((None, 8, D_MODEL), mod_row),
        _resident((None, None, D_MODEL, 2 * FFN_DIM), lambda i: (l, 0, 0, 0)),
        _resident((None, None, FFN_DIM, D_MODEL), lambda i: (l, 0, 0, 0)),
        pl.BlockSpec((None, 8, D_MODEL), mod_row),
        _resident((None, D_MODEL, OFF_GATES), lambda i: (l, 0, 0)),
        pl.BlockSpec((None, 1, ATT_Q_W), lambda i: (l, 0, 0)),
        pl.BlockSpec((None, 1, ATT_KV_W), lambda i: (l, 0, 0)),
        _resident((ATT_Q_W, ATT_Q_W), lambda i: (0, 0)),
    ]
    args = [x, mp_ffn, w_ffn_in, w_ffn_out, mp, w_in, q_gain_t, k_gain_t, bd]
    if use_rope:
        tiles_per_seq = seq_len // tm
        in_specs += [pl.BlockSpec((tm, LANES), lambda i: (i % tiles_per_seq, 0))] * 2
        args += list(rope)
    widths = [4 * RET_W, ATT_Q_W, ATT_Q_W, None, CONV_DIM, CONV_DIM]
    out_specs = [pl.BlockSpec((tm, w), row) for w in widths if w]
    out_shape = [jax.ShapeDtypeStruct((n, w), BF16) for w in widths if w]
    out_specs.insert(3, pl.BlockSpec((ATT_KV_W, tm), lambda i: (0, i)))
    out_shape.insert(3, jax.ShapeDtypeStruct((ATT_KV_W, n), BF16))
    out_specs.insert(0, pl.BlockSpec((tm, D_MODEL), row))
    out_shape.insert(0, jax.ShapeDtypeStruct((n, D_MODEL), F32))
    if emit_kv:
        out_specs += [pl.BlockSpec((tm, ATT_KV_W), row)] * 2
        out_shape += [jax.ShapeDtypeStruct((n, ATT_KV_W), F32)] * 2
    return pl.pallas_call(
        functools.partial(_inproj_kernel, use_rope=use_rope, emit_kv=emit_kv),
        grid=(n // tm,),
        in_specs=in_specs,
        out_specs=out_specs,
        out_shape=out_shape,
        scratch_shapes=[pltpu.VMEM((tm, FFN_DIM), BF16)],
        compiler_params=pltpu.CompilerParams(
            dimension_semantics=("arbitrary",), vmem_limit_bytes=VMEM_LIMIT),
        name="ffn_inproj",
    )(*args)


RET_GROUP = 8
RET_SCAN_UNROLL = 2
TAB_DMASK, TAB_DQF, TAB_DQB, TAB_DKF, TAB_DKB, TAB_DCF, TAB_DCB, N_TAB = range(8)


def _log_sigmoid(x):
    return jnp.minimum(x, 0.0) - jnp.log(1.0 + jnp.exp(-jnp.abs(x)))


def _chunk_rows(start):
    if isinstance(start, int):
        return pl.ds(start, RET_CHUNK)
    return pl.ds(pl.multiple_of(start, RET_CHUNK), RET_CHUNK)


def _retention_kernel(*refs, n_seq, n_chunks, has_s0, emit_state):
    dl_ref, q_ref, k_ref, v_ref, g_ref, gn_ref = refs[:6]
    pos = 6
    if has_s0:
        s0_ref = refs[pos]
        pos += 1
    o_ref = refs[pos]
    pos += 1
    if emit_state:
        st_ref = refs[pos]
        pos += 1
    tab_ref, kv_ref, ent_ref = refs[pos:pos + 3]

    C = RET_CHUNK
    hd = pl.program_id(1)
    row = lax.broadcasted_iota(jnp.int32, (C, C), 0).astype(F32)
    col = lax.broadcasted_iota(jnp.int32, (C, C), 1).astype(F32)
    diff = row - col
    lgf = _log_sigmoid(jnp.full((C, C), dl_ref[0, hd], F32))
    lgb = _log_sigmoid(jnp.full((C, C), dl_ref[1, hd], F32))
    tab_ref[TAB_DMASK] = (jnp.where(diff >= 0, jnp.exp(lgf * jnp.maximum(diff, 0.0)), 0.0)
                          + jnp.where(diff <= 0, jnp.exp(lgb * jnp.maximum(-diff, 0.0)), 0.0))
    tab_ref[TAB_DQF] = jnp.exp(lgf * (row + 1.0))
    tab_ref[TAB_DQB] = jnp.exp(lgb * (C - row))
    tab_ref[TAB_DKF] = jnp.exp(lgf * (C - 1.0 - row))
    tab_ref[TAB_DKB] = jnp.exp(lgb * row)
    tab_ref[TAB_DCF] = jnp.exp(lgf * C)
    tab_ref[TAB_DCB] = jnp.exp(lgb * C)

    def kv_phase(items):
        ops = []
        for r, _ in items:
            rows = _chunk_rows(r)
            k = k_ref[rows, :].astype(F32)
            vT = v_ref[rows, :].astype(F32).T.astype(BF16)
            kk = jnp.concatenate([(k * tab_ref[TAB_DKF]).astype(BF16),
                                  (k * tab_ref[TAB_DKB]).astype(BF16)], axis=1)
            ops.append((vT, kk))
        for (_, i), (vT, kk) in zip(items, ops):
            kv_ref[i] = _dot(vT, kk)

    def scan_phase(s, sc):
        if has_s0:
            st0 = (s0_ref[0].T, s0_ref[1].T)
        else:
            st0 = (jnp.zeros((RET_DV, RET_DK), F32), jnp.zeros((RET_DV, RET_DK), F32))

        def scan_step(t, st):
            st_f, st_b = st
            cb = n_chunks - 1 - t
            ent_ref[sc + t, :, 0:RET_DK] = st_f.astype(BF16)
            ent_ref[sc + cb, :, RET_DK:2 * RET_DK] = st_b.astype(BF16)
            st_f = st_f * tab_ref[TAB_DCF] + kv_ref[sc + t, :, 0:RET_DK]
            st_b = st_b * tab_ref[TAB_DCB] + kv_ref[sc + cb, :, RET_DK:2 * RET_DK]
            return st_f, st_b

        if n_chunks <= RET_SCAN_UNROLL:
            st = st0
            for t in range(n_chunks):
                st = scan_step(t, st)
        else:
            st = lax.fori_loop(0, n_chunks, scan_step, st0)
        if emit_state:
            st_ref[s, 0] = st[0].T
            st_ref[s, 1] = st[1].T

    def score_stage(items):
        qs = [q_ref[_chunk_rows(r), :] for r, _ in items]
        att = [_dot_nt(q, k_ref[_chunk_rows(r), :]) for q, (r, _) in zip(qs, items)]
        att = [(a * tab_ref[TAB_DMASK]).astype(BF16) for a in att]
        qq = []
        for q in qs:
            qf = q.astype(F32)
            qq.append(jnp.concatenate([(qf * tab_ref[TAB_DQF]).astype(BF16),
                                       (qf * tab_ref[TAB_DQB]).astype(BF16)], axis=1))
        return att, qq

    def product_stage(items, att, qq):
        return [_dot(a, v_ref[_chunk_rows(r), :]) + _dot_nt(x, ent_ref[i])
                for a, x, (r, i) in zip(att, qq, items)]

    def norm_stage(items, outs):
        for o, (r, _) in zip(outs, items):
            rows = _chunk_rows(r)
            ms = jnp.mean(o * o, axis=-1, keepdims=True)
            on = (o * lax.rsqrt(ms + EPS)) * gn_ref[...]
            o_ref[rows, :] = (g_ref[rows, :].astype(F32) * on).astype(BF16)

    items = [(s * n_chunks * C + c * C, s * n_chunks + c) for s in range(n_seq) for c in range(n_chunks)]
    assert len(items) % RET_GROUP == 0
    groups = [items[i:i + RET_GROUP] for i in range(0, len(items), RET_GROUP)]
    for grp in groups:
        kv_phase(grp)
    for s in range(n_seq):
        scan_phase(s, s * n_chunks)
    scored, products = {}, {}
    for t in range(len(groups) + 2):
        if t < len(groups):
            scored[t] = score_stage(groups[t])
        if 0 <= t - 1 < len(groups):
            products[t - 1] = product_stage(groups[t - 1], *scored.pop(t - 1))
        if 0 <= t - 2 < len(groups):
            norm_stage(groups[t - 2], products.pop(t - 2))


def _retention(ret_in, decay_logit_l, gn_l, s0, seq_len, n_seq_blk, emit_state):
    n = ret_in.shape[0]
    n_chunks = seq_len // RET_CHUNK
    tb = n_seq_blk * seq_len
    has_s0 = s0 is not None
    n_off = RET_W // RET_DK

    def branch(k):
        return pl.BlockSpec((tb, RET_DK), lambda i, h: (i, k * n_off + h))

    in_specs = [pl.BlockSpec(memory_space=pltpu.SMEM), branch(0), branch(1), branch(2), branch(3),
                pl.BlockSpec((1, RET_DV), lambda i, h: (0, h))]
    args = [decay_logit_l, ret_in, ret_in, ret_in, ret_in, gn_l]
    if has_s0:
        assert n_seq_blk == 1
        in_specs.append(pl.BlockSpec((None, 2, None, RET_DK, RET_DV), lambda i, h: (i, 0, h, 0, 0)))
        args.append(s0)
    out_specs = [pl.BlockSpec((tb, RET_DV), lambda i, h: (i, h))]
    out_shape = [jax.ShapeDtypeStruct((n, RET_W), BF16)]
    if emit_state:
        out_specs.append(pl.BlockSpec((n_seq_blk, 2, None, RET_DK, RET_DV), lambda i, h: (i, 0, h, 0, 0)))
        out_shape.append(jax.ShapeDtypeStruct((n // seq_len, 2, N_RET_HEADS, RET_DK, RET_DV), F32))
    n_slots = n_chunks * n_seq_blk
    return pl.pallas_call(
        functools.partial(_retention_kernel, n_seq=n_seq_blk, n_chunks=n_chunks, has_s0=has_s0,
                          emit_state=emit_state),
        grid=(n // tb, N_RET_HEADS),
        in_specs=in_specs,
        out_specs=out_specs,
        out_shape=out_shape,
        scratch_shapes=[
            pltpu.VMEM((N_TAB, RET_CHUNK, RET_CHUNK), F32),
            pltpu.VMEM((n_slots, RET_DV, 2 * RET_DK), F32),
            pltpu.VMEM((n_slots, RET_DV, 2 * RET_DK), BF16),
        ],
        compiler_params=pltpu.CompilerParams(
            dimension_semantics=("arbitrary", "arbitrary"), vmem_limit_bytes=VMEM_LIMIT),
        name="retention",
    )(*args)


ATT_TQ = 1024
ATT_KB = 512
ATT_SHORT_MAX = 512
ATT_RC = 16
NEG_BIG = -1e30
SUM_ROWS = BF16_SUBLANES


def _attention_kernel(*refs, n_cache, n_new):
    q_ref = refs[0]
    pos = 1
    if n_cache:
        kc_ref, vc_ref = refs[pos:pos + 2]
        pos += 2
    kn_ref, vn_ref, o_ref, qs_ref, acc_ref = refs[pos:pos + 5]
    s_slots = refs[pos + 5:pos + 7]
    p_slots = refs[pos + 7:pos + 9]

    tq = q_ref.shape[0]
    nq = GROUP * tq
    qf = q_ref[...].astype(F32)
    head = lax.broadcasted_iota(jnp.int32, (tq, KV_LANES), 1) // HEAD_DIM
    for g in range(GROUP):
        qs_ref[:, g * tq:(g + 1) * tq] = jnp.where(head == g, qf, 0.0).T.astype(BF16)
    acc_ref[...] = jnp.zeros(acc_ref.shape, F32)

    kb = min(ATT_KB, n_new)
    n_cb = 1 if n_cache else 0
    n_nb = n_new // kb
    assert n_new % kb == 0
    n_blk = n_cb + n_nb

    def block(t):
        if isinstance(t, int) and t < n_cb:
            return kc_ref, vc_ref, pl.ds(0, n_cache)
        lo = (t - n_cb) * kb
        return kn_ref, vn_ref, (pl.ds(lo, kb) if isinstance(lo, int) else pl.ds(pl.multiple_of(lo, kb), kb))

    def scores(t, par):
        k_ref, _, keys = block(t)
        s = _dot(k_ref[keys, :], qs_ref[...])
        s_slots[par][0:keys.size, :] = s
        return jnp.max(s, axis=0, keepdims=True)

    def softmax(n_keys, par, m, s_max):
        s_ref, p_ref = s_slots[par], p_slots[par]
        m_new = jnp.maximum(m, s_max)
        m_rows = jnp.broadcast_to(m_new, (ATT_RC, nq))
        for r in range(0, n_keys, ATT_RC):
            p_ref[r:r + ATT_RC, :] = jnp.exp2((s_ref[r:r + ATT_RC, :] - m_rows).astype(BF16))
        return m_new, jnp.exp2(m - m_new)

    def pv(t, par, alpha):
        _, vT_ref, keys = block(t)
        v1 = jnp.concatenate([vT_ref[:, keys], jnp.ones((SUM_ROWS, keys.size), BF16)], axis=0)
        acc_ref[...] = alpha * acc_ref[...] + _dot(v1, p_slots[par][0:keys.size, :])

    def stage(t, par, carry):
        m, alpha, s_max = carry
        next_max = s_max
        if not isinstance(t, int) or t + 1 < n_blk:
            next_max = scores(t + 1, 1 - par)
        if not isinstance(t, int) or t >= 1:
            pv(t - 1, 1 - par, alpha)
        return softmax(block(t)[2].size, par, m, s_max) + (next_max,)

    carry = (jnp.full((1, nq), NEG_BIG, F32), jnp.zeros((1, nq), F32), scores(0, 0))
    lo_t = min(n_cb + 1, n_blk)
    n_loop = max(n_blk - 1 - lo_t, 0)
    if n_loop % 2:
        lo_t += 1
        n_loop -= 1
    for t in range(lo_t):
        carry = stage(t, t % 2, carry)

    def pair(i, carry):
        t = lo_t + 2 * i
        carry = stage(t, lo_t % 2, carry)
        return stage(t + 1, (lo_t + 1) % 2, carry)

    carry = lax.fori_loop(0, n_loop // 2, pair, carry)
    for t in range(lo_t + n_loop, n_blk):
        carry = stage(t, t % 2, carry)
    pv(n_blk - 1, (n_blk - 1) % 2, carry[1])

    acc = acc_ref[...]
    oT = acc[0:HEAD_DIM] / acc[HEAD_DIM:HEAD_DIM + 1]
    o4 = jnp.concatenate([oT[:, g * tq:(g + 1) * tq] for g in range(GROUP)], axis=0)
    o_ref[...] = o4.T.astype(BF16)


ATT_SEQ_PER_STEP = 4


def _attention_short_kernel(q_ref, k_ref, vT_ref, o_ref, *, n_seq):
    t_len = q_ref.shape[0] // n_seq
    head = lax.broadcasted_iota(jnp.int32, (t_len, KV_LANES), 1) // HEAD_DIM
    ones = jnp.ones((SUM_ROWS, t_len), BF16)
    items = [(slice(s * t_len, (s + 1) * t_len), kv) for s in range(n_seq) for kv in range(N_KV_HEADS)]

    def cols(kv):
        return slice(kv * KV_LANES, (kv + 1) * KV_LANES)

    def score_stage(item):
        rows, kv = item
        qf = q_ref[rows, cols(kv)].astype(F32)
        qsT = jnp.concatenate([jnp.where(head == g, qf, 0.0).T.astype(BF16) for g in range(GROUP)],
                              axis=1)
        return _dot(k_ref[rows, cols(kv)], qsT)

    def softmax_stage(sT):
        return jnp.exp2(sT - jnp.max(sT, axis=0, keepdims=True)).astype(BF16)

    def value_stage(item, pT):
        rows, kv = item
        v1 = jnp.concatenate([vT_ref[kv * HEAD_DIM:(kv + 1) * HEAD_DIM, rows], ones], axis=0)
        return _dot(v1, pT)

    def out_stage(item, acc):
        rows, kv = item
        oT = acc[0:HEAD_DIM] / acc[HEAD_DIM:HEAD_DIM + 1]
        o4 = jnp.concatenate([oT[:, g * t_len:(g + 1) * t_len] for g in range(GROUP)], axis=0)
        o_ref[rows, cols(kv)] = o4.T.astype(BF16)

    n = len(items)
    scores, probs, accs = {}, {}, {}
    for t in range(n + 3):
        if t < n:
            scores[t] = score_stage(items[t])
        if 0 <= t - 1 < n:
            probs[t - 1] = softmax_stage(scores.pop(t - 1))
        if 0 <= t - 2 < n:
            accs[t - 2] = value_stage(items[t - 2], probs.pop(t - 2))
        if 0 <= t - 3 < n:
            out_stage(items[t - 3], accs.pop(t - 3))


def _attention_short(q, kt, vT, seq_len):
    n = q.shape[0]
    tb = ATT_SEQ_PER_STEP * seq_len
    return pl.pallas_call(
        functools.partial(_attention_short_kernel, n_seq=ATT_SEQ_PER_STEP),
        grid=(n // tb,),
        in_specs=[pl.BlockSpec((tb, ATT_Q_W), lambda i: (i, 0)),
                  pl.BlockSpec((tb, ATT_Q_W), lambda i: (i, 0)),
                  pl.BlockSpec((ATT_KV_W, tb), lambda i: (0, i))],
        out_specs=pl.BlockSpec((tb, ATT_Q_W), lambda i: (i, 0)),
        out_shape=jax.ShapeDtypeStruct((n, ATT_Q_W), BF16),
        compiler_params=pltpu.CompilerParams(
            dimension_semantics=("arbitrary",), vmem_limit_bytes=VMEM_LIMIT),
        name="attention_short",
    )(q, kt, vT)


def _attention(q, kt, vT, cache, seq_len):
    n = q.shape[0]
    n_batch = n // seq_len
    tq = ATT_TQ
    nq = seq_len // tq
    n_cache = 0 if cache is None else cache[0].shape[1]
    if n_cache == 0 and seq_len <= ATT_SHORT_MAX:
        return _attention_short(q, kt, vT, seq_len)
    blk_rows = max(min(ATT_KB, seq_len), n_cache)
    in_specs = [pl.BlockSpec((tq, KV_LANES), lambda b, kv, i: (b * nq + i, kv))]
    args = [q]
    if n_cache:
        in_specs += [pl.BlockSpec((None, n_cache, KV_LANES), lambda b, kv, i: (b, 0, kv)),
                     pl.BlockSpec((None, HEAD_DIM, n_cache), lambda b, kv, i: (b, kv, 0))]
        args += list(cache)
    in_specs += [pl.BlockSpec((seq_len, KV_LANES), lambda b, kv, i: (b, kv)),
                 pl.BlockSpec((HEAD_DIM, seq_len), lambda b, kv, i: (kv, b))]
    args += [kt, vT]
    return pl.pallas_call(
        functools.partial(_attention_kernel, n_cache=n_cache, n_new=seq_len),
        grid=(n_batch, N_KV_HEADS, nq),
        in_specs=in_specs,
        out_specs=pl.BlockSpec((tq, KV_LANES), lambda b, kv, i: (b * nq + i, kv)),
        out_shape=jax.ShapeDtypeStruct((n, ATT_Q_W), BF16),
        scratch_shapes=[
            pltpu.VMEM((KV_LANES, GROUP * tq), BF16),
            pltpu.VMEM((HEAD_DIM + SUM_ROWS, GROUP * tq), F32),
            pltpu.VMEM((blk_rows, GROUP * tq), F32),
            pltpu.VMEM((blk_rows, GROUP * tq), F32),
            pltpu.VMEM((blk_rows, GROUP * tq), BF16),
            pltpu.VMEM((blk_rows, GROUP * tq), BF16),
        ],
        compiler_params=pltpu.CompilerParams(
            dimension_semantics=("arbitrary", "arbitrary", "arbitrary"), vmem_limit_bytes=VMEM_LIMIT),
        name="attention",
    )(*args)


MERGE_TM = 512
MERGE_SUB = 256


def _merge_kernel(h_ref, mp_ref, mpf_ref, ro_ref, ao_ref, cb_ref, u_ref, up_ref, un_ref, cw_ref,
                  wg_ref, wr_ref, wa_ref, wc_ref, wo_ref, wfi_ref, wfo_ref, o_ref, mid_ref, hid_ref,
                  *, seq_len):
    tm = h_ref.shape[0]
    sub = MERGE_SUB
    n_sub = tm // sub
    i = pl.program_id(0)
    cw = cw_ref[...]
    r = lax.broadcasted_iota(jnp.int32, (sub, CONV_DIM), 0)
    edge = BF16_SUBLANES

    def branches(s):
        r0 = s * sub
        rows = slice(r0, r0 + sub)
        yr = _dot(ro_ref[rows, :], wr_ref[...])
        ya = _dot(ao_ref[rows, :], wa_ref[...])
        u = u_ref[rows, :].astype(F32)
        before = up_ref if s == 0 else u_ref.at[r0 - edge:r0, :]
        after = un_ref if s == n_sub - 1 else u_ref.at[r0 + sub:r0 + sub + edge, :]
        prev_row = before[edge - 1:edge, :].astype(F32)
        next_row = after[0:1, :].astype(F32)
        t = (i * tm + r0 + r) % seq_len
        u_prev = jnp.where(r == 0, prev_row, pltpu.roll(u, 1, axis=0))
        u_next = jnp.where(r == sub - 1, next_row, pltpu.roll(u, sub - 1, axis=0))
        u_prev = jnp.where(t == 0, 0.0, u_prev)
        u_next = jnp.where(t == seq_len - 1, 0.0, u_next)
        conv = u_prev * cw[0:1] + u * cw[1:2] + u_next * cw[2:3]
        yc = _dot((cb_ref[rows, :].astype(F32) * conv).astype(BF16), wc_ref[...])
        xb = _mod_norm(h_ref[rows, :], mp_ref[...]).astype(BF16)
        merged = _sigmoid(_dot(xb, wg_ref[0, :,0:D_MODEL])) * yr
        merged = merged + _sigmoid(_dot(xb, wg_ref[0, :,D_MODEL:2 * D_MODEL])) * ya
        merged = merged + _sigmoid(_dot(xb, wg_ref[0, :,2 * D_MODEL:3 * D_MODEL])) * yc
        return merged.astype(BF16)

    def project(s, merged):
        rows = slice(s * sub, (s + 1) * sub)
        mid_ref[rows, :] = h_ref[rows, :] + mp_ref[2:3, :] * _dot(merged, wo_ref[...])

    pending = branches(0)
    for s in range(1, n_sub):
        nxt = branches(s)
        project(s - 1, pending)
        pending = nxt
    project(n_sub - 1, pending)
    o_ref[...] = _ffn_tile(mid_ref[...], mpf_ref[...], wfi_ref, wfo_ref, hid_ref)


def _merge_ffn(h, mp, mp_ffn, rows_per_mod, ret_o, att_o, cb, u, conv_w8, w_gates, w_ret_o, w_att_o,
               w_conv_o, w_o, w_ffn_in, w_ffn_out, l, seq_len):
    n = h.shape[0]
    tm = MERGE_TM
    hb = tm // BF16_SUBLANES
    last = n // BF16_SUBLANES - 1
    row = lambda i: (i, 0)
    wspec = lambda k: _resident((None, k, D_MODEL), lambda i: (l, 0, 0))
    mod_row = lambda i: ((i * tm) // rows_per_mod, 0, 0)
    return pl.pallas_call(
        functools.partial(_merge_kernel, seq_len=seq_len),
        grid=(n // tm,),
        in_specs=[
            pl.BlockSpec((tm, D_MODEL), row),
            pl.BlockSpec((None, 8, D_MODEL), mod_row),
            pl.BlockSpec((None, 8, D_MODEL), mod_row),
            pl.BlockSpec((tm, RET_W), row),
            pl.BlockSpec((tm, ATT_Q_W), row),
            pl.BlockSpec((tm, CONV_DIM), row),
            pl.BlockSpec((tm, CONV_DIM), row),
            pl.BlockSpec((BF16_SUBLANES, CONV_DIM), lambda i: (jnp.maximum(i * hb - 1, 0), 0)),
            pl.BlockSpec((BF16_SUBLANES, CONV_DIM), lambda i: (jnp.minimum((i + 1) * hb, last), 0)),
            pl.BlockSpec((None, 8, CONV_DIM), lambda i: (l, 0, 0)),
            _resident((pl.Element(1), pl.Element(D_MODEL), pl.Element(3 * D_MODEL)),
                      lambda i: (l, 0, OFF_GATES)),
            wspec(RET_W), wspec(ATT_Q_W), wspec(CONV_DIM), wspec(D_MODEL),
            _resident((None, None, D_MODEL, 2 * FFN_DIM), lambda i: (l, 1, 0, 0)),
            _resident((None, None, FFN_DIM, D_MODEL), lambda i: (l, 1, 0, 0)),
        ],
        out_specs=pl.BlockSpec((tm, D_MODEL), row),
        out_shape=jax.ShapeDtypeStruct((n, D_MODEL), F32),
        scratch_shapes=[pltpu.VMEM((tm, D_MODEL), F32), pltpu.VMEM((tm, FFN_DIM), BF16)],
        compiler_params=pltpu.CompilerParams(
            dimension_semantics=("arbitrary",), vmem_limit_bytes=VMEM_LIMIT),
        name="merge_ffn",
    )(h, mp, mp_ffn, ret_o, att_o, cb, u, u, u, conv_w8, w_gates, w_ret_o, w_att_o, w_conv_o, w_o,
      w_ffn_in, w_ffn_out)


def _rope_tables(n_tok):
    rows = n_tok // GRID_W
    t_row = np.repeat(np.arange(rows, dtype=np.float64), GRID_W)
    t_col = np.tile(np.arange(GRID_W, dtype=np.float64), rows)
    n_freq = HEAD_DIM // 4
    inv = ROPE_THETA ** (-np.arange(n_freq, dtype=np.float64) / n_freq)
    ang = np.concatenate([t_row[:, None] * inv, t_col[:, None] * inv], axis=-1)
    cos, sin = np.cos(ang), np.sin(ang)
    cos64 = np.concatenate([cos, cos], axis=-1)
    sin64 = np.concatenate([-sin, sin], axis=-1)
    return (jnp.asarray(np.tile(cos64, (1, LANES // HEAD_DIM)), F32),
            jnp.asarray(np.tile(sin64, (1, LANES // HEAD_DIM)), F32))


def _mod_pack(mod_l, rows, sub, norm_w_row):
    r0 = rows[0]
    nr = len(rows)
    m = mod_l[r0:r0 + nr, 3 * sub * D_MODEL:3 * (sub + 1) * D_MODEL].reshape(nr, 3, D_MODEL)
    nw = jnp.broadcast_to(norm_w_row[None, None, :], (nr, 1, D_MODEL))
    pad = jnp.zeros((nr, 4, D_MODEL), F32)
    return jnp.concatenate([m, nw, pad], axis=1)


def kernel(x_prompt, x_sample, c, state_ret, cache_k, cache_v, c_ctx, w_ada, b_ada, norm_w, w_ffn_in,
           w_ffn_out, w_in, ret_decay_logit, ret_gn, q_gain, k_gain, conv_w, w_ret_o, w_att_o, w_conv_o,
           w_o):
    n_ctx_b, ctx_len, _ = x_prompt.shape
    n_lat_b, lat_len, _ = x_sample.shape

    w_ffn_in_b = w_ffn_in.astype(BF16)
    w_ffn_out_b = w_ffn_out.astype(BF16)
    w_in_b = w_in.astype(BF16)
    w_gates_b = w_in_b
    w_ret_o_b = w_ret_o.astype(BF16)
    w_att_o_b = w_att_o.astype(BF16)
    w_conv_o_b = w_conv_o.astype(BF16)
    w_o_b = w_o.astype(BF16)

    q_gain_t = jnp.tile(q_gain, (1, N_HEADS)).reshape(DEPTH, 1, ATT_Q_W)
    k_gain_t = jnp.tile(k_gain, (1, N_KV_HEADS)).reshape(DEPTH, 1, ATT_KV_W)
    conv_w8 = jnp.pad(conv_w, ((0, 0), (0, 8 - conv_w.shape[1]), (0, 0)))
    gid = np.arange(ATT_Q_W) // HEAD_DIM
    bd = jnp.asarray(gid[:, None] == gid[None, :], BF16)
    rope = _rope_tables(lat_len)

    cond8 = jnp.zeros((8, D_MODEL), F32).at[0].set(c_ctx).at[1:1 + n_lat_b].set(c)
    mod = _ada(cond8, w_ada, b_ada)

    ck = cache_k.astype(BF16)
    ck = jnp.broadcast_to(ck[:, :, :, :, None, :], ck.shape[:4] + (GROUP, HEAD_DIM))
    cache_kt = ck.reshape(ck.shape[0], ck.shape[1], ck.shape[2], N_KV_HEADS * KV_LANES)
    cache_vT = jnp.transpose(cache_v.astype(BF16), (0, 1, 3, 4, 2)).reshape(
        cache_v.shape[0], cache_v.shape[1], ATT_KV_W, cache_v.shape[2])

    groups = (
        dict(x=x_prompt.reshape(n_ctx_b * ctx_len, D_MODEL), rows=[0], seq=ctx_len, ctx=True),
        dict(x=x_sample.reshape(n_lat_b * lat_len, D_MODEL), rows=list(range(1, 1 + n_lat_b)),
             seq=lat_len, ctx=False),
    )
    results = []
    for grp in groups:
        h = grp["x"]
        seq = grp["seq"]
        is_ctx = grp["ctx"]
        rpm = h.shape[0] // len(grp["rows"])
        states, keys, values = [], [], []
        for l in range(DEPTH):
            mp = [_mod_pack(mod[l], grp["rows"], s, norm_w[l, s]) for s in range(3)]
            outs = _ffn_inproj(h, mp[0], mp[1], rpm, w_ffn_in_b, w_ffn_out_b, w_in_b, l, q_gain_t,
                               k_gain_t, bd, None if is_ctx else rope, seq, emit_kv=is_ctx)
            h, ret_in, qn, kt, vT, cb, u = outs[:7]
            if is_ctx:
                ret_o, st = _retention(ret_in, ret_decay_logit[l], ret_gn[l][None, :], None, seq,
                                       n_seq_blk=16, emit_state=True)
                states.append(st)
                keys.append(outs[7].reshape(n_ctx_b, seq, N_KV_HEADS, HEAD_DIM))
                values.append(outs[8].reshape(n_ctx_b, seq, N_KV_HEADS, HEAD_DIM))
                att_o = _attention(qn, kt, vT, None, seq)
            else:
                (ret_o,) = _retention(ret_in, ret_decay_logit[l], ret_gn[l][None, :], state_ret[:, l],
                                      seq, n_seq_blk=1, emit_state=False)
                att_o = _attention(qn, kt, vT, (cache_kt[:, l], cache_vT[:, l]), seq)
            h = _merge_ffn(h, mp[1], mp[2], rpm, ret_o, att_o, cb, u, conv_w8, w_gates_b, w_ret_o_b,
                           w_att_o_b, w_conv_o_b, w_o_b, w_ffn_in_b, w_ffn_out_b, l, seq)
        results.append((h, states, keys, values))

    (y_ctx, states, keys, values), (y_lat, _, _, _) = results
    y_prompt = y_ctx.reshape(x_prompt.shape)
    y_sample = y_lat.reshape(x_sample.shape)
    new_state_ret = jnp.stack(states, axis=1)
    new_cache_k = jnp.stack(keys, axis=1)
    new_cache_v = jnp.stack(values, axis=1)
    return (y_prompt, y_sample, new_state_ret, new_cache_k, new_cache_v)
```
